```python
import math
import jax
import jax.numpy as jnp
from jax import lax
import numpy as np

D_MODEL = 1024
BATCH = 32
SEQ = 256
DEPTH = 2
DEC_BATCH = 2
DEC_SEQ = 4096
PAST_LEN = 512

GRID_W = 64
EPS = 1e-6
N_BRANCH = 3
LRU_W = 512
LRU_BLOCKS = 8
LRU_BLK = LRU_W // LRU_BLOCKS
LRU_CONV = 4
LRU_C = 8.0
N_HEADS = 8
D_NOPE = 64
D_ROPE = 32
D_QK = D_NOPE + D_ROPE
D_V = 64
Q_RANK = 256
KV_RANK = 128
ROPE_BASE = 10000.0
Q_BLOCK = 128
HY_W = 512
HY_ORDER = 2
HY_CONV = 3
HY_EMB = 33
HY_HID = 64
HY_FAST_PCT = 0.3
HY_SLOW_PCT = 1.5
D_FF = 2816
N_EXPERTS = 8
TOP_K = 2
D_FF_E = 1408
IN_SIZES = (LRU_W, Q_RANK, KV_RANK, D_ROPE, (HY_ORDER + 1) * HY_W, N_BRANCH * D_MODEL)
IN_W = sum(IN_SIZES)
F32 = jnp.float32

kernel_name = 'hybrid_diffusion_prefix_step'


def rmsnorm(x, g):
    xf = x.astype(F32)
    y = xf * lax.rsqrt(jnp.mean(xf * xf, axis=-1, keepdims=True) + EPS)
    return (y * g.astype(F32)).astype(x.dtype)


def dwconv(x, w, b, pad_left):
    K, L = w.shape[0], x.shape[1]
    xp = jnp.pad(x, ((0, 0), (pad_left, K - 1 - pad_left), (0, 0)))
    return b + sum(w[k] * xp[:, k:k + L] for k in range(K))


def adaln_terms(cond, w, b):
    m = jax.nn.silu(cond) @ w + b
    return m.reshape(cond.shape[0], 6, 1, D_MODEL)


def axial_rope_angles(L):
    rows = L // GRID_W
    row = jnp.repeat(jnp.arange(rows, dtype=F32), GRID_W)
    col = jnp.tile(jnp.arange(GRID_W, dtype=F32), rows)
    half = D_ROPE // 2
    inv_freq = ROPE_BASE ** (-jnp.arange(0, half, 2, dtype=F32) / half)
    ang = jnp.concatenate([row[:, None] * inv_freq, col[:, None] * inv_freq], axis=-1)
    return jnp.cos(ang), jnp.sin(ang)


def apply_rope(x, cos, sin):
    bshape = (x.shape[1],) + (1,) * (x.ndim - 3) + (D_ROPE // 2,)
    cos, sin = cos.reshape(bshape), sin.reshape(bshape)
    xf = x.astype(F32)
    x1, x2 = xf[..., 0::2], xf[..., 1::2]
    y = jnp.stack([x1 * cos - x2 * sin, x1 * sin + x2 * cos], axis=-1)
    return y.reshape(x.shape).astype(x.dtype)


def mla_queries(q_c, q_norm, w_uq, q_qknorm, rope):
    q = jnp.einsum('blr,rhd->blhd', rmsnorm(q_c, q_norm), w_uq)
    q = rmsnorm(q, q_qknorm)
    if rope is not None:
        q = jnp.concatenate([q[..., :D_NOPE], apply_rope(q[..., D_NOPE:], *rope)], axis=-1)
    return q


def mla_keys_values(c_kv, k_r, w_uk, w_uv, k_qknorm, rope):
    B, S, _ = c_kv.shape
    k_nope = jnp.einsum('bsr,rhd->bshd', c_kv, w_uk)
    k_rope = jnp.broadcast_to(k_r[:, :, None, :], (B, S, N_HEADS, D_ROPE))
    k = rmsnorm(jnp.concatenate([k_nope, k_rope], axis=-1), k_qknorm)
    if rope is not None:
        k = jnp.concatenate([k[..., :D_NOPE], apply_rope(k[..., D_NOPE:], *rope)], axis=-1)
    v = jnp.einsum('bsr,rhd->bshd', c_kv, w_uv)
    return k, v


def block_attention(q, k, v):
    B, L, H, _ = q.shape
    nb = L // Q_BLOCK
    qb = jnp.moveaxis(q.reshape(B, nb, Q_BLOCK, H, D_QK), 1, 0)
    scale = 1.0 / math.sqrt(D_QK)

    def one_block(qi):
        s = jnp.einsum('bqhd,bshd->bhqs', qi, k).astype(F32) * scale
        pr = jax.nn.softmax(s, axis=-1).astype(v.dtype)
        return jnp.einsum('bhqs,bshd->bqhd', pr, v)

    o = lax.map(one_block, qb)
    return jnp.moveaxis(o, 0, 1).reshape(B, L, H * D_V)


def _lin_combine(e1, e2):
    a1, b1 = e1
    a2, b2 = e2
    return a1 * a2, a2 * b1 + b2


def rglru_scan(xc, w_gate, b_gate, lam, h0, reverse):
    B, L, _ = xc.shape
    xb = xc.reshape(B, L, LRU_BLOCKS, LRU_BLK)
    g = jnp.einsum('blnj,gnjk->gblnk', xb, w_gate.astype(F32)).reshape(2, B, L, LRU_W)
    g = g + b_gate.astype(F32)[:, None, None, :]
    r, i = jax.nn.sigmoid(g[0]), jax.nn.sigmoid(g[1])
    log_a = LRU_C * r * jax.nn.log_sigmoid(lam.astype(F32))
    a = jnp.exp(log_a)
    b = jnp.sqrt(-jnp.expm1(2.0 * log_a)) * (i * xc)
    if h0 is not None:
        edge = -1 if reverse else 0
        b = b.at[:, edge].add(a[:, edge] * h0)
    _, h = lax.associative_scan(_lin_combine, (a, b), reverse=reverse, axis=1)
    return h


def rglru_mixer(u, conv_w, conv_b, w_gate, b_gate, lam, h0):
    xc = dwconv(u, conv_w, conv_b, 2).astype(F32)
    if h0 is None:
        hf = rglru_scan(xc, w_gate[0], b_gate[0], lam[0], None, False)
        hb = rglru_scan(xc, w_gate[1], b_gate[1], lam[1], None, True)
        state = jnp.stack([hf[:, -1], hb[:, 0]], axis=1).astype(u.dtype)
        return (hf + hb).astype(u.dtype), state
    h0f = h0.astype(F32)
    hf = rglru_scan(xc, w_gate[0], b_gate[0], lam[0], h0f[:, 0], False)
    hb = rglru_scan(xc, w_gate[1], b_gate[1], lam[1], h0f[:, 1], True)
    return (hf + hb).astype(u.dtype)


def hyena_filter_spectra(L, w1, b1, f1, w2, b2, f2, w3):
    t = jnp.linspace(0.0, 1.0, L, dtype=F32)[:, None]
    bands = (HY_EMB - 1) // 2
    w = (2.0 * math.pi / L) * jnp.arange(L, dtype=F32)[:, None]
    f = jnp.linspace(1e-4, bands - 1, bands, dtype=F32)[None, :]
    z = jnp.concatenate([t, jnp.cos(f * w), -jnp.sin(f * w)], axis=-1)
    h = jnp.sin(f1.astype(F32) * (z @ w1.astype(F32) + b1.astype(F32)))
    h = jnp.sin(f2.astype(F32) * (h @ w2.astype(F32) + b2.astype(F32)))
    h = (h @ w3.astype(F32)).reshape(L, HY_ORDER, 2, HY_W)
    deltas = jnp.linspace(math.log(1e-2) / HY_FAST_PCT, math.log(1e-2) / HY_SLOW_PCT, HY_W, dtype=F32)
    h = h * jnp.exp(-t * jnp.abs(deltas))[:, None, None, :]
    h = h / (jnp.sum(jnp.abs(h), axis=(0, 2), keepdims=True) + EPS)
    fwd, bwd = h[:, :, 0], h[:, :, 1]
    k = jnp.concatenate([fwd.at[0].add(bwd[0]), jnp.zeros_like(fwd[:1]), bwd[1:][::-1]], axis=0)
    return jnp.fft.rfft(k, axis=0)


def fft_long_conv(u, k_spec, bias):
    L = u.shape[1]
    y = jnp.fft.irfft(jnp.fft.rfft(u, n=2 * L, axis=1) * k_spec[None], n=2 * L, axis=1)[:, :L]
    return y + u * bias


def hyena_mixer(u, conv_w, conv_b, w1, b1, f1, w2, b2, f2, w3, bias):
    L = u.shape[1]
    parts = jnp.split(dwconv(u, conv_w, conv_b, 1).astype(F32), HY_ORDER + 1, axis=-1)
    k_spec = hyena_filter_spectra(L, w1, b1, f1, w2, b2, f2, w3)
    z = parts[0]
    for n in range(HY_ORDER):
        z = parts[n + 1] * fft_long_conv(z, k_spec[:, n], bias[n].astype(F32))
    return z.astype(u.dtype)


def swiglu(x, w_gate, w_up, w_down):
    return (jax.nn.silu(x @ w_gate) * (x @ w_up)) @ w_down


def moe_swiglu(x, w_router, w_gate, w_up, w_down):
    B, L, D = x.shape
    xt = x.reshape(B * L, D)
    probs = jax.nn.softmax((xt @ w_router).astype(F32), axis=-1)
    top_p, top_i = lax.top_k(probs, TOP_K)
    top_p = top_p / jnp.sum(top_p, axis=-1, keepdims=True)
    combine = jnp.einsum('nk,nke->ne', top_p, jax.nn.one_hot(top_i, N_EXPERTS, dtype=F32)).astype(x.dtype)
    y = jnp.zeros_like(xt)
    for e in range(N_EXPERTS):
        y = y + combine[:, e:e + 1] * swiglu(xt, w_gate[e], w_up[e], w_down[e])
    return y.reshape(B, L, D)


def in_split_points():
    return [int(s) for s in np.cumsum(IN_SIZES)[:-1]]


def trunk_layer(x, mod, p, ffn_w, is_moe, ctx):
    B, L, _ = x.shape
    shift1, scale1, gate1, shift2, scale2, gate2 = (mod[:, i] for i in range(6))
    xm = rmsnorm(x, p['norm1']) * (1 + scale1) + shift1
    u_lru, q_c, c_kv, k_r, u_hy, g_logit = jnp.split(xm @ p['w_in'], in_split_points(), axis=-1)
    rope = None if ctx is None else axial_rope_angles(L)
    c_kv = rmsnorm(c_kv, p['mla_kv_norm'])
    q = mla_queries(q_c, p['mla_q_norm'], p['mla_w_uq'], p['mla_q_qknorm'], rope)
    k, v = mla_keys_values(c_kv, k_r, p['mla_w_uk'], p['mla_w_uv'], p['mla_k_qknorm'], rope)
    if ctx is not None:
        k_c, v_c = mla_keys_values(ctx[0], ctx[1], p['mla_w_uk'], p['mla_w_uv'], p['mla_k_qknorm'], None)
        k = jnp.concatenate([k, k_c], axis=1)
        v = jnp.concatenate([v, v_c], axis=1)
    y_mla = block_attention(q, k, v)
    lru_args = (p['lru_conv_w'], p['lru_conv_b'], p['lru_w_gate'], p['lru_b_gate'], p['lru_lambda'])
    if ctx is None:
        y_lru, lru_state = rglru_mixer(u_lru, *lru_args, None)
    else:
        y_lru = rglru_mixer(u_lru, *lru_args, ctx[2])
    y_hy = hyena_mixer(u_hy, p['hy_conv_w'], p['hy_conv_b'], p['hy_w1'], p['hy_b1'], p['hy_freq1'],
                       p['hy_w2'], p['hy_b2'], p['hy_freq2'], p['hy_w3'], p['hy_bias'])
    gates = jax.nn.sigmoid(g_logit).reshape(B, L, N_BRANCH, D_MODEL)
    merged = (gates[:, :, 0] * (y_lru @ p['w_lru_out'])
              + gates[:, :, 1] * (y_mla @ p['w_mla_out'])
              + gates[:, :, 2] * (y_hy @ p['w_hy_out']))
    x = x + gate1 * (merged @ p['w_out'])
    xm2 = rmsnorm(x, p['norm2']) * (1 + scale2) + shift2
    y_ffn = moe_swiglu(xm2, *ffn_w) if is_moe else swiglu(xm2, *ffn_w)
    x = x + gate2 * y_ffn
    if ctx is None:
        return x, (c_kv, k_r, lru_state)
    return x


def setup_inputs(seed: int = 0) -> dict:
    key = jax.random.key(seed)
    keys = iter(jax.random.split(key, 64))

    def nrm(shape, scale):
        return scale * jax.random.normal(next(keys), shape, F32)

    def gain(shape):
        return 1.0 + nrm(shape, 0.05)

    n_dense, n_moe = (DEPTH + 1) // 2, DEPTH // 2
    a_init = jax.random.uniform(next(keys), (DEPTH, 2, LRU_W), F32, 0.9, 0.999)
    return {
        'x_prompt': nrm((BATCH, SEQ, D_MODEL), 1.0),
        'x_sample': nrm((DEC_BATCH, DEC_SEQ, D_MODEL), 1.0),
        'cache_ckv': nrm((DEC_BATCH, DEPTH, PAST_LEN, KV_RANK), 1.0),
        'cache_krope': nrm((DEC_BATCH, DEPTH, PAST_LEN, D_ROPE), 1.0),
        'state_lru': nrm((DEC_BATCH, DEPTH, 2, LRU_W), 0.5),
        'c': nrm((DEC_BATCH, D_MODEL), 1.0),
        'c_ctx': nrm((D_MODEL,), 1.0),
        'norm1': gain((DEPTH, D_MODEL)),
        'norm2': gain((DEPTH, D_MODEL)),
        'w_ada': nrm((DEPTH, D_MODEL, 6 * D_MODEL), 0.5 * D_MODEL ** -0.5),
        'b_ada': nrm((DEPTH, 6 * D_MODEL), 0.02),
        'w_in': nrm((DEPTH, D_MODEL, IN_W), D_MODEL ** -0.5),
        'mla_q_norm': gain((DEPTH, Q_RANK)),
        'mla_kv_norm': gain((DEPTH, KV_RANK)),
        'mla_w_uq': nrm((DEPTH, Q_RANK, N_HEADS, D_QK), Q_RANK ** -0.5),
        'mla_w_uk': nrm((DEPTH, KV_RANK, N_HEADS, D_NOPE), KV_RANK ** -0.5),
        'mla_w_uv': nrm((DEPTH, KV_RANK, N_HEADS, D_V), KV_RANK ** -0.5),
        'mla_q_qknorm': gain((DEPTH, D_QK)),
        'mla_k_qknorm': gain((DEPTH, D_QK)),
        'lru_conv_w': nrm((DEPTH, LRU_CONV, LRU_W), LRU_CONV ** -0.5),
        'lru_conv_b': nrm((DEPTH, LRU_W), 0.02),
        'lru_w_gate': nrm((DEPTH, 2, 2, LRU_BLOCKS, LRU_BLK, LRU_BLK), LRU_BLK ** -0.5),
        'lru_b_gate': nrm((DEPTH, 2, 2, LRU_W), 0.1),
        'lru_lambda': jnp.log(a_init) - jnp.log1p(-a_init),
        'hy_conv_w': nrm((DEPTH, HY_CONV, (HY_ORDER + 1) * HY_W), HY_CONV ** -0.5),
        'hy_conv_b': nrm((DEPTH, (HY_ORDER + 1) * HY_W), 0.02),
        'hy_w1': nrm((DEPTH, HY_EMB, HY_HID), HY_EMB ** -0.5),
        'hy_b1': nrm((DEPTH, HY_HID), 0.1),
        'hy_freq1': gain((DEPTH, HY_HID)),
        'hy_w2': nrm((DEPTH, HY_HID, HY_HID), HY_HID ** -0.5),
        'hy_b2': nrm((DEPTH, HY_HID), 0.1),
        'hy_freq2': gain((DEPTH, HY_HID)),
        'hy_w3': nrm((DEPTH, HY_HID, HY_ORDER * 2 * HY_W), HY_HID ** -0.5),
        'hy_bias': nrm((DEPTH, HY_ORDER, HY_W), 0.5),
        'w_lru_out': nrm((DEPTH, LRU_W, D_MODEL), LRU_W ** -0.5),
        'w_mla_out': nrm((DEPTH, N_HEADS * D_V, D_MODEL), (N_HEADS * D_V) ** -0.5),
        'w_hy_out': nrm((DEPTH, HY_W, D_MODEL), HY_W ** -0.5),
        'w_out': nrm((DEPTH, D_MODEL, D_MODEL), D_MODEL ** -0.5),
        'ffn_w_gate': nrm((n_dense, D_MODEL, D_FF), D_MODEL ** -0.5),
        'ffn_w_up': nrm((n_dense, D_MODEL, D_FF), D_MODEL ** -0.5),
        'ffn_w_down': nrm((n_dense, D_FF, D_MODEL), D_FF ** -0.5),
        'moe_w_router': nrm((n_moe, D_MODEL, N_EXPERTS), D_MODEL ** -0.5),
        'moe_w_gate': nrm((n_moe, N_EXPERTS, D_MODEL, D_FF_E), D_MODEL ** -0.5),
        'moe_w_up': nrm((n_moe, N_EXPERTS, D_MODEL, D_FF_E), D_MODEL ** -0.5),
        'moe_w_down': nrm((n_moe, N_EXPERTS, D_FF_E, D_MODEL), D_FF_E ** -0.5),
    }


def reference(x_prompt, x_sample, cache_ckv, cache_krope, state_lru, c, c_ctx,
              norm1, norm2, w_ada, b_ada, w_in,
              mla_q_norm, mla_kv_norm, mla_w_uq, mla_w_uk, mla_w_uv, mla_q_qknorm, mla_k_qknorm,
              lru_conv_w, lru_conv_b, lru_w_gate, lru_b_gate, lru_lambda,
              hy_conv_w, hy_conv_b, hy_w1, hy_b1, hy_freq1, hy_w2, hy_b2, hy_freq2, hy_w3, hy_bias,
              w_lru_out, w_mla_out, w_hy_out, w_out,
              ffn_w_gate, ffn_w_up, ffn_w_down,
              moe_w_router, moe_w_gate, moe_w_up, moe_w_down):
    stacks = dict(norm1=norm1, norm2=norm2, w_in=w_in,
                  mla_q_norm=mla_q_norm, mla_kv_norm=mla_kv_norm, mla_w_uq=mla_w_uq, mla_w_uk=mla_w_uk,
                  mla_w_uv=mla_w_uv, mla_q_qknorm=mla_q_qknorm, mla_k_qknorm=mla_k_qknorm,
                  lru_conv_w=lru_conv_w, lru_conv_b=lru_conv_b, lru_w_gate=lru_w_gate,
                  lru_b_gate=lru_b_gate, lru_lambda=lru_lambda,
                  hy_conv_w=hy_conv_w, hy_conv_b=hy_conv_b, hy_w1=hy_w1, hy_b1=hy_b1, hy_freq1=hy_freq1,
                  hy_w2=hy_w2, hy_b2=hy_b2, hy_freq2=hy_freq2, hy_w3=hy_w3, hy_bias=hy_bias,
                  w_lru_out=w_lru_out, w_mla_out=w_mla_out, w_hy_out=w_hy_out, w_out=w_out)

    def layer_weights(l):
        return {name: w[l] for name, w in stacks.items()}

    def ffn_weights(l):
        j = l // 2
        if l % 2 == 0:
            return False, (ffn_w_gate[j], ffn_w_up[j], ffn_w_down[j])
        return True, (moe_w_router[j], moe_w_gate[j], moe_w_up[j], moe_w_down[j])

    h = x_prompt
    ckv_list, kr_list, st_list = [], [], []
    for l in range(DEPTH):
        mod = adaln_terms(c_ctx[None, :], w_ada[l], b_ada[l])
        is_moe, fw = ffn_weights(l)
        h, (ckv_l, kr_l, st_l) = trunk_layer(h, mod, layer_weights(l), fw, is_moe, None)
        ckv_list.append(ckv_l)
        kr_list.append(kr_l)
        st_list.append(st_l)
    y_prompt = h
    new_cache_ckv = jnp.stack(ckv_list, axis=1)
    new_cache_krope = jnp.stack(kr_list, axis=1)
    new_state_lru = jnp.stack(st_list, axis=1)

    g = x_sample
    for l in range(DEPTH):
        mod = adaln_terms(c, w_ada[l], b_ada[l])
        is_moe, fw = ffn_weights(l)
        g = trunk_layer(g, mod, layer_weights(l), fw, is_moe,
                        (cache_ckv[:, l], cache_krope[:, l], state_lru[:, l]))
    y_sample = g
    return (y_prompt, y_sample, new_cache_ckv, new_cache_krope, new_state_lru)
```

```python
import functools
import math

import jax
import jax.numpy as jnp
from jax import lax
from jax.experimental import pallas as pl
from jax.experimental.pallas import tpu as pltpu

F32 = jnp.float32
BF16 = jnp.bfloat16

D_MODEL = 1024
BATCH = 32
SEQ = 256
DEPTH = 2
DEC_BATCH = 2
DEC_SEQ = 4096
PAST_LEN = 512
GRID_W = 64
EPS = 1e-6
LRU_W = 512
LRU_BLOCKS = 8
LRU_C = 8.0
N_HEADS = 8
D_NOPE = 64
D_ROPE = 32
D_QK = D_NOPE + D_ROPE
D_V = 64
Q_RANK = 256
KV_RANK = 128
ROPE_BASE = 10000.0
HY_W = 512
HY_ORDER = 2
HY_EMB = 33
HY_HID = 64
HY_FAST_PCT = 0.3
HY_SLOW_PCT = 1.5
D_FF = 2816
N_EXPERTS = 8
D_FF_E = 1408

LANES = 128
SUBLANES = 8
VMEM_LIMIT = 56 * 1024 * 1024

M_CTX = BATCH * SEQ
M_LAT = DEC_BATCH * DEC_SEQ
M_TOK = M_CTX + M_LAT
TM = 256
TM_FFN = 512
HEAD_PAD = LANES
QK_SCALE = 1.0 / math.sqrt(D_QK)

FFT_N1 = 64
FFT_N2 = 128
FFT_K1_BLOCK = 4


def _cparams(sem, vmem=VMEM_LIMIT):
    return pltpu.CompilerParams(dimension_semantics=sem, vmem_limit_bytes=vmem)


def _dot(a, b):
    return jnp.dot(a, b, preferred_element_type=F32)


def _rms(x, g):
    ms = jnp.mean(x * x, axis=-1, keepdims=True)
    return x * lax.rsqrt(ms + EPS) * g


def _sigmoid(x):
    return 1.0 / (1.0 + jnp.exp(-x))


def _ada_kernel(c_ref, w_ref, b_ref, o_ref):
    c = c_ref[...]
    s = (c * _sigmoid(c)).astype(BF16)
    o_ref[...] = _dot(s, w_ref[...].astype(BF16)) + b_ref[...]


def _adaln(cond, w_ada, b_ada):
    tn = 1024
    n6 = 6 * D_MODEL
    return pl.pallas_call(
        _ada_kernel,
        grid=(DEPTH, n6 // tn),
        in_specs=[
            pl.BlockSpec((SUBLANES, D_MODEL), lambda l, j: (0, 0)),
            pl.BlockSpec((None, D_MODEL, tn), lambda l, j: (l, 0, j)),
            pl.BlockSpec((None, 1, tn), lambda l, j: (l, 0, j)),
        ],
        out_specs=pl.BlockSpec((None, SUBLANES, tn), lambda l, j: (l, 0, j)),
        out_shape=jax.ShapeDtypeStruct((DEPTH, SUBLANES, n6), F32),
        compiler_params=_cparams(("arbitrary", "arbitrary")),
        name="adaln",
    )(cond, w_ada, b_ada.reshape(DEPTH, 1, n6))


def _mod_row(i, tm):
    nctx = M_CTX // tm
    per = DEC_SEQ // tm
    return jnp.where(i < nctx, 0, 1 + (i - nctx) // per)


def _rope_blk(i, tm):
    nctx = M_CTX // tm
    per = DEC_SEQ // tm
    return jnp.where(i < nctx, 0, 1 + (i - nctx) % per)


def _swap_pairs(x, even):
    return jnp.where(even, pltpu.roll(x, LANES - 1, 1), pltpu.roll(x, 1, 1))


def _finish_heads(raw, extra, gain_ref, cos, sin, out_ref, scale):
    tm = raw.shape[0]
    lane = lax.broadcasted_iota(jnp.int32, (tm, LANES), 1)
    even = (lane & 1) == 0
    for h in range(N_HEADS):
        sl = slice(HEAD_PAD * h, HEAD_PAD * (h + 1))
        xh = raw[:, sl]
        if extra is not None:
            xh = xh + extra
        ms = jnp.sum(xh * xh, axis=-1, keepdims=True) * (1.0 / D_QK)
        xh = xh * lax.rsqrt(ms + EPS) * gain_ref[:, sl]
        if cos is not None:
            xh = xh * cos + _swap_pairs(xh, even) * sin
        if scale != 1.0:
            xh = xh * scale
        out_ref[:, sl] = xh.astype(BF16)


def _stage1_kernel(x_ref, mod_ref, g1_ref, w1_ref, gkv_ref, gqn_ref, wuq_ref, gq_ref, wuk_ref, gk_ref,
                   wuv_ref, cos_ref, sin_ref,
                   ulru_ref, uhy_ref, ckv_ref, krb_ref, q_ref, k_ref, v_ref):
    x = x_ref[...]
    xm = _rms(x, g1_ref[...]) * (1.0 + mod_ref[1:2, :]) + mod_ref[0:1, :]
    xb = xm.astype(BF16)
    ulru_ref[...] = _dot(xb, w1_ref[:, 0:512])
    qc = _dot(xb, w1_ref[:, 512:768])
    ckv = _dot(xb, w1_ref[:, 768:896])
    krb = _dot(xb, w1_ref[:, 896:1024])
    uhy_ref[...] = _dot(xb, w1_ref[:, 1024:2560])
    ckvn = _rms(ckv, gkv_ref[...])
    ckv_ref[...] = ckvn
    krb_ref[...] = krb
    qn = _rms(qc, gqn_ref[...]).astype(BF16)
    qraw = _dot(qn, wuq_ref[...])
    cb = ckvn.astype(BF16)
    kraw = _dot(cb, wuk_ref[...])
    v_ref[...] = _dot(cb, wuv_ref[...]).astype(BF16)
    cos = cos_ref[...]
    sin = sin_ref[...]
    _finish_heads(qraw, None, gq_ref, cos, sin, q_ref, QK_SCALE)
    _finish_heads(kraw, krb, gk_ref, cos, sin, k_ref, 1.0)


def _stage1(x, modl, g1, w1, gkv, gqn, wuq, gq, wuk, gk, wuv, cos_t, sin_t):
    tm = TM
    full = lambda shape: pl.BlockSpec(shape, lambda i: (0,) * len(shape))
    row = lambda cols: pl.BlockSpec((tm, cols), lambda i: (i, 0))
    hw = N_HEADS * HEAD_PAD
    return pl.pallas_call(
        _stage1_kernel,
        grid=(M_TOK // tm,),
        in_specs=[
            row(D_MODEL),
            pl.BlockSpec((None, SUBLANES, D_MODEL), lambda i: (_mod_row(i, tm), 0, 0)),
            full((1, D_MODEL)),
            full((D_MODEL, 2560)),
            full((1, KV_RANK)),
            full((1, Q_RANK)),
            full((Q_RANK, hw)),
            full((1, hw)),
            full((KV_RANK, hw)),
            full((1, hw)),
            full((KV_RANK, N_HEADS * D_V)),
            pl.BlockSpec((tm, LANES), lambda i: (_rope_blk(i, tm), 0)),
            pl.BlockSpec((tm, LANES), lambda i: (_rope_blk(i, tm), 0)),
        ],
        out_specs=[row(LRU_W), row(3 * HY_W), row(KV_RANK), row(LANES), row(hw), row(hw), row(N_HEADS * D_V)],
        out_shape=[
            jax.ShapeDtypeStruct((M_TOK, LRU_W), F32),
            jax.ShapeDtypeStruct((M_TOK, 3 * HY_W), F32),
            jax.ShapeDtypeStruct((M_TOK, KV_RANK), F32),
            jax.ShapeDtypeStruct((M_TOK, LANES), F32),
            jax.ShapeDtypeStruct((M_TOK, hw), BF16),
            jax.ShapeDtypeStruct((M_TOK, hw), BF16),
            jax.ShapeDtypeStruct((M_TOK, N_HEADS * D_V), BF16),
        ],
        compiler_params=_cparams(("arbitrary",)),
        name="stage1",
    )(x, modl, g1, w1, gkv, gqn, wuq, gq, wuk, gk, wuv, cos_t, sin_t)


def _kvprep_kernel(ckv_ref, krb_ref, wuk_ref, gk_ref, wuv_ref, k_ref, v_ref):
    cb = ckv_ref[...].astype(BF16)
    kraw = _dot(cb, wuk_ref[...])
    v_ref[...] = _dot(cb, wuv_ref[...]).astype(BF16)
    _finish_heads(kraw, krb_ref[...], gk_ref, None, None, k_ref, 1.0)


def _kvprep(ckv, krb, wuk, gk, wuv):
    rows = ckv.shape[0]
    tm = TM
    hw = N_HEADS * HEAD_PAD
    full = lambda shape: pl.BlockSpec(shape, lambda i: (0,) * len(shape))
    row = lambda cols: pl.BlockSpec((tm, cols), lambda i: (i, 0))
    return pl.pallas_call(
        _kvprep_kernel,
        grid=(rows // tm,),
        in_specs=[row(KV_RANK), row(LANES), full((KV_RANK, hw)), full((1, hw)), full((KV_RANK, N_HEADS * D_V))],
        out_specs=[row(hw), row(N_HEADS * D_V)],
        out_shape=[jax.ShapeDtypeStruct((rows, hw), BF16), jax.ShapeDtypeStruct((rows, N_HEADS * D_V), BF16)],
        compiler_params=_cparams(("arbitrary",)),
        name="kvprep",
    )(ckv, krb, wuk, gk, wuv)


def _attn_kernel(*refs, heads, nseg):
    q_ref = refs[0]
    k_refs = refs[1:1 + nseg]
    v_refs = refs[1 + nseg:1 + 2 * nseg]
    o_ref = refs[1 + 2 * nseg]
    tq = q_ref.shape[0]
    lane = lax.broadcasted_iota(jnp.int32, (tq, LANES), 1)
    low = lane < D_V
    for pair in range(heads // 2):
        outs = []
        for j in range(2):
            h = 2 * pair + j
            sl = slice(HEAD_PAD * h, HEAD_PAD * (h + 1))
            q = q_ref[:, sl]
            s = [lax.dot_general(q, kr[:, sl], (((1,), (1,)), ((), ())), preferred_element_type=F32)
                 for kr in k_refs]
            m = jnp.max(s[0], axis=-1, keepdims=True)
            for si in s[1:]:
                m = jnp.maximum(m, jnp.max(si, axis=-1, keepdims=True))
            acc = None
            den = None
            for si, vr in zip(s, v_refs):
                p = jnp.exp(si - m)
                d = jnp.sum(p, axis=-1, keepdims=True)
                o = _dot(p.astype(BF16), vr[:, LANES * pair:LANES * (pair + 1)])
                acc = o if acc is None else acc + o
                den = d if den is None else den + d
            outs.append(acc / den)
        o_ref[:, LANES * pair:LANES * (pair + 1)] = jnp.where(low, outs[0], outs[1]).astype(BF16)


def _attention_ctx(q, k, v):
    hw = N_HEADS * HEAD_PAD
    vw = N_HEADS * D_V
    return pl.pallas_call(
        functools.partial(_attn_kernel, heads=N_HEADS, nseg=1),
        grid=(BATCH,),
        in_specs=[
            pl.BlockSpec((SEQ, hw), lambda b: (b, 0)),
            pl.BlockSpec((SEQ, hw), lambda b: (b, 0)),
            pl.BlockSpec((SEQ, vw), lambda b: (b, 0)),
        ],
        out_specs=pl.BlockSpec((SEQ, vw), lambda b: (b, 0)),
        out_shape=jax.ShapeDtypeStruct((M_CTX, vw), BF16),
        compiler_params=_cparams(("arbitrary",)),
        name="attn_ctx",
    )(q, k, v)


def _attention_lat(q, k, v, kc, vc):
    tq = 256
    nq = DEC_SEQ // tq
    qoff = M_CTX // tq
    koff = M_CTX // DEC_SEQ
    return pl.pallas_call(
        functools.partial(_attn_kernel, heads=2, nseg=2),
        grid=(DEC_BATCH, N_HEADS // 2, nq),
        in_specs=[
            pl.BlockSpec((tq, 2 * HEAD_PAD), lambda b, p, i: (qoff + b * nq + i, p)),
            pl.BlockSpec((DEC_SEQ, 2 * HEAD_PAD), lambda b, p, i: (koff + b, p)),
            pl.BlockSpec((PAST_LEN, 2 * HEAD_PAD), lambda b, p, i: (b, p)),
            pl.BlockSpec((DEC_SEQ, 2 * D_V), lambda b, p, i: (koff + b, p)),
            pl.BlockSpec((PAST_LEN, 2 * D_V), lambda b, p, i: (b, p)),
        ],
        out_specs=pl.BlockSpec((tq, 2 * D_V), lambda b, p, i: (b * nq + i, p)),
        out_shape=jax.ShapeDtypeStruct((M_LAT, N_HEADS * D_V), BF16),
        compiler_params=_cparams(("arbitrary", "arbitrary", "arbitrary")),
        name="attn_lat",
    )(q, k, kc, v, vc)


def _lru_kernel(*refs, reverse, tc, nchunks):
    if reverse:
        (up_ref, uc_ref, un_ref, hf_ref, cw_ref, cb_ref, wr_ref, wi_ref, br_ref, bi_ref, lam_ref, h0_ref,
         y_ref, st_ref, ext_sc, a_sc, b_sc, h_sc, car_sc) = refs
    else:
        (up_ref, uc_ref, un_ref, cw_ref, cb_ref, wr_ref, wi_ref, br_ref, bi_ref, lam_ref, h0_ref,
         y_ref, st_ref, ext_sc, a_sc, b_sc, h_sc, car_sc) = refs
    c = pl.program_id(1)
    chunk = (nchunks - 1 - c) if reverse else c
    prev = jnp.where(chunk == 0, 0.0, up_ref[...])
    nxt = jnp.where(chunk == nchunks - 1, 0.0, un_ref[...])
    ext_sc[0:SUBLANES, :] = prev
    ext_sc[SUBLANES:SUBLANES + tc, :] = uc_ref[...]
    ext_sc[SUBLANES + tc:2 * SUBLANES + tc, :] = nxt
    xc = cb_ref[...]
    for k in range(4):
        xc = xc + cw_ref[k:k + 1, :] * ext_sc[SUBLANES - 2 + k:SUBLANES - 2 + k + tc, :]
    xb = xc.astype(BF16)
    r = _sigmoid(_dot(xb, wr_ref[...]) + br_ref[...])
    gi = _sigmoid(_dot(xb, wi_ref[...]) + bi_ref[...])
    lam = lam_ref[...]
    logsig = -(jnp.maximum(-lam, 0.0) + jnp.log1p(jnp.exp(-jnp.abs(lam))))
    la = LRU_C * r * logsig
    a = jnp.exp(la)
    a_sc[...] = a
    b_sc[...] = jnp.sqrt(-jnp.tanh(la) * (a * a + 1.0)) * (gi * xc)

    @pl.when(c == 0)
    def _():
        car_sc[...] = h0_ref[...]

    def body(j, h):
        t = (tc - 1 - j) if reverse else j
        h = a_sc[pl.ds(t, 1), :] * h + b_sc[pl.ds(t, 1), :]
        h_sc[pl.ds(t, 1), :] = h
        return h

    h = lax.fori_loop(0, tc, body, car_sc[0:1, :], unroll=8)
    car_sc[0:1, :] = h
    st_ref[...] = jnp.broadcast_to(h, (SUBLANES, LRU_W))
    if reverse:
        y_ref[...] = (hf_ref[...] + h_sc[...]).astype(BF16)
    else:
        y_ref[...] = h_sc[...]


def _lru_dir(u, hf, cw, cb, wr, wi, br, bi, lam, h0, *, reverse, row_off, nseq, seqlen, tc):
    nchunks = seqlen // tc
    hb = M_TOK // SUBLANES

    def chunk_of(c):
        return (nchunks - 1 - c) if reverse else c

    def cur(b, c):
        return ((row_off + b * seqlen) // tc + chunk_of(c), 0)

    def prv(b, c):
        return (jnp.maximum((row_off + b * seqlen + chunk_of(c) * tc) // SUBLANES - 1, 0), 0)

    def nxt(b, c):
        return (jnp.minimum((row_off + b * seqlen + (chunk_of(c) + 1) * tc) // SUBLANES, hb - 1), 0)

    def out_cur(b, c):
        return ((b * seqlen) // tc + chunk_of(c), 0)

    full = lambda shape: pl.BlockSpec(shape, lambda b, c: (0,) * len(shape))
    in_specs = [pl.BlockSpec((SUBLANES, LRU_W), prv), pl.BlockSpec((tc, LRU_W), cur),
                pl.BlockSpec((SUBLANES, LRU_W), nxt)]
    args = [u, u, u]
    if reverse:
        in_specs.append(pl.BlockSpec((tc, LRU_W), out_cur))
        args.append(hf)
    in_specs += [full((SUBLANES, LRU_W)), full((1, LRU_W)), full((LRU_W, LRU_W)), full((LRU_W, LRU_W)),
                 full((1, LRU_W)), full((1, LRU_W)), full((1, LRU_W)),
                 pl.BlockSpec((None, SUBLANES, LRU_W), lambda b, c: (b, 0, 0))]
    args += [cw, cb, wr, wi, br, bi, lam, h0]
    return pl.pallas_call(
        functools.partial(_lru_kernel, reverse=reverse, tc=tc, nchunks=nchunks),
        grid=(nseq, nchunks),
        in_specs=in_specs,
        out_specs=[pl.BlockSpec((tc, LRU_W), out_cur),
                   pl.BlockSpec((None, SUBLANES, LRU_W), lambda b, c: (b, 0, 0))],
        out_shape=[jax.ShapeDtypeStruct((nseq * seqlen, LRU_W), BF16 if reverse else F32),
                   jax.ShapeDtypeStruct((nseq, SUBLANES, LRU_W), F32)],
        scratch_shapes=[pltpu.VMEM((tc + 2 * SUBLANES, LRU_W), F32), pltpu.VMEM((tc, LRU_W), F32),
                        pltpu.VMEM((tc, LRU_W), F32), pltpu.VMEM((tc, LRU_W), F32),
                        pltpu.VMEM((SUBLANES, LRU_W), F32)],
        compiler_params=_cparams(("arbitrary", "arbitrary")),
        name="lru_bwd" if reverse else "lru_fwd",
    )(*args)


def _lru_mixer(u, p, h0, *, row_off, nseq, seqlen, tc):
    kw = dict(row_off=row_off, nseq=nseq, seqlen=seqlen, tc=tc)
    hf, stf = _lru_dir(u, None, p["cw"], p["cb"], p["wr"][0], p["wi"][0], p["br"][0], p["bi"][0], p["lam"][0],
                       h0[0], reverse=False, **kw)
    y, stb = _lru_dir(u, hf, p["cw"], p["cb"], p["wr"][1], p["wi"][1], p["br"][1], p["bi"][1], p["lam"][1],
                      h0[1], reverse=True, **kw)
    return y, stf[:, 0, :], stb[:, 0, :]


def _shortconv_kernel(up_ref, uc_ref, un_ref, cw_ref, cb_ref, v_ref, x1_ref, x2_ref, ext_sc, *, tc, nchunks):
    c = pl.program_id(1)
    prev = jnp.where(c == 0, 0.0, up_ref[...])
    nxt = jnp.where(c == nchunks - 1, 0.0, un_ref[...])
    ext_sc[0:SUBLANES, :] = prev
    ext_sc[SUBLANES:SUBLANES + tc, :] = uc_ref[...]
    ext_sc[SUBLANES + tc:2 * SUBLANES + tc, :] = nxt
    for part, o_ref in enumerate((v_ref, x1_ref, x2_ref)):
        cs = slice(HY_W * part, HY_W * (part + 1))
        acc = cb_ref[:, cs]
        for k in range(3):
            acc = acc + cw_ref[k:k + 1, cs] * ext_sc[SUBLANES - 1 + k:SUBLANES - 1 + k + tc, cs]
        o_ref[...] = acc


def _shortconv(u, cw, cb, *, row_off, nseq, seqlen, tc):
    nchunks = seqlen // tc
    w = 3 * HY_W
    hb = M_TOK // SUBLANES
    cur = lambda b, c: ((row_off + b * seqlen) // tc + c, 0)
    prv = lambda b, c: (jnp.maximum((row_off + b * seqlen + c * tc) // SUBLANES - 1, 0), 0)
    nxt = lambda b, c: (jnp.minimum((row_off + b * seqlen + (c + 1) * tc) // SUBLANES, hb - 1), 0)
    out = lambda b, c: ((b * seqlen) // tc + c, 0)
    full = lambda shape: pl.BlockSpec(shape, lambda b, c: (0,) * len(shape))
    rows = nseq * seqlen
    return pl.pallas_call(
        functools.partial(_shortconv_kernel, tc=tc, nchunks=nchunks),
        grid=(nseq, nchunks),
        in_specs=[pl.BlockSpec((SUBLANES, w), prv), pl.BlockSpec((tc, w), cur), pl.BlockSpec((SUBLANES, w), nxt),
                  full((SUBLANES, w)), full((1, w))],
        out_specs=[pl.BlockSpec((tc, HY_W), out)] * 3,
        out_shape=[jax.ShapeDtypeStruct((rows, HY_W), F32)] * 3,
        scratch_shapes=[pltpu.VMEM((tc + 2 * SUBLANES, w), F32)],
        compiler_params=_cparams(("arbitrary", "arbitrary")),
        name="hy_shortconv",
    )(u, u, u, cw, cb)


def _hyfilt_kernel(z_ref, t_ref, w1_ref, b1_ref, f1_ref, w2_ref, b2_ref, f2_ref, w3_ref, ad_ref, h_ref, s_ref):
    i = pl.program_id(0)
    z = z_ref[...].astype(BF16)
    h = jnp.sin(f1_ref[...] * (_dot(z, w1_ref[...]) + b1_ref[...]))
    h = jnp.sin(f2_ref[...] * (_dot(h.astype(BF16), w2_ref[...]) + b2_ref[...]))
    h = _dot(h.astype(BF16), w3_ref[...])
    t = t_ref[...]
    ncol = h.shape[1] // LANES
    win = jnp.concatenate([jnp.exp(-t * ad_ref[:, LANES * j:LANES * (j + 1)]) for j in range(ncol)], axis=1)
    h = h * win
    h_ref[...] = h

    @pl.when(i == 0)
    def _():
        s_ref[...] = jnp.zeros_like(s_ref)

    s_ref[0:1, :] = s_ref[0:1, :] + jnp.sum(jnp.abs(h), axis=0, keepdims=True)


def _hyfilt(feats, tcol, w1, b1, f1, w2, b2, f2, w3, absdelta):
    L = feats.shape[0]
    tl = min(L, 512)
    wcols = HY_ORDER * 2 * HY_W
    full = lambda shape: pl.BlockSpec(shape, lambda i: (0,) * len(shape))
    return pl.pallas_call(
        _hyfilt_kernel,
        grid=(L // tl,),
        in_specs=[pl.BlockSpec((tl, LANES), lambda i: (i, 0)), pl.BlockSpec((tl, LANES), lambda i: (i, 0)),
                  full((LANES, HY_HID)), full((1, HY_HID)), full((1, HY_HID)),
                  full((HY_HID, HY_HID)), full((1, HY_HID)), full((1, HY_HID)),
                  full((HY_HID, wcols)), full((1, wcols))],
        out_specs=[pl.BlockSpec((tl, wcols), lambda i: (i, 0)), full((SUBLANES, wcols))],
        out_shape=[jax.ShapeDtypeStruct((L, wcols), F32), jax.ShapeDtypeStruct((SUBLANES, wcols), F32)],
        compiler_params=_cparams(("arbitrary",)),
        name="hy_filter",
    )(feats, tcol, w1, b1, f1, w2, b2, f2, w3, absdelta)


def _combine_spectrum(zr, zi, s_ref, hr_out, hi_out):
    for o in range(HY_ORDER):
        f = slice(2 * HY_W * o, 2 * HY_W * o + HY_W)
        b = slice(2 * HY_W * o + HY_W, 2 * HY_W * (o + 1))
        den = s_ref[0:1, f] + s_ref[0:1, b] + EPS
        hr_out(o, (zr[:, f] + zr[:, b]) / den)
        hi_out(o, (zi[:, f] - zi[:, b]) / den)


def _ctx_spec_kernel(f_ref, h_ref, s_ref, o_ref):
    n = f_ref.shape[0] // 2
    z = _dot(f_ref[...], h_ref[...].astype(BF16))
    zr, zi = z[:n], z[n:]

    def put_r(o, val):
        o_ref[0, :, HY_W * o:HY_W * (o + 1)] = val

    def put_i(o, val):
        o_ref[1, :, HY_W * o:HY_W * (o + 1)] = val

    _combine_spectrum(zr, zi, s_ref, put_r, put_i)


def _ctx_spectrum(fmat, hdec, s):
    n = fmat.shape[0] // 2
    return pl.pallas_call(
        _ctx_spec_kernel,
        out_shape=jax.ShapeDtypeStruct((2, n, HY_ORDER * HY_W), F32),
        compiler_params=pltpu.CompilerParams(vmem_limit_bytes=VMEM_LIMIT),
        name="hy_ctx_spectrum",
    )(fmat, hdec, s)


def _ctx_conv_kernel(z_ref, x_ref, f_ref, fi_ref, h_ref, bias_ref, o_ref, *, nb, seqlen):
    n = f_ref.shape[0] // 2
    hr = h_ref[0]
    hi = h_ref[1]
    for b in range(nb):
        rs = slice(seqlen * b, seqlen * (b + 1))
        zt = z_ref[rs, :]
        zf = _dot(f_ref[...], zt.astype(BF16))
        zr, zi = zf[:n], zf[n:]
        y = jnp.concatenate([zr * hr - zi * hi, zr * hi + zi * hr], axis=0).astype(BF16)
        conv = _dot(fi_ref[...], y)
        o_ref[rs, :] = (x_ref[rs, :] * (conv + zt * bias_ref[...])).astype(o_ref.dtype)


def _ctx_conv(z, xg, fmat, finv, hspec, bias, order, out_dtype):
    nb = 4
    n = fmat.shape[0] // 2
    rows = nb * SEQ
    return pl.pallas_call(
        functools.partial(_ctx_conv_kernel, nb=nb, seqlen=SEQ),
        grid=(BATCH // nb,),
        in_specs=[pl.BlockSpec((rows, HY_W), lambda i: (i, 0)), pl.BlockSpec((rows, HY_W), lambda i: (i, 0)),
                  pl.BlockSpec(fmat.shape, lambda i: (0, 0)), pl.BlockSpec(finv.shape, lambda i: (0, 0)),
                  pl.BlockSpec((2, n, HY_W), lambda i: (0, 0, order)),
                  pl.BlockSpec((1, HY_W), lambda i: (0, 0))],
        out_specs=pl.BlockSpec((rows, HY_W), lambda i: (i, 0)),
        out_shape=jax.ShapeDtypeStruct((M_CTX, HY_W), out_dtype),
        compiler_params=_cparams(("arbitrary",)),
        name="hy_ctx_conv",
    )(z, xg, fmat, finv, hspec, bias)


def _dft_a_kernel(w_ref, x_ref, o_ref):
    o_ref[...] = _dot(w_ref[...], x_ref[...].astype(BF16)).astype(o_ref.dtype)


def _dft_a(w, x, out_dtype=BF16):
    nb, kk, cols = x.shape
    r = w.shape[0]
    tcw = 8192
    return pl.pallas_call(
        _dft_a_kernel,
        grid=(nb, cols // tcw),
        in_specs=[pl.BlockSpec((r, kk), lambda b, j: (0, 0)), pl.BlockSpec((None, kk, tcw), lambda b, j: (b, 0, j))],
        out_specs=pl.BlockSpec((None, r, tcw), lambda b, j: (b, 0, j)),
        out_shape=jax.ShapeDtypeStruct((nb, r, cols), out_dtype),
        compiler_params=_cparams(("arbitrary", "arbitrary")),
        name="hy_dft_a",
    )(w, x)


def _lat_spec_kernel(a_ref, g_ref, s_ref, o_ref):
    n2 = FFT_N2
    for kk in range(FFT_K1_BLOCK):
        a = jnp.concatenate([a_ref[0, kk], a_ref[1, kk]], axis=0)
        z = _dot(g_ref[kk], a)
        zr, zi = z[:n2], z[n2:]

        def put_r(o, val, kk=kk):
            o_ref[0, kk, :, HY_W * o:HY_W * (o + 1)] = val

        def put_i(o, val, kk=kk):
            o_ref[1, kk, :, HY_W * o:HY_W * (o + 1)] = val

        _combine_spectrum(zr, zi, s_ref, put_r, put_i)


def _lat_spectrum(a4, gtab, s):
    kb = FFT_K1_BLOCK
    wc = HY_ORDER * 2 * HY_W
    return pl.pallas_call(
        _lat_spec_kernel,
        grid=(FFT_N1 // kb,),
        in_specs=[pl.BlockSpec((2, kb, FFT_N2, wc), lambda i: (0, i, 0, 0)),
                  pl.BlockSpec((kb, 2 * FFT_N2, 2 * FFT_N2), lambda i: (i, 0, 0)),
                  pl.BlockSpec((SUBLANES, wc), lambda i: (0, 0))],
        out_specs=pl.BlockSpec((2, kb, FFT_N2, HY_ORDER * HY_W), lambda i: (0, i, 0, 0)),
        out_shape=jax.ShapeDtypeStruct((2, FFT_N1, FFT_N2, HY_ORDER * HY_W), F32),
        compiler_params=_cparams(("arbitrary",)),
        name="hy_lat_spectrum",
    )(a4, gtab, s)


def _bhb_kernel(a_ref, g_ref, gi_ref, h_ref, o_ref):
    n2 = FFT_N2
    for kk in range(FFT_K1_BLOCK):
        a = jnp.concatenate([a_ref[0, kk], a_ref[1, kk]], axis=0)
        z = _dot(g_ref[kk], a)
        zr, zi = z[:n2], z[n2:]
        hr = h_ref[0, kk]
        hi = h_ref[1, kk]
        y = jnp.concatenate([zr * hr - zi * hi, zr * hi + zi * hr], axis=0).astype(BF16)
        bp = _dot(gi_ref[kk], y)
        o_ref[0, kk] = bp[:n2].astype(BF16)
        o_ref[1, kk] = bp[n2:].astype(BF16)


def _bhb(a5, gtab, gitab, hspec, order):
    kb = FFT_K1_BLOCK
    nb = a5.shape[0]
    blk = (None, 2, kb, FFT_N2, HY_W)
    return pl.pallas_call(
        _bhb_kernel,
        grid=(FFT_N1 // kb, nb),
        in_specs=[pl.BlockSpec(blk, lambda i, b: (b, 0, i, 0, 0)),
                  pl.BlockSpec((kb, 2 * FFT_N2, 2 * FFT_N2), lambda i, b: (i, 0, 0)),
                  pl.BlockSpec((kb, 2 * FFT_N2, 2 * FFT_N2), lambda i, b: (i, 0, 0)),
                  pl.BlockSpec((2, kb, FFT_N2, HY_W), lambda i, b: (0, i, 0, order))],
        out_specs=pl.BlockSpec(blk, lambda i, b: (b, 0, i, 0, 0)),
        out_shape=jax.ShapeDtypeStruct(a5.shape, BF16),
        compiler_params=_cparams(("arbitrary", "arbitrary")),
        name="hy_bhb",
    )(a5, gtab, gitab, hspec)


def _dft_ainv_kernel(w_ref, bp_ref, z_ref, x_ref, bias_ref, o_ref):
    conv = _dot(w_ref[...], bp_ref[...])
    o_ref[...] = (x_ref[...] * (conv + z_ref[...] * bias_ref[...])).astype(o_ref.dtype)


def _dft_ainv_gate(w, bp, z, xg, bias_t, out_dtype):
    nb, r2, cols = bp.shape
    r = w.shape[0]
    tcw = 8192
    return pl.pallas_call(
        _dft_ainv_kernel,
        grid=(nb, cols // tcw),
        in_specs=[pl.BlockSpec((r, r2), lambda b, j: (0, 0)),
                  pl.BlockSpec((None, r2, tcw), lambda b, j: (b, 0, j)),
                  pl.BlockSpec((None, r, tcw), lambda b, j: (b, 0, j)),
                  pl.BlockSpec((None, r, tcw), lambda b, j: (b, 0, j)),
                  pl.BlockSpec((1, tcw), lambda b, j: (0, j))],
        out_specs=pl.BlockSpec((None, r, tcw), lambda b, j: (b, 0, j)),
        out_shape=jax.ShapeDtypeStruct((nb, r, cols), out_dtype),
        compiler_params=_cparams(("arbitrary", "arbitrary")),
        name="hy_dft_ainv",
    )(w, bp, z, xg, bias_t)


def _angle(m, n):
    return (m % n).astype(F32) * (2.0 * math.pi / n)


def _ctx_tables():
    n = 2 * SEQ
    k = jnp.arange(n, dtype=jnp.int32)[:, None]
    t = jnp.arange(SEQ, dtype=jnp.int32)[None, :]
    th = _angle(k * t, n)
    fmat = jnp.concatenate([jnp.cos(th), -jnp.sin(th)], axis=0)
    finv = jnp.concatenate([jnp.cos(th).T, -jnp.sin(th).T], axis=1) / n
    return fmat.astype(BF16), finv.astype(BF16)


def _lat_tables():
    n1, n2 = FFT_N1, FFT_N2
    n = n1 * n2
    k1 = jnp.arange(n1, dtype=jnp.int32)
    th1 = _angle(k1[:, None] * jnp.arange(n1 // 2, dtype=jnp.int32)[None, :], n1)
    wa = jnp.concatenate([jnp.cos(th1), -jnp.sin(th1)], axis=0)
    wainv = jnp.concatenate([jnp.cos(th1).T, -jnp.sin(th1).T], axis=1) / n
    k2 = jnp.arange(n2, dtype=jnp.int32)
    nn2 = jnp.arange(n2, dtype=jnp.int32)
    kfull = k1[:, None, None] + n1 * k2[None, :, None]
    th = _angle(kfull * nn2[None, None, :], n)
    gr, gi = jnp.cos(th), -jnp.sin(th)
    g = jnp.concatenate([jnp.concatenate([gr, -gi], axis=2), jnp.concatenate([gi, gr], axis=2)], axis=1)
    thT = jnp.swapaxes(th, 1, 2)
    ir, ii = jnp.cos(thT), jnp.sin(thT)
    ginv = jnp.concatenate([jnp.concatenate([ir, -ii], axis=2), jnp.concatenate([ii, ir], axis=2)], axis=1)
    return wa.astype(BF16), wainv.astype(BF16), g.astype(BF16), ginv.astype(BF16)


def _filter_features(L):
    t = jnp.linspace(0.0, 1.0, L, dtype=F32)[:, None]
    bands = (HY_EMB - 1) // 2
    w = (2.0 * math.pi / L) * jnp.arange(L, dtype=F32)[:, None]
    f = jnp.linspace(1e-4, bands - 1, bands, dtype=F32)[None, :]
    z = jnp.concatenate([t, jnp.cos(f * w), -jnp.sin(f * w)], axis=-1)
    z = jnp.pad(z, ((0, 0), (0, LANES - HY_EMB)))
    return z, jnp.broadcast_to(t, (L, LANES))


def _hyena_filter(p, L):
    feats, tcol = _filter_features(L)
    deltas = jnp.linspace(math.log(1e-2) / HY_FAST_PCT, math.log(1e-2) / HY_SLOW_PCT, HY_W, dtype=F32)
    absdelta = jnp.tile(jnp.abs(deltas), HY_ORDER * 2)[None, :]
    return _hyfilt(feats, tcol, p["w1"], p["b1"], p["f1"], p["w2"], p["b2"], p["f2"], p["w3"], absdelta)


def _hyena_ctx(u_hy, p, tabs):
    fmat, finv = tabs
    v, x1, x2 = _shortconv(u_hy, p["cw"], p["cb"], row_off=0, nseq=BATCH, seqlen=SEQ, tc=SEQ)
    hdec, s = _hyena_filter(p, SEQ)
    hspec = _ctx_spectrum(fmat, hdec, s)
    z = _ctx_conv(v, x1, fmat, finv, hspec, p["bias"][0:1], 0, F32)
    return _ctx_conv(z, x2, fmat, finv, hspec, p["bias"][1:2], 1, BF16)


def _hyena_lat(u_hy, p, tabs):
    wa, wainv, gtab, gitab = tabs
    n1, n2 = FFT_N1, FFT_N2
    L = DEC_SEQ
    v, x1, x2 = _shortconv(u_hy, p["cw"], p["cb"], row_off=M_CTX, nseq=DEC_BATCH, seqlen=L, tc=512)
    hdec, s = _hyena_filter(p, L)
    ah = _dft_a(wa, hdec.reshape(1, n1 // 2, n2 * HY_ORDER * 2 * HY_W))
    hspec = _lat_spectrum(ah.reshape(2, n1, n2, HY_ORDER * 2 * HY_W), gtab, s)
    view = lambda a: a.reshape(DEC_BATCH, n1 // 2, n2 * HY_W)
    z = view(v)
    gates = (view(x1), view(x2))
    for o in range(HY_ORDER):
        a = _dft_a(wa, z)
        bp = _bhb(a.reshape(DEC_BATCH, 2, n1, n2, HY_W), gtab, gitab, hspec, o)
        bias_t = jnp.tile(p["bias"][o], n2)[None, :]
        z = _dft_ainv_gate(wainv, bp.reshape(DEC_BATCH, 2 * n1, n2 * HY_W), z, gates[o], bias_t,
                           F32 if o + 1 < HY_ORDER else BF16)
    return z.reshape(M_LAT, HY_W)


def _stage3_kernel(x_ref, mod_ref, g1_ref, wg_ref, ylru_ref, ymla_ref, yhy_ref, wl_ref, wm_ref, wh_ref, wo_ref,
                   g2_ref, xo_ref, xm2_ref):
    x = x_ref[...]
    xm = _rms(x, g1_ref[...]) * (1.0 + mod_ref[1:2, :]) + mod_ref[0:1, :]
    xb = xm.astype(BF16)
    merged = None
    for bidx, (y_ref, w_ref) in enumerate(((ylru_ref, wl_ref), (ymla_ref, wm_ref), (yhy_ref, wh_ref))):
        gate = _sigmoid(_dot(xb, wg_ref[:, D_MODEL * bidx:D_MODEL * (bidx + 1)]))
        term = gate * _dot(y_ref[...], w_ref[...])
        merged = term if merged is None else merged + term
    xo = x + mod_ref[2:3, :] * _dot(merged.astype(BF16), wo_ref[...])
    xo_ref[...] = xo
    xm2 = _rms(xo, g2_ref[...]) * (1.0 + mod_ref[4:5, :]) + mod_ref[3:4, :]
    xm2_ref[...] = xm2.astype(BF16)


def _stage3(x, modl, g1, wg, ylru, ymla, yhy, wl, wm, wh, wo, g2):
    tm = TM
    full = lambda shape: pl.BlockSpec(shape, lambda i: (0,) * len(shape))
    row = lambda cols: pl.BlockSpec((tm, cols), lambda i: (i, 0))
    return pl.pallas_call(
        _stage3_kernel,
        grid=(M_TOK // tm,),
        in_specs=[row(D_MODEL),
                  pl.BlockSpec((None, SUBLANES, D_MODEL), lambda i: (_mod_row(i, tm), 0, 0)),
                  full((1, D_MODEL)), full((D_MODEL, 3 * D_MODEL)),
                  row(LRU_W), row(N_HEADS * D_V), row(HY_W),
                  full((LRU_W, D_MODEL)), full((N_HEADS * D_V, D_MODEL)), full((HY_W, D_MODEL)),
                  full((D_MODEL, D_MODEL)), full((1, D_MODEL))],
        out_specs=[row(D_MODEL), row(D_MODEL)],
        out_shape=[jax.ShapeDtypeStruct((M_TOK, D_MODEL), F32), jax.ShapeDtypeStruct((M_TOK, D_MODEL), BF16)],
        compiler_params=_cparams(("arbitrary",)),
        name="stage3",
    )(x, modl, g1, wg, ylru, ymla, yhy, wl, wm, wh, wo, g2)


def _ffn_kernel(xm_ref, x_ref, mod_ref, wg_ref, wu_ref, wd_ref, o_ref, acc_sc, *, nchunks):
    j = pl.program_id(1)
    xb = xm_ref[...]
    g = _dot(xb, wg_ref[...])
    u = _dot(xb, wu_ref[...])
    hid = (g * _sigmoid(g) * u).astype(BF16)
    part = _dot(hid, wd_ref[...])

    @pl.when(j == 0)
    def _():
        acc_sc[...] = part

    @pl.when(j > 0)
    def _():
        acc_sc[...] = acc_sc[...] + part

    @pl.when(j == nchunks - 1)
    def _():
        o_ref[...] = x_ref[...] + mod_ref[5:6, :] * acc_sc[...]


def _ffn_dense(xm2, x, modl, wg, wu, wd):
    tm = TM_FFN
    nchunks = 2
    cw = D_FF // nchunks
    return pl.pallas_call(
        functools.partial(_ffn_kernel, nchunks=nchunks),
        grid=(M_TOK // tm, nchunks),
        in_specs=[pl.BlockSpec((tm, D_MODEL), lambda i, j: (i, 0)),
                  pl.BlockSpec((tm, D_MODEL), lambda i, j: (i, 0)),
                  pl.BlockSpec((None, SUBLANES, D_MODEL), lambda i, j: (_mod_row(i, tm), 0, 0)),
                  pl.BlockSpec((D_MODEL, cw), lambda i, j: (0, j)),
                  pl.BlockSpec((D_MODEL, cw), lambda i, j: (0, j)),
                  pl.BlockSpec((cw, D_MODEL), lambda i, j: (j, 0))],
        out_specs=pl.BlockSpec((tm, D_MODEL), lambda i, j: (i, 0)),
        out_shape=jax.ShapeDtypeStruct((M_TOK, D_MODEL), F32),
        scratch_shapes=[pltpu.VMEM((tm, D_MODEL), F32)],
        compiler_params=_cparams(("arbitrary", "arbitrary")),
        name="ffn_dense",
    )(xm2, x, modl, wg, wu, wd)


def _moe_kernel(xm_ref, x_ref, mod_ref, wr_ref, wg_ref, wu_ref, wd_ref, o_ref, acc_sc, comb_sc):
    e = pl.program_id(1)
    xb = xm_ref[...]
    tm = xb.shape[0]
    lane = lax.broadcasted_iota(jnp.int32, (tm, LANES), 1)

    @pl.when(e == 0)
    def _():
        logits = jnp.where(lane < N_EXPERTS, _dot(xb, wr_ref[...]), -1e30)
        mx = jnp.max(logits, axis=-1, keepdims=True)
        ex = jnp.exp(logits - mx)
        probs = ex / jnp.sum(ex, axis=-1, keepdims=True)
        p1 = jnp.max(probs, axis=-1, keepdims=True)
        i1 = jnp.min(jnp.where(probs == p1, lane, LANES), axis=-1, keepdims=True)
        rest = jnp.where(lane == i1, -1.0, probs)
        p2 = jnp.max(rest, axis=-1, keepdims=True)
        i2 = jnp.min(jnp.where(rest == p2, lane, LANES), axis=-1, keepdims=True)
        tot = p1 + p2
        comb_sc[...] = jnp.where(lane == i1, p1 / tot, 0.0) + jnp.where(lane == i2, p2 / tot, 0.0)
        acc_sc[...] = jnp.zeros_like(acc_sc)

    g = _dot(xb, wg_ref[...])
    u = _dot(xb, wu_ref[...])
    hid = (g * _sigmoid(g) * u).astype(BF16)
    part = _dot(hid, wd_ref[...])
    wcol = jnp.sum(jnp.where(lane == e, comb_sc[...], 0.0), axis=-1, keepdims=True)
    acc_sc[...] = acc_sc[...] + wcol * part

    @pl.when(e == N_EXPERTS - 1)
    def _():
        o_ref[...] = x_ref[...] + mod_ref[5:6, :] * acc_sc[...]


def _ffn_moe(xm2, x, modl, wr, wg, wu, wd):
    tm = TM_FFN
    return pl.pallas_call(
        _moe_kernel,
        grid=(M_TOK // tm, N_EXPERTS),
        in_specs=[pl.BlockSpec((tm, D_MODEL), lambda i, e: (i, 0)),
                  pl.BlockSpec((tm, D_MODEL), lambda i, e: (i, 0)),
                  pl.BlockSpec((None, SUBLANES, D_MODEL), lambda i, e: (_mod_row(i, tm), 0, 0)),
                  pl.BlockSpec((D_MODEL, LANES), lambda i, e: (0, 0)),
                  pl.BlockSpec((None, D_MODEL, D_FF_E), lambda i, e: (e, 0, 0)),
                  pl.BlockSpec((None, D_MODEL, D_FF_E), lambda i, e: (e, 0, 0)),
                  pl.BlockSpec((None, D_FF_E, D_MODEL), lambda i, e: (e, 0, 0))],
        out_specs=pl.BlockSpec((tm, D_MODEL), lambda i, e: (i, 0)),
        out_shape=jax.ShapeDtypeStruct((M_TOK, D_MODEL), F32),
        scratch_shapes=[pltpu.VMEM((tm, D_MODEL), F32), pltpu.VMEM((tm, LANES), F32)],
        compiler_params=_cparams(("arbitrary", "arbitrary")),
        name="ffn_moe",
    )(xm2, x, modl, wr, wg, wu, wd)


def _block_diag(w):
    nb, bs, _ = w.shape
    eye = jnp.eye(nb, dtype=w.dtype)
    return jnp.einsum("njk,nm->njmk", w, eye).reshape(nb * bs, nb * bs)


def _head_pad_cols(w, width):
    r = w.shape[0]
    return jnp.pad(w, ((0, 0), (0, 0), (0, HEAD_PAD - width))).reshape(r, N_HEADS * HEAD_PAD)


def _head_gain(g):
    return jnp.tile(jnp.pad(g, (0, HEAD_PAD - D_QK)), N_HEADS)[None, :]


def _rope_tables(tm):
    rows = DEC_SEQ // GRID_W
    row = jnp.repeat(jnp.arange(rows, dtype=F32), GRID_W)
    col = jnp.tile(jnp.arange(GRID_W, dtype=F32), rows)
    half = D_ROPE // 2
    inv_freq = ROPE_BASE ** (-jnp.arange(0, half, 2, dtype=F32) / half)
    ang = jnp.concatenate([row[:, None] * inv_freq, col[:, None] * inv_freq], axis=-1)
    cos, sin = jnp.cos(ang), jnp.sin(ang)
    cos2 = jnp.repeat(cos, 2, axis=1)
    sin2 = jnp.stack([-sin, sin], axis=-1).reshape(DEC_SEQ, D_ROPE)
    cos_t = jnp.pad(cos2, ((0, 0), (D_NOPE, HEAD_PAD - D_QK)), constant_values=1.0)
    sin_t = jnp.pad(sin2, ((0, 0), (D_NOPE, HEAD_PAD - D_QK)))
    cos_t = jnp.concatenate([jnp.ones((tm, HEAD_PAD), F32), cos_t], axis=0)
    sin_t = jnp.concatenate([jnp.zeros((tm, HEAD_PAD), F32), sin_t], axis=0)
    return cos_t, sin_t


def kernel(x_prompt, x_sample, cache_ckv, cache_krope, state_lru, c, c_ctx, norm1, norm2, w_ada, b_ada, w_in, mla_q_norm, mla_kv_norm, mla_w_uq, mla_w_uk, mla_w_uv, mla_q_qknorm, mla_k_qknorm, lru_conv_w, lru_conv_b, lru_w_gate, lru_b_gate, lru_lambda, hy_conv_w, hy_conv_b, hy_w1, hy_b1, hy_freq1, hy_w2, hy_b2, hy_freq2, hy_w3, hy_bias, w_lru_out, w_mla_out, w_hy_out, w_out, ffn_w_gate, ffn_w_up, ffn_w_down, moe_w_router, moe_w_gate, moe_w_up, moe_w_down):
    x = jnp.concatenate([x_prompt.reshape(M_CTX, D_MODEL), x_sample.reshape(M_LAT, D_MODEL)], axis=0)

    cond = jnp.concatenate([c_ctx[None, :], c, jnp.zeros((SUBLANES - 1 - DEC_BATCH, D_MODEL), F32)], axis=0)
    mod = _adaln(cond, w_ada, b_ada).reshape(DEPTH, SUBLANES, 6, D_MODEL)
    mod = jnp.pad(mod, ((0, 0), (0, 0), (0, SUBLANES - 6), (0, 0)))

    cos_t, sin_t = _rope_tables(TM)
    ctx_tabs = _ctx_tables()
    lat_tabs = _lat_tables()
    zero_state = jnp.zeros((BATCH, SUBLANES, LRU_W), F32)

    ckv_out, kr_out, st_out = [], [], []
    for l in range(DEPTH):
        wl = w_in[l]
        w1 = jnp.concatenate([wl[:, :896], jnp.zeros((D_MODEL, D_NOPE), F32), wl[:, 896:928],
                              jnp.zeros((D_MODEL, HEAD_PAD - D_QK), F32), wl[:, 928:2464]], axis=1).astype(BF16)
        wgates = wl[:, 2464:].astype(BF16)
        wuq = _head_pad_cols(mla_w_uq[l], D_QK).astype(BF16)
        wuk = _head_pad_cols(mla_w_uk[l], D_NOPE).astype(BF16)
        wuv = mla_w_uv[l].reshape(KV_RANK, N_HEADS * D_V).astype(BF16)
        gq = _head_gain(mla_q_qknorm[l])
        gk = _head_gain(mla_k_qknorm[l])

        ulru, uhy, ckv, krb, q, k, v = _stage1(
            x, mod[l], norm1[l][None, :], w1, mla_kv_norm[l][None, :], mla_q_norm[l][None, :],
            wuq, gq, wuk, gk, wuv, cos_t, sin_t)
        ckv_out.append(ckv[:M_CTX].reshape(BATCH, SEQ, KV_RANK))
        kr_out.append(krb[:M_CTX, D_NOPE:D_QK].reshape(BATCH, SEQ, D_ROPE))

        kc, vc = _kvprep(cache_ckv[:, l].reshape(DEC_BATCH * PAST_LEN, KV_RANK),
                         jnp.pad(cache_krope[:, l].reshape(DEC_BATCH * PAST_LEN, D_ROPE),
                                 ((0, 0), (D_NOPE, HEAD_PAD - D_QK))),
                         wuk, gk, wuv)
        ymla = jnp.concatenate([_attention_ctx(q, k, v), _attention_lat(q, k, v, kc, vc)], axis=0)

        lp = dict(
            cw=jnp.pad(lru_conv_w[l], ((0, SUBLANES - 4), (0, 0))), cb=lru_conv_b[l][None, :],
            wr=[_block_diag(lru_w_gate[l, d, 0]).astype(BF16) for d in range(2)],
            wi=[_block_diag(lru_w_gate[l, d, 1]).astype(BF16) for d in range(2)],
            br=[lru_b_gate[l, d, 0][None, :] for d in range(2)],
            bi=[lru_b_gate[l, d, 1][None, :] for d in range(2)],
            lam=[lru_lambda[l, d][None, :] for d in range(2)])
        y_c, stf, stb = _lru_mixer(ulru, lp, (zero_state, zero_state), row_off=0, nseq=BATCH, seqlen=SEQ, tc=SEQ)
        st_out.append(jnp.stack([stf, stb], axis=1))
        h0 = [jnp.broadcast_to(state_lru[:, l, d][:, None, :], (DEC_BATCH, SUBLANES, LRU_W)) for d in range(2)]
        y_l, _, _ = _lru_mixer(ulru, lp, h0, row_off=M_CTX, nseq=DEC_BATCH, seqlen=DEC_SEQ, tc=512)
        ylru = jnp.concatenate([y_c, y_l], axis=0)

        hp = dict(
            cw=jnp.pad(hy_conv_w[l], ((0, SUBLANES - 3), (0, 0))), cb=hy_conv_b[l][None, :],
            w1=jnp.pad(hy_w1[l], ((0, LANES - HY_EMB), (0, 0))).astype(BF16), b1=hy_b1[l][None, :],
            f1=hy_freq1[l][None, :], w2=hy_w2[l].astype(BF16), b2=hy_b2[l][None, :], f2=hy_freq2[l][None, :],
            w3=hy_w3[l].astype(BF16), bias=hy_bias[l])
        yhy = jnp.concatenate([_hyena_ctx(uhy, hp, ctx_tabs), _hyena_lat(uhy, hp, lat_tabs)], axis=0)

        x, xm2 = _stage3(x, mod[l], norm1[l][None, :], wgates, ylru, ymla, yhy,
                         w_lru_out[l].astype(BF16), w_mla_out[l].astype(BF16), w_hy_out[l].astype(BF16),
                         w_out[l].astype(BF16), norm2[l][None, :])
        j = l // 2
        if l % 2 == 0:
            x = _ffn_dense(xm2, x, mod[l], ffn_w_gate[j].astype(BF16), ffn_w_up[j].astype(BF16),
                           ffn_w_down[j].astype(BF16))
        else:
            wr = jnp.pad(moe_w_router[j], ((0, 0), (0, LANES - N_EXPERTS))).astype(BF16)
            x = _ffn_moe(xm2, x, mod[l], wr, moe_w_gate[j].astype(BF16), moe_w_up[j].astype(BF16),
                         moe_w_down[j].astype(BF16))

    y_prompt = x[:M_CTX].reshape(BATCH, SEQ, D_MODEL)
    y_sample = x[M_CTX:].reshape(DEC_BATCH, DEC_SEQ, D_MODEL)
    return (y_prompt, y_sample, jnp.stack(ckv_out, axis=1), jnp.stack(kr_out, axis=1), jnp.stack(st_out, axis=1))
```

```python
import functools
import math

import jax
import jax.numpy as jnp
from jax import lax
from jax.experimental import pallas as pl
from jax.experimental.pallas import tpu as pltpu

F32 = jnp.float32
BF16 = jnp.bfloat16

D_MODEL = 1024
BATCH = 32
SEQ = 256
DEPTH = 2
DEC_BATCH = 2
DEC_SEQ = 4096
PAST_LEN = 512
GRID_W = 64
EPS = 1e-6
LRU_W = 512
LRU_BLOCKS = 8
LRU_C = 8.0
N_HEADS = 8
D_NOPE = 64
D_ROPE = 32
D_QK = D_NOPE + D_ROPE
D_V = 64
Q_RANK = 256
KV_RANK = 128
ROPE_BASE = 10000.0
HY_W = 512
HY_ORDER = 2
HY_EMB = 33
HY_HID = 64
HY_FAST_PCT = 0.3
HY_SLOW_PCT = 1.5
D_FF = 2816
N_EXPERTS = 8
D_FF_E = 1408

LANES = 128
SUBLANES = 8
VMEM_LIMIT = 56 * 1024 * 1024

M_CTX = BATCH * SEQ
M_LAT = DEC_BATCH * DEC_SEQ
M_TOK = M_CTX + M_LAT
TM1 = 256
TM3 = 512
TM_FFN = 512
W1_COLS = 2688
HEAD_PAD = LANES
QK_SCALE = math.log2(math.e) / math.sqrt(D_QK)
ATTN_KEY_CHUNK = 512

FFT_N1 = 64
FFT_N2 = 128
FFT_K1_BLOCK = 4


def _cparams(sem, vmem=VMEM_LIMIT):
    return pltpu.CompilerParams(dimension_semantics=sem, vmem_limit_bytes=vmem)


def _dot(a, b):
    return jnp.dot(a, b, preferred_element_type=F32)


def _rms(x, g):
    ms = jnp.mean(x * x, axis=-1, keepdims=True)
    return x * lax.rsqrt(ms + EPS) * g


def _sigmoid(x):
    return 1.0 / (1.0 + jnp.exp(-x))


def _ada_kernel(c_ref, w_ref, b_ref, o_ref):
    c = c_ref[...]
    s = (c * _sigmoid(c)).astype(BF16)
    o_ref[...] = _dot(s, w_ref[...].astype(BF16)) + b_ref[...]


def _adaln(cond, w_ada, b_ada):
    tn = 1024
    n6 = 6 * D_MODEL
    return pl.pallas_call(
        _ada_kernel,
        grid=(DEPTH, n6 // tn),
        in_specs=[
            pl.BlockSpec((SUBLANES, D_MODEL), lambda l, j: (0, 0)),
            pl.BlockSpec((None, D_MODEL, tn), lambda l, j: (l, 0, j)),
            pl.BlockSpec((None, 1, tn), lambda l, j: (l, 0, j)),
        ],
        out_specs=pl.BlockSpec((None, SUBLANES, tn), lambda l, j: (l, 0, j)),
        out_shape=jax.ShapeDtypeStruct((DEPTH, SUBLANES, n6), F32),
        compiler_params=_cparams(("arbitrary", "arbitrary")),
        name="adaln",
    )(cond, w_ada, b_ada.reshape(DEPTH, 1, n6))


def _mod_row(i, tm):
    nctx = M_CTX // tm
    per = DEC_SEQ // tm
    return jnp.where(i < nctx, 0, 1 + (i - nctx) // per)


def _rope_blk(i, tm):
    nctx = M_CTX // tm
    per = DEC_SEQ // tm
    return jnp.where(i < nctx, 0, 1 + (i - nctx) % per)


def _finish_head(raw, raw_sw, gc, gs, out_ref, sl):
    ms = jnp.sum(raw * raw, axis=-1, keepdims=True) * (1.0 / D_QK)
    rs = lax.rsqrt(ms + EPS)
    val = raw * gc
    if gs is not None:
        val = val + raw_sw * gs
    out_ref[:, sl] = (val * rs).astype(BF16)


def _stage1_kernel(x_ref, mod_ref, g1_ref, w1_ref, gkv_ref, gqn_ref, wuq_ref, wuqs_ref, gq_ref, wuk_ref, gk_ref,
                   wuv_ref, cos_ref, sin_ref,
                   ulru_ref, uhy_ref, ckv_ref, krb_ref, q_ref, k_ref, v_ref):
    x = x_ref[...]
    xm = _rms(x, g1_ref[...]) * (1.0 + mod_ref[1:2, :]) + mod_ref[0:1, :]
    xb = xm.astype(BF16)
    ulru_ref[...] = _dot(xb, w1_ref[:, 0:512])
    qc = _dot(xb, w1_ref[:, 512:768])
    ckv = _dot(xb, w1_ref[:, 768:896])
    krb = _dot(xb, w1_ref[:, 896:1024])
    krs = _dot(xb, w1_ref[:, 1024:1152])
    uhy_ref[...] = _dot(xb, w1_ref[:, 1152:2688])
    ckvn = _rms(ckv, gkv_ref[...])
    ckv_ref[...] = ckvn
    krb_ref[...] = krb
    qn = _rms(qc, gqn_ref[...]).astype(BF16)
    cb = ckvn.astype(BF16)
    v_ref[...] = _dot(cb, wuv_ref[...]).astype(BF16)
    cos = cos_ref[...]
    sin = sin_ref[...]
    gcq = cos * (gq_ref[0:1, :] * QK_SCALE)
    gsq = sin * (gq_ref[1:2, :] * QK_SCALE)
    gck = cos * gk_ref[0:1, :]
    gsk = sin * gk_ref[1:2, :]
    for h in range(N_HEADS):
        sl = slice(HEAD_PAD * h, HEAD_PAD * (h + 1))
        _finish_head(_dot(qn, wuq_ref[:, sl]), _dot(qn, wuqs_ref[:, sl]), gcq, gsq, q_ref, sl)
        _finish_head(_dot(cb, wuk_ref[:, sl]) + krb, krs, gck, gsk, k_ref, sl)


def _stage1(x, modl, g1, w1, gkv, gqn, wuq, wuqs, gq, wuk, gk, wuv, cos_t, sin_t):
    tm = TM1
    full = lambda shape: pl.BlockSpec(shape, lambda i: (0,) * len(shape))
    row = lambda cols: pl.BlockSpec((tm, cols), lambda i: (i, 0))
    hw = N_HEADS * HEAD_PAD
    return pl.pallas_call(
        _stage1_kernel,
        grid=(M_TOK // tm,),
        in_specs=[
            row(D_MODEL),
            pl.BlockSpec((None, SUBLANES, D_MODEL), lambda i: (_mod_row(i, tm), 0, 0)),
            full((1, D_MODEL)),
            full((D_MODEL, W1_COLS)),
            full((1, KV_RANK)),
            full((1, Q_RANK)),
            full((Q_RANK, hw)),
            full((Q_RANK, hw)),
            full((SUBLANES, HEAD_PAD)),
            full((KV_RANK, hw)),
            full((SUBLANES, HEAD_PAD)),
            full((KV_RANK, N_HEADS * D_V)),
            pl.BlockSpec((tm, LANES), lambda i: (_rope_blk(i, tm), 0)),
            pl.BlockSpec((tm, LANES), lambda i: (_rope_blk(i, tm), 0)),
        ],
        out_specs=[row(LRU_W), row(3 * HY_W), row(KV_RANK), row(LANES), row(hw), row(hw), row(N_HEADS * D_V)],
        out_shape=[
            jax.ShapeDtypeStruct((M_TOK, LRU_W), F32),
            jax.ShapeDtypeStruct((M_TOK, 3 * HY_W), F32),
            jax.ShapeDtypeStruct((M_TOK, KV_RANK), F32),
            jax.ShapeDtypeStruct((M_TOK, LANES), F32),
            jax.ShapeDtypeStruct((M_TOK, hw), BF16),
            jax.ShapeDtypeStruct((M_TOK, hw), BF16),
            jax.ShapeDtypeStruct((M_TOK, N_HEADS * D_V), BF16),
        ],
        compiler_params=_cparams(("arbitrary",)),
        name="stage1",
    )(x, modl, g1, w1, gkv, gqn, wuq, wuqs, gq, wuk, gk, wuv, cos_t, sin_t)


def _kvprep_kernel(ckv_ref, krb_ref, wuk_ref, gk_ref, wuv_ref, k_ref, v_ref):
    cb = ckv_ref[...].astype(BF16)
    v_ref[...] = _dot(cb, wuv_ref[...]).astype(BF16)
    krb = krb_ref[...]
    for h in range(N_HEADS):
        sl = slice(HEAD_PAD * h, HEAD_PAD * (h + 1))
        _finish_head(_dot(cb, wuk_ref[:, sl]) + krb, None, gk_ref[0:1, :], None, k_ref, sl)


def _kvprep(ckv, krb, wuk, gk, wuv):
    rows = ckv.shape[0]
    tm = TM1
    hw = N_HEADS * HEAD_PAD
    full = lambda shape: pl.BlockSpec(shape, lambda i: (0,) * len(shape))
    row = lambda cols: pl.BlockSpec((tm, cols), lambda i: (i, 0))
    return pl.pallas_call(
        _kvprep_kernel,
        grid=(rows // tm,),
        in_specs=[row(KV_RANK), row(LANES), full((KV_RANK, hw)), full((SUBLANES, HEAD_PAD)),
                  full((KV_RANK, N_HEADS * D_V))],
        out_specs=[row(hw), row(N_HEADS * D_V)],
        out_shape=[jax.ShapeDtypeStruct((rows, hw), BF16), jax.ShapeDtypeStruct((rows, N_HEADS * D_V), BF16)],
        compiler_params=_cparams(("arbitrary",)),
        name="kvprep",
    )(ckv, krb, wuk, gk, wuv)


def _attn_kernel(*refs, heads, nseg):
    q_ref = refs[0]
    k_refs = refs[1:1 + nseg]
    v_refs = refs[1 + nseg:1 + 2 * nseg]
    o_ref = refs[1 + 2 * nseg]
    tq = q_ref.shape[0]
    lane = lax.broadcasted_iota(jnp.int32, (tq, LANES), 1)
    low = lane < D_V
    for pair in range(heads // 2):
        outs = []
        for j in range(2):
            h = 2 * pair + j
            sl = slice(HEAD_PAD * h, HEAD_PAD * (h + 1))
            q = q_ref[:, sl]
            s = [lax.dot_general(q, kr[:, sl], (((1,), (1,)), ((), ())), preferred_element_type=F32)
                 for kr in k_refs]
            m = jnp.max(s[0], axis=-1, keepdims=True)
            for si in s[1:]:
                m = jnp.maximum(m, jnp.max(si, axis=-1, keepdims=True))
            acc = None
            den = None
            for si, vr in zip(s, v_refs):
                p = jnp.exp2(si - m)
                d = jnp.sum(p, axis=-1, keepdims=True)
                o = _dot(p.astype(BF16), vr[:, LANES * pair:LANES * (pair + 1)])
                acc = o if acc is None else acc + o
                den = d if den is None else den + d
            outs.append(acc / den)
        o_ref[:, LANES * pair:LANES * (pair + 1)] = jnp.where(low, outs[0], outs[1]).astype(BF16)


def _attention_ctx(q, k, v):
    hw = N_HEADS * HEAD_PAD
    vw = N_HEADS * D_V
    return pl.pallas_call(
        functools.partial(_attn_kernel, heads=N_HEADS, nseg=1),
        grid=(BATCH,),
        in_specs=[
            pl.BlockSpec((SEQ, hw), lambda b: (b, 0)),
            pl.BlockSpec((SEQ, hw), lambda b: (b, 0)),
            pl.BlockSpec((SEQ, vw), lambda b: (b, 0)),
        ],
        out_specs=pl.BlockSpec((SEQ, vw), lambda b: (b, 0)),
        out_shape=jax.ShapeDtypeStruct((M_CTX, vw), BF16),
        compiler_params=_cparams(("arbitrary",)),
        name="attn_ctx",
    )(q, k, v)


def _attention_lat(q, k, v, kc, vc):
    tq = 256
    nq = DEC_SEQ // tq
    qoff = M_CTX // tq
    koff = M_CTX // DEC_SEQ
    return pl.pallas_call(
        functools.partial(_attn_kernel, heads=2, nseg=2),
        grid=(DEC_BATCH, N_HEADS // 2, nq),
        in_specs=[
            pl.BlockSpec((tq, 2 * HEAD_PAD), lambda b, p, i: (qoff + b * nq + i, p)),
            pl.BlockSpec((DEC_SEQ, 2 * HEAD_PAD), lambda b, p, i: (koff + b, p)),
            pl.BlockSpec((PAST_LEN, 2 * HEAD_PAD), lambda b, p, i: (b, p)),
            pl.BlockSpec((DEC_SEQ, 2 * D_V), lambda b, p, i: (koff + b, p)),
            pl.BlockSpec((PAST_LEN, 2 * D_V), lambda b, p, i: (b, p)),
        ],
        out_specs=pl.BlockSpec((tq, 2 * D_V), lambda b, p, i: (b * nq + i, p)),
        out_shape=jax.ShapeDtypeStruct((M_LAT, N_HEADS * D_V), BF16),
        compiler_params=_cparams(("arbitrary", "arbitrary", "arbitrary")),
        name="attn_lat",
    )(q, k, kc, v, vc)


def _lru_kernel(*refs, reverse, tc, nchunks):
    if reverse:
        (up_ref, uc_ref, un_ref, hf_ref, cw_ref, cb_ref, wr_ref, wi_ref, br_ref, bi_ref, lam_ref, h0_ref,
         y_ref, st_ref, ext_sc, a_sc, b_sc, h_sc, car_sc) = refs
    else:
        (up_ref, uc_ref, un_ref, cw_ref, cb_ref, wr_ref, wi_ref, br_ref, bi_ref, lam_ref, h0_ref,
         y_ref, st_ref, ext_sc, a_sc, b_sc, h_sc, car_sc) = refs
    c = pl.program_id(1)
    chunk = (nchunks - 1 - c) if reverse else c
    prev = jnp.where(chunk == 0, 0.0, up_ref[...])
    nxt = jnp.where(chunk == nchunks - 1, 0.0, un_ref[...])
    ext_sc[0:SUBLANES, :] = prev
    ext_sc[SUBLANES:SUBLANES + tc, :] = uc_ref[...]
    ext_sc[SUBLANES + tc:2 * SUBLANES + tc, :] = nxt
    xc = cb_ref[...]
    for k in range(4):
        xc = xc + cw_ref[k:k + 1, :] * ext_sc[SUBLANES - 2 + k:SUBLANES - 2 + k + tc, :]
    xb = xc.astype(BF16)
    r = _sigmoid(_dot(xb, wr_ref[...]) + br_ref[...])
    gi = _sigmoid(_dot(xb, wi_ref[...]) + bi_ref[...])
    lam = lam_ref[...]
    logsig = -(jnp.maximum(-lam, 0.0) + jnp.log1p(jnp.exp(-jnp.abs(lam))))
    la = LRU_C * r * logsig
    a = jnp.exp(la)
    a_sc[...] = a
    b_sc[...] = jnp.sqrt(-jnp.tanh(la) * (a * a + 1.0)) * (gi * xc)

    @pl.when(c == 0)
    def _():
        car_sc[...] = h0_ref[...]

    def body(j, h):
        t = (tc - 1 - j) if reverse else j
        h = a_sc[pl.ds(t, 1), :] * h + b_sc[pl.ds(t, 1), :]
        h_sc[pl.ds(t, 1), :] = h
        return h

    h = lax.fori_loop(0, tc, body, car_sc[0:1, :], unroll=8)
    car_sc[0:1, :] = h
    st_ref[...] = jnp.broadcast_to(h, (SUBLANES, LRU_W))
    if reverse:
        y_ref[...] = (hf_ref[...] + h_sc[...]).astype(BF16)
    else:
        y_ref[...] = h_sc[...]


def _lru_dir(u, hf, cw, cb, wr, wi, br, bi, lam, h0, *, reverse, row_off, nseq, seqlen, tc):
    nchunks = seqlen // tc
    hb = M_TOK // SUBLANES

    def chunk_of(c):
        return (nchunks - 1 - c) if reverse else c

    def cur(b, c):
        return ((row_off + b * seqlen) // tc + chunk_of(c), 0)

    def prv(b, c):
        return (jnp.maximum((row_off + b * seqlen + chunk_of(c) * tc) // SUBLANES - 1, 0), 0)

    def nxt(b, c):
        return (jnp.minimum((row_off + b * seqlen + (chunk_of(c) + 1) * tc) // SUBLANES, hb - 1), 0)

    def out_cur(b, c):
        return ((b * seqlen) // tc + chunk_of(c), 0)

    full = lambda shape: pl.BlockSpec(shape, lambda b, c: (0,) * len(shape))
    in_specs = [pl.BlockSpec((SUBLANES, LRU_W), prv), pl.BlockSpec((tc, LRU_W), cur),
                pl.BlockSpec((SUBLANES, LRU_W), nxt)]
    args = [u, u, u]
    if reverse:
        in_specs.append(pl.BlockSpec((tc, LRU_W), out_cur))
        args.append(hf)
    in_specs += [full((SUBLANES, LRU_W)), full((1, LRU_W)), full((LRU_W, LRU_W)), full((LRU_W, LRU_W)),
                 full((1, LRU_W)), full((1, LRU_W)), full((1, LRU_W)),
                 pl.BlockSpec((None, SUBLANES, LRU_W), lambda b, c: (b, 0, 0))]
    args += [cw, cb, wr, wi, br, bi, lam, h0]
    return pl.pallas_call(
        functools.partial(_lru_kernel, reverse=reverse, tc=tc, nchunks=nchunks),
        grid=(nseq, nchunks),
        in_specs=in_specs,
        out_specs=[pl.BlockSpec((tc, LRU_W), out_cur),
                   pl.BlockSpec((None, SUBLANES, LRU_W), lambda b, c: (b, 0, 0))],
        out_shape=[jax.ShapeDtypeStruct((nseq * seqlen, LRU_W), BF16 if reverse else F32),
                   jax.ShapeDtypeStruct((nseq, SUBLANES, LRU_W), F32)],
        scratch_shapes=[pltpu.VMEM((tc + 2 * SUBLANES, LRU_W), F32), pltpu.VMEM((tc, LRU_W), F32),
                        pltpu.VMEM((tc, LRU_W), F32), pltpu.VMEM((tc, LRU_W), F32),
                        pltpu.VMEM((SUBLANES, LRU_W), F32)],
        compiler_params=_cparams(("arbitrary", "arbitrary")),
        name="lru_bwd" if reverse else "lru_fwd",
    )(*args)


def _lru_mixer(u, p, h0, *, row_off, nseq, seqlen, tc):
    kw = dict(row_off=row_off, nseq=nseq, seqlen=seqlen, tc=tc)
    hf, stf = _lru_dir(u, None, p["cw"], p["cb"], p["wr"][0], p["wi"][0], p["br"][0], p["bi"][0], p["lam"][0],
                       h0[0], reverse=False, **kw)
    y, stb = _lru_dir(u, hf, p["cw"], p["cb"], p["wr"][1], p["wi"][1], p["br"][1], p["bi"][1], p["lam"][1],
                      h0[1], reverse=True, **kw)
    return y, stf[:, 0, :], stb[:, 0, :]


def _shortconv_kernel(up_ref, uc_ref, un_ref, cw_ref, cb_ref, v_ref, x1_ref, x2_ref, ext_sc, *, tc, nchunks):
    c = pl.program_id(1)
    prev = jnp.where(c == 0, 0.0, up_ref[...])
    nxt = jnp.where(c == nchunks - 1, 0.0, un_ref[...])
    ext_sc[0:SUBLANES, :] = prev
    ext_sc[SUBLANES:SUBLANES + tc, :] = uc_ref[...]
    ext_sc[SUBLANES + tc:2 * SUBLANES + tc, :] = nxt
    for part, o_ref in enumerate((v_ref, x1_ref, x2_ref)):
        cs = slice(HY_W * part, HY_W * (part + 1))
        acc = cb_ref[:, cs]
        for k in range(3):
            acc = acc + cw_ref[k:k + 1, cs] * ext_sc[SUBLANES - 1 + k:SUBLANES - 1 + k + tc, cs]
        o_ref[...] = acc


def _shortconv(u, cw, cb, *, row_off, nseq, seqlen, tc):
    nchunks = seqlen // tc
    w = 3 * HY_W
    hb = M_TOK // SUBLANES
    cur = lambda b, c: ((row_off + b * seqlen) // tc + c, 0)
    prv = lambda b, c: (jnp.maximum((row_off + b * seqlen + c * tc) // SUBLANES - 1, 0), 0)
    nxt = lambda b, c: (jnp.minimum((row_off + b * seqlen + (c + 1) * tc) // SUBLANES, hb - 1), 0)
    out = lambda b, c: ((b * seqlen) // tc + c, 0)
    full = lambda shape: pl.BlockSpec(shape, lambda b, c: (0,) * len(shape))
    rows = nseq * seqlen
    return pl.pallas_call(
        functools.partial(_shortconv_kernel, tc=tc, nchunks=nchunks),
        grid=(nseq, nchunks),
        in_specs=[pl.BlockSpec((SUBLANES, w), prv), pl.BlockSpec((tc, w), cur), pl.BlockSpec((SUBLANES, w), nxt),
                  full((SUBLANES, w)), full((1, w))],
        out_specs=[pl.BlockSpec((tc, HY_W), out)] * 3,
        out_shape=[jax.ShapeDtypeStruct((rows, HY_W), F32)] * 3,
        scratch_shapes=[pltpu.VMEM((tc + 2 * SUBLANES, w), F32)],
        compiler_params=_cparams(("arbitrary", "arbitrary")),
        name="hy_shortconv",
    )(u, u, u, cw, cb)


def _hyfilt_kernel(z_ref, t_ref, w1_ref, b1_ref, f1_ref, w2_ref, b2_ref, f2_ref, w3_ref, ad_ref, h_ref, s_ref):
    i = pl.program_id(0)
    z = z_ref[...].astype(BF16)
    h = jnp.sin(f1_ref[...] * (_dot(z, w1_ref[...]) + b1_ref[...]))
    h = jnp.sin(f2_ref[...] * (_dot(h.astype(BF16), w2_ref[...]) + b2_ref[...]))
    h = _dot(h.astype(BF16), w3_ref[...])
    t = t_ref[...]
    ncol = h.shape[1] // LANES
    win = jnp.concatenate([jnp.exp(-t * ad_ref[:, LANES * j:LANES * (j + 1)]) for j in range(ncol)], axis=1)
    h = h * win
    h_ref[...] = h

    @pl.when(i == 0)
    def _():
        s_ref[...] = jnp.zeros_like(s_ref)

    s_ref[0:1, :] = s_ref[0:1, :] + jnp.sum(jnp.abs(h), axis=0, keepdims=True)


def _hyfilt(feats, tcol, w1, b1, f1, w2, b2, f2, w3, absdelta):
    L = feats.shape[0]
    tl = min(L, 512)
    wcols = HY_ORDER * 2 * HY_W
    full = lambda shape: pl.BlockSpec(shape, lambda i: (0,) * len(shape))
    return pl.pallas_call(
        _hyfilt_kernel,
        grid=(L // tl,),
        in_specs=[pl.BlockSpec((tl, LANES), lambda i: (i, 0)), pl.BlockSpec((tl, LANES), lambda i: (i, 0)),
                  full((LANES, HY_HID)), full((1, HY_HID)), full((1, HY_HID)),
                  full((HY_HID, HY_HID)), full((1, HY_HID)), full((1, HY_HID)),
                  full((HY_HID, wcols)), full((1, wcols))],
        out_specs=[pl.BlockSpec((tl, wcols), lambda i: (i, 0)), full((SUBLANES, wcols))],
        out_shape=[jax.ShapeDtypeStruct((L, wcols), F32), jax.ShapeDtypeStruct((SUBLANES, wcols), F32)],
        compiler_params=_cparams(("arbitrary",)),
        name="hy_filter",
    )(feats, tcol, w1, b1, f1, w2, b2, f2, w3, absdelta)


def _combine_spectrum(zr, zi, s_ref, hr_out, hi_out):
    for o in range(HY_ORDER):
        f = slice(2 * HY_W * o, 2 * HY_W * o + HY_W)
        b = slice(2 * HY_W * o + HY_W, 2 * HY_W * (o + 1))
        den = s_ref[0:1, f] + s_ref[0:1, b] + EPS
        hr_out(o, (zr[:, f] + zr[:, b]) / den)
        hi_out(o, (zi[:, f] - zi[:, b]) / den)


def _ctx_spec_kernel(f_ref, h_ref, s_ref, o_ref):
    n = f_ref.shape[0] // 2
    z = _dot(f_ref[...], h_ref[...].astype(BF16))
    zr, zi = z[:n], z[n:]

    def put_r(o, val):
        o_ref[0, :, HY_W * o:HY_W * (o + 1)] = val

    def put_i(o, val):
        o_ref[1, :, HY_W * o:HY_W * (o + 1)] = val

    _combine_spectrum(zr, zi, s_ref, put_r, put_i)


def _ctx_spectrum(fmat, hdec, s):
    n = fmat.shape[0] // 2
    return pl.pallas_call(
        _ctx_spec_kernel,
        out_shape=jax.ShapeDtypeStruct((2, n, HY_ORDER * HY_W), F32),
        compiler_params=pltpu.CompilerParams(vmem_limit_bytes=VMEM_LIMIT),
        name="hy_ctx_spectrum",
    )(fmat, hdec, s)


def _ctx_conv_kernel(z_ref, x_ref, f_ref, fi_ref, h_ref, bias_ref, o_ref, *, nb, seqlen):
    n = f_ref.shape[0] // 2
    hr = h_ref[0]
    hi = h_ref[1]
    for b in range(nb):
        rs = slice(seqlen * b, seqlen * (b + 1))
        zt = z_ref[rs, :]
        zf = _dot(f_ref[...], zt.astype(BF16))
        zr, zi = zf[:n], zf[n:]
        y = jnp.concatenate([zr * hr - zi * hi, zr * hi + zi * hr], axis=0).astype(BF16)
        conv = _dot(fi_ref[...], y)
        o_ref[rs, :] = (x_ref[rs, :] * (conv + zt * bias_ref[...])).astype(o_ref.dtype)


def _ctx_conv(z, xg, fmat, finv, hspec, bias, order, out_dtype):
    nb = 4
    n = fmat.shape[0] // 2
    rows = nb * SEQ
    return pl.pallas_call(
        functools.partial(_ctx_conv_kernel, nb=nb, seqlen=SEQ),
        grid=(BATCH // nb,),
        in_specs=[pl.BlockSpec((rows, HY_W), lambda i: (i, 0)), pl.BlockSpec((rows, HY_W), lambda i: (i, 0)),
                  pl.BlockSpec(fmat.shape, lambda i: (0, 0)), pl.BlockSpec(finv.shape, lambda i: (0, 0)),
                  pl.BlockSpec((2, n, HY_W), lambda i: (0, 0, order)),
                  pl.BlockSpec((1, HY_W), lambda i: (0, 0))],
        out_specs=pl.BlockSpec((rows, HY_W), lambda i: (i, 0)),
        out_shape=jax.ShapeDtypeStruct((M_CTX, HY_W), out_dtype),
        compiler_params=_cparams(("arbitrary",)),
        name="hy_ctx_conv",
    )(z, xg, fmat, finv, hspec, bias)


def _dft_a_kernel(w_ref, x_ref, o_ref):
    o_ref[...] = _dot(w_ref[...], x_ref[...].astype(BF16)).astype(o_ref.dtype)


def _dft_a(w, x, out_dtype=BF16):
    nb, kk, cols = x.shape
    r = w.shape[0]
    tcw = 8192
    return pl.pallas_call(
        _dft_a_kernel,
        grid=(nb, cols // tcw),
        in_specs=[pl.BlockSpec((r, kk), lambda b, j: (0, 0)), pl.BlockSpec((None, kk, tcw), lambda b, j: (b, 0, j))],
        out_specs=pl.BlockSpec((None, r, tcw), lambda b, j: (b, 0, j)),
        out_shape=jax.ShapeDtypeStruct((nb, r, cols), out_dtype),
        compiler_params=_cparams(("arbitrary", "arbitrary")),
        name="hy_dft_a",
    )(w, x)


def _lat_spec_kernel(a_ref, g_ref, s_ref, o_ref):
    n2 = FFT_N2
    for kk in range(FFT_K1_BLOCK):
        a = jnp.concatenate([a_ref[0, kk], a_ref[1, kk]], axis=0)
        z = _dot(g_ref[kk], a)
        zr, zi = z[:n2], z[n2:]

        def put_r(o, val, kk=kk):
            o_ref[0, kk, :, HY_W * o:HY_W * (o + 1)] = val

        def put_i(o, val, kk=kk):
            o_ref[1, kk, :, HY_W * o:HY_W * (o + 1)] = val

        _combine_spectrum(zr, zi, s_ref, put_r, put_i)


def _lat_spectrum(a4, gtab, s):
    kb = FFT_K1_BLOCK
    wc = HY_ORDER * 2 * HY_W
    return pl.pallas_call(
        _lat_spec_kernel,
        grid=(FFT_N1 // kb,),
        in_specs=[pl.BlockSpec((2, kb, FFT_N2, wc), lambda i: (0, i, 0, 0)),
                  pl.BlockSpec((kb, 2 * FFT_N2, 2 * FFT_N2), lambda i: (i, 0, 0)),
                  pl.BlockSpec((SUBLANES, wc), lambda i: (0, 0))],
        out_specs=pl.BlockSpec((2, kb, FFT_N2, HY_ORDER * HY_W), lambda i: (0, i, 0, 0)),
        out_shape=jax.ShapeDtypeStruct((2, FFT_N1, FFT_N2, HY_ORDER * HY_W), F32),
        compiler_params=_cparams(("arbitrary",)),
        name="hy_lat_spectrum",
    )(a4, gtab, s)


def _bhb_kernel(a_ref, g_ref, gi_ref, h_ref, o_ref):
    n2 = FFT_N2
    for kk in range(FFT_K1_BLOCK):
        a = jnp.concatenate([a_ref[0, kk], a_ref[1, kk]], axis=0)
        z = _dot(g_ref[kk], a)
        zr, zi = z[:n2], z[n2:]
        hr = h_ref[0, kk]
        hi = h_ref[1, kk]
        y = jnp.concatenate([zr * hr - zi * hi, zr * hi + zi * hr], axis=0).astype(BF16)
        bp = _dot(gi_ref[kk], y)
        o_ref[0, kk] = bp[:n2].astype(BF16)
        o_ref[1, kk] = bp[n2:].astype(BF16)


def _bhb(a5, gtab, gitab, hspec, order):
    kb = FFT_K1_BLOCK
    nb = a5.shape[0]
    blk = (None, 2, kb, FFT_N2, HY_W)
    return pl.pallas_call(
        _bhb_kernel,
        grid=(FFT_N1 // kb, nb),
        in_specs=[pl.BlockSpec(blk, lambda i, b: (b, 0, i, 0, 0)),
                  pl.BlockSpec((kb, 2 * FFT_N2, 2 * FFT_N2), lambda i, b: (i, 0, 0)),
                  pl.BlockSpec((kb, 2 * FFT_N2, 2 * FFT_N2), lambda i, b: (i, 0, 0)),
                  pl.BlockSpec((2, kb, FFT_N2, HY_W), lambda i, b: (0, i, 0, order))],
        out_specs=pl.BlockSpec(blk, lambda i, b: (b, 0, i, 0, 0)),
        out_shape=jax.ShapeDtypeStruct(a5.shape, BF16),
        compiler_params=_cparams(("arbitrary", "arbitrary")),
        name="hy_bhb",
    )(a5, gtab, gitab, hspec)


def _dft_ainv_kernel(w_ref, bp_ref, z_ref, x_ref, bias_ref, o_ref):
    conv = _dot(w_ref[...], bp_ref[...])
    o_ref[...] = (x_ref[...] * (conv + z_ref[...] * bias_ref[...])).astype(o_ref.dtype)


def _dft_ainv_gate(w, bp, z, xg, bias_t, out_dtype):
    nb, r2, cols = bp.shape
    r = w.shape[0]
    tcw = 8192
    return pl.pallas_call(
        _dft_ainv_kernel,
        grid=(nb, cols // tcw),
        in_specs=[pl.BlockSpec((r, r2), lambda b, j: (0, 0)),
                  pl.BlockSpec((None, r2, tcw), lambda b, j: (b, 0, j)),
                  pl.BlockSpec((None, r, tcw), lambda b, j: (b, 0, j)),
                  pl.BlockSpec((None, r, tcw), lambda b, j: (b, 0, j)),
                  pl.BlockSpec((1, tcw), lambda b, j: (0, j))],
        out_specs=pl.BlockSpec((None, r, tcw), lambda b, j: (b, 0, j)),
        out_shape=jax.ShapeDtypeStruct((nb, r, cols), out_dtype),
        compiler_params=_cparams(("arbitrary", "arbitrary")),
        name="hy_dft_ainv",
    )(w, bp, z, xg, bias_t)


def _angle(m, n):
    return (m % n).astype(F32) * (2.0 * math.pi / n)


def _ctx_tables():
    n = 2 * SEQ
    k = jnp.arange(n, dtype=jnp.int32)[:, None]
    t = jnp.arange(SEQ, dtype=jnp.int32)[None, :]
    th = _angle(k * t, n)
    fmat = jnp.concatenate([jnp.cos(th), -jnp.sin(th)], axis=0)
    finv = jnp.concatenate([jnp.cos(th).T, -jnp.sin(th).T], axis=1) / n
    return fmat.astype(BF16), finv.astype(BF16)


def _lat_tables():
    n1, n2 = FFT_N1, FFT_N2
    n = n1 * n2
    k1 = jnp.arange(n1, dtype=jnp.int32)
    th1 = _angle(k1[:, None] * jnp.arange(n1 // 2, dtype=jnp.int32)[None, :], n1)
    wa = jnp.concatenate([jnp.cos(th1), -jnp.sin(th1)], axis=0)
    wainv = jnp.concatenate([jnp.cos(th1).T, -jnp.sin(th1).T], axis=1) / n
    k2 = jnp.arange(n2, dtype=jnp.int32)
    nn2 = jnp.arange(n2, dtype=jnp.int32)
    kfull = k1[:, None, None] + n1 * k2[None, :, None]
    th = _angle(kfull * nn2[None, None, :], n)
    gr, gi = jnp.cos(th), -jnp.sin(th)
    g = jnp.concatenate([jnp.concatenate([gr, -gi], axis=2), jnp.concatenate([gi, gr], axis=2)], axis=1)
    thT = jnp.swapaxes(th, 1, 2)
    ir, ii = jnp.cos(thT), jnp.sin(thT)
    ginv = jnp.concatenate([jnp.concatenate([ir, -ii], axis=2), jnp.concatenate([ii, ir], axis=2)], axis=1)
    return wa.astype(BF16), wainv.astype(BF16), g.astype(BF16), ginv.astype(BF16)


def _filter_features(L):
    t = jnp.linspace(0.0, 1.0, L, dtype=F32)[:, None]
    bands = (HY_EMB - 1) // 2
    w = (2.0 * math.pi / L) * jnp.arange(L, dtype=F32)[:, None]
    f = jnp.linspace(1e-4, bands - 1, bands, dtype=F32)[None, :]
    z = jnp.concatenate([t, jnp.cos(f * w), -jnp.sin(f * w)], axis=-1)
    z = jnp.pad(z, ((0, 0), (0, LANES - HY_EMB)))
    return z, jnp.broadcast_to(t, (L, LANES))


def _hyena_filter(p, L):
    feats, tcol = _filter_features(L)
    deltas = jnp.linspace(math.log(1e-2) / HY_FAST_PCT, math.log(1e-2) / HY_SLOW_PCT, HY_W, dtype=F32)
    absdelta = jnp.tile(jnp.abs(deltas), HY_ORDER * 2)[None, :]
    return _hyfilt(feats, tcol, p["w1"], p["b1"], p["f1"], p["w2"], p["b2"], p["f2"], p["w3"], absdelta)


def _hyena_ctx(u_hy, p, tabs):
    fmat, finv = tabs
    v, x1, x2 = _shortconv(u_hy, p["cw"], p["cb"], row_off=0, nseq=BATCH, seqlen=SEQ, tc=SEQ)
    hdec, s = _hyena_filter(p, SEQ)
    hspec = _ctx_spectrum(fmat, hdec, s)
    z = _ctx_conv(v, x1, fmat, finv, hspec, p["bias"][0:1], 0, F32)
    return _ctx_conv(z, x2, fmat, finv, hspec, p["bias"][1:2], 1, BF16)


def _hyena_lat(u_hy, p, tabs):
    wa, wainv, gtab, gitab = tabs
    n1, n2 = FFT_N1, FFT_N2
    L = DEC_SEQ
    v, x1, x2 = _shortconv(u_hy, p["cw"], p["cb"], row_off=M_CTX, nseq=DEC_BATCH, seqlen=L, tc=512)
    hdec, s = _hyena_filter(p, L)
    ah = _dft_a(wa, hdec.reshape(1, n1 // 2, n2 * HY_ORDER * 2 * HY_W))
    hspec = _lat_spectrum(ah.reshape(2, n1, n2, HY_ORDER * 2 * HY_W), gtab, s)
    view = lambda a: a.reshape(DEC_BATCH, n1 // 2, n2 * HY_W)
    z = view(v)
    gates = (view(x1), view(x2))
    for o in range(HY_ORDER):
        a = _dft_a(wa, z)
        bp = _bhb(a.reshape(DEC_BATCH, 2, n1, n2, HY_W), gtab, gitab, hspec, o)
        bias_t = jnp.tile(p["bias"][o], n2)[None, :]
        z = _dft_ainv_gate(wainv, bp.reshape(DEC_BATCH, 2 * n1, n2 * HY_W), z, gates[o], bias_t,
                           F32 if o + 1 < HY_ORDER else BF16)
    return z.reshape(M_LAT, HY_W)


def _stage3_kernel(x_ref, mod_ref, g1_ref, wg_ref, ylru_ref, ymla_ref, yhy_ref, wl_ref, wm_ref, wh_ref, wo_ref,
                   g2_ref, xo_ref, xm2_ref):
    x = x_ref[...]
    xm = _rms(x, g1_ref[...]) * (1.0 + mod_ref[1:2, :]) + mod_ref[0:1, :]
    xb = xm.astype(BF16)
    merged = None
    for bidx, (y_ref, w_ref) in enumerate(((ylru_ref, wl_ref), (ymla_ref, wm_ref), (yhy_ref, wh_ref))):
        gate = _sigmoid(_dot(xb, wg_ref[:, D_MODEL * bidx:D_MODEL * (bidx + 1)]))
        term = gate * _dot(y_ref[...], w_ref[...])
        merged = term if merged is None else merged + term
    xo = x + mod_ref[2:3, :] * _dot(merged.astype(BF16), wo_ref[...])
    xo_ref[...] = xo
    xm2 = _rms(xo, g2_ref[...]) * (1.0 + mod_ref[4:5, :]) + mod_ref[3:4, :]
    xm2_ref[...] = xm2.astype(BF16)


def _stage3(x, modl, g1, wg, ylru, ymla, yhy, wl, wm, wh, wo, g2):
    tm = TM3
    full = lambda shape: pl.BlockSpec(shape, lambda i: (0,) * len(shape))
    row = lambda cols: pl.BlockSpec((tm, cols), lambda i: (i, 0))
    return pl.pallas_call(
        _stage3_kernel,
        grid=(M_TOK // tm,),
        in_specs=[row(D_MODEL),
                  pl.BlockSpec((None, SUBLANES, D_MODEL), lambda i: (_mod_row(i, tm), 0, 0)),
                  full((1, D_MODEL)), full((D_MODEL, 3 * D_MODEL)),
                  row(LRU_W), row(N_HEADS * D_V), row(HY_W),
                  full((LRU_W, D_MODEL)), full((N_HEADS * D_V, D_MODEL)), full((HY_W, D_MODEL)),
                  full((D_MODEL, D_MODEL)), full((1, D_MODEL))],
        out_specs=[row(D_MODEL), row(D_MODEL)],
        out_shape=[jax.ShapeDtypeStruct((M_TOK, D_MODEL), F32), jax.ShapeDtypeStruct((M_TOK, D_MODEL), BF16)],
        compiler_params=_cparams(("arbitrary",)),
        name="stage3",
    )(x, modl, g1, wg, ylru, ymla, yhy, wl, wm, wh, wo, g2)


def _ffn_kernel(xm_ref, x_ref, mod_ref, wg_ref, wu_ref, wd_ref, o_ref, acc_sc, *, nchunks):
    j = pl.program_id(1)
    xb = xm_ref[...]
    g = _dot(xb, wg_ref[...])
    u = _dot(xb, wu_ref[...])
    hid = (g * _sigmoid(g) * u).astype(BF16)
    part = _dot(hid, wd_ref[...])

    @pl.when(j == 0)
    def _():
        acc_sc[...] = part

    @pl.when(j > 0)
    def _():
        acc_sc[...] = acc_sc[...] + part

    @pl.when(j == nchunks - 1)
    def _():
        o_ref[...] = x_ref[...] + mod_ref[5:6, :] * acc_sc[...]


def _ffn_dense(xm2, x, modl, wg, wu, wd):
    tm = TM_FFN
    nchunks = 2
    cw = D_FF // nchunks
    return pl.pallas_call(
        functools.partial(_ffn_kernel, nchunks=nchunks),
        grid=(M_TOK // tm, nchunks),
        in_specs=[pl.BlockSpec((tm, D_MODEL), lambda i, j: (i, 0)),
                  pl.BlockSpec((tm, D_MODEL), lambda i, j: (i, 0)),
                  pl.BlockSpec((None, SUBLANES, D_MODEL), lambda i, j: (_mod_row(i, tm), 0, 0)),
                  pl.BlockSpec((D_MODEL, cw), lambda i, j: (0, j)),
                  pl.BlockSpec((D_MODEL, cw), lambda i, j: (0, j)),
                  pl.BlockSpec((cw, D_MODEL), lambda i, j: (j, 0))],
        out_specs=pl.BlockSpec((tm, D_MODEL), lambda i, j: (i, 0)),
        out_shape=jax.ShapeDtypeStruct((M_TOK, D_MODEL), F32),
        scratch_shapes=[pltpu.VMEM((tm, D_MODEL), F32)],
        compiler_params=_cparams(("arbitrary", "arbitrary")),
        name="ffn_dense",
    )(xm2, x, modl, wg, wu, wd)


MOE_TILE = 256
MOE_ROWS = 2 * M_TOK + N_EXPERTS * MOE_TILE
MOE_TILES = MOE_ROWS // MOE_TILE
MOE_PAD_ROWS = MOE_ROWS - 2 * M_TOK


def _router_kernel(xm_ref, wr_ref, tri_ref, sel_ref, xp_ref, cnt_ref, base_sc):
    i = pl.program_id(0)
    xb = xm_ref[...]
    tm = xb.shape[0]
    lane = lax.broadcasted_iota(jnp.int32, (tm, LANES), 1)

    @pl.when(i == 0)
    def _():
        base_sc[...] = jnp.zeros_like(base_sc)

    logits = jnp.where(lane < N_EXPERTS, _dot(xb, wr_ref[...]), -1e30)
    mx = jnp.max(logits, axis=-1, keepdims=True)
    ex = jnp.exp(logits - mx)
    probs = ex / jnp.sum(ex, axis=-1, keepdims=True)
    p1 = jnp.max(probs, axis=-1, keepdims=True)
    i1 = jnp.min(jnp.where(probs == p1, lane, LANES), axis=-1, keepdims=True)
    rest = jnp.where(lane == i1, -1.0, probs)
    p2 = jnp.max(rest, axis=-1, keepdims=True)
    i2 = jnp.min(jnp.where(rest == p2, lane, LANES), axis=-1, keepdims=True)
    tot = p1 + p2
    oh1 = lane == i1
    oh2 = lane == i2
    oh = jnp.where(oh1 | oh2, 1.0, 0.0)
    before = base_sc[0:1, :] + _dot(tri_ref[...], oh.astype(BF16))
    r1 = jnp.sum(jnp.where(oh1, before, 0.0), axis=-1, keepdims=True)
    r2 = jnp.sum(jnp.where(oh2, before, 0.0), axis=-1, keepdims=True)
    base_sc[0:1, :] = base_sc[0:1, :] + jnp.sum(oh, axis=0, keepdims=True)
    cnt_ref[...] = base_sc[...]
    sel = jnp.where(lane == 0, p1 / tot, 0.0) + jnp.where(lane == 1, p2 / tot, 0.0)
    sel = sel + jnp.where(lane == 2, i1.astype(F32), 0.0) + jnp.where(lane == 3, i2.astype(F32), 0.0)
    sel_ref[...] = sel + jnp.where(lane == 4, r1, 0.0) + jnp.where(lane == 5, r2, 0.0)
    xp_ref[...] = xb.astype(F32)


def _moe_router(xm2, wr):
    tm = TM_FFN
    tri = jnp.tril(jnp.ones((tm, tm), F32), -1).astype(BF16)
    return pl.pallas_call(
        _router_kernel,
        grid=(M_TOK // tm,),
        in_specs=[pl.BlockSpec((tm, D_MODEL), lambda i: (i, 0)),
                  pl.BlockSpec((D_MODEL, LANES), lambda i: (0, 0)),
                  pl.BlockSpec((tm, tm), lambda i: (0, 0))],
        out_specs=[pl.BlockSpec((tm, LANES), lambda i: (i, 0)),
                   pl.BlockSpec((tm, D_MODEL), lambda i: (i, 0)),
                   pl.BlockSpec((SUBLANES, LANES), lambda i: (0, 0))],
        out_shape=[jax.ShapeDtypeStruct((M_TOK, LANES), F32),
                   jax.ShapeDtypeStruct((M_TOK, D_MODEL), F32),
                   jax.ShapeDtypeStruct((SUBLANES, LANES), F32)],
        scratch_shapes=[pltpu.VMEM((SUBLANES, LANES), F32)],
        compiler_params=_cparams(("arbitrary",)),
        name="moe_router",
    )(xm2, wr, tri)


def _row_copy(src, srow, dst, drow, sem):
    return pltpu.make_async_copy(src.at[pl.ds(srow, 1), :], dst.at[pl.ds(drow, 1), :], sem)


def _dispatch_kernel(pos_ref, pad_ref, xp_ref, xs_ref, ring_sc, zero_sc, sem, *, nsteps):
    i = pl.program_id(0)
    slot = i % 2
    nrow = MOE_TILE

    def wait_slot(s):
        for _ in range(2):
            pltpu.make_async_copy(ring_sc.at[s], xs_ref.at[pl.ds(0, nrow), :], sem.at[s]).wait()

    @pl.when(i >= 2)
    def _():
        wait_slot(slot)

    ring_sc[slot] = xp_ref[...]

    def body(t, c):
        _row_copy(ring_sc.at[slot], t, xs_ref, pos_ref[0, 2 * t], sem.at[slot]).start()
        _row_copy(ring_sc.at[slot], t, xs_ref, pos_ref[0, 2 * t + 1], sem.at[slot]).start()
        return c

    lax.fori_loop(0, nrow, body, 0, unroll=4)

    @pl.when(i == nsteps - 1)
    def _():
        zero_sc[...] = jnp.zeros_like(zero_sc)

        def zbody(t, c):
            _row_copy(zero_sc, 0, xs_ref, pad_ref[t], sem.at[2]).start()
            return c

        lax.fori_loop(0, MOE_PAD_ROWS, zbody, 0, unroll=4)
        wait_slot(1 - slot)
        wait_slot(slot)

        for _ in range(MOE_PAD_ROWS // nrow):
            pltpu.make_async_copy(ring_sc.at[0], xs_ref.at[pl.ds(0, nrow), :], sem.at[2]).wait()


def _moe_dispatch(xp, pos, padrows):
    nsteps = M_TOK // MOE_TILE
    return pl.pallas_call(
        functools.partial(_dispatch_kernel, nsteps=nsteps),
        grid=(nsteps,),
        in_specs=[pl.BlockSpec((None, 1, 2 * MOE_TILE), lambda i: (i, 0, 0), memory_space=pltpu.SMEM),
                  pl.BlockSpec(memory_space=pltpu.SMEM),
                  pl.BlockSpec((MOE_TILE, D_MODEL), lambda i: (i, 0))],
        out_specs=pl.BlockSpec(memory_space=pl.ANY),
        out_shape=jax.ShapeDtypeStruct((MOE_ROWS, D_MODEL), F32),
        scratch_shapes=[pltpu.VMEM((2, MOE_TILE, D_MODEL), F32), pltpu.VMEM((SUBLANES, D_MODEL), F32),
                        pltpu.SemaphoreType.DMA((3,))],
        compiler_params=_cparams(("arbitrary",)),
        name="moe_dispatch",
    )(pos, padrows, xp)


def _experts_kernel(te_ref, nu_ref, xs_ref, wg_ref, wu_ref, wd_ref, ys_ref):
    i = pl.program_id(0)

    @pl.when(i < nu_ref[0])
    def _():
        xb = xs_ref[...].astype(BF16)
        g = _dot(xb, wg_ref[...])
        u = _dot(xb, wu_ref[...])
        hid = (g * _sigmoid(g) * u).astype(BF16)
        ys_ref[...] = _dot(hid, wd_ref[...])

    @pl.when(i >= nu_ref[0])
    def _():
        ys_ref[...] = jnp.zeros_like(ys_ref)


def _moe_experts(tile_expert, n_used, xs, wg, wu, wd):
    grid_spec = pltpu.PrefetchScalarGridSpec(
        num_scalar_prefetch=2,
        grid=(MOE_TILES,),
        in_specs=[pl.BlockSpec((MOE_TILE, D_MODEL), lambda i, te, nu: (i, 0)),
                  pl.BlockSpec((None, D_MODEL, D_FF_E), lambda i, te, nu: (te[i], 0, 0)),
                  pl.BlockSpec((None, D_MODEL, D_FF_E), lambda i, te, nu: (te[i], 0, 0)),
                  pl.BlockSpec((None, D_FF_E, D_MODEL), lambda i, te, nu: (te[i], 0, 0))],
        out_specs=pl.BlockSpec((MOE_TILE, D_MODEL), lambda i, te, nu: (i, 0)),
    )
    return pl.pallas_call(
        _experts_kernel,
        grid_spec=grid_spec,
        out_shape=jax.ShapeDtypeStruct((MOE_ROWS, D_MODEL), F32),
        compiler_params=_cparams(("arbitrary",)),
        name="moe_experts",
    )(tile_expert, n_used, xs, wg, wu, wd)


def _combine_kernel(pos_ref, x_ref, mod_ref, sel_ref, ys_ref, o_ref, buf_sc, sem, *, nsteps):
    i = pl.program_id(0)
    slot = i % 2
    nrow = MOE_TILE

    def start(s, off):
        def body(t, c):
            _row_copy(ys_ref, pos_ref[0, off + 2 * t], buf_sc.at[s, 0], t, sem.at[s]).start()
            _row_copy(ys_ref, pos_ref[0, off + 2 * t + 1], buf_sc.at[s, 1], t, sem.at[s]).start()
            return c
        lax.fori_loop(0, nrow, body, 0, unroll=4)

    @pl.when(i == 0)
    def _():
        start(0, 0)

    @pl.when(i + 1 < nsteps)
    def _():
        start(1 - slot, 2 * nrow)

    for k in range(2):
        pltpu.make_async_copy(ys_ref.at[pl.ds(0, nrow), :], buf_sc.at[slot, k], sem.at[slot]).wait()
    lane = lax.broadcasted_iota(jnp.int32, (nrow, LANES), 1)
    sel = sel_ref[...]
    w1 = jnp.sum(jnp.where(lane == 0, sel, 0.0), axis=-1, keepdims=True)
    w2 = jnp.sum(jnp.where(lane == 1, sel, 0.0), axis=-1, keepdims=True)
    y = w1 * buf_sc[slot, 0] + w2 * buf_sc[slot, 1]
    o_ref[...] = x_ref[...] + mod_ref[5:6, :] * y


def _moe_combine(pos2, x, modl, sel, ys):
    nsteps = M_TOK // MOE_TILE
    tm = MOE_TILE
    return pl.pallas_call(
        functools.partial(_combine_kernel, nsteps=nsteps),
        grid=(nsteps,),
        in_specs=[pl.BlockSpec((None, 1, 4 * MOE_TILE), lambda i: (i, 0, 0), memory_space=pltpu.SMEM),
                  pl.BlockSpec((tm, D_MODEL), lambda i: (i, 0)),
                  pl.BlockSpec((None, SUBLANES, D_MODEL), lambda i: (_mod_row(i, tm), 0, 0)),
                  pl.BlockSpec((tm, LANES), lambda i: (i, 0)),
                  pl.BlockSpec(memory_space=pl.ANY)],
        out_specs=pl.BlockSpec((tm, D_MODEL), lambda i: (i, 0)),
        out_shape=jax.ShapeDtypeStruct((M_TOK, D_MODEL), F32),
        scratch_shapes=[pltpu.VMEM((2, 2, MOE_TILE, D_MODEL), F32), pltpu.SemaphoreType.DMA((2,))],
        compiler_params=_cparams(("arbitrary",)),
        name="moe_combine",
    )(pos2, x, modl, sel, ys)


def _ffn_moe(xm2, x, modl, wr, wg, wu, wd):
    sel, xp, cnt = _moe_router(xm2, wr)
    counts = cnt[0, :N_EXPERTS].astype(jnp.int32)
    padded = ((counts + MOE_TILE - 1) // MOE_TILE) * MOE_TILE
    ends = jnp.cumsum(padded)
    offs = ends - padded
    experts = sel[:, 2:4].astype(jnp.int32)
    ranks = sel[:, 4:6].astype(jnp.int32)
    pos = offs[experts] + ranks
    tile_start = jnp.arange(MOE_TILES, dtype=jnp.int32) * MOE_TILE
    tile_expert = jnp.minimum(jnp.sum(tile_start[:, None] >= ends[None, :], axis=1), N_EXPERTS - 1)
    n_used = (ends[-1] // MOE_TILE).astype(jnp.int32)[None]
    rows = jnp.arange(MOE_ROWS, dtype=jnp.int32)
    row_expert = jnp.repeat(tile_expert, MOE_TILE)
    written = (rows < ends[-1]) & (rows - offs[row_expert] < counts[row_expert])
    padrows = jnp.nonzero(~written, size=MOE_PAD_ROWS)[0].astype(jnp.int32)
    pos_tiles = pos.reshape(M_TOK // MOE_TILE, 1, 2 * MOE_TILE)
    xs = _moe_dispatch(xp, pos_tiles, padrows)
    ys = _moe_experts(tile_expert.astype(jnp.int32), n_used, xs, wg, wu, wd)
    nxt = jnp.concatenate([pos_tiles[1:], pos_tiles[-1:]], axis=0)
    pos2 = jnp.concatenate([pos_tiles, nxt], axis=2)
    return _moe_combine(pos2, x, modl, sel, ys)


def _block_diag(w):
    nb, bs, _ = w.shape
    eye = jnp.eye(nb, dtype=w.dtype)
    return jnp.einsum("njk,nm->njmk", w, eye).reshape(nb * bs, nb * bs)


def _head_pad_cols(w, width):
    r = w.shape[0]
    return jnp.pad(w, ((0, 0), (0, 0), (0, HEAD_PAD - width))).reshape(r, N_HEADS * HEAD_PAD)


def _swap_rope_pairs(a):
    nope, rope = a[..., :D_NOPE], a[..., D_NOPE:]
    sw = rope.reshape(rope.shape[:-1] + (D_ROPE // 2, 2))[..., ::-1].reshape(rope.shape)
    return jnp.concatenate([nope, sw], axis=-1)


def _head_gain(g):
    rows = jnp.stack([g, _swap_rope_pairs(g)], axis=0)
    return jnp.pad(rows, ((0, SUBLANES - 2), (0, HEAD_PAD - D_QK)))


def _rope_tables(tm):
    rows = DEC_SEQ // GRID_W
    row = jnp.repeat(jnp.arange(rows, dtype=F32), GRID_W)
    col = jnp.tile(jnp.arange(GRID_W, dtype=F32), rows)
    half = D_ROPE // 2
    inv_freq = ROPE_BASE ** (-jnp.arange(0, half, 2, dtype=F32) / half)
    ang = jnp.concatenate([row[:, None] * inv_freq, col[:, None] * inv_freq], axis=-1)
    cos, sin = jnp.cos(ang), jnp.sin(ang)
    cos2 = jnp.repeat(cos, 2, axis=1)
    sin2 = jnp.stack([-sin, sin], axis=-1).reshape(DEC_SEQ, D_ROPE)
    cos_t = jnp.pad(cos2, ((0, 0), (D_NOPE, HEAD_PAD - D_QK)), constant_values=1.0)
    sin_t = jnp.pad(sin2, ((0, 0), (D_NOPE, HEAD_PAD - D_QK)))
    cos_t = jnp.concatenate([jnp.ones((tm, HEAD_PAD), F32), cos_t], axis=0)
    sin_t = jnp.concatenate([jnp.zeros((tm, HEAD_PAD), F32), sin_t], axis=0)
    return cos_t, sin_t


def kernel(x_prompt, x_sample, cache_ckv, cache_krope, state_lru, c, c_ctx, norm1, norm2, w_ada, b_ada, w_in, mla_q_norm, mla_kv_norm, mla_w_uq, mla_w_uk, mla_w_uv, mla_q_qknorm, mla_k_qknorm, lru_conv_w, lru_conv_b, lru_w_gate, lru_b_gate, lru_lambda, hy_conv_w, hy_conv_b, hy_w1, hy_b1, hy_freq1, hy_w2, hy_b2, hy_freq2, hy_w3, hy_bias, w_lru_out, w_mla_out, w_hy_out, w_out, ffn_w_gate, ffn_w_up, ffn_w_down, moe_w_router, moe_w_gate, moe_w_up, moe_w_down):
    x = jnp.concatenate([x_prompt.reshape(M_CTX, D_MODEL), x_sample.reshape(M_LAT, D_MODEL)], axis=0)

    cond = jnp.concatenate([c_ctx[None, :], c, jnp.zeros((SUBLANES - 1 - DEC_BATCH, D_MODEL), F32)], axis=0)
    mod = _adaln(cond, w_ada, b_ada).reshape(DEPTH, SUBLANES, 6, D_MODEL)
    mod = jnp.pad(mod, ((0, 0), (0, 0), (0, SUBLANES - 6), (0, 0)))

    cos_t, sin_t = _rope_tables(TM1)
    ctx_tabs = _ctx_tables()
    lat_tabs = _lat_tables()
    zero_state = jnp.zeros((BATCH, SUBLANES, LRU_W), F32)

    ckv_out, kr_out, st_out = [], [], []
    for l in range(DEPTH):
        wl = w_in[l]
        wkr = jnp.concatenate([jnp.zeros((D_MODEL, D_NOPE), F32), wl[:, 896:928]], axis=1)
        krblk = lambda w: jnp.pad(w, ((0, 0), (0, HEAD_PAD - D_QK)))
        w1 = jnp.concatenate([wl[:, :896], krblk(wkr), krblk(_swap_rope_pairs(wkr)), wl[:, 928:2464]],
                             axis=1).astype(BF16)
        wgates = wl[:, 2464:].astype(BF16)
        wuq = _head_pad_cols(mla_w_uq[l], D_QK).astype(BF16)
        wuqs = _head_pad_cols(_swap_rope_pairs(mla_w_uq[l]), D_QK).astype(BF16)
        wuk = _head_pad_cols(mla_w_uk[l], D_NOPE).astype(BF16)
        wuv = mla_w_uv[l].reshape(KV_RANK, N_HEADS * D_V).astype(BF16)
        gq = _head_gain(mla_q_qknorm[l])
        gk = _head_gain(mla_k_qknorm[l])

        ulru, uhy, ckv, krb, q, k, v = _stage1(
            x, mod[l], norm1[l][None, :], w1, mla_kv_norm[l][None, :], mla_q_norm[l][None, :],
            wuq, wuqs, gq, wuk, gk, wuv, cos_t, sin_t)
        ckv_out.append(ckv[:M_CTX].reshape(BATCH, SEQ, KV_RANK))
        kr_out.append(krb[:M_CTX, D_NOPE:D_QK].reshape(BATCH, SEQ, D_ROPE))

        kc, vc = _kvprep(cache_ckv[:, l].reshape(DEC_BATCH * PAST_LEN, KV_RANK),
                         jnp.pad(cache_krope[:, l].reshape(DEC_BATCH * PAST_LEN, D_ROPE),
                                 ((0, 0), (D_NOPE, HEAD_PAD - D_QK))),
                         wuk, gk, wuv)
        ymla = jnp.concatenate([_attention_ctx(q, k, v), _attention_lat(q, k, v, kc, vc)], axis=0)

        lp = dict(
            cw=jnp.pad(lru_conv_w[l], ((0, SUBLANES - 4), (0, 0))), cb=lru_conv_b[l][None, :],
            wr=[_block_diag(lru_w_gate[l, d, 0]).astype(BF16) for d in range(2)],
            wi=[_block_diag(lru_w_gate[l, d, 1]).astype(BF16) for d in range(2)],
            br=[lru_b_gate[l, d, 0][None, :] for d in range(2)],
            bi=[lru_b_gate[l, d, 1][None, :] for d in range(2)],
            lam=[lru_lambda[l, d][None, :] for d in range(2)])
        y_c, stf, stb = _lru_mixer(ulru, lp, (zero_state, zero_state), row_off=0, nseq=BATCH, seqlen=SEQ, tc=SEQ)
        st_out.append(jnp.stack([stf, stb], axis=1))
        h0 = [jnp.broadcast_to(state_lru[:, l, d][:, None, :], (DEC_BATCH, SUBLANES, LRU_W)) for d in range(2)]
        y_l, _, _ = _lru_mixer(ulru, lp, h0, row_off=M_CTX, nseq=DEC_BATCH, seqlen=DEC_SEQ, tc=512)
        ylru = jnp.concatenate([y_c, y_l], axis=0)

        hp = dict(
            cw=jnp.pad(hy_conv_w[l], ((0, SUBLANES - 3), (0, 0))), cb=hy_conv_b[l][None, :],
            w1=jnp.pad(hy_w1[l], ((0, LANES - HY_EMB), (0, 0))).astype(BF16), b1=hy_b1[l][None, :],
            f1=hy_freq1[l][None, :], w2=hy_w2[l].astype(BF16), b2=hy_b2[l][None, :], f2=hy_freq2[l][None, :],
            w3=hy_w3[l].astype(BF16), bias=hy_bias[l])
        yhy = jnp.concatenate([_hyena_ctx(uhy, hp, ctx_tabs), _hyena_lat(uhy, hp, lat_tabs)], axis=0)

        x, xm2 = _stage3(x, mod[l], norm1[l][None, :], wgates, ylru, ymla, yhy,
                         w_lru_out[l].astype(BF16), w_mla_out[l].astype(BF16), w_hy_out[l].astype(BF16),
                         w_out[l].astype(BF16), norm2[l][None, :])
        j = l // 2
        if l % 2 == 0:
            x = _ffn_dense(xm2, x, mod[l], ffn_w_gate[j].astype(BF16), ffn_w_up[j].astype(BF16),
                           ffn_w_down[j].astype(BF16))
        else:
            wr = jnp.pad(moe_w_router[j], ((0, 0), (0, LANES - N_EXPERTS))).astype(BF16)
            x = _ffn_moe(xm2, x, mod[l], wr, moe_w_gate[j].astype(BF16), moe_w_up[j].astype(BF16),
                         moe_w_down[j].astype(BF16))

    y_prompt = x[:M_CTX].reshape(BATCH, SEQ, D_MODEL)
    y_sample = x[M_CTX:].reshape(DEC_BATCH, DEC_SEQ, D_MODEL)
    return (y_prompt, y_sample, jnp.stack(ckv_out, axis=1), jnp.stack(kr_out, axis=1), jnp.stack(st_out, axis=1))
```

```python
import functools
import math

import jax
import jax.numpy as jnp
from jax import lax
from jax.experimental import pallas as pl
from jax.experimental.pallas import tpu as pltpu

F32 = jnp.float32
BF16 = jnp.bfloat16

D_MODEL = 1024
BATCH = 32
SEQ = 256
DEPTH = 2
DEC_BATCH = 2
DEC_SEQ = 4096
PAST_LEN = 512
GRID_W = 64
EPS = 1e-6
LRU_W = 512
LRU_BLOCKS = 8
LRU_C = 8.0
N_HEADS = 8
D_NOPE = 64
D_ROPE = 32
D_QK = D_NOPE + D_ROPE
D_V = 64
Q_RANK = 256
KV_RANK = 128
ROPE_BASE = 10000.0
HY_W = 512
HY_ORDER = 2
HY_EMB = 33
HY_HID = 64
HY_FAST_PCT = 0.3
HY_SLOW_PCT = 1.5
D_FF = 2816
N_EXPERTS = 8
D_FF_E = 1408

LANES = 128
SUBLANES = 8
VMEM_LIMIT = 56 * 1024 * 1024

M_CTX = BATCH * SEQ
M_LAT = DEC_BATCH * DEC_SEQ
M_TOK = M_CTX + M_LAT
TM1 = 256
TM3 = 512
TM_FFN = 512
W1_COLS = 2688
HEAD_PAD = LANES
QK_SCALE = math.log2(math.e) / math.sqrt(D_QK)
ATTN_KEY_CHUNK = 512

FFT_N1 = 64
FFT_N2 = 128


def _cparams(sem, vmem=VMEM_LIMIT):
    return pltpu.CompilerParams(dimension_semantics=sem, vmem_limit_bytes=vmem)


def _dot(a, b):
    return jnp.dot(a, b, preferred_element_type=F32)


def _rms(x, g):
    ms = jnp.mean(x * x, axis=-1, keepdims=True)
    return x * lax.rsqrt(ms + EPS) * g


def _sigmoid(x):
    return 1.0 / (1.0 + jnp.exp(-x))


def _ada_kernel(c_ref, w_ref, b_ref, o_ref):
    c = c_ref[...]
    s = (c * _sigmoid(c)).astype(BF16)
    o_ref[...] = _dot(s, w_ref[...].astype(BF16)) + b_ref[...]


def _adaln(cond, w_ada, b_ada):
    tn = 1024
    n6 = 6 * D_MODEL
    return pl.pallas_call(
        _ada_kernel,
        grid=(DEPTH, n6 // tn),
        in_specs=[
            pl.BlockSpec((SUBLANES, D_MODEL), lambda l, j: (0, 0)),
            pl.BlockSpec((None, D_MODEL, tn), lambda l, j: (l, 0, j)),
            pl.BlockSpec((None, 1, tn), lambda l, j: (l, 0, j)),
        ],
        out_specs=pl.BlockSpec((None, SUBLANES, tn), lambda l, j: (l, 0, j)),
        out_shape=jax.ShapeDtypeStruct((DEPTH, SUBLANES, n6), F32),
        compiler_params=_cparams(("arbitrary", "arbitrary")),
        name="adaln",
    )(cond, w_ada, b_ada.reshape(DEPTH, 1, n6))


def _mod_row(i, tm):
    nctx = M_CTX // tm
    per = DEC_SEQ // tm
    return jnp.where(i < nctx, 0, 1 + (i - nctx) // per)


def _rope_blk(i, tm):
    nctx = M_CTX // tm
    per = DEC_SEQ // tm
    return jnp.where(i < nctx, 0, 1 + (i - nctx) % per)


def _finish_head(raw, raw_sw, gc, gs, out_ref, sl):
    ms = jnp.sum(raw * raw, axis=-1, keepdims=True) * (1.0 / D_QK)
    rs = lax.rsqrt(ms + EPS)
    val = raw * gc
    if gs is not None:
        val = val + raw_sw * gs
    out_ref[:, sl] = (val * rs).astype(BF16)


def _stage1_kernel(x_ref, mod_ref, g1_ref, w1_ref, gkv_ref, gqn_ref, wuq_ref, wuqs_ref, gq_ref, wuk_ref, gk_ref,
                   wuv_ref, cos_ref, sin_ref,
                   ulru_ref, uhy_ref, ckv_ref, krb_ref, q_ref, k_ref, v_ref):
    x = x_ref[...]
    xm = _rms(x, g1_ref[...]) * (1.0 + mod_ref[1:2, :]) + mod_ref[0:1, :]
    xb = xm.astype(BF16)
    ulru_ref[...] = _dot(xb, w1_ref[:, 0:512])
    qc = _dot(xb, w1_ref[:, 512:768])
    ckv = _dot(xb, w1_ref[:, 768:896])
    krb = _dot(xb, w1_ref[:, 896:1024])
    krs = _dot(xb, w1_ref[:, 1024:1152])
    uhy_ref[...] = _dot(xb, w1_ref[:, 1152:2688])
    ckvn = _rms(ckv, gkv_ref[...])
    ckv_ref[...] = ckvn
    krb_ref[...] = krb
    qn = _rms(qc, gqn_ref[...]).astype(BF16)
    cb = ckvn.astype(BF16)
    v_ref[...] = _dot(cb, wuv_ref[...]).astype(BF16)
    cos = cos_ref[...]
    sin = sin_ref[...]
    gcq = cos * (gq_ref[0:1, :] * QK_SCALE)
    gsq = sin * (gq_ref[1:2, :] * QK_SCALE)
    gck = cos * gk_ref[0:1, :]
    gsk = sin * gk_ref[1:2, :]
    for h in range(N_HEADS):
        sl = slice(HEAD_PAD * h, HEAD_PAD * (h + 1))
        _finish_head(_dot(qn, wuq_ref[:, sl]), _dot(qn, wuqs_ref[:, sl]), gcq, gsq, q_ref, sl)
        _finish_head(_dot(cb, wuk_ref[:, sl]) + krb, krs, gck, gsk, k_ref, sl)


def _stage1(x, modl, g1, w1, gkv, gqn, wuq, wuqs, gq, wuk, gk, wuv, cos_t, sin_t):
    tm = TM1
    full = lambda shape: pl.BlockSpec(shape, lambda i: (0,) * len(shape))
    row = lambda cols: pl.BlockSpec((tm, cols), lambda i: (i, 0))
    hw = N_HEADS * HEAD_PAD
    return pl.pallas_call(
        _stage1_kernel,
        grid=(M_TOK // tm,),
        in_specs=[
            row(D_MODEL),
            pl.BlockSpec((None, SUBLANES, D_MODEL), lambda i: (_mod_row(i, tm), 0, 0)),
            full((1, D_MODEL)),
            full((D_MODEL, W1_COLS)),
            full((1, KV_RANK)),
            full((1, Q_RANK)),
            full((Q_RANK, hw)),
            full((Q_RANK, hw)),
            full((SUBLANES, HEAD_PAD)),
            full((KV_RANK, hw)),
            full((SUBLANES, HEAD_PAD)),
            full((KV_RANK, N_HEADS * D_V)),
            pl.BlockSpec((tm, LANES), lambda i: (_rope_blk(i, tm), 0)),
            pl.BlockSpec((tm, LANES), lambda i: (_rope_blk(i, tm), 0)),
        ],
        out_specs=[row(LRU_W), row(3 * HY_W), row(KV_RANK), row(LANES), row(hw), row(hw), row(N_HEADS * D_V)],
        out_shape=[
            jax.ShapeDtypeStruct((M_TOK, LRU_W), F32),
            jax.ShapeDtypeStruct((M_TOK, 3 * HY_W), F32),
            jax.ShapeDtypeStruct((M_TOK, KV_RANK), F32),
            jax.ShapeDtypeStruct((M_TOK, LANES), F32),
            jax.ShapeDtypeStruct((M_TOK, hw), BF16),
            jax.ShapeDtypeStruct((M_TOK, hw), BF16),
            jax.ShapeDtypeStruct((M_TOK, N_HEADS * D_V), BF16),
        ],
        compiler_params=_cparams(("arbitrary",)),
        name="stage1",
    )(x, modl, g1, w1, gkv, gqn, wuq, wuqs, gq, wuk, gk, wuv, cos_t, sin_t)


def _kvprep_kernel(ckv_ref, krb_ref, wuk_ref, gk_ref, wuv_ref, k_ref, v_ref):
    cb = ckv_ref[...].astype(BF16)
    v_ref[...] = _dot(cb, wuv_ref[...]).astype(BF16)
    krb = krb_ref[...]
    for h in range(N_HEADS):
        sl = slice(HEAD_PAD * h, HEAD_PAD * (h + 1))
        _finish_head(_dot(cb, wuk_ref[:, sl]) + krb, None, gk_ref[0:1, :], None, k_ref, sl)


def _kvprep(ckv, krb, wuk, gk, wuv):
    rows = ckv.shape[0]
    tm = TM1
    hw = N_HEADS * HEAD_PAD
    full = lambda shape: pl.BlockSpec(shape, lambda i: (0,) * len(shape))
    row = lambda cols: pl.BlockSpec((tm, cols), lambda i: (i, 0))
    return pl.pallas_call(
        _kvprep_kernel,
        grid=(rows // tm,),
        in_specs=[row(KV_RANK), row(LANES), full((KV_RANK, hw)), full((SUBLANES, HEAD_PAD)),
                  full((KV_RANK, N_HEADS * D_V))],
        out_specs=[row(hw), row(N_HEADS * D_V)],
        out_shape=[jax.ShapeDtypeStruct((rows, hw), BF16), jax.ShapeDtypeStruct((rows, N_HEADS * D_V), BF16)],
        compiler_params=_cparams(("arbitrary",)),
        name="kvprep",
    )(ckv, krb, wuk, gk, wuv)


def _attn_kernel(*refs, heads, nseg):
    q_ref = refs[0]
    k_refs = refs[1:1 + nseg]
    v_refs = refs[1 + nseg:1 + 2 * nseg]
    o_ref = refs[1 + 2 * nseg]
    tq = q_ref.shape[0]
    lane = lax.broadcasted_iota(jnp.int32, (tq, LANES), 1)
    low = lane < D_V
    for pair in range(heads // 2):
        outs = []
        for j in range(2):
            h = 2 * pair + j
            sl = slice(HEAD_PAD * h, HEAD_PAD * (h + 1))
            q = q_ref[:, sl]
            s = [lax.dot_general(q, kr[:, sl], (((1,), (1,)), ((), ())), preferred_element_type=F32)
                 for kr in k_refs]
            m = jnp.max(s[0], axis=-1, keepdims=True)
            for si in s[1:]:
                m = jnp.maximum(m, jnp.max(si, axis=-1, keepdims=True))
            acc = None
            den = None
            for si, vr in zip(s, v_refs):
                p = jnp.exp2(si - m)
                d = jnp.sum(p, axis=-1, keepdims=True)
                o = _dot(p.astype(BF16), vr[:, LANES * pair:LANES * (pair + 1)])
                acc = o if acc is None else acc + o
                den = d if den is None else den + d
            outs.append(acc / den)
        o_ref[:, LANES * pair:LANES * (pair + 1)] = jnp.where(low, outs[0], outs[1]).astype(BF16)


def _attention_ctx(q, k, v):
    hw = N_HEADS * HEAD_PAD
    vw = N_HEADS * D_V
    return pl.pallas_call(
        functools.partial(_attn_kernel, heads=N_HEADS, nseg=1),
        grid=(BATCH,),
        in_specs=[
            pl.BlockSpec((SEQ, hw), lambda b: (b, 0)),
            pl.BlockSpec((SEQ, hw), lambda b: (b, 0)),
            pl.BlockSpec((SEQ, vw), lambda b: (b, 0)),
        ],
        out_specs=pl.BlockSpec((SEQ, vw), lambda b: (b, 0)),
        out_shape=jax.ShapeDtypeStruct((M_CTX, vw), BF16),
        compiler_params=_cparams(("arbitrary",)),
        name="attn_ctx",
    )(q, k, v)


def _attention_lat(q, k, v, kc, vc):
    tq = 256
    nq = DEC_SEQ // tq
    qoff = M_CTX // tq
    koff = M_CTX // DEC_SEQ
    return pl.pallas_call(
        functools.partial(_attn_kernel, heads=2, nseg=2),
        grid=(DEC_BATCH, N_HEADS // 2, nq),
        in_specs=[
            pl.BlockSpec((tq, 2 * HEAD_PAD), lambda b, p, i: (qoff + b * nq + i, p)),
            pl.BlockSpec((DEC_SEQ, 2 * HEAD_PAD), lambda b, p, i: (koff + b, p)),
            pl.BlockSpec((PAST_LEN, 2 * HEAD_PAD), lambda b, p, i: (b, p)),
            pl.BlockSpec((DEC_SEQ, 2 * D_V), lambda b, p, i: (koff + b, p)),
            pl.BlockSpec((PAST_LEN, 2 * D_V), lambda b, p, i: (b, p)),
        ],
        out_specs=pl.BlockSpec((tq, 2 * D_V), lambda b, p, i: (b * nq + i, p)),
        out_shape=jax.ShapeDtypeStruct((M_LAT, N_HEADS * D_V), BF16),
        compiler_params=_cparams(("arbitrary", "arbitrary", "arbitrary")),
        name="attn_lat",
    )(q, k, kc, v, vc)


def _lru_kernel(*refs, reverse, tc, nchunks):
    if reverse:
        (up_ref, uc_ref, un_ref, hf_ref, cw_ref, cb_ref, wr_ref, wi_ref, br_ref, bi_ref, lam_ref, h0_ref,
         y_ref, st_ref, ext_sc, a_sc, b_sc, h_sc, car_sc) = refs
    else:
        (up_ref, uc_ref, un_ref, cw_ref, cb_ref, wr_ref, wi_ref, br_ref, bi_ref, lam_ref, h0_ref,
         y_ref, st_ref, ext_sc, a_sc, b_sc, h_sc, car_sc) = refs
    c = pl.program_id(1)
    chunk = (nchunks - 1 - c) if reverse else c
    prev = jnp.where(chunk == 0, 0.0, up_ref[...])
    nxt = jnp.where(chunk == nchunks - 1, 0.0, un_ref[...])
    ext_sc[0:SUBLANES, :] = prev
    ext_sc[SUBLANES:SUBLANES + tc, :] = uc_ref[...]
    ext_sc[SUBLANES + tc:2 * SUBLANES + tc, :] = nxt
    xc = cb_ref[...]
    for k in range(4):
        xc = xc + cw_ref[k:k + 1, :] * ext_sc[SUBLANES - 2 + k:SUBLANES - 2 + k + tc, :]
    xb = xc.astype(BF16)
    r = _sigmoid(_dot(xb, wr_ref[...]) + br_ref[...])
    gi = _sigmoid(_dot(xb, wi_ref[...]) + bi_ref[...])
    lam = lam_ref[...]
    logsig = -(jnp.maximum(-lam, 0.0) + jnp.log1p(jnp.exp(-jnp.abs(lam))))
    la = LRU_C * r * logsig
    a = jnp.exp(la)
    a_sc[...] = a
    b_sc[...] = jnp.sqrt(-jnp.tanh(la) * (a * a + 1.0)) * (gi * xc)

    @pl.when(c == 0)
    def _():
        car_sc[...] = h0_ref[...]

    def body(j, h):
        t = (tc - 1 - j) if reverse else j
        h = a_sc[pl.ds(t, 1), :] * h + b_sc[pl.ds(t, 1), :]
        h_sc[pl.ds(t, 1), :] = h
        return h

    h = lax.fori_loop(0, tc, body, car_sc[0:1, :], unroll=8)
    car_sc[0:1, :] = h
    st_ref[...] = jnp.broadcast_to(h, (SUBLANES, LRU_W))
    if reverse:
        y_ref[...] = (hf_ref[...] + h_sc[...]).astype(BF16)
    else:
        y_ref[...] = h_sc[...]


def _lru_dir(u, hf, cw, cb, wr, wi, br, bi, lam, h0, *, reverse, row_off, nseq, seqlen, tc):
    nchunks = seqlen // tc
    hb = M_TOK // SUBLANES

    def chunk_of(c):
        return (nchunks - 1 - c) if reverse else c

    def cur(b, c):
        return ((row_off + b * seqlen) // tc + chunk_of(c), 0)

    def prv(b, c):
        return (jnp.maximum((row_off + b * seqlen + chunk_of(c) * tc) // SUBLANES - 1, 0), 0)

    def nxt(b, c):
        return (jnp.minimum((row_off + b * seqlen + (chunk_of(c) + 1) * tc) // SUBLANES, hb - 1), 0)

    def out_cur(b, c):
        return ((b * seqlen) // tc + chunk_of(c), 0)

    full = lambda shape: pl.BlockSpec(shape, lambda b, c: (0,) * len(shape))
    in_specs = [pl.BlockSpec((SUBLANES, LRU_W), prv), pl.BlockSpec((tc, LRU_W), cur),
                pl.BlockSpec((SUBLANES, LRU_W), nxt)]
    args = [u, u, u]
    if reverse:
        in_specs.append(pl.BlockSpec((tc, LRU_W), out_cur))
        args.append(hf)
    in_specs += [full((SUBLANES, LRU_W)), full((1, LRU_W)), full((LRU_W, LRU_W)), full((LRU_W, LRU_W)),
                 full((1, LRU_W)), full((1, LRU_W)), full((1, LRU_W)),
                 pl.BlockSpec((None, SUBLANES, LRU_W), lambda b, c: (b, 0, 0))]
    args += [cw, cb, wr, wi, br, bi, lam, h0]
    return pl.pallas_call(
        functools.partial(_lru_kernel, reverse=reverse, tc=tc, nchunks=nchunks),
        grid=(nseq, nchunks),
        in_specs=in_specs,
        out_specs=[pl.BlockSpec((tc, LRU_W), out_cur),
                   pl.BlockSpec((None, SUBLANES, LRU_W), lambda b, c: (b, 0, 0))],
        out_shape=[jax.ShapeDtypeStruct((nseq * seqlen, LRU_W), BF16 if reverse else F32),
                   jax.ShapeDtypeStruct((nseq, SUBLANES, LRU_W), F32)],
        scratch_shapes=[pltpu.VMEM((tc + 2 * SUBLANES, LRU_W), F32), pltpu.VMEM((tc, LRU_W), F32),
                        pltpu.VMEM((tc, LRU_W), F32), pltpu.VMEM((tc, LRU_W), F32),
                        pltpu.VMEM((SUBLANES, LRU_W), F32)],
        compiler_params=_cparams(("arbitrary", "arbitrary")),
        name="lru_bwd" if reverse else "lru_fwd",
    )(*args)


def _lru_mixer(u, p, h0, *, row_off, nseq, seqlen, tc):
    kw = dict(row_off=row_off, nseq=nseq, seqlen=seqlen, tc=tc)
    hf, stf = _lru_dir(u, None, p["cw"], p["cb"], p["wr"][0], p["wi"][0], p["br"][0], p["bi"][0], p["lam"][0],
                       h0[0], reverse=False, **kw)
    y, stb = _lru_dir(u, hf, p["cw"], p["cb"], p["wr"][1], p["wi"][1], p["br"][1], p["bi"][1], p["lam"][1],
                      h0[1], reverse=True, **kw)
    return y, stf[:, 0, :], stb[:, 0, :]


def _shortconv_kernel(up_ref, uc_ref, un_ref, cw_ref, cb_ref, v_ref, x1_ref, x2_ref, ext_sc, *, tc, nchunks):
    c = pl.program_id(1)
    prev = jnp.where(c == 0, 0.0, up_ref[...])
    nxt = jnp.where(c == nchunks - 1, 0.0, un_ref[...])
    ext_sc[0:SUBLANES, :] = prev
    ext_sc[SUBLANES:SUBLANES + tc, :] = uc_ref[...]
    ext_sc[SUBLANES + tc:2 * SUBLANES + tc, :] = nxt
    for part, o_ref in enumerate((v_ref, x1_ref, x2_ref)):
        cs = slice(HY_W * part, HY_W * (part + 1))
        acc = cb_ref[:, cs]
        for k in range(3):
            acc = acc + cw_ref[k:k + 1, cs] * ext_sc[SUBLANES - 1 + k:SUBLANES - 1 + k + tc, cs]
        o_ref[...] = acc


def _shortconv(u, cw, cb, *, row_off, nseq, seqlen, tc):
    nchunks = seqlen // tc
    w = 3 * HY_W
    hb = M_TOK // SUBLANES
    cur = lambda b, c: ((row_off + b * seqlen) // tc + c, 0)
    prv = lambda b, c: (jnp.maximum((row_off + b * seqlen + c * tc) // SUBLANES - 1, 0), 0)
    nxt = lambda b, c: (jnp.minimum((row_off + b * seqlen + (c + 1) * tc) // SUBLANES, hb - 1), 0)
    out = lambda b, c: ((b * seqlen) // tc + c, 0)
    full = lambda shape: pl.BlockSpec(shape, lambda b, c: (0,) * len(shape))
    rows = nseq * seqlen
    return pl.pallas_call(
        functools.partial(_shortconv_kernel, tc=tc, nchunks=nchunks),
        grid=(nseq, nchunks),
        in_specs=[pl.BlockSpec((SUBLANES, w), prv), pl.BlockSpec((tc, w), cur), pl.BlockSpec((SUBLANES, w), nxt),
                  full((SUBLANES, w)), full((1, w))],
        out_specs=[pl.BlockSpec((tc, HY_W), out)] * 3,
        out_shape=[jax.ShapeDtypeStruct((rows, HY_W), F32)] * 3,
        scratch_shapes=[pltpu.VMEM((tc + 2 * SUBLANES, w), F32)],
        compiler_params=_cparams(("arbitrary", "arbitrary")),
        name="hy_shortconv",
    )(u, u, u, cw, cb)


def _hyfilt_kernel(z_ref, t_ref, w1_ref, b1_ref, f1_ref, w2_ref, b2_ref, f2_ref, w3_ref, ad_ref, h_ref, s_ref):
    i = pl.program_id(0)
    z = z_ref[...].astype(BF16)
    h = jnp.sin(f1_ref[...] * (_dot(z, w1_ref[...]) + b1_ref[...]))
    h = jnp.sin(f2_ref[...] * (_dot(h.astype(BF16), w2_ref[...]) + b2_ref[...]))
    h = _dot(h.astype(BF16), w3_ref[...])
    t = t_ref[...]
    ncol = h.shape[1] // LANES
    win = jnp.concatenate([jnp.exp(-t * ad_ref[:, LANES * j:LANES * (j + 1)]) for j in range(ncol)], axis=1)
    h = h * win
    h_ref[...] = h

    @pl.when(i == 0)
    def _():
        s_ref[...] = jnp.zeros_like(s_ref)

    s_ref[0:1, :] = s_ref[0:1, :] + jnp.sum(jnp.abs(h), axis=0, keepdims=True)


def _hyfilt(feats, tcol, w1, b1, f1, w2, b2, f2, w3, absdelta):
    L = feats.shape[0]
    tl = min(L, 512)
    wcols = HY_ORDER * 2 * HY_W
    full = lambda shape: pl.BlockSpec(shape, lambda i: (0,) * len(shape))
    return pl.pallas_call(
        _hyfilt_kernel,
        grid=(L // tl,),
        in_specs=[pl.BlockSpec((tl, LANES), lambda i: (i, 0)), pl.BlockSpec((tl, LANES), lambda i: (i, 0)),
                  full((LANES, HY_HID)), full((1, HY_HID)), full((1, HY_HID)),
                  full((HY_HID, HY_HID)), full((1, HY_HID)), full((1, HY_HID)),
                  full((HY_HID, wcols)), full((1, wcols))],
        out_specs=[pl.BlockSpec((tl, wcols), lambda i: (i, 0)), full((SUBLANES, wcols))],
        out_shape=[jax.ShapeDtypeStruct((L, wcols), F32), jax.ShapeDtypeStruct((SUBLANES, wcols), F32)],
        compiler_params=_cparams(("arbitrary",)),
        name="hy_filter",
    )(feats, tcol, w1, b1, f1, w2, b2, f2, w3, absdelta)


def _combine_spectrum(zr, zi, s_ref, hr_out, hi_out):
    for o in range(HY_ORDER):
        f = slice(2 * HY_W * o, 2 * HY_W * o + HY_W)
        b = slice(2 * HY_W * o + HY_W, 2 * HY_W * (o + 1))
        den = s_ref[0:1, f] + s_ref[0:1, b] + EPS
        hr_out(o, (zr[:, f] + zr[:, b]) / den)
        hi_out(o, (zi[:, f] - zi[:, b]) / den)


def _ctx_spec_kernel(f_ref, h_ref, s_ref, o_ref):
    n = f_ref.shape[0] // 2
    z = _dot(f_ref[...], h_ref[...].astype(BF16))
    zr, zi = z[:n], z[n:]

    def put_r(o, val):
        o_ref[0, :, HY_W * o:HY_W * (o + 1)] = val

    def put_i(o, val):
        o_ref[1, :, HY_W * o:HY_W * (o + 1)] = val

    _combine_spectrum(zr, zi, s_ref, put_r, put_i)


def _ctx_spectrum(fmat, hdec, s):
    n = fmat.shape[0] // 2
    return pl.pallas_call(
        _ctx_spec_kernel,
        out_shape=jax.ShapeDtypeStruct((2, n, HY_ORDER * HY_W), F32),
        compiler_params=pltpu.CompilerParams(vmem_limit_bytes=VMEM_LIMIT),
        name="hy_ctx_spectrum",
    )(fmat, hdec, s)


def _ctx_conv_kernel(z_ref, x_ref, f_ref, fi_ref, h_ref, bias_ref, o_ref, *, nb, seqlen):
    n = f_ref.shape[0] // 2
    hr = h_ref[0]
    hi = h_ref[1]
    for b in range(nb):
        rs = slice(seqlen * b, seqlen * (b + 1))
        zt = z_ref[rs, :]
        zf = _dot(f_ref[...], zt.astype(BF16))
        zr, zi = zf[:n], zf[n:]
        y = jnp.concatenate([zr * hr - zi * hi, zr * hi + zi * hr], axis=0).astype(BF16)
        conv = _dot(fi_ref[...], y)
        o_ref[rs, :] = (x_ref[rs, :] * (conv + zt * bias_ref[...])).astype(o_ref.dtype)


def _ctx_conv(z, xg, fmat, finv, hspec, bias, order, out_dtype):
    nb = 4
    n = fmat.shape[0] // 2
    rows = nb * SEQ
    return pl.pallas_call(
        functools.partial(_ctx_conv_kernel, nb=nb, seqlen=SEQ),
        grid=(BATCH // nb,),
        in_specs=[pl.BlockSpec((rows, HY_W), lambda i: (i, 0)), pl.BlockSpec((rows, HY_W), lambda i: (i, 0)),
                  pl.BlockSpec(fmat.shape, lambda i: (0, 0)), pl.BlockSpec(finv.shape, lambda i: (0, 0)),
                  pl.BlockSpec((2, n, HY_W), lambda i: (0, 0, order)),
                  pl.BlockSpec((1, HY_W), lambda i: (0, 0))],
        out_specs=pl.BlockSpec((rows, HY_W), lambda i: (i, 0)),
        out_shape=jax.ShapeDtypeStruct((M_CTX, HY_W), out_dtype),
        compiler_params=_cparams(("arbitrary",)),
        name="hy_ctx_conv",
    )(z, xg, fmat, finv, hspec, bias)


K1U = FFT_N1 // 2 + 1
SLABS = 72
PITCH = FFT_N2 + SUBLANES
NROW1 = FFT_N1 // 2
LAT_UNROLL_R = 8
LAT_UNROLL_K = 3


def _pitch_copy_in(src_ref, col, dst_sc):
    for n1 in range(NROW1):
        dst_sc[PITCH * n1:PITCH * n1 + FFT_N2, :] = src_ref[FFT_N2 * n1:FFT_N2 * (n1 + 1), col]


def _dft_stage_a(zp_sc, wa_ref, a_sc):
    def body(r, c):
        x = zp_sc[pl.ds(r, NROW1, stride=PITCH), :].astype(BF16)
        a_sc[pl.ds(r, SLABS, stride=PITCH), :] = _dot(wa_ref[...], x)
        return c

    lax.fori_loop(0, FFT_N2, body, 0, unroll=LAT_UNROLL_R)


def _load_k1(a_sc, k1):
    base = pl.multiple_of(k1 * (2 * PITCH), SUBLANES)
    a = jnp.concatenate([a_sc[pl.ds(base, FFT_N2), :], a_sc[pl.ds(base + PITCH, FFT_N2), :]], axis=0)
    return base, a.astype(BF16)


def _lat_spec_kernel(h_ref, s_ref, wa_ref, g_ref, o_ref, hp_sc, af_sc, ab_sc):
    _pitch_copy_in(h_ref, slice(0, LANES), hp_sc)
    _dft_stage_a(hp_sc, wa_ref, af_sc)
    _pitch_copy_in(h_ref, slice(LANES, 2 * LANES), hp_sc)
    _dft_stage_a(hp_sc, wa_ref, ab_sc)
    den = s_ref[0:1, 0:LANES] + s_ref[0:1, LANES:2 * LANES] + EPS

    def kbody(k1, c):
        _, af = _load_k1(af_sc, k1)
        _, ab = _load_k1(ab_sc, k1)
        zf = _dot(g_ref[k1], af)
        zb = _dot(g_ref[k1], ab)
        o_ref[0, k1] = (zf[:FFT_N2] + zb[:FFT_N2]) / den
        o_ref[1, k1] = (zf[FFT_N2:] - zb[FFT_N2:]) / den
        return c

    lax.fori_loop(0, K1U, kbody, 0, unroll=LAT_UNROLL_K)


def _lat_spectrum(hdec, s, wa, gtab):
    nblk = HY_ORDER * HY_W // LANES
    slab = NROW1 * PITCH
    return pl.pallas_call(
        _lat_spec_kernel,
        grid=(nblk,),
        in_specs=[pl.BlockSpec((DEC_SEQ, 2 * LANES), lambda i: (0, i)),
                  pl.BlockSpec((SUBLANES, 2 * LANES), lambda i: (0, i)),
                  pl.BlockSpec(wa.shape, lambda i: (0, 0)),
                  pl.BlockSpec(gtab.shape, lambda i: (0, 0, 0))],
        out_specs=pl.BlockSpec((2, K1U, FFT_N2, LANES), lambda i: (0, 0, 0, i)),
        out_shape=jax.ShapeDtypeStruct((2, K1U, FFT_N2, HY_ORDER * HY_W), F32),
        scratch_shapes=[pltpu.VMEM((slab, LANES), F32), pltpu.VMEM((SLABS * PITCH, LANES), F32),
                        pltpu.VMEM((SLABS * PITCH, LANES), F32)],
        compiler_params=_cparams(("arbitrary",)),
        name="hy_lat_spectrum",
    )(hdec, s, wa, gtab)


def _lat_conv_kernel(z_ref, x_ref, wa_ref, wai_ref, g_ref, gi_ref, h_ref, bias_ref, o_ref,
                     zp_sc, xp_sc, op_sc, a_sc):
    full = slice(None)
    _pitch_copy_in(z_ref, full, zp_sc)
    _pitch_copy_in(x_ref, full, xp_sc)
    _dft_stage_a(zp_sc, wa_ref, a_sc)

    def kbody(k1, c):
        base, a = _load_k1(a_sc, k1)
        z = _dot(g_ref[k1], a)
        zr, zi = z[:FFT_N2], z[FFT_N2:]
        hr = h_ref[0, k1]
        hi = h_ref[1, k1]
        y = jnp.concatenate([zr * hr - zi * hi, zr * hi + zi * hr], axis=0).astype(BF16)
        bp = _dot(gi_ref[k1], y)
        a_sc[pl.ds(base, FFT_N2), :] = bp[:FFT_N2]
        a_sc[pl.ds(base + PITCH, FFT_N2), :] = bp[FFT_N2:]
        return c

    lax.fori_loop(0, K1U, kbody, 0, unroll=LAT_UNROLL_K)
    bias = bias_ref[...]

    def rbody(r, c):
        yb = a_sc[pl.ds(r, SLABS, stride=PITCH), :].astype(BF16)
        conv = _dot(wai_ref[...], yb)
        zz = zp_sc[pl.ds(r, NROW1, stride=PITCH), :]
        xx = xp_sc[pl.ds(r, NROW1, stride=PITCH), :]
        op_sc[pl.ds(r, NROW1, stride=PITCH), :] = xx * (conv + zz * bias)
        return c

    lax.fori_loop(0, FFT_N2, rbody, 0, unroll=LAT_UNROLL_R)
    for n1 in range(NROW1):
        o_ref[FFT_N2 * n1:FFT_N2 * (n1 + 1), :] = op_sc[PITCH * n1:PITCH * n1 + FFT_N2, :]


def _lat_conv(z, xg, wa, wainv, gtab, gitab, hspec, bias, order):
    ncb = HY_W // LANES
    slab = NROW1 * PITCH
    blk = pl.BlockSpec((DEC_SEQ, LANES), lambda cb, b: (b, cb))
    const = lambda a: pl.BlockSpec(a.shape, lambda cb, b: (0,) * a.ndim)
    return pl.pallas_call(
        _lat_conv_kernel,
        grid=(ncb, DEC_BATCH),
        in_specs=[blk, blk, const(wa), const(wainv), const(gtab), const(gitab),
                  pl.BlockSpec((2, K1U, FFT_N2, LANES), lambda cb, b: (0, 0, 0, order * ncb + cb)),
                  pl.BlockSpec((1, LANES), lambda cb, b: (0, cb))],
        out_specs=blk,
        out_shape=jax.ShapeDtypeStruct((M_LAT, HY_W), F32),
        scratch_shapes=[pltpu.VMEM((slab, LANES), F32), pltpu.VMEM((slab, LANES), F32),
                        pltpu.VMEM((slab, LANES), F32), pltpu.VMEM((SLABS * PITCH, LANES), F32)],
        compiler_params=_cparams(("arbitrary", "arbitrary")),
        name="hy_lat_conv",
    )(z, xg, wa, wainv, gtab, gitab, hspec, bias)


def _angle(m, n):
    return (m % n).astype(F32) * (2.0 * math.pi / n)


def _ctx_tables():
    n = 2 * SEQ
    k = jnp.arange(n, dtype=jnp.int32)[:, None]
    t = jnp.arange(SEQ, dtype=jnp.int32)[None, :]
    th = _angle(k * t, n)
    fmat = jnp.concatenate([jnp.cos(th), -jnp.sin(th)], axis=0)
    finv = jnp.concatenate([jnp.cos(th).T, -jnp.sin(th).T], axis=1) / n
    return fmat.astype(BF16), finv.astype(BF16)


def _lat_tables():
    n1, n2 = FFT_N1, FFT_N2
    n = n1 * n2
    k1 = jnp.arange(K1U, dtype=jnp.int32)
    th1 = _angle(k1[:, None] * jnp.arange(NROW1, dtype=jnp.int32)[None, :], n1)
    wa = jnp.stack([jnp.cos(th1), -jnp.sin(th1)], axis=1).reshape(2 * K1U, NROW1)
    wa = jnp.pad(wa, ((0, SLABS - 2 * K1U), (0, 0)))
    wgt = jnp.where((k1 == 0) | (k1 == n1 // 2), 1.0, 2.0)[:, None] / n
    wainv = jnp.stack([wgt * jnp.cos(th1), -wgt * jnp.sin(th1)], axis=1).reshape(2 * K1U, NROW1).T
    wainv = jnp.pad(wainv, ((0, 0), (0, SLABS - 2 * K1U)))
    k2 = jnp.arange(n2, dtype=jnp.int32)
    nn2 = jnp.arange(n2, dtype=jnp.int32)
    kfull = k1[:, None, None] + n1 * k2[None, :, None]
    th = _angle(kfull * nn2[None, None, :], n)
    gr, gi = jnp.cos(th), -jnp.sin(th)
    g = jnp.concatenate([jnp.concatenate([gr, -gi], axis=2), jnp.concatenate([gi, gr], axis=2)], axis=1)
    thT = jnp.swapaxes(th, 1, 2)
    ir, ii = jnp.cos(thT), jnp.sin(thT)
    ginv = jnp.concatenate([jnp.concatenate([ir, -ii], axis=2), jnp.concatenate([ii, ir], axis=2)], axis=1)
    return wa.astype(BF16), wainv.astype(BF16), g.astype(BF16), ginv.astype(BF16)


def _filter_features(L):
    t = jnp.linspace(0.0, 1.0, L, dtype=F32)[:, None]
    bands = (HY_EMB - 1) // 2
    w = (2.0 * math.pi / L) * jnp.arange(L, dtype=F32)[:, None]
    f = jnp.linspace(1e-4, bands - 1, bands, dtype=F32)[None, :]
    z = jnp.concatenate([t, jnp.cos(f * w), -jnp.sin(f * w)], axis=-1)
    z = jnp.pad(z, ((0, 0), (0, LANES - HY_EMB)))
    return z, jnp.broadcast_to(t, (L, LANES))


def _hyena_filter(p, L, blocked):
    feats, tcol = _filter_features(L)
    deltas = jnp.linspace(math.log(1e-2) / HY_FAST_PCT, math.log(1e-2) / HY_SLOW_PCT, HY_W, dtype=F32)
    absdelta = jnp.tile(jnp.abs(deltas), HY_ORDER * 2)[None, :]
    w3 = p["w3"]
    if blocked:
        reorder = lambda a: a.reshape(a.shape[0], HY_ORDER, 2, HY_W // LANES, LANES).transpose(
            0, 1, 3, 2, 4).reshape(a.shape[0], HY_ORDER * 2 * HY_W)
        w3, absdelta = reorder(w3), reorder(absdelta)
    return _hyfilt(feats, tcol, p["w1"], p["b1"], p["f1"], p["w2"], p["b2"], p["f2"], w3, absdelta)


def _hyena_ctx(u_hy, p, tabs):
    fmat, finv = tabs
    v, x1, x2 = _shortconv(u_hy, p["cw"], p["cb"], row_off=0, nseq=BATCH, seqlen=SEQ, tc=SEQ)
    hdec, s = _hyena_filter(p, SEQ, False)
    hspec = _ctx_spectrum(fmat, hdec, s)
    z = _ctx_conv(v, x1, fmat, finv, hspec, p["bias"][0:1], 0, F32)
    return _ctx_conv(z, x2, fmat, finv, hspec, p["bias"][1:2], 1, F32)


def _hyena_lat(u_hy, p, tabs):
    wa, wainv, gtab, gitab = tabs
    v, x1, x2 = _shortconv(u_hy, p["cw"], p["cb"], row_off=M_CTX, nseq=DEC_BATCH, seqlen=DEC_SEQ, tc=512)
    hdec, s = _hyena_filter(p, DEC_SEQ, True)
    hspec = _lat_spectrum(hdec, s, wa, gtab)
    z = _lat_conv(v, x1, wa, wainv, gtab, gitab, hspec, p["bias"][0:1], 0)
    return _lat_conv(z, x2, wa, wainv, gtab, gitab, hspec, p["bias"][1:2], 1)


def _stage3_kernel(x_ref, mod_ref, g1_ref, wg_ref, ylc_ref, yll_ref, ymc_ref, yml_ref, yhc_ref, yhl_ref,
                   wl_ref, wm_ref, wh_ref, wo_ref, g2_ref, xo_ref, xm2_ref, *, nctx):
    is_ctx = pl.program_id(0) < nctx
    x = x_ref[...]
    xm = _rms(x, g1_ref[...]) * (1.0 + mod_ref[1:2, :]) + mod_ref[0:1, :]
    xb = xm.astype(BF16)
    merged = None
    branches = ((ylc_ref, yll_ref, wl_ref), (ymc_ref, yml_ref, wm_ref), (yhc_ref, yhl_ref, wh_ref))
    for bidx, (yc_ref, yl_ref, w_ref) in enumerate(branches):
        gate = _sigmoid(_dot(xb, wg_ref[:, D_MODEL * bidx:D_MODEL * (bidx + 1)]))
        y = jnp.where(is_ctx, yc_ref[...], yl_ref[...]).astype(BF16)
        term = gate * _dot(y, w_ref[...])
        merged = term if merged is None else merged + term
    xo = x + mod_ref[2:3, :] * _dot(merged.astype(BF16), wo_ref[...])
    xo_ref[...] = xo
    xm2 = _rms(xo, g2_ref[...]) * (1.0 + mod_ref[4:5, :]) + mod_ref[3:4, :]
    xm2_ref[...] = xm2.astype(BF16)


def _stage3(x, modl, g1, wg, ylru, ymla, yhy, wl, wm, wh, wo, g2):
    tm = TM3
    nctx = M_CTX // tm
    full = lambda shape: pl.BlockSpec(shape, lambda i: (0,) * len(shape))
    row = lambda cols: pl.BlockSpec((tm, cols), lambda i: (i, 0))
    ctx = lambda cols: pl.BlockSpec((tm, cols), lambda i: (jnp.minimum(i, nctx - 1), 0))
    lat = lambda cols: pl.BlockSpec((tm, cols), lambda i: (jnp.maximum(i - nctx, 0), 0))
    return pl.pallas_call(
        functools.partial(_stage3_kernel, nctx=nctx),
        grid=(M_TOK // tm,),
        in_specs=[row(D_MODEL),
                  pl.BlockSpec((None, SUBLANES, D_MODEL), lambda i: (_mod_row(i, tm), 0, 0)),
                  full((1, D_MODEL)), full((D_MODEL, 3 * D_MODEL)),
                  ctx(LRU_W), lat(LRU_W), ctx(N_HEADS * D_V), lat(N_HEADS * D_V), ctx(HY_W), lat(HY_W),
                  full((LRU_W, D_MODEL)), full((N_HEADS * D_V, D_MODEL)), full((HY_W, D_MODEL)),
                  full((D_MODEL, D_MODEL)), full((1, D_MODEL))],
        out_specs=[row(D_MODEL), row(D_MODEL)],
        out_shape=[jax.ShapeDtypeStruct((M_TOK, D_MODEL), F32), jax.ShapeDtypeStruct((M_TOK, D_MODEL), BF16)],
        compiler_params=_cparams(("arbitrary",)),
        name="stage3",
    )(x, modl, g1, wg, *ylru, *ymla, *yhy, wl, wm, wh, wo, g2)


def _ffn_kernel(xm_ref, x_ref, mod_ref, wg_ref, wu_ref, wd_ref, o_ref, acc_sc, *, nchunks):
    j = pl.program_id(1)
    xb = xm_ref[...]
    g = _dot(xb, wg_ref[...])
    u = _dot(xb, wu_ref[...])
    hid = (g * _sigmoid(g) * u).astype(BF16)
    part = _dot(hid, wd_ref[...])

    @pl.when(j == 0)
    def _():
        acc_sc[...] = part

    @pl.when(j > 0)
    def _():
        acc_sc[...] = acc_sc[...] + part

    @pl.when(j == nchunks - 1)
    def _():
        o_ref[...] = x_ref[...] + mod_ref[5:6, :] * acc_sc[...]


def _ffn_dense(xm2, x, modl, wg, wu, wd):
    tm = TM_FFN
    nchunks = 2
    cw = D_FF // nchunks
    return pl.pallas_call(
        functools.partial(_ffn_kernel, nchunks=nchunks),
        grid=(M_TOK // tm, nchunks),
        in_specs=[pl.BlockSpec((tm, D_MODEL), lambda i, j: (i, 0)),
                  pl.BlockSpec((tm, D_MODEL), lambda i, j: (i, 0)),
                  pl.BlockSpec((None, SUBLANES, D_MODEL), lambda i, j: (_mod_row(i, tm), 0, 0)),
                  pl.BlockSpec((D_MODEL, cw), lambda i, j: (0, j)),
                  pl.BlockSpec((D_MODEL, cw), lambda i, j: (0, j)),
                  pl.BlockSpec((cw, D_MODEL), lambda i, j: (j, 0))],
        out_specs=pl.BlockSpec((tm, D_MODEL), lambda i, j: (i, 0)),
        out_shape=jax.ShapeDtypeStruct((M_TOK, D_MODEL), F32),
        scratch_shapes=[pltpu.VMEM((tm, D_MODEL), F32)],
        compiler_params=_cparams(("arbitrary", "arbitrary")),
        name="ffn_dense",
    )(xm2, x, modl, wg, wu, wd)


MOE_TILE = 256
MOE_ROWS = 2 * M_TOK + N_EXPERTS * MOE_TILE
MOE_TILES = MOE_ROWS // MOE_TILE
MOE_PAD_ROWS = MOE_ROWS - 2 * M_TOK


def _router_kernel(xm_ref, wr_ref, tri_ref, sel_ref, xp_ref, cnt_ref, base_sc):
    i = pl.program_id(0)
    xb = xm_ref[...]
    tm = xb.shape[0]
    lane = lax.broadcasted_iota(jnp.int32, (tm, LANES), 1)

    @pl.when(i == 0)
    def _():
        base_sc[...] = jnp.zeros_like(base_sc)

    logits = jnp.where(lane < N_EXPERTS, _dot(xb, wr_ref[...]), -1e30)
    mx = jnp.max(logits, axis=-1, keepdims=True)
    ex = jnp.exp(logits - mx)
    probs = ex / jnp.sum(ex, axis=-1, keepdims=True)
    p1 = jnp.max(probs, axis=-1, keepdims=True)
    i1 = jnp.min(jnp.where(probs == p1, lane, LANES), axis=-1, keepdims=True)
    rest = jnp.where(lane == i1, -1.0, probs)
    p2 = jnp.max(rest, axis=-1, keepdims=True)
    i2 = jnp.min(jnp.where(rest == p2, lane, LANES), axis=-1, keepdims=True)
    tot = p1 + p2
    oh1 = lane == i1
    oh2 = lane == i2
    oh = jnp.where(oh1 | oh2, 1.0, 0.0)
    before = base_sc[0:1, :] + _dot(tri_ref[...], oh.astype(BF16))
    r1 = jnp.sum(jnp.where(oh1, before, 0.0), axis=-1, keepdims=True)
    r2 = jnp.sum(jnp.where(oh2, before, 0.0), axis=-1, keepdims=True)
    base_sc[0:1, :] = base_sc[0:1, :] + jnp.sum(oh, axis=0, keepdims=True)
    cnt_ref[...] = base_sc[...]
    sel = jnp.where(lane == 0, p1 / tot, 0.0) + jnp.where(lane == 1, p2 / tot, 0.0)
    sel = sel + jnp.where(lane == 2, i1.astype(F32), 0.0) + jnp.where(lane == 3, i2.astype(F32), 0.0)
    sel_ref[...] = sel + jnp.where(lane == 4, r1, 0.0) + jnp.where(lane == 5, r2, 0.0)
    xp_ref[...] = xb.astype(F32)


def _moe_router(xm2, wr):
    tm = TM_FFN
    tri = jnp.tril(jnp.ones((tm, tm), F32), -1).astype(BF16)
    return pl.pallas_call(
        _router_kernel,
        grid=(M_TOK // tm,),
        in_specs=[pl.BlockSpec((tm, D_MODEL), lambda i: (i, 0)),
                  pl.BlockSpec((D_MODEL, LANES), lambda i: (0, 0)),
                  pl.BlockSpec((tm, tm), lambda i: (0, 0))],
        out_specs=[pl.BlockSpec((tm, LANES), lambda i: (i, 0)),
                   pl.BlockSpec((tm, D_MODEL), lambda i: (i, 0)),
                   pl.BlockSpec((SUBLANES, LANES), lambda i: (0, 0))],
        out_shape=[jax.ShapeDtypeStruct((M_TOK, LANES), F32),
                   jax.ShapeDtypeStruct((M_TOK, D_MODEL), F32),
                   jax.ShapeDtypeStruct((SUBLANES, LANES), F32)],
        scratch_shapes=[pltpu.VMEM((SUBLANES, LANES), F32)],
        compiler_params=_cparams(("arbitrary",)),
        name="moe_router",
    )(xm2, wr, tri)


def _row_copy(src, srow, dst, drow, sem):
    return pltpu.make_async_copy(src.at[pl.ds(srow, 1), :], dst.at[pl.ds(drow, 1), :], sem)


def _dispatch_kernel(pos_ref, pad_ref, xp_ref, xs_ref, ring_sc, zero_sc, sem, *, nsteps):
    i = pl.program_id(0)
    slot = i % 2
    nrow = MOE_TILE

    def wait_slot(s):
        for _ in range(2):
            pltpu.make_async_copy(ring_sc.at[s], xs_ref.at[pl.ds(0, nrow), :], sem.at[s]).wait()

    @pl.when(i >= 2)
    def _():
        wait_slot(slot)

    ring_sc[slot] = xp_ref[...]

    def body(t, c):
        _row_copy(ring_sc.at[slot], t, xs_ref, pos_ref[0, 2 * t], sem.at[slot]).start()
        _row_copy(ring_sc.at[slot], t, xs_ref, pos_ref[0, 2 * t + 1], sem.at[slot]).start()
        return c

    lax.fori_loop(0, nrow, body, 0, unroll=4)

    @pl.when(i == nsteps - 1)
    def _():
        zero_sc[...] = jnp.zeros_like(zero_sc)

        def zbody(t, c):
            _row_copy(zero_sc, 0, xs_ref, pad_ref[t], sem.at[2]).start()
            return c

        lax.fori_loop(0, MOE_PAD_ROWS, zbody, 0, unroll=4)
        wait_slot(1 - slot)
        wait_slot(slot)

        for _ in range(MOE_PAD_ROWS // nrow):
            pltpu.make_async_copy(ring_sc.at[0], xs_ref.at[pl.ds(0, nrow), :], sem.at[2]).wait()


def _moe_dispatch(xp, pos, padrows):
    nsteps = M_TOK // MOE_TILE
    return pl.pallas_call(
        functools.partial(_dispatch_kernel, nsteps=nsteps),
        grid=(nsteps,),
        in_specs=[pl.BlockSpec((None, 1, 2 * MOE_TILE), lambda i: (i, 0, 0), memory_space=pltpu.SMEM),
                  pl.BlockSpec(memory_space=pltpu.SMEM),
                  pl.BlockSpec((MOE_TILE, D_MODEL), lambda i: (i, 0))],
        out_specs=pl.BlockSpec(memory_space=pl.ANY),
        out_shape=jax.ShapeDtypeStruct((MOE_ROWS, D_MODEL), F32),
        scratch_shapes=[pltpu.VMEM((2, MOE_TILE, D_MODEL), F32), pltpu.VMEM((SUBLANES, D_MODEL), F32),
                        pltpu.SemaphoreType.DMA((3,))],
        compiler_params=_cparams(("arbitrary",)),
        name="moe_dispatch",
    )(pos, padrows, xp)


def _experts_kernel(te_ref, nu_ref, xs_ref, wg_ref, wu_ref, wd_ref, ys_ref):
    i = pl.program_id(0)

    @pl.when(i < nu_ref[0])
    def _():
        xb = xs_ref[...].astype(BF16)
        g = _dot(xb, wg_ref[...])
        u = _dot(xb, wu_ref[...])
        hid = (g * _sigmoid(g) * u).astype(BF16)
        ys_ref[...] = _dot(hid, wd_ref[...])

    @pl.when(i >= nu_ref[0])
    def _():
        ys_ref[...] = jnp.zeros_like(ys_ref)


def _moe_experts(tile_expert, n_used, xs, wg, wu, wd):
    grid_spec = pltpu.PrefetchScalarGridSpec(
        num_scalar_prefetch=2,
        grid=(MOE_TILES,),
        in_specs=[pl.BlockSpec((MOE_TILE, D_MODEL), lambda i, te, nu: (i, 0)),
                  pl.BlockSpec((None, D_MODEL, D_FF_E), lambda i, te, nu: (te[i], 0, 0)),
                  pl.BlockSpec((None, D_MODEL, D_FF_E), lambda i, te, nu: (te[i], 0, 0)),
                  pl.BlockSpec((None, D_FF_E, D_MODEL), lambda i, te, nu: (te[i], 0, 0))],
        out_specs=pl.BlockSpec((MOE_TILE, D_MODEL), lambda i, te, nu: (i, 0)),
    )
    return pl.pallas_call(
        _experts_kernel,
        grid_spec=grid_spec,
        out_shape=jax.ShapeDtypeStruct((MOE_ROWS, D_MODEL), F32),
        compiler_params=_cparams(("arbitrary",)),
        name="moe_experts",
    )(tile_expert, n_used, xs, wg, wu, wd)


def _combine_kernel(pos_ref, x_ref, mod_ref, sel_ref, ys_ref, oc_ref, ol_ref, buf_sc, sem, *, nsteps, nctx):
    i = pl.program_id(0)
    slot = i % 2
    nrow = MOE_TILE

    def start(s, off):
        def body(t, c):
            _row_copy(ys_ref, pos_ref[0, off + 2 * t], buf_sc.at[s, 0], t, sem.at[s]).start()
            _row_copy(ys_ref, pos_ref[0, off + 2 * t + 1], buf_sc.at[s, 1], t, sem.at[s]).start()
            return c
        lax.fori_loop(0, nrow, body, 0, unroll=4)

    @pl.when(i == 0)
    def _():
        start(0, 0)

    @pl.when(i + 1 < nsteps)
    def _():
        start(1 - slot, 2 * nrow)

    for k in range(2):
        pltpu.make_async_copy(ys_ref.at[pl.ds(0, nrow), :], buf_sc.at[slot, k], sem.at[slot]).wait()
    lane = lax.broadcasted_iota(jnp.int32, (nrow, LANES), 1)
    sel = sel_ref[...]
    w1 = jnp.sum(jnp.where(lane == 0, sel, 0.0), axis=-1, keepdims=True)
    w2 = jnp.sum(jnp.where(lane == 1, sel, 0.0), axis=-1, keepdims=True)
    y = w1 * buf_sc[slot, 0] + w2 * buf_sc[slot, 1]
    val = x_ref[...] + mod_ref[5:6, :] * y

    @pl.when(i < nctx)
    def _():
        oc_ref[...] = val

    @pl.when(i >= nctx)
    def _():
        ol_ref[...] = val


def _moe_combine(pos2, x, modl, sel, ys):
    nsteps = M_TOK // MOE_TILE
    tm = MOE_TILE
    nctx = M_CTX // tm
    return pl.pallas_call(
        functools.partial(_combine_kernel, nsteps=nsteps, nctx=nctx),
        grid=(nsteps,),
        in_specs=[pl.BlockSpec((None, 1, 4 * MOE_TILE), lambda i: (i, 0, 0), memory_space=pltpu.SMEM),
                  pl.BlockSpec((tm, D_MODEL), lambda i: (i, 0)),
                  pl.BlockSpec((None, SUBLANES, D_MODEL), lambda i: (_mod_row(i, tm), 0, 0)),
                  pl.BlockSpec((tm, LANES), lambda i: (i, 0)),
                  pl.BlockSpec(memory_space=pl.ANY)],
        out_specs=[pl.BlockSpec((tm, D_MODEL), lambda i: (jnp.minimum(i, nctx - 1), 0)),
                   pl.BlockSpec((tm, D_MODEL), lambda i: (jnp.maximum(i - nctx, 0), 0))],
        out_shape=[jax.ShapeDtypeStruct((M_CTX, D_MODEL), F32), jax.ShapeDtypeStruct((M_LAT, D_MODEL), F32)],
        scratch_shapes=[pltpu.VMEM((2, 2, MOE_TILE, D_MODEL), F32), pltpu.SemaphoreType.DMA((2,))],
        compiler_params=_cparams(("arbitrary",)),
        name="moe_combine",
    )(pos2, x, modl, sel, ys)


def _ffn_moe(xm2, x, modl, wr, wg, wu, wd):
    sel, xp, cnt = _moe_router(xm2, wr)
    counts = cnt[0, :N_EXPERTS].astype(jnp.int32)
    padded = ((counts + MOE_TILE - 1) // MOE_TILE) * MOE_TILE
    ends = jnp.cumsum(padded)
    offs = ends - padded
    experts = sel[:, 2:4].astype(jnp.int32)
    ranks = sel[:, 4:6].astype(jnp.int32)
    pos = offs[experts] + ranks
    tile_start = jnp.arange(MOE_TILES, dtype=jnp.int32) * MOE_TILE
    tile_expert = jnp.minimum(jnp.sum(tile_start[:, None] >= ends[None, :], axis=1), N_EXPERTS - 1)
    n_used = (ends[-1] // MOE_TILE).astype(jnp.int32)[None]
    rows = jnp.arange(MOE_ROWS, dtype=jnp.int32)
    row_expert = jnp.repeat(tile_expert, MOE_TILE)
    written = (rows < ends[-1]) & (rows - offs[row_expert] < counts[row_expert])
    padrows = jnp.nonzero(~written, size=MOE_PAD_ROWS)[0].astype(jnp.int32)
    pos_tiles = pos.reshape(M_TOK // MOE_TILE, 1, 2 * MOE_TILE)
    xs = _moe_dispatch(xp, pos_tiles, padrows)
    ys = _moe_experts(tile_expert.astype(jnp.int32), n_used, xs, wg, wu, wd)
    nxt = jnp.concatenate([pos_tiles[1:], pos_tiles[-1:]], axis=0)
    pos2 = jnp.concatenate([pos_tiles, nxt], axis=2)
    return tuple(_moe_combine(pos2, x, modl, sel, ys))


def _block_diag(w):
    nb, bs, _ = w.shape
    eye = jnp.eye(nb, dtype=w.dtype)
    return jnp.einsum("njk,nm->njmk", w, eye).reshape(nb * bs, nb * bs)


def _head_pad_cols(w, width):
    r = w.shape[0]
    return jnp.pad(w, ((0, 0), (0, 0), (0, HEAD_PAD - width))).reshape(r, N_HEADS * HEAD_PAD)


def _swap_rope_pairs(a):
    nope, rope = a[..., :D_NOPE], a[..., D_NOPE:]
    sw = rope.reshape(rope.shape[:-1] + (D_ROPE // 2, 2))[..., ::-1].reshape(rope.shape)
    return jnp.concatenate([nope, sw], axis=-1)


def _head_gain(g):
    rows = jnp.stack([g, _swap_rope_pairs(g)], axis=0)
    return jnp.pad(rows, ((0, SUBLANES - 2), (0, HEAD_PAD - D_QK)))


def _rope_tables(tm):
    rows = DEC_SEQ // GRID_W
    row = jnp.repeat(jnp.arange(rows, dtype=F32), GRID_W)
    col = jnp.tile(jnp.arange(GRID_W, dtype=F32), rows)
    half = D_ROPE // 2
    inv_freq = ROPE_BASE ** (-jnp.arange(0, half, 2, dtype=F32) / half)
    ang = jnp.concatenate([row[:, None] * inv_freq, col[:, None] * inv_freq], axis=-1)
    cos, sin = jnp.cos(ang), jnp.sin(ang)
    cos2 = jnp.repeat(cos, 2, axis=1)
    sin2 = jnp.stack([-sin, sin], axis=-1).reshape(DEC_SEQ, D_ROPE)
    cos_t = jnp.pad(cos2, ((0, 0), (D_NOPE, HEAD_PAD - D_QK)), constant_values=1.0)
    sin_t = jnp.pad(sin2, ((0, 0), (D_NOPE, HEAD_PAD - D_QK)))
    cos_t = jnp.concatenate([jnp.ones((tm, HEAD_PAD), F32), cos_t], axis=0)
    sin_t = jnp.concatenate([jnp.zeros((tm, HEAD_PAD), F32), sin_t], axis=0)
    return cos_t, sin_t


def kernel(x_prompt, x_sample, cache_ckv, cache_krope, state_lru, c, c_ctx, norm1, norm2, w_ada, b_ada, w_in, mla_q_norm, mla_kv_norm, mla_w_uq, mla_w_uk, mla_w_uv, mla_q_qknorm, mla_k_qknorm, lru_conv_w, lru_conv_b, lru_w_gate, lru_b_gate, lru_lambda, hy_conv_w, hy_conv_b, hy_w1, hy_b1, hy_freq1, hy_w2, hy_b2, hy_freq2, hy_w3, hy_bias, w_lru_out, w_mla_out, w_hy_out, w_out, ffn_w_gate, ffn_w_up, ffn_w_down, moe_w_router, moe_w_gate, moe_w_up, moe_w_down):
    x = jnp.concatenate([x_prompt.reshape(M_CTX, D_MODEL), x_sample.reshape(M_LAT, D_MODEL)], axis=0)

    cond = jnp.concatenate([c_ctx[None, :], c, jnp.zeros((SUBLANES - 1 - DEC_BATCH, D_MODEL), F32)], axis=0)
    mod = _adaln(cond, w_ada, b_ada).reshape(DEPTH, SUBLANES, 6, D_MODEL)
    mod = jnp.pad(mod, ((0, 0), (0, 0), (0, SUBLANES - 6), (0, 0)))

    cos_t, sin_t = _rope_tables(TM1)
    ctx_tabs = _ctx_tables()
    lat_tabs = _lat_tables()
    zero_state = jnp.zeros((BATCH, SUBLANES, LRU_W), F32)

    ckv_out, kr_out, st_out = [], [], []
    for l in range(DEPTH):
        wl = w_in[l]
        wkr = jnp.concatenate([jnp.zeros((D_MODEL, D_NOPE), F32), wl[:, 896:928]], axis=1)
        krblk = lambda w: jnp.pad(w, ((0, 0), (0, HEAD_PAD - D_QK)))
        w1 = jnp.concatenate([wl[:, :896], krblk(wkr), krblk(_swap_rope_pairs(wkr)), wl[:, 928:2464]],
                             axis=1).astype(BF16)
        wgates = wl[:, 2464:].astype(BF16)
        wuq = _head_pad_cols(mla_w_uq[l], D_QK).astype(BF16)
        wuqs = _head_pad_cols(_swap_rope_pairs(mla_w_uq[l]), D_QK).astype(BF16)
        wuk = _head_pad_cols(mla_w_uk[l], D_NOPE).astype(BF16)
        wuv = mla_w_uv[l].reshape(KV_RANK, N_HEADS * D_V).astype(BF16)
        gq = _head_gain(mla_q_qknorm[l])
        gk = _head_gain(mla_k_qknorm[l])

        ulru, uhy, ckv, krb, q, k, v = _stage1(
            x, mod[l], norm1[l][None, :], w1, mla_kv_norm[l][None, :], mla_q_norm[l][None, :],
            wuq, wuqs, gq, wuk, gk, wuv, cos_t, sin_t)
        ckv_out.append(ckv[:M_CTX].reshape(BATCH, SEQ, KV_RANK))
        kr_out.append(krb[:M_CTX, D_NOPE:D_QK].reshape(BATCH, SEQ, D_ROPE))

        kc, vc = _kvprep(cache_ckv[:, l].reshape(DEC_BATCH * PAST_LEN, KV_RANK),
                         jnp.pad(cache_krope[:, l].reshape(DEC_BATCH * PAST_LEN, D_ROPE),
                                 ((0, 0), (D_NOPE, HEAD_PAD - D_QK))),
                         wuk, gk, wuv)
        ymla = (_attention_ctx(q, k, v), _attention_lat(q, k, v, kc, vc))

        lp = dict(
            cw=jnp.pad(lru_conv_w[l], ((0, SUBLANES - 4), (0, 0))), cb=lru_conv_b[l][None, :],
            wr=[_block_diag(lru_w_gate[l, d, 0]).astype(BF16) for d in range(2)],
            wi=[_block_diag(lru_w_gate[l, d, 1]).astype(BF16) for d in range(2)],
            br=[lru_b_gate[l, d, 0][None, :] for d in range(2)],
            bi=[lru_b_gate[l, d, 1][None, :] for d in range(2)],
            lam=[lru_lambda[l, d][None, :] for d in range(2)])
        y_c, stf, stb = _lru_mixer(ulru, lp, (zero_state, zero_state), row_off=0, nseq=BATCH, seqlen=SEQ, tc=SEQ)
        st_out.append(jnp.stack([stf, stb], axis=1))
        h0 = [jnp.broadcast_to(state_lru[:, l, d][:, None, :], (DEC_BATCH, SUBLANES, LRU_W)) for d in range(2)]
        y_l, _, _ = _lru_mixer(ulru, lp, h0, row_off=M_CTX, nseq=DEC_BATCH, seqlen=DEC_SEQ, tc=512)
        ylru = (y_c, y_l)

        hp = dict(
            cw=jnp.pad(hy_conv_w[l], ((0, SUBLANES - 3), (0, 0))), cb=hy_conv_b[l][None, :],
            w1=jnp.pad(hy_w1[l], ((0, LANES - HY_EMB), (0, 0))).astype(BF16), b1=hy_b1[l][None, :],
            f1=hy_freq1[l][None, :], w2=hy_w2[l].astype(BF16), b2=hy_b2[l][None, :], f2=hy_freq2[l][None, :],
            w3=hy_w3[l].astype(BF16), bias=hy_bias[l])
        yhy = (_hyena_ctx(uhy, hp, ctx_tabs), _hyena_lat(uhy, hp, lat_tabs))

        x, xm2 = _stage3(x, mod[l], norm1[l][None, :], wgates, ylru, ymla, yhy,
                         w_lru_out[l].astype(BF16), w_mla_out[l].astype(BF16), w_hy_out[l].astype(BF16),
                         w_out[l].astype(BF16), norm2[l][None, :])
        j = l // 2
        if l % 2 == 0:
            x = _ffn_dense(xm2, x, mod[l], ffn_w_gate[j].astype(BF16), ffn_w_up[j].astype(BF16),
                           ffn_w_down[j].astype(BF16))
        else:
            wr = jnp.pad(moe_w_router[j], ((0, 0), (0, LANES - N_EXPERTS))).astype(BF16)
            x = _ffn_moe(xm2, x, mod[l], wr, moe_w_gate[j].astype(BF16), moe_w_up[j].astype(BF16),
                         moe_w_down[j].astype(BF16))
            if l + 1 < DEPTH:
                x = jnp.concatenate(x, axis=0)

    xc, xl = x if isinstance(x, tuple) else (x[:M_CTX], x[M_CTX:])
    y_prompt = xc.reshape(BATCH, SEQ, D_MODEL)
    y_sample = xl.reshape(DEC_BATCH, DEC_SEQ, D_MODEL)
    return (y_prompt, y_sample, jnp.stack(ckv_out, axis=1), jnp.stack(kr_out, axis=1), jnp.stack(st_out, axis=1))
```

```python
import functools
import math

import jax
import jax.numpy as jnp
from jax import lax
from jax.experimental import pallas as pl
from jax.experimental.pallas import tpu as pltpu

F32 = jnp.float32
BF16 = jnp.bfloat16

D_MODEL = 1024
BATCH = 32
SEQ = 256
DEPTH = 2
DEC_BATCH = 2
DEC_SEQ = 4096
PAST_LEN = 512
GRID_W = 64
EPS = 1e-6
LRU_W = 512
LRU_BLOCKS = 8
LRU_C = 8.0
N_HEADS = 8
D_NOPE = 64
D_ROPE = 32
D_QK = D_NOPE + D_ROPE
D_V = 64
Q_RANK = 256
KV_RANK = 128
ROPE_BASE = 10000.0
HY_W = 512
HY_ORDER = 2
HY_EMB = 33
HY_HID = 64
HY_FAST_PCT = 0.3
HY_SLOW_PCT = 1.5
D_FF = 2816
N_EXPERTS = 8
D_FF_E = 1408

LANES = 128
SUBLANES = 8
VMEM_LIMIT = 56 * 1024 * 1024

M_CTX = BATCH * SEQ
M_LAT = DEC_BATCH * DEC_SEQ
M_TOK = M_CTX + M_LAT
TM1 = 512
TM3 = 512
TM_FFN = 512
W1_COLS = 2688
HEAD_PAD = LANES
QK_SCALE = math.log2(math.e) / math.sqrt(D_QK)
ATTN_TQ = 256

FFT_N1 = 64
FFT_N2 = 128


def _cparams(sem, vmem=VMEM_LIMIT):
    return pltpu.CompilerParams(dimension_semantics=sem, vmem_limit_bytes=vmem)


def _dot(a, b):
    return jnp.dot(a, b, preferred_element_type=F32)


def _rms(x, g):
    ms = jnp.mean(x * x, axis=-1, keepdims=True)
    return x * lax.rsqrt(ms + EPS) * g


def _sigmoid(x):
    return 1.0 / (1.0 + jnp.exp(-x))


def _ada_kernel(c_ref, w_ref, b_ref, o_ref):
    c = c_ref[...]
    s = (c * _sigmoid(c)).astype(BF16)
    o_ref[...] = _dot(s, w_ref[...].astype(BF16)) + b_ref[...]


def _adaln(cond, w_ada, b_ada):
    tn = 1024
    n6 = 6 * D_MODEL
    return pl.pallas_call(
        _ada_kernel,
        grid=(DEPTH, n6 // tn),
        in_specs=[
            pl.BlockSpec((SUBLANES, D_MODEL), lambda l, j: (0, 0)),
            pl.BlockSpec((None, D_MODEL, tn), lambda l, j: (l, 0, j)),
            pl.BlockSpec((None, 1, tn), lambda l, j: (l, 0, j)),
        ],
        out_specs=pl.BlockSpec((None, SUBLANES, tn), lambda l, j: (l, 0, j)),
        out_shape=jax.ShapeDtypeStruct((DEPTH, SUBLANES, n6), F32),
        compiler_params=_cparams(("arbitrary", "arbitrary")),
        name="adaln",
    )(cond, w_ada, b_ada.reshape(DEPTH, 1, n6))


def _mod_row(i, tm):
    nctx = M_CTX // tm
    per = DEC_SEQ // tm
    return jnp.where(i < nctx, 0, 1 + (i - nctx) // per)


def _rope_blk(i, tm):
    nctx = M_CTX // tm
    per = DEC_SEQ // tm
    return jnp.where(i < nctx, 0, 1 + (i - nctx) % per)


def _finish_head(raw, raw_sw, gc, gs, out_ref, sl):
    ms = jnp.sum(raw * raw, axis=-1, keepdims=True) * (1.0 / D_QK)
    rs = lax.rsqrt(ms + EPS)
    val = raw * gc
    if gs is not None:
        val = val + raw_sw * gs
    out_ref[:, sl] = (val * rs).astype(BF16)


def _read_tokens(x_refs, nctx):
    if len(x_refs) == 1:
        return x_refs[0][...]
    return jnp.where(pl.program_id(0) < nctx, x_refs[0][...], x_refs[1][...])


def _token_specs(x, tm):
    nctx = M_CTX // tm
    if len(x) == 1:
        return [pl.BlockSpec((tm, D_MODEL), lambda i: (i, 0))]
    return [pl.BlockSpec((tm, D_MODEL), lambda i: (jnp.minimum(i, nctx - 1), 0)),
            pl.BlockSpec((tm, D_MODEL), lambda i: (jnp.maximum(i - nctx, 0), 0))]


def _stage1_kernel(*refs, nx, nctx):
    (mod_ref, g1_ref, w1_ref, gkv_ref, gqn_ref, wuq_ref, wuqs_ref, gq_ref, wuk_ref, gk_ref,
     wuv_ref, cos_ref, sin_ref, ulru_ref, uhy_ref, ckv_ref, krb_ref, q_ref, k_ref, v_ref) = refs[nx:]
    x = _read_tokens(refs[:nx], nctx)
    xm = _rms(x, g1_ref[...]) * (1.0 + mod_ref[1:2, :]) + mod_ref[0:1, :]
    xb = xm.astype(BF16)
    ulru_ref[...] = _dot(xb, w1_ref[:, 0:512])
    qc = _dot(xb, w1_ref[:, 512:768])
    ckv = _dot(xb, w1_ref[:, 768:896])
    krb = _dot(xb, w1_ref[:, 896:1024])
    krs = _dot(xb, w1_ref[:, 1024:1152])
    uhy_ref[...] = _dot(xb, w1_ref[:, 1152:2688])
    ckvn = _rms(ckv, gkv_ref[...])
    ckv_ref[...] = ckvn
    krb_ref[...] = krb
    qn = _rms(qc, gqn_ref[...]).astype(BF16)
    cb = ckvn.astype(BF16)
    v_ref[...] = _dot(cb, wuv_ref[...]).astype(BF16)
    cos = cos_ref[...]
    sin = sin_ref[...]
    gcq = cos * (gq_ref[0:1, :] * QK_SCALE)
    gsq = sin * (gq_ref[1:2, :] * QK_SCALE)
    gck = cos * gk_ref[0:1, :]
    gsk = sin * gk_ref[1:2, :]
    for h in range(N_HEADS):
        sl = slice(HEAD_PAD * h, HEAD_PAD * (h + 1))
        _finish_head(_dot(qn, wuq_ref[:, sl]), _dot(qn, wuqs_ref[:, sl]), gcq, gsq, q_ref, sl)
        _finish_head(_dot(cb, wuk_ref[:, sl]) + krb, krs, gck, gsk, k_ref, sl)


def _stage1(x, modl, g1, w1, gkv, gqn, wuq, wuqs, gq, wuk, gk, wuv, cos_t, sin_t):
    tm = TM1
    full = lambda shape: pl.BlockSpec(shape, lambda i: (0,) * len(shape))
    row = lambda cols: pl.BlockSpec((tm, cols), lambda i: (i, 0))
    hw = N_HEADS * HEAD_PAD
    return pl.pallas_call(
        functools.partial(_stage1_kernel, nx=len(x), nctx=M_CTX // tm),
        grid=(M_TOK // tm,),
        in_specs=_token_specs(x, tm) + [
            pl.BlockSpec((None, SUBLANES, D_MODEL), lambda i: (_mod_row(i, tm), 0, 0)),
            full((1, D_MODEL)),
            full((D_MODEL, W1_COLS)),
            full((1, KV_RANK)),
            full((1, Q_RANK)),
            full((Q_RANK, hw)),
            full((Q_RANK, hw)),
            full((SUBLANES, HEAD_PAD)),
            full((KV_RANK, hw)),
            full((SUBLANES, HEAD_PAD)),
            full((KV_RANK, N_HEADS * D_V)),
            pl.BlockSpec((tm, LANES), lambda i: (_rope_blk(i, tm), 0)),
            pl.BlockSpec((tm, LANES), lambda i: (_rope_blk(i, tm), 0)),
        ],
        out_specs=[row(LRU_W), row(3 * HY_W), row(KV_RANK), row(LANES), row(hw), row(hw), row(N_HEADS * D_V)],
        out_shape=[
            jax.ShapeDtypeStruct((M_TOK, LRU_W), F32),
            jax.ShapeDtypeStruct((M_TOK, 3 * HY_W), F32),
            jax.ShapeDtypeStruct((M_TOK, KV_RANK), F32),
            jax.ShapeDtypeStruct((M_TOK, LANES), F32),
            jax.ShapeDtypeStruct((M_TOK, hw), BF16),
            jax.ShapeDtypeStruct((M_TOK, hw), BF16),
            jax.ShapeDtypeStruct((M_TOK, N_HEADS * D_V), BF16),
        ],
        compiler_params=_cparams(("arbitrary",)),
        name="stage1",
    )(*x, modl, g1, w1, gkv, gqn, wuq, wuqs, gq, wuk, gk, wuv, cos_t, sin_t)


def _kvprep_kernel(ckv_ref, krb_ref, wuk_ref, gk_ref, wuv_ref, k_ref, v_ref):
    cb = ckv_ref[...].astype(BF16)
    v_ref[...] = _dot(cb, wuv_ref[...]).astype(BF16)
    krb = krb_ref[...]
    for h in range(N_HEADS):
        sl = slice(HEAD_PAD * h, HEAD_PAD * (h + 1))
        _finish_head(_dot(cb, wuk_ref[:, sl]) + krb, None, gk_ref[0:1, :], None, k_ref, sl)


def _kvprep(ckv, krb, wuk, gk, wuv):
    rows = ckv.shape[0]
    tm = TM1
    hw = N_HEADS * HEAD_PAD
    full = lambda shape: pl.BlockSpec(shape, lambda i: (0,) * len(shape))
    row = lambda cols: pl.BlockSpec((tm, cols), lambda i: (i, 0))
    return pl.pallas_call(
        _kvprep_kernel,
        grid=(rows // tm,),
        in_specs=[row(KV_RANK), row(LANES), full((KV_RANK, hw)), full((SUBLANES, HEAD_PAD)),
                  full((KV_RANK, N_HEADS * D_V))],
        out_specs=[row(hw), row(N_HEADS * D_V)],
        out_shape=[jax.ShapeDtypeStruct((rows, hw), BF16), jax.ShapeDtypeStruct((rows, N_HEADS * D_V), BF16)],
        compiler_params=_cparams(("arbitrary",)),
        name="kvprep",
    )(ckv, krb, wuk, gk, wuv)


def _attn_kernel(*refs, heads, nseg):
    q_ref = refs[0]
    k_refs = refs[1:1 + nseg]
    v_refs = refs[1 + nseg:1 + 2 * nseg]
    o_ref = refs[1 + 2 * nseg]
    tq = q_ref.shape[0]
    lane = lax.broadcasted_iota(jnp.int32, (tq, LANES), 1)
    low = lane < D_V
    for pair in range(heads // 2):
        outs = []
        for j in range(2):
            h = 2 * pair + j
            sl = slice(HEAD_PAD * h, HEAD_PAD * (h + 1))
            q = q_ref[:, sl]
            s = [lax.dot_general(q, kr[:, sl], (((1,), (1,)), ((), ())), preferred_element_type=F32)
                 for kr in k_refs]
            m = jnp.max(s[0], axis=-1, keepdims=True)
            for si in s[1:]:
                m = jnp.maximum(m, jnp.max(si, axis=-1, keepdims=True))
            acc = None
            den = None
            for si, vr in zip(s, v_refs):
                p = jnp.exp2(si - m)
                d = jnp.sum(p, axis=-1, keepdims=True)
                o = _dot(p.astype(BF16), vr[:, LANES * pair:LANES * (pair + 1)])
                acc = o if acc is None else acc + o
                den = d if den is None else den + d
            outs.append(acc / den)
        o_ref[:, LANES * pair:LANES * (pair + 1)] = jnp.where(low, outs[0], outs[1]).astype(BF16)


def _attention_ctx(q, k, v):
    hw = N_HEADS * HEAD_PAD
    vw = N_HEADS * D_V
    return pl.pallas_call(
        functools.partial(_attn_kernel, heads=N_HEADS, nseg=1),
        grid=(BATCH,),
        in_specs=[
            pl.BlockSpec((SEQ, hw), lambda b: (b, 0)),
            pl.BlockSpec((SEQ, hw), lambda b: (b, 0)),
            pl.BlockSpec((SEQ, vw), lambda b: (b, 0)),
        ],
        out_specs=pl.BlockSpec((SEQ, vw), lambda b: (b, 0)),
        out_shape=jax.ShapeDtypeStruct((M_CTX, vw), BF16),
        compiler_params=_cparams(("arbitrary",)),
        name="attn_ctx",
    )(q, k, v)


def _attention_lat(q, k, v, kc, vc):
    tq = ATTN_TQ
    nq = DEC_SEQ // tq
    qoff = M_CTX // tq
    koff = M_CTX // DEC_SEQ
    return pl.pallas_call(
        functools.partial(_attn_kernel, heads=2, nseg=2),
        grid=(DEC_BATCH, N_HEADS // 2, nq),
        in_specs=[
            pl.BlockSpec((tq, 2 * HEAD_PAD), lambda b, p, i: (qoff + b * nq + i, p)),
            pl.BlockSpec((DEC_SEQ, 2 * HEAD_PAD), lambda b, p, i: (koff + b, p)),
            pl.BlockSpec((PAST_LEN, 2 * HEAD_PAD), lambda b, p, i: (b, p)),
            pl.BlockSpec((DEC_SEQ, 2 * D_V), lambda b, p, i: (koff + b, p)),
            pl.BlockSpec((PAST_LEN, 2 * D_V), lambda b, p, i: (b, p)),
        ],
        out_specs=pl.BlockSpec((tq, 2 * D_V), lambda b, p, i: (b * nq + i, p)),
        out_shape=jax.ShapeDtypeStruct((M_LAT, N_HEADS * D_V), BF16),
        compiler_params=_cparams(("arbitrary", "arbitrary", "arbitrary")),
        name="attn_lat",
    )(q, k, kc, v, vc)


def _lru_kernel(*refs, reverse, tc, nchunks):
    if reverse:
        (up_ref, uc_ref, un_ref, hf_ref, cw_ref, cb_ref, wr_ref, wi_ref, br_ref, bi_ref, lam_ref, h0_ref,
         y_ref, st_ref, ext_sc, a_sc, b_sc, car_sc) = refs
    else:
        (up_ref, uc_ref, un_ref, cw_ref, cb_ref, wr_ref, wi_ref, br_ref, bi_ref, lam_ref, h0_ref,
         y_ref, st_ref, ext_sc, a_sc, b_sc, car_sc) = refs
    c = pl.program_id(1)
    chunk = (nchunks - 1 - c) if reverse else c
    prev = jnp.where(chunk == 0, 0.0, up_ref[...])
    nxt = jnp.where(chunk == nchunks - 1, 0.0, un_ref[...])
    ext_sc[0:SUBLANES, :] = prev
    ext_sc[SUBLANES:SUBLANES + tc, :] = uc_ref[...]
    ext_sc[SUBLANES + tc:2 * SUBLANES + tc, :] = nxt
    xc = cb_ref[...]
    for k in range(4):
        xc = xc + cw_ref[k:k + 1, :] * ext_sc[SUBLANES - 2 + k:SUBLANES - 2 + k + tc, :]
    xb = xc.astype(BF16)
    r = _sigmoid(_dot(xb, wr_ref[...]) + br_ref[...])
    gi = _sigmoid(_dot(xb, wi_ref[...]) + bi_ref[...])
    lam = lam_ref[...]
    logsig = -(jnp.maximum(-lam, 0.0) + jnp.log1p(jnp.exp(-jnp.abs(lam))))
    la = LRU_C * r * logsig
    a = jnp.exp(la)
    v = -jnp.tanh(la) * (a * a + 1.0)
    bc = jnp.where(v > 0.0, v * lax.rsqrt(v), 0.0) * (gi * xc)

    @pl.when(c == 0)
    def _():
        car_sc[...] = h0_ref[...]

    nseg = SUBLANES
    sl = tc // nseg
    sp = sl + SUBLANES
    nlb = LRU_W // LANES
    for j in range(nlb):
        for s in range(nseg):
            rows = slice(sl * s, sl * (s + 1))
            dst = slice((j * nseg + s) * sp, (j * nseg + s) * sp + sl)
            a_sc[dst, :] = a[rows, LANES * j:LANES * (j + 1)]
            b_sc[dst, :] = bc[rows, LANES * j:LANES * (j + 1)]

    def body(k, carry):
        i = (sl - 1 - k) if reverse else k
        hs, ps = carry
        hn, pn = [], []
        for j in range(nlb):
            idx = pl.ds(j * nseg * sp + i, nseg, stride=sp)
            av = a_sc[idx, :]
            h = av * hs[j] + b_sc[idx, :]
            p = av * ps[j]
            a_sc[idx, :] = p
            b_sc[idx, :] = h
            hn.append(h)
            pn.append(p)
        return tuple(hn), tuple(pn)

    zero = jnp.zeros((nseg, LANES), F32)
    one = jnp.ones((nseg, LANES), F32)
    hend, pend = lax.fori_loop(0, sl, body, ((zero,) * nlb, (one,) * nlb), unroll=4)

    order = range(nseg - 1, -1, -1) if reverse else range(nseg)
    for j in range(nlb):
        lanes = slice(LANES * j, LANES * (j + 1))
        cin = car_sc[0:1, lanes]
        for s in order:
            rows = slice(sl * s, sl * (s + 1))
            src = slice((j * nseg + s) * sp, (j * nseg + s) * sp + sl)
            h = b_sc[src, :] + a_sc[src, :] * cin
            if reverse:
                y_ref[rows, lanes] = (hf_ref[rows, lanes] + h).astype(BF16)
            else:
                y_ref[rows, lanes] = h
            cin = hend[j][s:s + 1, :] + pend[j][s:s + 1, :] * cin
        car_sc[0:1, lanes] = cin
        st_ref[:, lanes] = jnp.broadcast_to(cin, (SUBLANES, LANES))


def _lru_dir(u, hf, cw, cb, wr, wi, br, bi, lam, h0, *, reverse, row_off, nseq, seqlen, tc):
    nchunks = seqlen // tc
    hb = M_TOK // SUBLANES

    def chunk_of(c):
        return (nchunks - 1 - c) if reverse else c

    def cur(b, c):
        return ((row_off + b * seqlen) // tc + chunk_of(c), 0)

    def prv(b, c):
        return (jnp.maximum((row_off + b * seqlen + chunk_of(c) * tc) // SUBLANES - 1, 0), 0)

    def nxt(b, c):
        return (jnp.minimum((row_off + b * seqlen + (chunk_of(c) + 1) * tc) // SUBLANES, hb - 1), 0)

    def out_cur(b, c):
        return ((b * seqlen) // tc + chunk_of(c), 0)

    full = lambda shape: pl.BlockSpec(shape, lambda b, c: (0,) * len(shape))
    in_specs = [pl.BlockSpec((SUBLANES, LRU_W), prv), pl.BlockSpec((tc, LRU_W), cur),
                pl.BlockSpec((SUBLANES, LRU_W), nxt)]
    args = [u, u, u]
    if reverse:
        in_specs.append(pl.BlockSpec((tc, LRU_W), out_cur))
        args.append(hf)
    in_specs += [full((SUBLANES, LRU_W)), full((1, LRU_W)), full((LRU_W, LRU_W)), full((LRU_W, LRU_W)),
                 full((1, LRU_W)), full((1, LRU_W)), full((1, LRU_W)),
                 pl.BlockSpec((None, SUBLANES, LRU_W), lambda b, c: (b, 0, 0))]
    args += [cw, cb, wr, wi, br, bi, lam, h0]
    return pl.pallas_call(
        functools.partial(_lru_kernel, reverse=reverse, tc=tc, nchunks=nchunks),
        grid=(nseq, nchunks),
        in_specs=in_specs,
        out_specs=[pl.BlockSpec((tc, LRU_W), out_cur),
                   pl.BlockSpec((None, SUBLANES, LRU_W), lambda b, c: (b, 0, 0))],
        out_shape=[jax.ShapeDtypeStruct((nseq * seqlen, LRU_W), BF16 if reverse else F32),
                   jax.ShapeDtypeStruct((nseq, SUBLANES, LRU_W), F32)],
        scratch_shapes=[pltpu.VMEM((tc + 2 * SUBLANES, LRU_W), F32),
                        pltpu.VMEM(((LRU_W // LANES) * (tc + SUBLANES * SUBLANES), LANES), F32),
                        pltpu.VMEM(((LRU_W // LANES) * (tc + SUBLANES * SUBLANES), LANES), F32),
                        pltpu.VMEM((SUBLANES, LRU_W), F32)],
        compiler_params=_cparams(("arbitrary", "arbitrary")),
        name="lru_bwd" if reverse else "lru_fwd",
    )(*args)


def _lru_mixer(u, p, h0, *, row_off, nseq, seqlen, tc):
    kw = dict(row_off=row_off, nseq=nseq, seqlen=seqlen, tc=tc)
    hf, stf = _lru_dir(u, None, p["cw"], p["cb"], p["wr"][0], p["wi"][0], p["br"][0], p["bi"][0], p["lam"][0],
                       h0[0], reverse=False, **kw)
    y, stb = _lru_dir(u, hf, p["cw"], p["cb"], p["wr"][1], p["wi"][1], p["br"][1], p["bi"][1], p["lam"][1],
                      h0[1], reverse=True, **kw)
    return y, stf[:, 0, :], stb[:, 0, :]


def _shortconv_kernel(up_ref, uc_ref, un_ref, cw_ref, cb_ref, v_ref, x1_ref, x2_ref, ext_sc, *, tc, nchunks):
    c = pl.program_id(1)
    prev = jnp.where(c == 0, 0.0, up_ref[...])
    nxt = jnp.where(c == nchunks - 1, 0.0, un_ref[...])
    ext_sc[0:SUBLANES, :] = prev
    ext_sc[SUBLANES:SUBLANES + tc, :] = uc_ref[...]
    ext_sc[SUBLANES + tc:2 * SUBLANES + tc, :] = nxt
    for part, o_ref in enumerate((v_ref, x1_ref, x2_ref)):
        cs = slice(HY_W * part, HY_W * (part + 1))
        acc = cb_ref[:, cs]
        for k in range(3):
            acc = acc + cw_ref[k:k + 1, cs] * ext_sc[SUBLANES - 1 + k:SUBLANES - 1 + k + tc, cs]
        o_ref[...] = acc


def _shortconv(u, cw, cb, *, row_off, nseq, seqlen, tc):
    nchunks = seqlen // tc
    w = 3 * HY_W
    hb = M_TOK // SUBLANES
    cur = lambda b, c: ((row_off + b * seqlen) // tc + c, 0)
    prv = lambda b, c: (jnp.maximum((row_off + b * seqlen + c * tc) // SUBLANES - 1, 0), 0)
    nxt = lambda b, c: (jnp.minimum((row_off + b * seqlen + (c + 1) * tc) // SUBLANES, hb - 1), 0)
    out = lambda b, c: ((b * seqlen) // tc + c, 0)
    full = lambda shape: pl.BlockSpec(shape, lambda b, c: (0,) * len(shape))
    rows = nseq * seqlen
    return pl.pallas_call(
        functools.partial(_shortconv_kernel, tc=tc, nchunks=nchunks),
        grid=(nseq, nchunks),
        in_specs=[pl.BlockSpec((SUBLANES, w), prv), pl.BlockSpec((tc, w), cur), pl.BlockSpec((SUBLANES, w), nxt),
                  full((SUBLANES, w)), full((1, w))],
        out_specs=[pl.BlockSpec((tc, HY_W), out)] * 3,
        out_shape=[jax.ShapeDtypeStruct((rows, HY_W), F32)] * 3,
        scratch_shapes=[pltpu.VMEM((tc + 2 * SUBLANES, w), F32)],
        compiler_params=_cparams(("arbitrary", "arbitrary")),
        name="hy_shortconv",
    )(u, u, u, cw, cb)


def _hyfilt_kernel(z_ref, t_ref, w1_ref, b1_ref, f1_ref, w2_ref, b2_ref, f2_ref, w3_ref, ad_ref, h_ref, s_ref):
    i = pl.program_id(0)
    z = z_ref[...].astype(BF16)
    h = jnp.sin(f1_ref[...] * (_dot(z, w1_ref[...]) + b1_ref[...]))
    h = jnp.sin(f2_ref[...] * (_dot(h.astype(BF16), w2_ref[...]) + b2_ref[...]))
    h = _dot(h.astype(BF16), w3_ref[...])
    t = t_ref[...]
    ncol = h.shape[1] // LANES
    win = jnp.concatenate([jnp.exp(-t * ad_ref[:, LANES * j:LANES * (j + 1)]) for j in range(ncol)], axis=1)
    h = h * win
    h_ref[...] = h

    @pl.when(i == 0)
    def _():
        s_ref[...] = jnp.zeros_like(s_ref)

    s_ref[0:1, :] = s_ref[0:1, :] + jnp.sum(jnp.abs(h), axis=0, keepdims=True)


def _hyfilt(feats, tcol, w1, b1, f1, w2, b2, f2, w3, absdelta):
    L = feats.shape[0]
    tl = min(L, 512)
    wcols = HY_ORDER * 2 * HY_W
    full = lambda shape: pl.BlockSpec(shape, lambda i: (0,) * len(shape))
    return pl.pallas_call(
        _hyfilt_kernel,
        grid=(L // tl,),
        in_specs=[pl.BlockSpec((tl, LANES), lambda i: (i, 0)), pl.BlockSpec((tl, LANES), lambda i: (i, 0)),
                  full((LANES, HY_HID)), full((1, HY_HID)), full((1, HY_HID)),
                  full((HY_HID, HY_HID)), full((1, HY_HID)), full((1, HY_HID)),
                  full((HY_HID, wcols)), full((1, wcols))],
        out_specs=[pl.BlockSpec((tl, wcols), lambda i: (i, 0)), full((SUBLANES, wcols))],
        out_shape=[jax.ShapeDtypeStruct((L, wcols), F32), jax.ShapeDtypeStruct((SUBLANES, wcols), F32)],
        compiler_params=_cparams(("arbitrary",)),
        name="hy_filter",
    )(feats, tcol, w1, b1, f1, w2, b2, f2, w3, absdelta)


def _combine_spectrum(zr, zi, s_ref, hr_out, hi_out):
    for o in range(HY_ORDER):
        f = slice(2 * HY_W * o, 2 * HY_W * o + HY_W)
        b = slice(2 * HY_W * o + HY_W, 2 * HY_W * (o + 1))
        den = s_ref[0:1, f] + s_ref[0:1, b] + EPS
        hr_out(o, (zr[:, f] + zr[:, b]) / den)
        hi_out(o, (zi[:, f] - zi[:, b]) / den)


def _ctx_spec_kernel(f_ref, h_ref, s_ref, o_ref):
    n = f_ref.shape[0] // 2
    z = _dot(f_ref[...], h_ref[...].astype(BF16))
    zr, zi = z[:n], z[n:]

    def put_r(o, val):
        o_ref[0, :, HY_W * o:HY_W * (o + 1)] = val

    def put_i(o, val):
        o_ref[1, :, HY_W * o:HY_W * (o + 1)] = val

    _combine_spectrum(zr, zi, s_ref, put_r, put_i)


def _ctx_spectrum(fmat, hdec, s):
    n = fmat.shape[0] // 2
    return pl.pallas_call(
        _ctx_spec_kernel,
        out_shape=jax.ShapeDtypeStruct((2, n, HY_ORDER * HY_W), F32),
        compiler_params=pltpu.CompilerParams(vmem_limit_bytes=VMEM_LIMIT),
        name="hy_ctx_spectrum",
    )(fmat, hdec, s)


def _ctx_conv_kernel(z_ref, x_ref, f_ref, fi_ref, h_ref, bias_ref, o_ref, *, nb, seqlen):
    n = f_ref.shape[0] // 2
    hr = h_ref[0]
    hi = h_ref[1]
    for b in range(nb):
        rs = slice(seqlen * b, seqlen * (b + 1))
        zt = z_ref[rs, :]
        zf = _dot(f_ref[...], zt.astype(BF16))
        zr, zi = zf[:n], zf[n:]
        y = jnp.concatenate([zr * hr - zi * hi, zr * hi + zi * hr], axis=0).astype(BF16)
        conv = _dot(fi_ref[...], y)
        o_ref[rs, :] = (x_ref[rs, :] * (conv + zt * bias_ref[...])).astype(o_ref.dtype)


def _ctx_conv(z, xg, fmat, finv, hspec, bias, order, out_dtype):
    nb = 4
    n = fmat.shape[0] // 2
    rows = nb * SEQ
    return pl.pallas_call(
        functools.partial(_ctx_conv_kernel, nb=nb, seqlen=SEQ),
        grid=(BATCH // nb,),
        in_specs=[pl.BlockSpec((rows, HY_W), lambda i: (i, 0)), pl.BlockSpec((rows, HY_W), lambda i: (i, 0)),
                  pl.BlockSpec(fmat.shape, lambda i: (0, 0)), pl.BlockSpec(finv.shape, lambda i: (0, 0)),
                  pl.BlockSpec((2, n, HY_W), lambda i: (0, 0, order)),
                  pl.BlockSpec((1, HY_W), lambda i: (0, 0))],
        out_specs=pl.BlockSpec((rows, HY_W), lambda i: (i, 0)),
        out_shape=jax.ShapeDtypeStruct((M_CTX, HY_W), out_dtype),
        compiler_params=_cparams(("arbitrary",)),
        name="hy_ctx_conv",
    )(z, xg, fmat, finv, hspec, bias)


K1U = FFT_N1 // 2 + 1
SLABS = 72
PITCH = FFT_N2 + SUBLANES
NROW1 = FFT_N1 // 2
LAT_UNROLL_R = 8
LAT_UNROLL_K = 3


def _pitch_copy_in(src_ref, col, dst_sc):
    for n1 in range(NROW1):
        dst_sc[PITCH * n1:PITCH * n1 + FFT_N2, :] = src_ref[FFT_N2 * n1:FFT_N2 * (n1 + 1), col]


def _dft_stage_a(zp_scs, wa_ref, a_scs):
    def body(r, c):
        for zp_sc, a_sc in zip(zp_scs, a_scs):
            x = zp_sc[pl.ds(r, NROW1, stride=PITCH), :].astype(BF16)
            a_sc[pl.ds(r, SLABS, stride=PITCH), :] = _dot(wa_ref[...], x)
        return c

    lax.fori_loop(0, FFT_N2, body, 0, unroll=LAT_UNROLL_R)


def _load_k1(a_sc, k1):
    base = pl.multiple_of(k1 * (2 * PITCH), SUBLANES)
    a = jnp.concatenate([a_sc[pl.ds(base, FFT_N2), :], a_sc[pl.ds(base + PITCH, FFT_N2), :]], axis=0)
    return base, a.astype(BF16)


def _lat_spec_kernel(h_ref, s_ref, wa_ref, g_ref, o_ref, hf_sc, hb_sc, af_sc, ab_sc):
    _pitch_copy_in(h_ref, slice(0, LANES), hf_sc)
    _pitch_copy_in(h_ref, slice(LANES, 2 * LANES), hb_sc)
    _dft_stage_a((hf_sc, hb_sc), wa_ref, (af_sc, ab_sc))
    den = s_ref[0:1, 0:LANES] + s_ref[0:1, LANES:2 * LANES] + EPS

    def kbody(k1, c):
        _, af = _load_k1(af_sc, k1)
        _, ab = _load_k1(ab_sc, k1)
        zf = _dot(g_ref[k1], af)
        zb = _dot(g_ref[k1], ab)
        o_ref[0, k1] = (zf[:FFT_N2] + zb[:FFT_N2]) / den
        o_ref[1, k1] = (zf[FFT_N2:] - zb[FFT_N2:]) / den
        return c

    lax.fori_loop(0, K1U, kbody, 0, unroll=LAT_UNROLL_K)


def _lat_spectrum(hdec, s, wa, gtab):
    nblk = HY_ORDER * HY_W // LANES
    slab = NROW1 * PITCH
    return pl.pallas_call(
        _lat_spec_kernel,
        grid=(nblk,),
        in_specs=[pl.BlockSpec((DEC_SEQ, 2 * LANES), lambda i: (0, i)),
                  pl.BlockSpec((SUBLANES, 2 * LANES), lambda i: (0, i)),
                  pl.BlockSpec(wa.shape, lambda i: (0, 0)),
                  pl.BlockSpec(gtab.shape, lambda i: (0, 0, 0))],
        out_specs=pl.BlockSpec((2, K1U, FFT_N2, LANES), lambda i: (0, 0, 0, i)),
        out_shape=jax.ShapeDtypeStruct((2, K1U, FFT_N2, HY_ORDER * HY_W), F32),
        scratch_shapes=[pltpu.VMEM((slab, LANES), F32), pltpu.VMEM((slab, LANES), F32),
                        pltpu.VMEM((SLABS * PITCH, LANES), F32), pltpu.VMEM((SLABS * PITCH, LANES), F32)],
        compiler_params=_cparams(("arbitrary",)),
        name="hy_lat_spectrum",
    )(hdec, s, wa, gtab)


def _lat_conv_kernel(z_ref, x_ref, wa_ref, wai_ref, g_ref, gi_ref, h_ref, bias_ref, o_ref,
                     zp_sc, xp_sc, op_sc, a_sc):
    full = slice(None)
    _pitch_copy_in(z_ref, full, zp_sc)
    _pitch_copy_in(x_ref, full, xp_sc)
    _dft_stage_a((zp_sc,), wa_ref, (a_sc,))

    def kbody(k1, c):
        base, a = _load_k1(a_sc, k1)
        z = _dot(g_ref[k1], a)
        zr, zi = z[:FFT_N2], z[FFT_N2:]
        hr = h_ref[0, k1]
        hi = h_ref[1, k1]
        y = jnp.concatenate([zr * hr - zi * hi, zr * hi + zi * hr], axis=0).astype(BF16)
        bp = _dot(gi_ref[k1], y)
        a_sc[pl.ds(base, FFT_N2), :] = bp[:FFT_N2]
        a_sc[pl.ds(base + PITCH, FFT_N2), :] = bp[FFT_N2:]
        return c

    lax.fori_loop(0, K1U, kbody, 0, unroll=LAT_UNROLL_K)
    bias = bias_ref[...]

    def rbody(r, c):
        yb = a_sc[pl.ds(r, SLABS, stride=PITCH), :].astype(BF16)
        conv = _dot(wai_ref[...], yb)
        zz = zp_sc[pl.ds(r, NROW1, stride=PITCH), :]
        xx = xp_sc[pl.ds(r, NROW1, stride=PITCH), :]
        op_sc[pl.ds(r, NROW1, stride=PITCH), :] = xx * (conv + zz * bias)
        return c

    lax.fori_loop(0, FFT_N2, rbody, 0, unroll=LAT_UNROLL_R)
    for n1 in range(NROW1):
        o_ref[FFT_N2 * n1:FFT_N2 * (n1 + 1), :] = op_sc[PITCH * n1:PITCH * n1 + FFT_N2, :]


def _lat_conv(z, xg, wa, wainv, gtab, gitab, hspec, bias, order):
    ncb = HY_W // LANES
    slab = NROW1 * PITCH
    blk = pl.BlockSpec((DEC_SEQ, LANES), lambda cb, b: (b, cb))
    const = lambda a: pl.BlockSpec(a.shape, lambda cb, b: (0,) * a.ndim)
    return pl.pallas_call(
        _lat_conv_kernel,
        grid=(ncb, DEC_BATCH),
        in_specs=[blk, blk, const(wa), const(wainv), const(gtab), const(gitab),
                  pl.BlockSpec((2, K1U, FFT_N2, LANES), lambda cb, b: (0, 0, 0, order * ncb + cb)),
                  pl.BlockSpec((1, LANES), lambda cb, b: (0, cb))],
        out_specs=blk,
        out_shape=jax.ShapeDtypeStruct((M_LAT, HY_W), F32),
        scratch_shapes=[pltpu.VMEM((slab, LANES), F32), pltpu.VMEM((slab, LANES), F32),
                        pltpu.VMEM((slab, LANES), F32), pltpu.VMEM((SLABS * PITCH, LANES), F32)],
        compiler_params=_cparams(("arbitrary", "arbitrary")),
        name="hy_lat_conv",
    )(z, xg, wa, wainv, gtab, gitab, hspec, bias)


def _angle(m, n):
    return (m % n).astype(F32) * (2.0 * math.pi / n)


def _ctx_tables():
    n = 2 * SEQ
    k = jnp.arange(n, dtype=jnp.int32)[:, None]
    t = jnp.arange(SEQ, dtype=jnp.int32)[None, :]
    th = _angle(k * t, n)
    fmat = jnp.concatenate([jnp.cos(th), -jnp.sin(th)], axis=0)
    finv = jnp.concatenate([jnp.cos(th).T, -jnp.sin(th).T], axis=1) / n
    return fmat.astype(BF16), finv.astype(BF16)


def _lat_tables():
    n1, n2 = FFT_N1, FFT_N2
    n = n1 * n2
    k1 = jnp.arange(K1U, dtype=jnp.int32)
    th1 = _angle(k1[:, None] * jnp.arange(NROW1, dtype=jnp.int32)[None, :], n1)
    wa = jnp.stack([jnp.cos(th1), -jnp.sin(th1)], axis=1).reshape(2 * K1U, NROW1)
    wa = jnp.pad(wa, ((0, SLABS - 2 * K1U), (0, 0)))
    wgt = jnp.where((k1 == 0) | (k1 == n1 // 2), 1.0, 2.0)[:, None] / n
    wainv = jnp.stack([wgt * jnp.cos(th1), -wgt * jnp.sin(th1)], axis=1).reshape(2 * K1U, NROW1).T
    wainv = jnp.pad(wainv, ((0, 0), (0, SLABS - 2 * K1U)))
    k2 = jnp.arange(n2, dtype=jnp.int32)
    nn2 = jnp.arange(n2, dtype=jnp.int32)
    kfull = k1[:, None, None] + n1 * k2[None, :, None]
    th = _angle(kfull * nn2[None, None, :], n)
    gr, gi = jnp.cos(th), -jnp.sin(th)
    g = jnp.concatenate([jnp.concatenate([gr, -gi], axis=2), jnp.concatenate([gi, gr], axis=2)], axis=1)
    thT = jnp.swapaxes(th, 1, 2)
    ir, ii = jnp.cos(thT), jnp.sin(thT)
    ginv = jnp.concatenate([jnp.concatenate([ir, -ii], axis=2), jnp.concatenate([ii, ir], axis=2)], axis=1)
    return wa.astype(BF16), wainv.astype(BF16), g.astype(BF16), ginv.astype(BF16)


def _filter_features(L):
    t = jnp.linspace(0.0, 1.0, L, dtype=F32)[:, None]
    bands = (HY_EMB - 1) // 2
    w = (2.0 * math.pi / L) * jnp.arange(L, dtype=F32)[:, None]
    f = jnp.linspace(1e-4, bands - 1, bands, dtype=F32)[None, :]
    z = jnp.concatenate([t, jnp.cos(f * w), -jnp.sin(f * w)], axis=-1)
    z = jnp.pad(z, ((0, 0), (0, LANES - HY_EMB)))
    return z, jnp.broadcast_to(t, (L, LANES))


def _hyena_filter(p, L, blocked):
    feats, tcol = _filter_features(L)
    deltas = jnp.linspace(math.log(1e-2) / HY_FAST_PCT, math.log(1e-2) / HY_SLOW_PCT, HY_W, dtype=F32)
    absdelta = jnp.tile(jnp.abs(deltas), HY_ORDER * 2)[None, :]
    w3 = p["w3"]
    if blocked:
        reorder = lambda a: a.reshape(a.shape[0], HY_ORDER, 2, HY_W // LANES, LANES).transpose(
            0, 1, 3, 2, 4).reshape(a.shape[0], HY_ORDER * 2 * HY_W)
        w3, absdelta = reorder(w3), reorder(absdelta)
    return _hyfilt(feats, tcol, p["w1"], p["b1"], p["f1"], p["w2"], p["b2"], p["f2"], w3, absdelta)


def _hyena_ctx(u_hy, p, tabs):
    fmat, finv = tabs
    v, x1, x2 = _shortconv(u_hy, p["cw"], p["cb"], row_off=0, nseq=BATCH, seqlen=SEQ, tc=SEQ)
    hdec, s = _hyena_filter(p, SEQ, False)
    hspec = _ctx_spectrum(fmat, hdec, s)
    z = _ctx_conv(v, x1, fmat, finv, hspec, p["bias"][0:1], 0, F32)
    return _ctx_conv(z, x2, fmat, finv, hspec, p["bias"][1:2], 1, F32)


def _hyena_lat(u_hy, p, tabs):
    wa, wainv, gtab, gitab = tabs
    v, x1, x2 = _shortconv(u_hy, p["cw"], p["cb"], row_off=M_CTX, nseq=DEC_BATCH, seqlen=DEC_SEQ, tc=512)
    hdec, s = _hyena_filter(p, DEC_SEQ, True)
    hspec = _lat_spectrum(hdec, s, wa, gtab)
    z = _lat_conv(v, x1, wa, wainv, gtab, gitab, hspec, p["bias"][0:1], 0)
    return _lat_conv(z, x2, wa, wainv, gtab, gitab, hspec, p["bias"][1:2], 1)


def _stage3_kernel(*refs, nx, nctx):
    (mod_ref, g1_ref, wg_ref, ylc_ref, yll_ref, ymc_ref, yml_ref, yhc_ref, yhl_ref,
     wl_ref, wm_ref, wh_ref, wo_ref, g2_ref, xo_ref, xm2_ref) = refs[nx:]
    is_ctx = pl.program_id(0) < nctx
    x = _read_tokens(refs[:nx], nctx)
    xm = _rms(x, g1_ref[...]) * (1.0 + mod_ref[1:2, :]) + mod_ref[0:1, :]
    xb = xm.astype(BF16)
    merged = None
    branches = ((ylc_ref, yll_ref, wl_ref), (ymc_ref, yml_ref, wm_ref), (yhc_ref, yhl_ref, wh_ref))
    for bidx, (yc_ref, yl_ref, w_ref) in enumerate(branches):
        gate = _sigmoid(_dot(xb, wg_ref[:, D_MODEL * bidx:D_MODEL * (bidx + 1)]))
        y = jnp.where(is_ctx, yc_ref[...], yl_ref[...]).astype(BF16)
        term = gate * _dot(y, w_ref[...])
        merged = term if merged is None else merged + term
    xo = x + mod_ref[2:3, :] * _dot(merged.astype(BF16), wo_ref[...])
    xo_ref[...] = xo
    xm2 = _rms(xo, g2_ref[...]) * (1.0 + mod_ref[4:5, :]) + mod_ref[3:4, :]
    xm2_ref[...] = xm2.astype(BF16)


def _stage3(x, modl, g1, wg, ylru, ymla, yhy, wl, wm, wh, wo, g2):
    tm = TM3
    nctx = M_CTX // tm
    full = lambda shape: pl.BlockSpec(shape, lambda i: (0,) * len(shape))
    row = lambda cols: pl.BlockSpec((tm, cols), lambda i: (i, 0))
    ctx = lambda cols: pl.BlockSpec((tm, cols), lambda i: (jnp.minimum(i, nctx - 1), 0))
    lat = lambda cols: pl.BlockSpec((tm, cols), lambda i: (jnp.maximum(i - nctx, 0), 0))
    return pl.pallas_call(
        functools.partial(_stage3_kernel, nx=len(x), nctx=nctx),
        grid=(M_TOK // tm,),
        in_specs=_token_specs(x, tm) + [
                  pl.BlockSpec((None, SUBLANES, D_MODEL), lambda i: (_mod_row(i, tm), 0, 0)),
                  full((1, D_MODEL)), full((D_MODEL, 3 * D_MODEL)),
                  ctx(LRU_W), lat(LRU_W), ctx(N_HEADS * D_V), lat(N_HEADS * D_V), ctx(HY_W), lat(HY_W),
                  full((LRU_W, D_MODEL)), full((N_HEADS * D_V, D_MODEL)), full((HY_W, D_MODEL)),
                  full((D_MODEL, D_MODEL)), full((1, D_MODEL))],
        out_specs=[row(D_MODEL), row(D_MODEL)],
        out_shape=[jax.ShapeDtypeStruct((M_TOK, D_MODEL), F32), jax.ShapeDtypeStruct((M_TOK, D_MODEL), BF16)],
        compiler_params=_cparams(("arbitrary",)),
        name="stage3",
    )(*x, modl, g1, wg, *ylru, *ymla, *yhy, wl, wm, wh, wo, g2)


def _ffn_kernel(xm_ref, x_ref, mod_ref, wg_ref, wu_ref, wd_ref, o_ref, acc_sc, *, nchunks):
    j = pl.program_id(1)
    xb = xm_ref[...]
    g = _dot(xb, wg_ref[...])
    u = _dot(xb, wu_ref[...])
    hid = (g * _sigmoid(g) * u).astype(BF16)
    part = _dot(hid, wd_ref[...])

    @pl.when(j == 0)
    def _():
        acc_sc[...] = part

    @pl.when(j > 0)
    def _():
        acc_sc[...] = acc_sc[...] + part

    @pl.when(j == nchunks - 1)
    def _():
        o_ref[...] = x_ref[...] + mod_ref[5:6, :] * acc_sc[...]


def _ffn_dense(xm2, x, modl, wg, wu, wd):
    tm = TM_FFN
    nchunks = 2
    cw = D_FF // nchunks
    return pl.pallas_call(
        functools.partial(_ffn_kernel, nchunks=nchunks),
        grid=(M_TOK // tm, nchunks),
        in_specs=[pl.BlockSpec((tm, D_MODEL), lambda i, j: (i, 0)),
                  pl.BlockSpec((tm, D_MODEL), lambda i, j: (i, 0)),
                  pl.BlockSpec((None, SUBLANES, D_MODEL), lambda i, j: (_mod_row(i, tm), 0, 0)),
                  pl.BlockSpec((D_MODEL, cw), lambda i, j: (0, j)),
                  pl.BlockSpec((D_MODEL, cw), lambda i, j: (0, j)),
                  pl.BlockSpec((cw, D_MODEL), lambda i, j: (j, 0))],
        out_specs=pl.BlockSpec((tm, D_MODEL), lambda i, j: (i, 0)),
        out_shape=jax.ShapeDtypeStruct((M_TOK, D_MODEL), F32),
        scratch_shapes=[pltpu.VMEM((tm, D_MODEL), F32)],
        compiler_params=_cparams(("arbitrary", "arbitrary")),
        name="ffn_dense",
    )(xm2, x, modl, wg, wu, wd)


MOE_TILE = 256
MOE_ROWS = 2 * M_TOK + N_EXPERTS * MOE_TILE
MOE_TILES = MOE_ROWS // MOE_TILE
MOE_PAD_ROWS = MOE_ROWS - 2 * M_TOK


def _router_kernel(xm_ref, wr_ref, tri_ref, sel_ref, xp_ref, cnt_ref, base_sc):
    i = pl.program_id(0)
    xb = xm_ref[...]
    tm = xb.shape[0]
    lane = lax.broadcasted_iota(jnp.int32, (tm, LANES), 1)

    @pl.when(i == 0)
    def _():
        base_sc[...] = jnp.zeros_like(base_sc)

    logits = jnp.where(lane < N_EXPERTS, _dot(xb, wr_ref[...]), -1e30)
    mx = jnp.max(logits, axis=-1, keepdims=True)
    ex = jnp.exp(logits - mx)
    probs = ex / jnp.sum(ex, axis=-1, keepdims=True)
    p1 = jnp.max(probs, axis=-1, keepdims=True)
    i1 = jnp.min(jnp.where(probs == p1, lane, LANES), axis=-1, keepdims=True)
    rest = jnp.where(lane == i1, -1.0, probs)
    p2 = jnp.max(rest, axis=-1, keepdims=True)
    i2 = jnp.min(jnp.where(rest == p2, lane, LANES), axis=-1, keepdims=True)
    tot = p1 + p2
    oh1 = lane == i1
    oh2 = lane == i2
    oh = jnp.where(oh1 | oh2, 1.0, 0.0)
    before = base_sc[0:1, :] + _dot(tri_ref[...], oh.astype(BF16))
    r1 = jnp.sum(jnp.where(oh1, before, 0.0), axis=-1, keepdims=True)
    r2 = jnp.sum(jnp.where(oh2, before, 0.0), axis=-1, keepdims=True)
    base_sc[0:1, :] = base_sc[0:1, :] + jnp.sum(oh, axis=0, keepdims=True)
    cnt_ref[...] = base_sc[...]
    sel = jnp.where(lane == 0, p1 / tot, 0.0) + jnp.where(lane == 1, p2 / tot, 0.0)
    sel = sel + jnp.where(lane == 2, i1.astype(F32), 0.0) + jnp.where(lane == 3, i2.astype(F32), 0.0)
    sel_ref[...] = sel + jnp.where(lane == 4, r1, 0.0) + jnp.where(lane == 5, r2, 0.0)
    xp_ref[...] = xb.astype(F32)


def _moe_router(xm2, wr):
    tm = TM_FFN
    tri = jnp.tril(jnp.ones((tm, tm), F32), -1).astype(BF16)
    return pl.pallas_call(
        _router_kernel,
        grid=(M_TOK // tm,),
        in_specs=[pl.BlockSpec((tm, D_MODEL), lambda i: (i, 0)),
                  pl.BlockSpec((D_MODEL, LANES), lambda i: (0, 0)),
                  pl.BlockSpec((tm, tm), lambda i: (0, 0))],
        out_specs=[pl.BlockSpec((tm, LANES), lambda i: (i, 0)),
                   pl.BlockSpec((tm, D_MODEL), lambda i: (i, 0)),
                   pl.BlockSpec((SUBLANES, LANES), lambda i: (0, 0))],
        out_shape=[jax.ShapeDtypeStruct((M_TOK, LANES), F32),
                   jax.ShapeDtypeStruct((M_TOK, D_MODEL), F32),
                   jax.ShapeDtypeStruct((SUBLANES, LANES), F32)],
        scratch_shapes=[pltpu.VMEM((SUBLANES, LANES), F32)],
        compiler_params=_cparams(("arbitrary",)),
        name="moe_router",
    )(xm2, wr, tri)


def _row_copy(src, srow, dst, drow, sem):
    return pltpu.make_async_copy(src.at[pl.ds(srow, 1), :], dst.at[pl.ds(drow, 1), :], sem)


def _dispatch_kernel(pos_ref, pad_ref, xp_ref, xs_ref, ring_sc, zero_sc, sem, *, nsteps):
    i = pl.program_id(0)
    slot = i % 2
    nrow = MOE_TILE

    def wait_slot(s):
        for _ in range(2):
            pltpu.make_async_copy(ring_sc.at[s], xs_ref.at[pl.ds(0, nrow), :], sem.at[s]).wait()

    @pl.when(i >= 2)
    def _():
        wait_slot(slot)

    ring_sc[slot] = xp_ref[...]

    def body(t, c):
        _row_copy(ring_sc.at[slot], t, xs_ref, pos_ref[0, 2 * t], sem.at[slot]).start()
        _row_copy(ring_sc.at[slot], t, xs_ref, pos_ref[0, 2 * t + 1], sem.at[slot]).start()
        return c

    lax.fori_loop(0, nrow, body, 0, unroll=4)

    @pl.when(i == nsteps - 1)
    def _():
        zero_sc[...] = jnp.zeros_like(zero_sc)

        def zbody(t, c):
            _row_copy(zero_sc, 0, xs_ref, pad_ref[t], sem.at[2]).start()
            return c

        lax.fori_loop(0, MOE_PAD_ROWS, zbody, 0, unroll=4)
        wait_slot(1 - slot)
        wait_slot(slot)

        for _ in range(MOE_PAD_ROWS // nrow):
            pltpu.make_async_copy(ring_sc.at[0], xs_ref.at[pl.ds(0, nrow), :], sem.at[2]).wait()


def _moe_dispatch(xp, pos, padrows):
    nsteps = M_TOK // MOE_TILE
    return pl.pallas_call(
        functools.partial(_dispatch_kernel, nsteps=nsteps),
        grid=(nsteps,),
        in_specs=[pl.BlockSpec((None, 1, 2 * MOE_TILE), lambda i: (i, 0, 0), memory_space=pltpu.SMEM),
                  pl.BlockSpec(memory_space=pltpu.SMEM),
                  pl.BlockSpec((MOE_TILE, D_MODEL), lambda i: (i, 0))],
        out_specs=pl.BlockSpec(memory_space=pl.ANY),
        out_shape=jax.ShapeDtypeStruct((MOE_ROWS, D_MODEL), F32),
        scratch_shapes=[pltpu.VMEM((2, MOE_TILE, D_MODEL), F32), pltpu.VMEM((SUBLANES, D_MODEL), F32),
                        pltpu.SemaphoreType.DMA((3,))],
        compiler_params=_cparams(("arbitrary",)),
        name="moe_dispatch",
    )(pos, padrows, xp)


def _experts_kernel(te_ref, nu_ref, xs_ref, wg_ref, wu_ref, wd_ref, ys_ref):
    i = pl.program_id(0)

    @pl.when(i < nu_ref[0])
    def _():
        xb = xs_ref[...].astype(BF16)
        g = _dot(xb, wg_ref[...])
        u = _dot(xb, wu_ref[...])
        hid = (g * _sigmoid(g) * u).astype(BF16)
        ys_ref[...] = _dot(hid, wd_ref[...])

    @pl.when(i >= nu_ref[0])
    def _():
        ys_ref[...] = jnp.zeros_like(ys_ref)


def _moe_experts(tile_expert, n_used, xs, wg, wu, wd):
    grid_spec = pltpu.PrefetchScalarGridSpec(
        num_scalar_prefetch=2,
        grid=(MOE_TILES,),
        in_specs=[pl.BlockSpec((MOE_TILE, D_MODEL), lambda i, te, nu: (i, 0)),
                  pl.BlockSpec((None, D_MODEL, D_FF_E), lambda i, te, nu: (te[i], 0, 0)),
                  pl.BlockSpec((None, D_MODEL, D_FF_E), lambda i, te, nu: (te[i], 0, 0)),
                  pl.BlockSpec((None, D_FF_E, D_MODEL), lambda i, te, nu: (te[i], 0, 0))],
        out_specs=pl.BlockSpec((MOE_TILE, D_MODEL), lambda i, te, nu: (i, 0)),
    )
    return pl.pallas_call(
        _experts_kernel,
        grid_spec=grid_spec,
        out_shape=jax.ShapeDtypeStruct((MOE_ROWS, D_MODEL), F32),
        compiler_params=_cparams(("arbitrary",)),
        name="moe_experts",
    )(tile_expert, n_used, xs, wg, wu, wd)


def _combine_kernel(pos_ref, x_ref, mod_ref, sel_ref, ys_ref, oc_ref, ol_ref, buf_sc, sem, *, nsteps, nctx):
    i = pl.program_id(0)
    slot = i % 2
    nrow = MOE_TILE

    def start(s, off):
        def body(t, c):
            _row_copy(ys_ref, pos_ref[0, off + 2 * t], buf_sc.at[s, 0], t, sem.at[s]).start()
            _row_copy(ys_ref, pos_ref[0, off + 2 * t + 1], buf_sc.at[s, 1], t, sem.at[s]).start()
            return c
        lax.fori_loop(0, nrow, body, 0, unroll=4)

    @pl.when(i == 0)
    def _():
        start(0, 0)

    @pl.when(i + 1 < nsteps)
    def _():
        start(1 - slot, 2 * nrow)

    for k in range(2):
        pltpu.make_async_copy(ys_ref.at[pl.ds(0, nrow), :], buf_sc.at[slot, k], sem.at[slot]).wait()
    lane = lax.broadcasted_iota(jnp.int32, (nrow, LANES), 1)
    sel = sel_ref[...]
    w1 = jnp.sum(jnp.where(lane == 0, sel, 0.0), axis=-1, keepdims=True)
    w2 = jnp.sum(jnp.where(lane == 1, sel, 0.0), axis=-1, keepdims=True)
    y = w1 * buf_sc[slot, 0] + w2 * buf_sc[slot, 1]
    val = x_ref[...] + mod_ref[5:6, :] * y

    @pl.when(i < nctx)
    def _():
        oc_ref[...] = val

    @pl.when(i >= nctx)
    def _():
        ol_ref[...] = val


def _moe_combine(pos2, x, modl, sel, ys):
    nsteps = M_TOK // MOE_TILE
    tm = MOE_TILE
    nctx = M_CTX // tm
    return pl.pallas_call(
        functools.partial(_combine_kernel, nsteps=nsteps, nctx=nctx),
        grid=(nsteps,),
        in_specs=[pl.BlockSpec((None, 1, 4 * MOE_TILE), lambda i: (i, 0, 0), memory_space=pltpu.SMEM),
                  pl.BlockSpec((tm, D_MODEL), lambda i: (i, 0)),
                  pl.BlockSpec((None, SUBLANES, D_MODEL), lambda i: (_mod_row(i, tm), 0, 0)),
                  pl.BlockSpec((tm, LANES), lambda i: (i, 0)),
                  pl.BlockSpec(memory_space=pl.ANY)],
        out_specs=[pl.BlockSpec((tm, D_MODEL), lambda i: (jnp.minimum(i, nctx - 1), 0)),
                   pl.BlockSpec((tm, D_MODEL), lambda i: (jnp.maximum(i - nctx, 0), 0))],
        out_shape=[jax.ShapeDtypeStruct((M_CTX, D_MODEL), F32), jax.ShapeDtypeStruct((M_LAT, D_MODEL), F32)],
        scratch_shapes=[pltpu.VMEM((2, 2, MOE_TILE, D_MODEL), F32), pltpu.SemaphoreType.DMA((2,))],
        compiler_params=_cparams(("arbitrary",)),
        name="moe_combine",
    )(pos2, x, modl, sel, ys)


def _ffn_moe(xm2, x, modl, wr, wg, wu, wd):
    sel, xp, cnt = _moe_router(xm2, wr)
    counts = cnt[0, :N_EXPERTS].astype(jnp.int32)
    padded = ((counts + MOE_TILE - 1) // MOE_TILE) * MOE_TILE
    ends = jnp.cumsum(padded)
    offs = ends - padded
    experts = sel[:, 2:4].astype(jnp.int32)
    ranks = sel[:, 4:6].astype(jnp.int32)
    pos = offs[experts] + ranks
    tile_start = jnp.arange(MOE_TILES, dtype=jnp.int32) * MOE_TILE
    tile_expert = jnp.minimum(jnp.sum(tile_start[:, None] >= ends[None, :], axis=1), N_EXPERTS - 1)
    n_used = (ends[-1] // MOE_TILE).astype(jnp.int32)[None]
    rows = jnp.arange(MOE_ROWS, dtype=jnp.int32)
    row_expert = jnp.repeat(tile_expert, MOE_TILE)
    written = (rows < ends[-1]) & (rows - offs[row_expert] < counts[row_expert])
    padrows = jnp.nonzero(~written, size=MOE_PAD_ROWS)[0].astype(jnp.int32)
    pos_tiles = pos.reshape(M_TOK // MOE_TILE, 1, 2 * MOE_TILE)
    xs = _moe_dispatch(xp, pos_tiles, padrows)
    ys = _moe_experts(tile_expert.astype(jnp.int32), n_used, xs, wg, wu, wd)
    nxt = jnp.concatenate([pos_tiles[1:], pos_tiles[-1:]], axis=0)
    pos2 = jnp.concatenate([pos_tiles, nxt], axis=2)
    return tuple(_moe_combine(pos2, x, modl, sel, ys))


def _block_diag(w):
    nb, bs, _ = w.shape
    eye = jnp.eye(nb, dtype=w.dtype)
    return jnp.einsum("njk,nm->njmk", w, eye).reshape(nb * bs, nb * bs)


def _head_pad_cols(w, width):
    r = w.shape[0]
    return jnp.pad(w, ((0, 0), (0, 0), (0, HEAD_PAD - width))).reshape(r, N_HEADS * HEAD_PAD)


def _swap_rope_pairs(a):
    nope, rope = a[..., :D_NOPE], a[..., D_NOPE:]
    sw = rope.reshape(rope.shape[:-1] + (D_ROPE // 2, 2))[..., ::-1].reshape(rope.shape)
    return jnp.concatenate([nope, sw], axis=-1)


def _head_gain(g):
    rows = jnp.stack([g, _swap_rope_pairs(g)], axis=0)
    return jnp.pad(rows, ((0, SUBLANES - 2), (0, HEAD_PAD - D_QK)))


def _rope_tables(tm):
    rows = DEC_SEQ // GRID_W
    row = jnp.repeat(jnp.arange(rows, dtype=F32), GRID_W)
    col = jnp.tile(jnp.arange(GRID_W, dtype=F32), rows)
    half = D_ROPE // 2
    inv_freq = ROPE_BASE ** (-jnp.arange(0, half, 2, dtype=F32) / half)
    ang = jnp.concatenate([row[:, None] * inv_freq, col[:, None] * inv_freq], axis=-1)
    cos, sin = jnp.cos(ang), jnp.sin(ang)
    cos2 = jnp.repeat(cos, 2, axis=1)
    sin2 = jnp.stack([-sin, sin], axis=-1).reshape(DEC_SEQ, D_ROPE)
    cos_t = jnp.pad(cos2, ((0, 0), (D_NOPE, HEAD_PAD - D_QK)), constant_values=1.0)
    sin_t = jnp.pad(sin2, ((0, 0), (D_NOPE, HEAD_PAD - D_QK)))
    cos_t = jnp.concatenate([jnp.ones((tm, HEAD_PAD), F32), cos_t], axis=0)
    sin_t = jnp.concatenate([jnp.zeros((tm, HEAD_PAD), F32), sin_t], axis=0)
    return cos_t, sin_t


def kernel(x_prompt, x_sample, cache_ckv, cache_krope, state_lru, c, c_ctx, norm1, norm2, w_ada, b_ada, w_in, mla_q_norm, mla_kv_norm, mla_w_uq, mla_w_uk, mla_w_uv, mla_q_qknorm, mla_k_qknorm, lru_conv_w, lru_conv_b, lru_w_gate, lru_b_gate, lru_lambda, hy_conv_w, hy_conv_b, hy_w1, hy_b1, hy_freq1, hy_w2, hy_b2, hy_freq2, hy_w3, hy_bias, w_lru_out, w_mla_out, w_hy_out, w_out, ffn_w_gate, ffn_w_up, ffn_w_down, moe_w_router, moe_w_gate, moe_w_up, moe_w_down):
    x = (x_prompt.reshape(M_CTX, D_MODEL), x_sample.reshape(M_LAT, D_MODEL))

    cond = jnp.concatenate([c_ctx[None, :], c, jnp.zeros((SUBLANES - 1 - DEC_BATCH, D_MODEL), F32)], axis=0)
    mod = _adaln(cond, w_ada, b_ada).reshape(DEPTH, SUBLANES, 6, D_MODEL)
    mod = jnp.pad(mod, ((0, 0), (0, 0), (0, SUBLANES - 6), (0, 0)))

    cos_t, sin_t = _rope_tables(TM1)
    ctx_tabs = _ctx_tables()
    lat_tabs = _lat_tables()
    zero_state = jnp.zeros((BATCH, SUBLANES, LRU_W), F32)

    ckv_out, kr_out, st_out = [], [], []
    for l in range(DEPTH):
        wl = w_in[l]
        wkr = jnp.concatenate([jnp.zeros((D_MODEL, D_NOPE), F32), wl[:, 896:928]], axis=1)
        krblk = lambda w: jnp.pad(w, ((0, 0), (0, HEAD_PAD - D_QK)))
        w1 = jnp.concatenate([wl[:, :896], krblk(wkr), krblk(_swap_rope_pairs(wkr)), wl[:, 928:2464]],
                             axis=1).astype(BF16)
        wgates = wl[:, 2464:].astype(BF16)
        wuq = _head_pad_cols(mla_w_uq[l], D_QK).astype(BF16)
        wuqs = _head_pad_cols(_swap_rope_pairs(mla_w_uq[l]), D_QK).astype(BF16)
        wuk = _head_pad_cols(mla_w_uk[l], D_NOPE).astype(BF16)
        wuv = mla_w_uv[l].reshape(KV_RANK, N_HEADS * D_V).astype(BF16)
        gq = _head_gain(mla_q_qknorm[l])
        gk = _head_gain(mla_k_qknorm[l])

        ulru, uhy, ckv, krb, q, k, v = _stage1(
            x, mod[l], norm1[l][None, :], w1, mla_kv_norm[l][None, :], mla_q_norm[l][None, :],
            wuq, wuqs, gq, wuk, gk, wuv, cos_t, sin_t)
        ckv_out.append(ckv[:M_CTX].reshape(BATCH, SEQ, KV_RANK))
        kr_out.append(krb[:M_CTX, D_NOPE:D_QK].reshape(BATCH, SEQ, D_ROPE))

        kc, vc = _kvprep(cache_ckv[:, l].reshape(DEC_BATCH * PAST_LEN, KV_RANK),
                         jnp.pad(cache_krope[:, l].reshape(DEC_BATCH * PAST_LEN, D_ROPE),
                                 ((0, 0), (D_NOPE, HEAD_PAD - D_QK))),
                         wuk, gk, wuv)
        ymla = (_attention_ctx(q, k, v), _attention_lat(q, k, v, kc, vc))

        lp = dict(
            cw=jnp.pad(lru_conv_w[l], ((0, SUBLANES - 4), (0, 0))), cb=lru_conv_b[l][None, :],
            wr=[_block_diag(lru_w_gate[l, d, 0]).astype(BF16) for d in range(2)],
            wi=[_block_diag(lru_w_gate[l, d, 1]).astype(BF16) for d in range(2)],
            br=[lru_b_gate[l, d, 0][None, :] for d in range(2)],
            bi=[lru_b_gate[l, d, 1][None, :] for d in range(2)],
            lam=[lru_lambda[l, d][None, :] for d in range(2)])
        y_c, stf, stb = _lru_mixer(ulru, lp, (zero_state, zero_state), row_off=0, nseq=BATCH, seqlen=SEQ, tc=SEQ)
        st_out.append(jnp.stack([stf, stb], axis=1))
        h0 = [jnp.broadcast_to(state_lru[:, l, d][:, None, :], (DEC_BATCH, SUBLANES, LRU_W)) for d in range(2)]
        y_l, _, _ = _lru_mixer(ulru, lp, h0, row_off=M_CTX, nseq=DEC_BATCH, seqlen=DEC_SEQ, tc=512)
        ylru = (y_c, y_l)

        hp = dict(
            cw=jnp.pad(hy_conv_w[l], ((0, SUBLANES - 3), (0, 0))), cb=hy_conv_b[l][None, :],
            w1=jnp.pad(hy_w1[l], ((0, LANES - HY_EMB), (0, 0))).astype(BF16), b1=hy_b1[l][None, :],
            f1=hy_freq1[l][None, :], w2=hy_w2[l].astype(BF16), b2=hy_b2[l][None, :], f2=hy_freq2[l][None, :],
            w3=hy_w3[l].astype(BF16), bias=hy_bias[l])
        yhy = (_hyena_ctx(uhy, hp, ctx_tabs), _hyena_lat(uhy, hp, lat_tabs))

        xmid, xm2 = _stage3(x, mod[l], norm1[l][None, :], wgates, ylru, ymla, yhy,
                            w_lru_out[l].astype(BF16), w_mla_out[l].astype(BF16), w_hy_out[l].astype(BF16),
                            w_out[l].astype(BF16), norm2[l][None, :])
        j = l // 2
        if l % 2 == 0:
            x = (_ffn_dense(xm2, xmid, mod[l], ffn_w_gate[j].astype(BF16), ffn_w_up[j].astype(BF16),
                            ffn_w_down[j].astype(BF16)),)
        else:
            wr = jnp.pad(moe_w_router[j], ((0, 0), (0, LANES - N_EXPERTS))).astype(BF16)
            x = _ffn_moe(xm2, xmid, mod[l], wr, moe_w_gate[j].astype(BF16), moe_w_up[j].astype(BF16),
                         moe_w_down[j].astype(BF16))

    xc, xl = x if len(x) == 2 else (x[0][:M_CTX], x[0][M_CTX:])
    y_prompt = xc.reshape(BATCH, SEQ, D_MODEL)
    y_sample = xl.reshape(DEC_BATCH, DEC_SEQ, D_MODEL)
    return (y_prompt, y_sample, jnp.stack(ckv_out, axis=1), jnp.stack(kr_out, axis=1), jnp.stack(st_out, axis=1))
```

```python
import functools
import math

import jax
import jax.numpy as jnp
from jax import lax
from jax.experimental import pallas as pl
from jax.experimental.pallas import tpu as pltpu

F32 = jnp.float32
BF16 = jnp.bfloat16

D_MODEL = 1024
BATCH = 32
SEQ = 256
DEPTH = 2
DEC_BATCH = 2
DEC_SEQ = 4096
PAST_LEN = 512
GRID_W = 64
EPS = 1e-6
LRU_W = 512
LRU_BLOCKS = 8
LRU_C = 8.0
N_HEADS = 8
D_NOPE = 64
D_ROPE = 32
D_QK = D_NOPE + D_ROPE
D_V = 64
Q_RANK = 256
KV_RANK = 128
ROPE_BASE = 10000.0
HY_W = 512
HY_ORDER = 2
HY_EMB = 33
HY_HID = 64
HY_FAST_PCT = 0.3
HY_SLOW_PCT = 1.5
D_FF = 2816
N_EXPERTS = 8
D_FF_E = 1408

LANES = 128
SUBLANES = 8
VMEM_LIMIT = 56 * 1024 * 1024

M_CTX = BATCH * SEQ
M_LAT = DEC_BATCH * DEC_SEQ
M_TOK = M_CTX + M_LAT
TM1 = 512
TM3 = 512
TM_FFN = 512
W1_COLS = 2688
HEAD_PAD = LANES
QK_SCALE = math.log2(math.e) / math.sqrt(D_QK)
ATTN_TQ = 256

FFT_N1 = 64
FFT_N2 = 128


def _cparams(sem, vmem=VMEM_LIMIT):
    return pltpu.CompilerParams(dimension_semantics=sem, vmem_limit_bytes=vmem)


def _dot(a, b):
    return jnp.dot(a, b, preferred_element_type=F32)


def _rms(x, g):
    ms = jnp.mean(x * x, axis=-1, keepdims=True)
    return x * lax.rsqrt(ms + EPS) * g


def _sigmoid(x):
    return 1.0 / (1.0 + jnp.exp(-x))


def _ada_kernel(c_ref, w_ref, b_ref, o_ref):
    c = c_ref[...]
    s = (c * _sigmoid(c)).astype(BF16)
    o_ref[...] = _dot(s, w_ref[...].astype(BF16)) + b_ref[...]


def _adaln(cond, w_ada, b_ada):
    tn = 1024
    n6 = 6 * D_MODEL
    return pl.pallas_call(
        _ada_kernel,
        grid=(DEPTH, n6 // tn),
        in_specs=[
            pl.BlockSpec((SUBLANES, D_MODEL), lambda l, j: (0, 0)),
            pl.BlockSpec((None, D_MODEL, tn), lambda l, j: (l, 0, j)),
            pl.BlockSpec((None, 1, tn), lambda l, j: (l, 0, j)),
        ],
        out_specs=pl.BlockSpec((None, SUBLANES, tn), lambda l, j: (l, 0, j)),
        out_shape=jax.ShapeDtypeStruct((DEPTH, SUBLANES, n6), F32),
        compiler_params=_cparams(("arbitrary", "arbitrary")),
        name="adaln",
    )(cond, w_ada, b_ada.reshape(DEPTH, 1, n6))


def _mod_row(i, tm):
    nctx = M_CTX // tm
    per = DEC_SEQ // tm
    return jnp.where(i < nctx, 0, 1 + (i - nctx) // per)


def _rope_blk(i, tm):
    nctx = M_CTX // tm
    per = DEC_SEQ // tm
    return jnp.where(i < nctx, 0, 1 + (i - nctx) % per)


def _finish_head(raw, raw_sw, gc, gs, out_ref, sl):
    ms = jnp.sum(raw * raw, axis=-1, keepdims=True) * (1.0 / D_QK)
    rs = lax.rsqrt(ms + EPS)
    val = raw * gc
    if gs is not None:
        val = val + raw_sw * gs
    out_ref[:, sl] = (val * rs).astype(BF16)


def _read_tokens(x_refs, nctx):
    if len(x_refs) == 1:
        return x_refs[0][...]
    return jnp.where(pl.program_id(0) < nctx, x_refs[0][...], x_refs[1][...])


def _token_specs(x, tm):
    nctx = M_CTX // tm
    if len(x) == 1:
        return [pl.BlockSpec((tm, D_MODEL), lambda i: (i, 0))]
    return [pl.BlockSpec((tm, D_MODEL), lambda i: (jnp.minimum(i, nctx - 1), 0)),
            pl.BlockSpec((tm, D_MODEL), lambda i: (jnp.maximum(i - nctx, 0), 0))]


def _stage1_kernel(*refs, nx, nctx):
    (mod_ref, g1_ref, w1_ref, gkv_ref, gqn_ref, wuq_ref, wuqs_ref, gq_ref, wuk_ref, gk_ref,
     wuv_ref, cos_ref, sin_ref, ulru_ref, uhy_ref, ckv_ref, krb_ref, q_ref, k_ref, v_ref) = refs[nx:]
    x = _read_tokens(refs[:nx], nctx)
    xm = _rms(x, g1_ref[...]) * (1.0 + mod_ref[1:2, :]) + mod_ref[0:1, :]
    xb = xm.astype(BF16)
    ulru_ref[...] = _dot(xb, w1_ref[:, 0:512])
    qc = _dot(xb, w1_ref[:, 512:768])
    ckv = _dot(xb, w1_ref[:, 768:896])
    krb = _dot(xb, w1_ref[:, 896:1024])
    krs = _dot(xb, w1_ref[:, 1024:1152])
    uhy_ref[...] = _dot(xb, w1_ref[:, 1152:2688])
    ckvn = _rms(ckv, gkv_ref[...])
    ckv_ref[...] = ckvn
    krb_ref[...] = krb
    qn = _rms(qc, gqn_ref[...]).astype(BF16)
    cb = ckvn.astype(BF16)
    v_ref[...] = _dot(cb, wuv_ref[...]).astype(BF16)
    cos = cos_ref[...]
    sin = sin_ref[...]
    gcq = cos * (gq_ref[0:1, :] * QK_SCALE)
    gsq = sin * (gq_ref[1:2, :] * QK_SCALE)
    gck = cos * gk_ref[0:1, :]
    gsk = sin * gk_ref[1:2, :]
    for h in range(N_HEADS):
        sl = slice(HEAD_PAD * h, HEAD_PAD * (h + 1))
        _finish_head(_dot(qn, wuq_ref[:, sl]), _dot(qn, wuqs_ref[:, sl]), gcq, gsq, q_ref, sl)
        _finish_head(_dot(cb, wuk_ref[:, sl]) + krb, krs, gck, gsk, k_ref, sl)


def _stage1(x, modl, g1, w1, gkv, gqn, wuq, wuqs, gq, wuk, gk, wuv, cos_t, sin_t):
    tm = TM1
    full = lambda shape: pl.BlockSpec(shape, lambda i: (0,) * len(shape))
    row = lambda cols: pl.BlockSpec((tm, cols), lambda i: (i, 0))
    hw = N_HEADS * HEAD_PAD
    return pl.pallas_call(
        functools.partial(_stage1_kernel, nx=len(x), nctx=M_CTX // tm),
        grid=(M_TOK // tm,),
        in_specs=_token_specs(x, tm) + [
            pl.BlockSpec((None, SUBLANES, D_MODEL), lambda i: (_mod_row(i, tm), 0, 0)),
            full((1, D_MODEL)),
            full((D_MODEL, W1_COLS)),
            full((1, KV_RANK)),
            full((1, Q_RANK)),
            full((Q_RANK, hw)),
            full((Q_RANK, hw)),
            full((SUBLANES, HEAD_PAD)),
            full((KV_RANK, hw)),
            full((SUBLANES, HEAD_PAD)),
            full((KV_RANK, N_HEADS * D_V)),
            pl.BlockSpec((tm, LANES), lambda i: (_rope_blk(i, tm), 0)),
            pl.BlockSpec((tm, LANES), lambda i: (_rope_blk(i, tm), 0)),
        ],
        out_specs=[row(LRU_W), row(3 * HY_W), row(KV_RANK), row(LANES), row(hw), row(hw), row(N_HEADS * D_V)],
        out_shape=[
            jax.ShapeDtypeStruct((M_TOK, LRU_W), F32),
            jax.ShapeDtypeStruct((M_TOK, 3 * HY_W), F32),
            jax.ShapeDtypeStruct((M_TOK, KV_RANK), F32),
            jax.ShapeDtypeStruct((M_TOK, LANES), F32),
            jax.ShapeDtypeStruct((M_TOK, hw), BF16),
            jax.ShapeDtypeStruct((M_TOK, hw), BF16),
            jax.ShapeDtypeStruct((M_TOK, N_HEADS * D_V), BF16),
        ],
        compiler_params=_cparams(("arbitrary",)),
        name="stage1",
    )(*x, modl, g1, w1, gkv, gqn, wuq, wuqs, gq, wuk, gk, wuv, cos_t, sin_t)


def _kvprep_kernel(ckv_ref, krb_ref, wuk_ref, gk_ref, wuv_ref, k_ref, v_ref):
    cb = ckv_ref[...].astype(BF16)
    v_ref[...] = _dot(cb, wuv_ref[...]).astype(BF16)
    krb = krb_ref[...]
    for h in range(N_HEADS):
        sl = slice(HEAD_PAD * h, HEAD_PAD * (h + 1))
        _finish_head(_dot(cb, wuk_ref[:, sl]) + krb, None, gk_ref[0:1, :], None, k_ref, sl)


def _kvprep(ckv, krb, wuk, gk, wuv):
    rows = ckv.shape[0]
    tm = TM1
    hw = N_HEADS * HEAD_PAD
    full = lambda shape: pl.BlockSpec(shape, lambda i: (0,) * len(shape))
    row = lambda cols: pl.BlockSpec((tm, cols), lambda i: (i, 0))
    return pl.pallas_call(
        _kvprep_kernel,
        grid=(rows // tm,),
        in_specs=[row(KV_RANK), row(LANES), full((KV_RANK, hw)), full((SUBLANES, HEAD_PAD)),
                  full((KV_RANK, N_HEADS * D_V))],
        out_specs=[row(hw), row(N_HEADS * D_V)],
        out_shape=[jax.ShapeDtypeStruct((rows, hw), BF16), jax.ShapeDtypeStruct((rows, N_HEADS * D_V), BF16)],
        compiler_params=_cparams(("arbitrary",)),
        name="kvprep",
    )(ckv, krb, wuk, gk, wuv)


def _attn_kernel(*refs, heads, nseg):
    q_ref = refs[0]
    k_refs = refs[1:1 + nseg]
    v_refs = refs[1 + nseg:1 + 2 * nseg]
    o_ref = refs[1 + 2 * nseg]
    tq = q_ref.shape[0]
    lane = lax.broadcasted_iota(jnp.int32, (tq, LANES), 1)
    low = lane < D_V
    for pair in range(heads // 2):
        outs = []
        for j in range(2):
            h = 2 * pair + j
            sl = slice(HEAD_PAD * h, HEAD_PAD * (h + 1))
            q = q_ref[:, sl]
            s = [lax.dot_general(q, kr[:, sl], (((1,), (1,)), ((), ())), preferred_element_type=F32)
                 for kr in k_refs]
            m = jnp.max(s[0], axis=-1, keepdims=True)
            for si in s[1:]:
                m = jnp.maximum(m, jnp.max(si, axis=-1, keepdims=True))
            acc = None
            den = None
            for si, vr in zip(s, v_refs):
                p = jnp.exp2(si - m)
                d = jnp.sum(p, axis=-1, keepdims=True)
                o = _dot(p.astype(BF16), vr[:, LANES * pair:LANES * (pair + 1)])
                acc = o if acc is None else acc + o
                den = d if den is None else den + d
            outs.append(acc / den)
        o_ref[:, LANES * pair:LANES * (pair + 1)] = jnp.where(low, outs[0], outs[1]).astype(BF16)


def _attention_ctx(q, k, v):
    hw = N_HEADS * HEAD_PAD
    vw = N_HEADS * D_V
    return pl.pallas_call(
        functools.partial(_attn_kernel, heads=N_HEADS, nseg=1),
        grid=(BATCH,),
        in_specs=[
            pl.BlockSpec((SEQ, hw), lambda b: (b, 0)),
            pl.BlockSpec((SEQ, hw), lambda b: (b, 0)),
            pl.BlockSpec((SEQ, vw), lambda b: (b, 0)),
        ],
        out_specs=pl.BlockSpec((SEQ, vw), lambda b: (b, 0)),
        out_shape=jax.ShapeDtypeStruct((M_CTX, vw), BF16),
        compiler_params=_cparams(("arbitrary",)),
        name="attn_ctx",
    )(q, k, v)


def _attention_lat(q, k, v, kc, vc):
    tq = ATTN_TQ
    nq = DEC_SEQ // tq
    qoff = M_CTX // tq
    koff = M_CTX // DEC_SEQ
    return pl.pallas_call(
        functools.partial(_attn_kernel, heads=2, nseg=2),
        grid=(DEC_BATCH, N_HEADS // 2, nq),
        in_specs=[
            pl.BlockSpec((tq, 2 * HEAD_PAD), lambda b, p, i: (qoff + b * nq + i, p)),
            pl.BlockSpec((DEC_SEQ, 2 * HEAD_PAD), lambda b, p, i: (koff + b, p)),
            pl.BlockSpec((PAST_LEN, 2 * HEAD_PAD), lambda b, p, i: (b, p)),
            pl.BlockSpec((DEC_SEQ, 2 * D_V), lambda b, p, i: (koff + b, p)),
            pl.BlockSpec((PAST_LEN, 2 * D_V), lambda b, p, i: (b, p)),
        ],
        out_specs=pl.BlockSpec((tq, 2 * D_V), lambda b, p, i: (b * nq + i, p)),
        out_shape=jax.ShapeDtypeStruct((M_LAT, N_HEADS * D_V), BF16),
        compiler_params=_cparams(("arbitrary", "arbitrary", "arbitrary")),
        name="attn_lat",
    )(q, k, kc, v, vc)


def _lru_kernel(*refs, reverse, tc, nchunks):
    if reverse:
        (up_ref, uc_ref, un_ref, hf_ref, cw_ref, cb_ref, wr_ref, wi_ref, br_ref, bi_ref, lam_ref, h0_ref,
         y_ref, st_ref, ext_sc, a_sc, b_sc, p_sc, h_sc, car_sc) = refs
    else:
        (up_ref, uc_ref, un_ref, cw_ref, cb_ref, wr_ref, wi_ref, br_ref, bi_ref, lam_ref, h0_ref,
         y_ref, st_ref, ext_sc, a_sc, b_sc, p_sc, h_sc, car_sc) = refs
    c = pl.program_id(1)
    chunk = (nchunks - 1 - c) if reverse else c
    prev = jnp.where(chunk == 0, 0.0, up_ref[...])
    nxt = jnp.where(chunk == nchunks - 1, 0.0, un_ref[...])
    ext_sc[0:SUBLANES, :] = prev
    ext_sc[SUBLANES:SUBLANES + tc, :] = uc_ref[...]
    ext_sc[SUBLANES + tc:2 * SUBLANES + tc, :] = nxt
    xc = cb_ref[...]
    for k in range(4):
        xc = xc + cw_ref[k:k + 1, :] * ext_sc[SUBLANES - 2 + k:SUBLANES - 2 + k + tc, :]
    xb = xc.astype(BF16)
    r = _sigmoid(_dot(xb, wr_ref[...]) + br_ref[...])
    gi = _sigmoid(_dot(xb, wi_ref[...]) + bi_ref[...])
    lam = lam_ref[...]
    logsig = -(jnp.maximum(-lam, 0.0) + jnp.log1p(jnp.exp(-jnp.abs(lam))))
    la = LRU_C * r * logsig
    a = jnp.exp(la)
    v = -jnp.tanh(la) * (a * a + 1.0)
    bc = jnp.where(v > 0.0, v * lax.rsqrt(v), 0.0) * (gi * xc)

    @pl.when(c == 0)
    def _():
        car_sc[...] = h0_ref[...]

    nseg = SUBLANES
    sl = tc // nseg
    sp = sl + SUBLANES
    nlb = LRU_W // LANES
    for j in range(nlb):
        for s in range(nseg):
            rows = slice(sl * s, sl * (s + 1))
            dst = slice((j * nseg + s) * sp, (j * nseg + s) * sp + sl)
            a_sc[dst, :] = a[rows, LANES * j:LANES * (j + 1)]
            b_sc[dst, :] = bc[rows, LANES * j:LANES * (j + 1)]

    def body(k, carry):
        i = (sl - 1 - k) if reverse else k
        hs, ps = carry
        hn, pn = [], []
        for j in range(nlb):
            idx = pl.ds(j * nseg * sp + i, nseg, stride=sp)
            av = a_sc[idx, :]
            h = av * hs[j] + b_sc[idx, :]
            p = av * ps[j]
            p_sc[idx, :] = p
            h_sc[idx, :] = h
            hn.append(h)
            pn.append(p)
        return tuple(hn), tuple(pn)

    zero = jnp.zeros((nseg, LANES), F32)
    one = jnp.ones((nseg, LANES), F32)
    hend, pend = lax.fori_loop(0, sl, body, ((zero,) * nlb, (one,) * nlb), unroll=4)

    order = range(nseg - 1, -1, -1) if reverse else range(nseg)
    for j in range(nlb):
        lanes = slice(LANES * j, LANES * (j + 1))
        cin = car_sc[0:1, lanes]
        for s in order:
            rows = slice(sl * s, sl * (s + 1))
            src = slice((j * nseg + s) * sp, (j * nseg + s) * sp + sl)
            h = h_sc[src, :] + p_sc[src, :] * cin
            if reverse:
                y_ref[rows, lanes] = (hf_ref[rows, lanes] + h).astype(BF16)
            else:
                y_ref[rows, lanes] = h
            cin = hend[j][s:s + 1, :] + pend[j][s:s + 1, :] * cin
        car_sc[0:1, lanes] = cin
        st_ref[:, lanes] = jnp.broadcast_to(cin, (SUBLANES, LANES))


def _lru_dir(u, hf, cw, cb, wr, wi, br, bi, lam, h0, *, reverse, row_off, nseq, seqlen, tc):
    nchunks = seqlen // tc
    hb = M_TOK // SUBLANES

    def chunk_of(c):
        return (nchunks - 1 - c) if reverse else c

    def cur(b, c):
        return ((row_off + b * seqlen) // tc + chunk_of(c), 0)

    def prv(b, c):
        return (jnp.maximum((row_off + b * seqlen + chunk_of(c) * tc) // SUBLANES - 1, 0), 0)

    def nxt(b, c):
        return (jnp.minimum((row_off + b * seqlen + (chunk_of(c) + 1) * tc) // SUBLANES, hb - 1), 0)

    def out_cur(b, c):
        return ((b * seqlen) // tc + chunk_of(c), 0)

    full = lambda shape: pl.BlockSpec(shape, lambda b, c: (0,) * len(shape))
    in_specs = [pl.BlockSpec((SUBLANES, LRU_W), prv), pl.BlockSpec((tc, LRU_W), cur),
                pl.BlockSpec((SUBLANES, LRU_W), nxt)]
    args = [u, u, u]
    if reverse:
        in_specs.append(pl.BlockSpec((tc, LRU_W), out_cur))
        args.append(hf)
    in_specs += [full((SUBLANES, LRU_W)), full((1, LRU_W)), full((LRU_W, LRU_W)), full((LRU_W, LRU_W)),
                 full((1, LRU_W)), full((1, LRU_W)), full((1, LRU_W)),
                 pl.BlockSpec((None, SUBLANES, LRU_W), lambda b, c: (b, 0, 0))]
    args += [cw, cb, wr, wi, br, bi, lam, h0]
    return pl.pallas_call(
        functools.partial(_lru_kernel, reverse=reverse, tc=tc, nchunks=nchunks),
        grid=(nseq, nchunks),
        in_specs=in_specs,
        out_specs=[pl.BlockSpec((tc, LRU_W), out_cur),
                   pl.BlockSpec((None, SUBLANES, LRU_W), lambda b, c: (b, 0, 0))],
        out_shape=[jax.ShapeDtypeStruct((nseq * seqlen, LRU_W), BF16 if reverse else F32),
                   jax.ShapeDtypeStruct((nseq, SUBLANES, LRU_W), F32)],
        scratch_shapes=[pltpu.VMEM((tc + 2 * SUBLANES, LRU_W), F32)]
        + [pltpu.VMEM(((LRU_W // LANES) * (tc + SUBLANES * SUBLANES), LANES), F32)] * 4
        + [pltpu.VMEM((SUBLANES, LRU_W), F32)],
        compiler_params=_cparams(("arbitrary", "arbitrary")),
        name="lru_bwd" if reverse else "lru_fwd",
    )(*args)


def _lru_mixer(u, p, h0, *, row_off, nseq, seqlen, tc):
    kw = dict(row_off=row_off, nseq=nseq, seqlen=seqlen, tc=tc)
    hf, stf = _lru_dir(u, None, p["cw"], p["cb"], p["wr"][0], p["wi"][0], p["br"][0], p["bi"][0], p["lam"][0],
                       h0[0], reverse=False, **kw)
    y, stb = _lru_dir(u, hf, p["cw"], p["cb"], p["wr"][1], p["wi"][1], p["br"][1], p["bi"][1], p["lam"][1],
                      h0[1], reverse=True, **kw)
    return y, stf[:, 0, :], stb[:, 0, :]


def _shortconv_kernel(up_ref, uc_ref, un_ref, cw_ref, cb_ref, v_ref, x1_ref, x2_ref, ext_sc, *, tc, nchunks):
    c = pl.program_id(1)
    prev = jnp.where(c == 0, 0.0, up_ref[...])
    nxt = jnp.where(c == nchunks - 1, 0.0, un_ref[...])
    ext_sc[0:SUBLANES, :] = prev
    ext_sc[SUBLANES:SUBLANES + tc, :] = uc_ref[...]
    ext_sc[SUBLANES + tc:2 * SUBLANES + tc, :] = nxt
    for part, o_ref in enumerate((v_ref, x1_ref, x2_ref)):
        cs = slice(HY_W * part, HY_W * (part + 1))
        acc = cb_ref[:, cs]
        for k in range(3):
            acc = acc + cw_ref[k:k + 1, cs] * ext_sc[SUBLANES - 1 + k:SUBLANES - 1 + k + tc, cs]
        o_ref[...] = acc


def _shortconv(u, cw, cb, *, row_off, nseq, seqlen, tc):
    nchunks = seqlen // tc
    w = 3 * HY_W
    hb = M_TOK // SUBLANES
    cur = lambda b, c: ((row_off + b * seqlen) // tc + c, 0)
    prv = lambda b, c: (jnp.maximum((row_off + b * seqlen + c * tc) // SUBLANES - 1, 0), 0)
    nxt = lambda b, c: (jnp.minimum((row_off + b * seqlen + (c + 1) * tc) // SUBLANES, hb - 1), 0)
    out = lambda b, c: ((b * seqlen) // tc + c, 0)
    full = lambda shape: pl.BlockSpec(shape, lambda b, c: (0,) * len(shape))
    rows = nseq * seqlen
    return pl.pallas_call(
        functools.partial(_shortconv_kernel, tc=tc, nchunks=nchunks),
        grid=(nseq, nchunks),
        in_specs=[pl.BlockSpec((SUBLANES, w), prv), pl.BlockSpec((tc, w), cur), pl.BlockSpec((SUBLANES, w), nxt),
                  full((SUBLANES, w)), full((1, w))],
        out_specs=[pl.BlockSpec((tc, HY_W), out)] * 3,
        out_shape=[jax.ShapeDtypeStruct((rows, HY_W), F32)] * 3,
        scratch_shapes=[pltpu.VMEM((tc + 2 * SUBLANES, w), F32)],
        compiler_params=_cparams(("arbitrary", "arbitrary")),
        name="hy_shortconv",
    )(u, u, u, cw, cb)


def _hyfilt_kernel(z_ref, t_ref, w1_ref, b1_ref, f1_ref, w2_ref, b2_ref, f2_ref, w3_ref, ad_ref, h_ref, s_ref):
    i = pl.program_id(0)
    z = z_ref[...].astype(BF16)
    h = jnp.sin(f1_ref[...] * (_dot(z, w1_ref[...]) + b1_ref[...]))
    h = jnp.sin(f2_ref[...] * (_dot(h.astype(BF16), w2_ref[...]) + b2_ref[...]))
    h = _dot(h.astype(BF16), w3_ref[...])
    t = t_ref[...]
    ncol = h.shape[1] // LANES
    win = jnp.concatenate([jnp.exp(-t * ad_ref[:, LANES * j:LANES * (j + 1)]) for j in range(ncol)], axis=1)
    h = h * win
    h_ref[...] = h

    @pl.when(i == 0)
    def _():
        s_ref[...] = jnp.zeros_like(s_ref)

    s_ref[0:1, :] = s_ref[0:1, :] + jnp.sum(jnp.abs(h), axis=0, keepdims=True)


def _hyfilt(feats, tcol, w1, b1, f1, w2, b2, f2, w3, absdelta):
    L = feats.shape[0]
    tl = min(L, 512)
    wcols = HY_ORDER * 2 * HY_W
    full = lambda shape: pl.BlockSpec(shape, lambda i: (0,) * len(shape))
    return pl.pallas_call(
        _hyfilt_kernel,
        grid=(L // tl,),
        in_specs=[pl.BlockSpec((tl, LANES), lambda i: (i, 0)), pl.BlockSpec((tl, LANES), lambda i: (i, 0)),
                  full((LANES, HY_HID)), full((1, HY_HID)), full((1, HY_HID)),
                  full((HY_HID, HY_HID)), full((1, HY_HID)), full((1, HY_HID)),
                  full((HY_HID, wcols)), full((1, wcols))],
        out_specs=[pl.BlockSpec((tl, wcols), lambda i: (i, 0)), full((SUBLANES, wcols))],
        out_shape=[jax.ShapeDtypeStruct((L, wcols), F32), jax.ShapeDtypeStruct((SUBLANES, wcols), F32)],
        compiler_params=_cparams(("arbitrary",)),
        name="hy_filter",
    )(feats, tcol, w1, b1, f1, w2, b2, f2, w3, absdelta)


def _combine_spectrum(zr, zi, s_ref, hr_out, hi_out):
    for o in range(HY_ORDER):
        f = slice(2 * HY_W * o, 2 * HY_W * o + HY_W)
        b = slice(2 * HY_W * o + HY_W, 2 * HY_W * (o + 1))
        den = s_ref[0:1, f] + s_ref[0:1, b] + EPS
        hr_out(o, (zr[:, f] + zr[:, b]) / den)
        hi_out(o, (zi[:, f] - zi[:, b]) / den)


def _ctx_spec_kernel(f_ref, h_ref, s_ref, o_ref):
    n = f_ref.shape[0] // 2
    z = _dot(f_ref[...], h_ref[...].astype(BF16))
    zr, zi = z[:n], z[n:]

    def put_r(o, val):
        o_ref[0, :, HY_W * o:HY_W * (o + 1)] = val

    def put_i(o, val):
        o_ref[1, :, HY_W * o:HY_W * (o + 1)] = val

    _combine_spectrum(zr, zi, s_ref, put_r, put_i)


def _ctx_spectrum(fmat, hdec, s):
    n = fmat.shape[0] // 2
    return pl.pallas_call(
        _ctx_spec_kernel,
        out_shape=jax.ShapeDtypeStruct((2, n, HY_ORDER * HY_W), F32),
        compiler_params=pltpu.CompilerParams(vmem_limit_bytes=VMEM_LIMIT),
        name="hy_ctx_spectrum",
    )(fmat, hdec, s)


def _ctx_conv_kernel(z_ref, x_ref, f_ref, fi_ref, h_ref, bias_ref, o_ref, *, nb, seqlen):
    n = f_ref.shape[0] // 2
    hr = h_ref[0]
    hi = h_ref[1]
    for b in range(nb):
        rs = slice(seqlen * b, seqlen * (b + 1))
        zt = z_ref[rs, :]
        zf = _dot(f_ref[...], zt.astype(BF16))
        zr, zi = zf[:n], zf[n:]
        y = jnp.concatenate([zr * hr - zi * hi, zr * hi + zi * hr], axis=0).astype(BF16)
        conv = _dot(fi_ref[...], y)
        o_ref[rs, :] = (x_ref[rs, :] * (conv + zt * bias_ref[...])).astype(o_ref.dtype)


def _ctx_conv(z, xg, fmat, finv, hspec, bias, order, out_dtype):
    nb = 4
    n = fmat.shape[0] // 2
    rows = nb * SEQ
    return pl.pallas_call(
        functools.partial(_ctx_conv_kernel, nb=nb, seqlen=SEQ),
        grid=(BATCH // nb,),
        in_specs=[pl.BlockSpec((rows, HY_W), lambda i: (i, 0)), pl.BlockSpec((rows, HY_W), lambda i: (i, 0)),
                  pl.BlockSpec(fmat.shape, lambda i: (0, 0)), pl.BlockSpec(finv.shape, lambda i: (0, 0)),
                  pl.BlockSpec((2, n, HY_W), lambda i: (0, 0, order)),
                  pl.BlockSpec((1, HY_W), lambda i: (0, 0))],
        out_specs=pl.BlockSpec((rows, HY_W), lambda i: (i, 0)),
        out_shape=jax.ShapeDtypeStruct((M_CTX, HY_W), out_dtype),
        compiler_params=_cparams(("arbitrary",)),
        name="hy_ctx_conv",
    )(z, xg, fmat, finv, hspec, bias)


K1U = FFT_N1 // 2 + 1
SLABS = 72
PITCH = FFT_N2 + SUBLANES
NROW1 = FFT_N1 // 2
LAT_UNROLL_R = 16
LAT_UNROLL_K = 11


def _pitch_copy_in(src_ref, col, dst_sc):
    for n1 in range(NROW1):
        dst_sc[PITCH * n1:PITCH * n1 + FFT_N2, :] = src_ref[FFT_N2 * n1:FFT_N2 * (n1 + 1), col]


def _dft_stage_a(zp_scs, wa_ref, a_scs):
    def body(r, c):
        for zp_sc, a_sc in zip(zp_scs, a_scs):
            x = zp_sc[pl.ds(r, NROW1, stride=PITCH), :].astype(BF16)
            a_sc[pl.ds(r, SLABS, stride=PITCH), :] = _dot(wa_ref[...], x)
        return c

    lax.fori_loop(0, FFT_N2, body, 0, unroll=LAT_UNROLL_R)


def _load_k1(a_sc, k1):
    base = pl.multiple_of(k1 * (2 * PITCH), SUBLANES)
    a = jnp.concatenate([a_sc[pl.ds(base, FFT_N2), :], a_sc[pl.ds(base + PITCH, FFT_N2), :]], axis=0)
    return base, a.astype(BF16)


def _lat_spec_kernel(h_ref, s_ref, wa_ref, g_ref, o_ref, hf_sc, hb_sc, af_sc, ab_sc):
    _pitch_copy_in(h_ref, slice(0, LANES), hf_sc)
    _pitch_copy_in(h_ref, slice(LANES, 2 * LANES), hb_sc)
    _dft_stage_a((hf_sc, hb_sc), wa_ref, (af_sc, ab_sc))
    den = s_ref[0:1, 0:LANES] + s_ref[0:1, LANES:2 * LANES] + EPS

    def kbody(k1, c):
        _, af = _load_k1(af_sc, k1)
        _, ab = _load_k1(ab_sc, k1)
        zf = _dot(g_ref[k1], af)
        zb = _dot(g_ref[k1], ab)
        o_ref[0, k1] = (zf[:FFT_N2] + zb[:FFT_N2]) / den
        o_ref[1, k1] = (zf[FFT_N2:] - zb[FFT_N2:]) / den
        return c

    lax.fori_loop(0, K1U, kbody, 0, unroll=LAT_UNROLL_K)


def _lat_spectrum(hdec, s, wa, gtab):
    nblk = HY_ORDER * HY_W // LANES
    slab = NROW1 * PITCH
    return pl.pallas_call(
        _lat_spec_kernel,
        grid=(nblk,),
        in_specs=[pl.BlockSpec((DEC_SEQ, 2 * LANES), lambda i: (0, i)),
                  pl.BlockSpec((SUBLANES, 2 * LANES), lambda i: (0, i)),
                  pl.BlockSpec(wa.shape, lambda i: (0, 0)),
                  pl.BlockSpec(gtab.shape, lambda i: (0, 0, 0))],
        out_specs=pl.BlockSpec((2, K1U, FFT_N2, LANES), lambda i: (0, 0, 0, i)),
        out_shape=jax.ShapeDtypeStruct((2, K1U, FFT_N2, HY_ORDER * HY_W), F32),
        scratch_shapes=[pltpu.VMEM((slab, LANES), F32), pltpu.VMEM((slab, LANES), F32),
                        pltpu.VMEM((SLABS * PITCH, LANES), F32), pltpu.VMEM((SLABS * PITCH, LANES), F32)],
        compiler_params=_cparams(("arbitrary",)),
        name="hy_lat_spectrum",
    )(hdec, s, wa, gtab)


def _lat_conv_kernel(z_ref, x_ref, wa_ref, wai_ref, g_ref, gi_ref, h_ref, bias_ref, o_ref,
                     zp_sc, xp_sc, op_sc, a_sc):
    full = slice(None)
    _pitch_copy_in(z_ref, full, zp_sc)
    _pitch_copy_in(x_ref, full, xp_sc)
    _dft_stage_a((zp_sc,), wa_ref, (a_sc,))

    def kbody(k1, c):
        base, a = _load_k1(a_sc, k1)
        z = _dot(g_ref[k1], a)
        zr, zi = z[:FFT_N2], z[FFT_N2:]
        hr = h_ref[0, k1]
        hi = h_ref[1, k1]
        y = jnp.concatenate([zr * hr - zi * hi, zr * hi + zi * hr], axis=0).astype(BF16)
        bp = _dot(gi_ref[k1], y)
        a_sc[pl.ds(base, FFT_N2), :] = bp[:FFT_N2]
        a_sc[pl.ds(base + PITCH, FFT_N2), :] = bp[FFT_N2:]
        return c

    lax.fori_loop(0, K1U, kbody, 0, unroll=LAT_UNROLL_K)
    bias = bias_ref[...]

    def rbody(r, c):
        yb = a_sc[pl.ds(r, SLABS, stride=PITCH), :].astype(BF16)
        conv = _dot(wai_ref[...], yb)
        zz = zp_sc[pl.ds(r, NROW1, stride=PITCH), :]
        xx = xp_sc[pl.ds(r, NROW1, stride=PITCH), :]
        op_sc[pl.ds(r, NROW1, stride=PITCH), :] = xx * (conv + zz * bias)
        return c

    lax.fori_loop(0, FFT_N2, rbody, 0, unroll=LAT_UNROLL_R)
    for n1 in range(NROW1):
        o_ref[FFT_N2 * n1:FFT_N2 * (n1 + 1), :] = op_sc[PITCH * n1:PITCH * n1 + FFT_N2, :]


def _lat_conv(z, xg, wa, wainv, gtab, gitab, hspec, bias, order):
    ncb = HY_W // LANES
    slab = NROW1 * PITCH
    blk = pl.BlockSpec((DEC_SEQ, LANES), lambda cb, b: (b, cb))
    const = lambda a: pl.BlockSpec(a.shape, lambda cb, b: (0,) * a.ndim)
    return pl.pallas_call(
        _lat_conv_kernel,
        grid=(ncb, DEC_BATCH),
        in_specs=[blk, blk, const(wa), const(wainv), const(gtab), const(gitab),
                  pl.BlockSpec((2, K1U, FFT_N2, LANES), lambda cb, b: (0, 0, 0, order * ncb + cb)),
                  pl.BlockSpec((1, LANES), lambda cb, b: (0, cb))],
        out_specs=blk,
        out_shape=jax.ShapeDtypeStruct((M_LAT, HY_W), F32),
        scratch_shapes=[pltpu.VMEM((slab, LANES), F32), pltpu.VMEM((slab, LANES), F32),
                        pltpu.VMEM((slab, LANES), F32), pltpu.VMEM((SLABS * PITCH, LANES), F32)],
        compiler_params=_cparams(("arbitrary", "arbitrary")),
        name="hy_lat_conv",
    )(z, xg, wa, wainv, gtab, gitab, hspec, bias)


def _angle(m, n):
    return (m % n).astype(F32) * (2.0 * math.pi / n)


def _ctx_tables():
    n = 2 * SEQ
    k = jnp.arange(n, dtype=jnp.int32)[:, None]
    t = jnp.arange(SEQ, dtype=jnp.int32)[None, :]
    th = _angle(k * t, n)
    fmat = jnp.concatenate([jnp.cos(th), -jnp.sin(th)], axis=0)
    finv = jnp.concatenate([jnp.cos(th).T, -jnp.sin(th).T], axis=1) / n
    return fmat.astype(BF16), finv.astype(BF16)


def _lat_tables():
    n1, n2 = FFT_N1, FFT_N2
    n = n1 * n2
    k1 = jnp.arange(K1U, dtype=jnp.int32)
    th1 = _angle(k1[:, None] * jnp.arange(NROW1, dtype=jnp.int32)[None, :], n1)
    wa = jnp.stack([jnp.cos(th1), -jnp.sin(th1)], axis=1).reshape(2 * K1U, NROW1)
    wa = jnp.pad(wa, ((0, SLABS - 2 * K1U), (0, 0)))
    wgt = jnp.where((k1 == 0) | (k1 == n1 // 2), 1.0, 2.0)[:, None] / n
    wainv = jnp.stack([wgt * jnp.cos(th1), -wgt * jnp.sin(th1)], axis=1).reshape(2 * K1U, NROW1).T
    wainv = jnp.pad(wainv, ((0, 0), (0, SLABS - 2 * K1U)))
    k2 = jnp.arange(n2, dtype=jnp.int32)
    nn2 = jnp.arange(n2, dtype=jnp.int32)
    kfull = k1[:, None, None] + n1 * k2[None, :, None]
    th = _angle(kfull * nn2[None, None, :], n)
    gr, gi = jnp.cos(th), -jnp.sin(th)
    g = jnp.concatenate([jnp.concatenate([gr, -gi], axis=2), jnp.concatenate([gi, gr], axis=2)], axis=1)
    thT = jnp.swapaxes(th, 1, 2)
    ir, ii = jnp.cos(thT), jnp.sin(thT)
    ginv = jnp.concatenate([jnp.concatenate([ir, -ii], axis=2), jnp.concatenate([ii, ir], axis=2)], axis=1)
    return wa.astype(BF16), wainv.astype(BF16), g.astype(BF16), ginv.astype(BF16)


def _filter_features(L):
    t = jnp.linspace(0.0, 1.0, L, dtype=F32)[:, None]
    bands = (HY_EMB - 1) // 2
    w = (2.0 * math.pi / L) * jnp.arange(L, dtype=F32)[:, None]
    f = jnp.linspace(1e-4, bands - 1, bands, dtype=F32)[None, :]
    z = jnp.concatenate([t, jnp.cos(f * w), -jnp.sin(f * w)], axis=-1)
    z = jnp.pad(z, ((0, 0), (0, LANES - HY_EMB)))
    return z, jnp.broadcast_to(t, (L, LANES))


def _hyena_filter(p, L, blocked):
    feats, tcol = _filter_features(L)
    deltas = jnp.linspace(math.log(1e-2) / HY_FAST_PCT, math.log(1e-2) / HY_SLOW_PCT, HY_W, dtype=F32)
    absdelta = jnp.tile(jnp.abs(deltas), HY_ORDER * 2)[None, :]
    w3 = p["w3"]
    if blocked:
        reorder = lambda a: a.reshape(a.shape[0], HY_ORDER, 2, HY_W // LANES, LANES).transpose(
            0, 1, 3, 2, 4).reshape(a.shape[0], HY_ORDER * 2 * HY_W)
        w3, absdelta = reorder(w3), reorder(absdelta)
    return _hyfilt(feats, tcol, p["w1"], p["b1"], p["f1"], p["w2"], p["b2"], p["f2"], w3, absdelta)


def _hyena_ctx(u_hy, p, tabs):
    fmat, finv = tabs
    v, x1, x2 = _shortconv(u_hy, p["cw"], p["cb"], row_off=0, nseq=BATCH, seqlen=SEQ, tc=SEQ)
    hdec, s = _hyena_filter(p, SEQ, False)
    hspec = _ctx_spectrum(fmat, hdec, s)
    z = _ctx_conv(v, x1, fmat, finv, hspec, p["bias"][0:1], 0, F32)
    return _ctx_conv(z, x2, fmat, finv, hspec, p["bias"][1:2], 1, F32)


def _hyena_lat(u_hy, p, tabs):
    wa, wainv, gtab, gitab = tabs
    v, x1, x2 = _shortconv(u_hy, p["cw"], p["cb"], row_off=M_CTX, nseq=DEC_BATCH, seqlen=DEC_SEQ, tc=512)
    hdec, s = _hyena_filter(p, DEC_SEQ, True)
    hspec = _lat_spectrum(hdec, s, wa, gtab)
    z = _lat_conv(v, x1, wa, wainv, gtab, gitab, hspec, p["bias"][0:1], 0)
    return _lat_conv(z, x2, wa, wainv, gtab, gitab, hspec, p["bias"][1:2], 1)


def _stage3_kernel(*refs, nx, nctx):
    (mod_ref, g1_ref, wg_ref, ylc_ref, yll_ref, ymc_ref, yml_ref, yhc_ref, yhl_ref,
     wl_ref, wm_ref, wh_ref, wo_ref, g2_ref, xo_ref, xm2_ref) = refs[nx:]
    is_ctx = pl.program_id(0) < nctx
    x = _read_tokens(refs[:nx], nctx)
    xm = _rms(x, g1_ref[...]) * (1.0 + mod_ref[1:2, :]) + mod_ref[0:1, :]
    xb = xm.astype(BF16)
    merged = None
    branches = ((ylc_ref, yll_ref, wl_ref), (ymc_ref, yml_ref, wm_ref), (yhc_ref, yhl_ref, wh_ref))
    for bidx, (yc_ref, yl_ref, w_ref) in enumerate(branches):
        gate = _sigmoid(_dot(xb, wg_ref[:, D_MODEL * bidx:D_MODEL * (bidx + 1)]))
        y = jnp.where(is_ctx, yc_ref[...], yl_ref[...]).astype(BF16)
        term = gate * _dot(y, w_ref[...])
        merged = term if merged is None else merged + term
    xo = x + mod_ref[2:3, :] * _dot(merged.astype(BF16), wo_ref[...])
    xo_ref[...] = xo
    xm2 = _rms(xo, g2_ref[...]) * (1.0 + mod_ref[4:5, :]) + mod_ref[3:4, :]
    xm2_ref[...] = xm2.astype(BF16)


def _stage3(x, modl, g1, wg, ylru, ymla, yhy, wl, wm, wh, wo, g2):
    tm = TM3
    nctx = M_CTX // tm
    full = lambda shape: pl.BlockSpec(shape, lambda i: (0,) * len(shape))
    row = lambda cols: pl.BlockSpec((tm, cols), lambda i: (i, 0))
    ctx = lambda cols: pl.BlockSpec((tm, cols), lambda i: (jnp.minimum(i, nctx - 1), 0))
    lat = lambda cols: pl.BlockSpec((tm, cols), lambda i: (jnp.maximum(i - nctx, 0), 0))
    return pl.pallas_call(
        functools.partial(_stage3_kernel, nx=len(x), nctx=nctx),
        grid=(M_TOK // tm,),
        in_specs=_token_specs(x, tm) + [
                  pl.BlockSpec((None, SUBLANES, D_MODEL), lambda i: (_mod_row(i, tm), 0, 0)),
                  full((1, D_MODEL)), full((D_MODEL, 3 * D_MODEL)),
                  ctx(LRU_W), lat(LRU_W), ctx(N_HEADS * D_V), lat(N_HEADS * D_V), ctx(HY_W), lat(HY_W),
                  full((LRU_W, D_MODEL)), full((N_HEADS * D_V, D_MODEL)), full((HY_W, D_MODEL)),
                  full((D_MODEL, D_MODEL)), full((1, D_MODEL))],
        out_specs=[row(D_MODEL), row(D_MODEL)],
        out_shape=[jax.ShapeDtypeStruct((M_TOK, D_MODEL), F32), jax.ShapeDtypeStruct((M_TOK, D_MODEL), BF16)],
        compiler_params=_cparams(("arbitrary",)),
        name="stage3",
    )(*x, modl, g1, wg, *ylru, *ymla, *yhy, wl, wm, wh, wo, g2)


def _ffn_kernel(xm_ref, x_ref, mod_ref, wg_ref, wu_ref, wd_ref, o_ref, acc_sc, *, nchunks):
    j = pl.program_id(1)
    xb = xm_ref[...]
    g = _dot(xb, wg_ref[...])
    u = _dot(xb, wu_ref[...])
    hid = (g * _sigmoid(g) * u).astype(BF16)
    part = _dot(hid, wd_ref[...])

    @pl.when(j == 0)
    def _():
        acc_sc[...] = part

    @pl.when(j > 0)
    def _():
        acc_sc[...] = acc_sc[...] + part

    @pl.when(j == nchunks - 1)
    def _():
        o_ref[...] = x_ref[...] + mod_ref[5:6, :] * acc_sc[...]


def _ffn_dense(xm2, x, modl, wg, wu, wd):
    tm = TM_FFN
    nchunks = 2
    cw = D_FF // nchunks
    return pl.pallas_call(
        functools.partial(_ffn_kernel, nchunks=nchunks),
        grid=(M_TOK // tm, nchunks),
        in_specs=[pl.BlockSpec((tm, D_MODEL), lambda i, j: (i, 0)),
                  pl.BlockSpec((tm, D_MODEL), lambda i, j: (i, 0)),
                  pl.BlockSpec((None, SUBLANES, D_MODEL), lambda i, j: (_mod_row(i, tm), 0, 0)),
                  pl.BlockSpec((D_MODEL, cw), lambda i, j: (0, j)),
                  pl.BlockSpec((D_MODEL, cw), lambda i, j: (0, j)),
                  pl.BlockSpec((cw, D_MODEL), lambda i, j: (j, 0))],
        out_specs=pl.BlockSpec((tm, D_MODEL), lambda i, j: (i, 0)),
        out_shape=jax.ShapeDtypeStruct((M_TOK, D_MODEL), F32),
        scratch_shapes=[pltpu.VMEM((tm, D_MODEL), F32)],
        compiler_params=_cparams(("arbitrary", "arbitrary")),
        name="ffn_dense",
    )(xm2, x, modl, wg, wu, wd)


MOE_TILE = 256
MOE_ROWS = 2 * M_TOK + N_EXPERTS * MOE_TILE
MOE_TILES = MOE_ROWS // MOE_TILE
MOE_PAD_ROWS = MOE_ROWS - 2 * M_TOK
MOE_SEG = D_MODEL // LANES


def _to_row_tiles(val, ref, base):
    n = val.shape[0]
    for j in range(MOE_SEG):
        ref[pl.ds(base + j, n, stride=MOE_SEG), :] = val[:, LANES * j:LANES * (j + 1)]


def _from_row_tiles(ref, base, n):
    return jnp.concatenate([ref[pl.ds(base + j, n, stride=MOE_SEG), :] for j in range(MOE_SEG)], axis=1)


def _router_kernel(xm_ref, wr_ref, tri_ref, sel_ref, xp_ref, cnt_ref, base_sc):
    i = pl.program_id(0)
    xb = xm_ref[...]
    tm = xb.shape[0]
    lane = lax.broadcasted_iota(jnp.int32, (tm, LANES), 1)

    @pl.when(i == 0)
    def _():
        base_sc[...] = jnp.zeros_like(base_sc)

    logits = jnp.where(lane < N_EXPERTS, _dot(xb, wr_ref[...]), -1e30)
    mx = jnp.max(logits, axis=-1, keepdims=True)
    ex = jnp.exp(logits - mx)
    probs = ex / jnp.sum(ex, axis=-1, keepdims=True)
    p1 = jnp.max(probs, axis=-1, keepdims=True)
    i1 = jnp.min(jnp.where(probs == p1, lane, LANES), axis=-1, keepdims=True)
    rest = jnp.where(lane == i1, -1.0, probs)
    p2 = jnp.max(rest, axis=-1, keepdims=True)
    i2 = jnp.min(jnp.where(rest == p2, lane, LANES), axis=-1, keepdims=True)
    tot = p1 + p2
    oh1 = lane == i1
    oh2 = lane == i2
    oh = jnp.where(oh1 | oh2, 1.0, 0.0)
    before = base_sc[0:1, :] + _dot(tri_ref[...], oh.astype(BF16))
    r1 = jnp.sum(jnp.where(oh1, before, 0.0), axis=-1, keepdims=True)
    r2 = jnp.sum(jnp.where(oh2, before, 0.0), axis=-1, keepdims=True)
    base_sc[0:1, :] = base_sc[0:1, :] + jnp.sum(oh, axis=0, keepdims=True)
    cnt_ref[...] = base_sc[...]
    sel = jnp.where(lane == 0, p1 / tot, 0.0) + jnp.where(lane == 1, p2 / tot, 0.0)
    sel = sel + jnp.where(lane == 2, i1.astype(F32), 0.0) + jnp.where(lane == 3, i2.astype(F32), 0.0)
    sel_ref[...] = sel + jnp.where(lane == 4, r1, 0.0) + jnp.where(lane == 5, r2, 0.0)
    _to_row_tiles(xb.astype(F32), xp_ref, 0)


def _moe_router(xm2, wr):
    tm = TM_FFN
    tri = jnp.tril(jnp.ones((tm, tm), F32), -1).astype(BF16)
    return pl.pallas_call(
        _router_kernel,
        grid=(M_TOK // tm,),
        in_specs=[pl.BlockSpec((tm, D_MODEL), lambda i: (i, 0)),
                  pl.BlockSpec((D_MODEL, LANES), lambda i: (0, 0)),
                  pl.BlockSpec((tm, tm), lambda i: (0, 0))],
        out_specs=[pl.BlockSpec((tm, LANES), lambda i: (i, 0)),
                   pl.BlockSpec((tm * MOE_SEG, LANES), lambda i: (i, 0)),
                   pl.BlockSpec((SUBLANES, LANES), lambda i: (0, 0))],
        out_shape=[jax.ShapeDtypeStruct((M_TOK, LANES), F32),
                   jax.ShapeDtypeStruct((M_TOK * MOE_SEG, LANES), F32),
                   jax.ShapeDtypeStruct((SUBLANES, LANES), F32)],
        scratch_shapes=[pltpu.VMEM((SUBLANES, LANES), F32)],
        compiler_params=_cparams(("arbitrary",)),
        name="moe_router",
    )(xm2, wr, tri)


def _row_copy(src, srow8, dst, drow8, sem):
    aligned = lambda r: r if isinstance(r, int) else pl.multiple_of(r, MOE_SEG)
    return pltpu.make_async_copy(src.at[pl.ds(aligned(srow8), MOE_SEG), :],
                                 dst.at[pl.ds(aligned(drow8), MOE_SEG), :], sem)


def _dispatch_kernel(pos_ref, pad_ref, xp_ref, xs_ref, ring_sc, zero_sc, sem, *, nsteps):
    i = pl.program_id(0)
    slot = i % 2
    nrow = MOE_TILE
    slot_rows = nrow * MOE_SEG

    def wait_slot(s):
        for _ in range(2):
            pltpu.make_async_copy(ring_sc.at[s], xs_ref.at[pl.ds(0, slot_rows), :], sem.at[s]).wait()

    @pl.when(i >= 2)
    def _():
        wait_slot(slot)

    ring_sc[slot] = xp_ref[...]

    def body(t, c):
        _row_copy(ring_sc.at[slot], t * MOE_SEG, xs_ref, pos_ref[0, 2 * t], sem.at[slot]).start()
        _row_copy(ring_sc.at[slot], t * MOE_SEG, xs_ref, pos_ref[0, 2 * t + 1], sem.at[slot]).start()
        return c

    lax.fori_loop(0, nrow, body, 0, unroll=4)

    @pl.when(i == nsteps - 1)
    def _():
        zero_sc[...] = jnp.zeros_like(zero_sc)

        def zbody(t, c):
            _row_copy(zero_sc, 0, xs_ref, pad_ref[t], sem.at[2]).start()
            return c

        lax.fori_loop(0, MOE_PAD_ROWS, zbody, 0, unroll=4)
        wait_slot(1 - slot)
        wait_slot(slot)

        for _ in range(MOE_PAD_ROWS // nrow):
            pltpu.make_async_copy(ring_sc.at[0], xs_ref.at[pl.ds(0, slot_rows), :], sem.at[2]).wait()


def _moe_dispatch(xp, pos, padrows):
    nsteps = M_TOK // MOE_TILE
    return pl.pallas_call(
        functools.partial(_dispatch_kernel, nsteps=nsteps),
        grid=(nsteps,),
        in_specs=[pl.BlockSpec((None, 1, 2 * MOE_TILE), lambda i: (i, 0, 0), memory_space=pltpu.SMEM),
                  pl.BlockSpec(memory_space=pltpu.SMEM),
                  pl.BlockSpec((MOE_TILE * MOE_SEG, LANES), lambda i: (i, 0))],
        out_specs=pl.BlockSpec(memory_space=pl.ANY),
        out_shape=jax.ShapeDtypeStruct((MOE_ROWS * MOE_SEG, LANES), F32),
        scratch_shapes=[pltpu.VMEM((2, MOE_TILE * MOE_SEG, LANES), F32), pltpu.VMEM((MOE_SEG, LANES), F32),
                        pltpu.SemaphoreType.DMA((3,))],
        compiler_params=_cparams(("arbitrary",)),
        name="moe_dispatch",
    )(pos, padrows, xp)


def _experts_kernel(te_ref, nu_ref, xs_ref, wg_ref, wu_ref, wd_ref, ys_ref):
    i = pl.program_id(0)

    @pl.when(i < nu_ref[0])
    def _():
        xb = _from_row_tiles(xs_ref, 0, MOE_TILE).astype(BF16)
        g = _dot(xb, wg_ref[...])
        u = _dot(xb, wu_ref[...])
        hid = (g * _sigmoid(g) * u).astype(BF16)
        _to_row_tiles(_dot(hid, wd_ref[...]), ys_ref, 0)

    @pl.when(i >= nu_ref[0])
    def _():
        ys_ref[...] = jnp.zeros_like(ys_ref)


def _moe_experts(tile_expert, n_used, xs, wg, wu, wd):
    grid_spec = pltpu.PrefetchScalarGridSpec(
        num_scalar_prefetch=2,
        grid=(MOE_TILES,),
        in_specs=[pl.BlockSpec((MOE_TILE * MOE_SEG, LANES), lambda i, te, nu: (i, 0)),
                  pl.BlockSpec((None, D_MODEL, D_FF_E), lambda i, te, nu: (te[i], 0, 0)),
                  pl.BlockSpec((None, D_MODEL, D_FF_E), lambda i, te, nu: (te[i], 0, 0)),
                  pl.BlockSpec((None, D_FF_E, D_MODEL), lambda i, te, nu: (te[i], 0, 0))],
        out_specs=pl.BlockSpec((MOE_TILE * MOE_SEG, LANES), lambda i, te, nu: (i, 0)),
    )
    return pl.pallas_call(
        _experts_kernel,
        grid_spec=grid_spec,
        out_shape=jax.ShapeDtypeStruct((MOE_ROWS * MOE_SEG, LANES), F32),
        compiler_params=_cparams(("arbitrary",)),
        name="moe_experts",
    )(tile_expert, n_used, xs, wg, wu, wd)


def _combine_kernel(pos_ref, x_ref, mod_ref, sel_ref, ys_ref, oc_ref, ol_ref, buf_sc, sem, *, nsteps, nctx):
    i = pl.program_id(0)
    slot = i % 2
    nrow = MOE_TILE
    part = nrow * MOE_SEG

    def start(s, off):
        def body(t, c):
            for k in range(2):
                _row_copy(ys_ref, pos_ref[0, off + 2 * t + k], buf_sc, (2 * s + k) * part + t * MOE_SEG,
                          sem.at[s]).start()
            return c
        lax.fori_loop(0, nrow, body, 0, unroll=4)

    @pl.when(i == 0)
    def _():
        start(0, 0)

    @pl.when(i + 1 < nsteps)
    def _():
        start(1 - slot, 2 * nrow)

    for k in range(2):
        pltpu.make_async_copy(ys_ref.at[pl.ds(0, part), :], buf_sc.at[pl.ds(0, part), :], sem.at[slot]).wait()
    lane = lax.broadcasted_iota(jnp.int32, (nrow, LANES), 1)
    sel = sel_ref[...]
    w1 = jnp.sum(jnp.where(lane == 0, sel, 0.0), axis=-1, keepdims=True)
    w2 = jnp.sum(jnp.where(lane == 1, sel, 0.0), axis=-1, keepdims=True)
    y = (w1 * _from_row_tiles(buf_sc, 2 * slot * part, nrow)
         + w2 * _from_row_tiles(buf_sc, (2 * slot + 1) * part, nrow))
    val = x_ref[...] + mod_ref[5:6, :] * y

    @pl.when(i < nctx)
    def _():
        oc_ref[...] = val

    @pl.when(i >= nctx)
    def _():
        ol_ref[...] = val


def _moe_combine(pos2, x, modl, sel, ys):
    nsteps = M_TOK // MOE_TILE
    tm = MOE_TILE
    nctx = M_CTX // tm
    return pl.pallas_call(
        functools.partial(_combine_kernel, nsteps=nsteps, nctx=nctx),
        grid=(nsteps,),
        in_specs=[pl.BlockSpec((None, 1, 4 * MOE_TILE), lambda i: (i, 0, 0), memory_space=pltpu.SMEM),
                  pl.BlockSpec((tm, D_MODEL), lambda i: (i, 0)),
                  pl.BlockSpec((None, SUBLANES, D_MODEL), lambda i: (_mod_row(i, tm), 0, 0)),
                  pl.BlockSpec((tm, LANES), lambda i: (i, 0)),
                  pl.BlockSpec(memory_space=pl.ANY)],
        out_specs=[pl.BlockSpec((tm, D_MODEL), lambda i: (jnp.minimum(i, nctx - 1), 0)),
                   pl.BlockSpec((tm, D_MODEL), lambda i: (jnp.maximum(i - nctx, 0), 0))],
        out_shape=[jax.ShapeDtypeStruct((M_CTX, D_MODEL), F32), jax.ShapeDtypeStruct((M_LAT, D_MODEL), F32)],
        scratch_shapes=[pltpu.VMEM((2 * 2 * MOE_TILE * MOE_SEG, LANES), F32), pltpu.SemaphoreType.DMA((2,))],
        compiler_params=_cparams(("arbitrary",)),
        name="moe_combine",
    )(pos2, x, modl, sel, ys)


def _ffn_moe(xm2, x, modl, wr, wg, wu, wd):
    sel, xp, cnt = _moe_router(xm2, wr)
    counts = cnt[0, :N_EXPERTS].astype(jnp.int32)
    padded = ((counts + MOE_TILE - 1) // MOE_TILE) * MOE_TILE
    ends = jnp.cumsum(padded)
    offs = ends - padded
    experts = sel[:, 2:4].astype(jnp.int32)
    ranks = sel[:, 4:6].astype(jnp.int32)
    pos = (offs[experts] + ranks) * MOE_SEG
    tile_start = jnp.arange(MOE_TILES, dtype=jnp.int32) * MOE_TILE
    tile_expert = jnp.minimum(jnp.sum(tile_start[:, None] >= ends[None, :], axis=1), N_EXPERTS - 1)
    n_used = (ends[-1] // MOE_TILE).astype(jnp.int32)[None]
    rows = jnp.arange(MOE_ROWS, dtype=jnp.int32)
    row_expert = jnp.repeat(tile_expert, MOE_TILE)
    written = (rows < ends[-1]) & (rows - offs[row_expert] < counts[row_expert])
    padrows = jnp.nonzero(~written, size=MOE_PAD_ROWS)[0].astype(jnp.int32) * MOE_SEG
    pos_tiles = pos.reshape(M_TOK // MOE_TILE, 1, 2 * MOE_TILE)
    xs = _moe_dispatch(xp, pos_tiles, padrows)
    ys = _moe_experts(tile_expert.astype(jnp.int32), n_used, xs, wg, wu, wd)
    nxt = jnp.concatenate([pos_tiles[1:], pos_tiles[-1:]], axis=0)
    pos2 = jnp.concatenate([pos_tiles, nxt], axis=2)
    return tuple(_moe_combine(pos2, x, modl, sel, ys))


def _block_diag(w):
    nb, bs, _ = w.shape
    eye = jnp.eye(nb, dtype=w.dtype)
    return jnp.einsum("njk,nm->njmk", w, eye).reshape(nb * bs, nb * bs)


def _head_pad_cols(w, width):
    r = w.shape[0]
    return jnp.pad(w, ((0, 0), (0, 0), (0, HEAD_PAD - width))).reshape(r, N_HEADS * HEAD_PAD)


def _swap_rope_pairs(a):
    nope, rope = a[..., :D_NOPE], a[..., D_NOPE:]
    sw = rope.reshape(rope.shape[:-1] + (D_ROPE // 2, 2))[..., ::-1].reshape(rope.shape)
    return jnp.concatenate([nope, sw], axis=-1)


def _head_gain(g):
    rows = jnp.stack([g, _swap_rope_pairs(g)], axis=0)
    return jnp.pad(rows, ((0, SUBLANES - 2), (0, HEAD_PAD - D_QK)))


def _rope_tables(tm):
    rows = DEC_SEQ // GRID_W
    row = jnp.repeat(jnp.arange(rows, dtype=F32), GRID_W)
    col = jnp.tile(jnp.arange(GRID_W, dtype=F32), rows)
    half = D_ROPE // 2
    inv_freq = ROPE_BASE ** (-jnp.arange(0, half, 2, dtype=F32) / half)
    ang = jnp.concatenate([row[:, None] * inv_freq, col[:, None] * inv_freq], axis=-1)
    cos, sin = jnp.cos(ang), jnp.sin(ang)
    cos2 = jnp.repeat(cos, 2, axis=1)
    sin2 = jnp.stack([-sin, sin], axis=-1).reshape(DEC_SEQ, D_ROPE)
    cos_t = jnp.pad(cos2, ((0, 0), (D_NOPE, HEAD_PAD - D_QK)), constant_values=1.0)
    sin_t = jnp.pad(sin2, ((0, 0), (D_NOPE, HEAD_PAD - D_QK)))
    cos_t = jnp.concatenate([jnp.ones((tm, HEAD_PAD), F32), cos_t], axis=0)
    sin_t = jnp.concatenate([jnp.zeros((tm, HEAD_PAD), F32), sin_t], axis=0)
    return cos_t, sin_t


def kernel(x_prompt, x_sample, cache_ckv, cache_krope, state_lru, c, c_ctx, norm1, norm2, w_ada, b_ada, w_in, mla_q_norm, mla_kv_norm, mla_w_uq, mla_w_uk, mla_w_uv, mla_q_qknorm, mla_k_qknorm, lru_conv_w, lru_conv_b, lru_w_gate, lru_b_gate, lru_lambda, hy_conv_w, hy_conv_b, hy_w1, hy_b1, hy_freq1, hy_w2, hy_b2, hy_freq2, hy_w3, hy_bias, w_lru_out, w_mla_out, w_hy_out, w_out, ffn_w_gate, ffn_w_up, ffn_w_down, moe_w_router, moe_w_gate, moe_w_up, moe_w_down):
    x = (x_prompt.reshape(M_CTX, D_MODEL), x_sample.reshape(M_LAT, D_MODEL))

    cond = jnp.concatenate([c_ctx[None, :], c, jnp.zeros((SUBLANES - 1 - DEC_BATCH, D_MODEL), F32)], axis=0)
    mod = _adaln(cond, w_ada, b_ada).reshape(DEPTH, SUBLANES, 6, D_MODEL)
    mod = jnp.pad(mod, ((0, 0), (0, 0), (0, SUBLANES - 6), (0, 0)))

    cos_t, sin_t = _rope_tables(TM1)
    ctx_tabs = _ctx_tables()
    lat_tabs = _lat_tables()
    zero_state = jnp.zeros((BATCH, SUBLANES, LRU_W), F32)

    ckv_out, kr_out, st_out = [], [], []
    for l in range(DEPTH):
        wl = w_in[l]
        wkr = jnp.concatenate([jnp.zeros((D_MODEL, D_NOPE), F32), wl[:, 896:928]], axis=1)
        krblk = lambda w: jnp.pad(w, ((0, 0), (0, HEAD_PAD - D_QK)))
        w1 = jnp.concatenate([wl[:, :896], krblk(wkr), krblk(_swap_rope_pairs(wkr)), wl[:, 928:2464]],
                             axis=1).astype(BF16)
        wgates = wl[:, 2464:].astype(BF16)
        wuq = _head_pad_cols(mla_w_uq[l], D_QK).astype(BF16)
        wuqs = _head_pad_cols(_swap_rope_pairs(mla_w_uq[l]), D_QK).astype(BF16)
        wuk = _head_pad_cols(mla_w_uk[l], D_NOPE).astype(BF16)
        wuv = mla_w_uv[l].reshape(KV_RANK, N_HEADS * D_V).astype(BF16)
        gq = _head_gain(mla_q_qknorm[l])
        gk = _head_gain(mla_k_qknorm[l])

        ulru, uhy, ckv, krb, q, k, v = _stage1(
            x, mod[l], norm1[l][None, :], w1, mla_kv_norm[l][None, :], mla_q_norm[l][None, :],
            wuq, wuqs, gq, wuk, gk, wuv, cos_t, sin_t)
        ckv_out.append(ckv[:M_CTX].reshape(BATCH, SEQ, KV_RANK))
        kr_out.append(krb[:M_CTX, D_NOPE:D_QK].reshape(BATCH, SEQ, D_ROPE))

        kc, vc = _kvprep(cache_ckv[:, l].reshape(DEC_BATCH * PAST_LEN, KV_RANK),
                         jnp.pad(cache_krope[:, l].reshape(DEC_BATCH * PAST_LEN, D_ROPE),
                                 ((0, 0), (D_NOPE, HEAD_PAD - D_QK))),
                         wuk, gk, wuv)
        ymla = (_attention_ctx(q, k, v), _attention_lat(q, k, v, kc, vc))

        lp = dict(
            cw=jnp.pad(lru_conv_w[l], ((0, SUBLANES - 4), (0, 0))), cb=lru_conv_b[l][None, :],
            wr=[_block_diag(lru_w_gate[l, d, 0]).astype(BF16) for d in range(2)],
            wi=[_block_diag(lru_w_gate[l, d, 1]).astype(BF16) for d in range(2)],
            br=[lru_b_gate[l, d, 0][None, :] for d in range(2)],
            bi=[lru_b_gate[l, d, 1][None, :] for d in range(2)],
            lam=[lru_lambda[l, d][None, :] for d in range(2)])
        y_c, stf, stb = _lru_mixer(ulru, lp, (zero_state, zero_state), row_off=0, nseq=BATCH, seqlen=SEQ, tc=SEQ)
        st_out.append(jnp.stack([stf, stb], axis=1))
        h0 = [jnp.broadcast_to(state_lru[:, l, d][:, None, :], (DEC_BATCH, SUBLANES, LRU_W)) for d in range(2)]
        y_l, _, _ = _lru_mixer(ulru, lp, h0, row_off=M_CTX, nseq=DEC_BATCH, seqlen=DEC_SEQ, tc=512)
        ylru = (y_c, y_l)

        hp = dict(
            cw=jnp.pad(hy_conv_w[l], ((0, SUBLANES - 3), (0, 0))), cb=hy_conv_b[l][None, :],
            w1=jnp.pad(hy_w1[l], ((0, LANES - HY_EMB), (0, 0))).astype(BF16), b1=hy_b1[l][None, :],
            f1=hy_freq1[l][None, :], w2=hy_w2[l].astype(BF16), b2=hy_b2[l][None, :], f2=hy_freq2[l][None, :],
            w3=hy_w3[l].astype(BF16), bias=hy_bias[l])
        yhy = (_hyena_ctx(uhy, hp, ctx_tabs), _hyena_lat(uhy, hp, lat_tabs))

        xmid, xm2 = _stage3(x, mod[l], norm1[l][None, :], wgates, ylru, ymla, yhy,
                            w_lru_out[l].astype(BF16), w_mla_out[l].astype(BF16), w_hy_out[l].astype(BF16),
                            w_out[l].astype(BF16), norm2[l][None, :])
        j = l // 2
        if l % 2 == 0:
            x = (_ffn_dense(xm2, xmid, mod[l], ffn_w_gate[j].astype(BF16), ffn_w_up[j].astype(BF16),
                            ffn_w_down[j].astype(BF16)),)
        else:
            wr = jnp.pad(moe_w_router[j], ((0, 0), (0, LANES - N_EXPERTS))).astype(BF16)
            x = _ffn_moe(xm2, xmid, mod[l], wr, moe_w_gate[j].astype(BF16), moe_w_up[j].astype(BF16),
                         moe_w_down[j].astype(BF16))

    xc, xl = x if len(x) == 2 else (x[0][:M_CTX], x[0][M_CTX:])
    y_prompt = xc.reshape(BATCH, SEQ, D_MODEL)
    y_sample = xl.reshape(DEC_BATCH, DEC_SEQ, D_MODEL)
    return (y_prompt, y_sample, jnp.stack(ckv_out, axis=1), jnp.stack(kr_out, axis=1), jnp.stack(st_out, axis=1))
```

```python
import functools
import math

import jax
import jax.numpy as jnp
from jax import lax
from jax.experimental import pallas as pl
from jax.experimental.pallas import tpu as pltpu

F32 = jnp.float32
BF16 = jnp.bfloat16

D_MODEL = 1024
BATCH = 32
SEQ = 256
DEPTH = 2
DEC_BATCH = 2
DEC_SEQ = 4096
PAST_LEN = 512
GRID_W = 64
EPS = 1e-6
LRU_W = 512
LRU_BLOCKS = 8
LRU_C = 8.0
N_HEADS = 8
D_NOPE = 64
D_ROPE = 32
D_QK = D_NOPE + D_ROPE
D_V = 64
Q_RANK = 256
KV_RANK = 128
ROPE_BASE = 10000.0
HY_W = 512
HY_ORDER = 2
HY_EMB = 33
HY_HID = 64
HY_FAST_PCT = 0.3
HY_SLOW_PCT = 1.5
D_FF = 2816
N_EXPERTS = 8
D_FF_E = 1408

LANES = 128
SUBLANES = 8
VMEM_LIMIT = 56 * 1024 * 1024

M_CTX = BATCH * SEQ
M_LAT = DEC_BATCH * DEC_SEQ
M_TOK = M_CTX + M_LAT
TM1 = 512
TM3 = 512
TM_FFN = 512
W1_COLS = 2688
HEAD_PAD = LANES
QK_SCALE = math.log2(math.e) / math.sqrt(D_QK)
ATTN_TQ = 256
ATTN_LAT_GROUPS = (((0, 0, DEC_SEQ // 2),), ((0, DEC_SEQ // 2, DEC_SEQ), (1, 0, PAST_LEN)))

FFT_N1 = 64
FFT_N2 = 128


def _cparams(sem, vmem=VMEM_LIMIT):
    return pltpu.CompilerParams(dimension_semantics=sem, vmem_limit_bytes=vmem)


def _dot(a, b):
    return jnp.dot(a, b, preferred_element_type=F32)


def _rms(x, g):
    ms = jnp.mean(x * x, axis=-1, keepdims=True)
    return x * lax.rsqrt(ms + EPS) * g


def _sigmoid(x):
    return 1.0 / (1.0 + jnp.exp(-x))


def _ada_kernel(c_ref, w_ref, b_ref, o_ref):
    c = c_ref[...]
    s = (c * _sigmoid(c)).astype(BF16)
    o_ref[...] = _dot(s, w_ref[...].astype(BF16)) + b_ref[...]


def _adaln(cond, w_ada, b_ada):
    tn = 1024
    n6 = 6 * D_MODEL
    return pl.pallas_call(
        _ada_kernel,
        grid=(DEPTH, n6 // tn),
        in_specs=[
            pl.BlockSpec((SUBLANES, D_MODEL), lambda l, j: (0, 0)),
            pl.BlockSpec((None, D_MODEL, tn), lambda l, j: (l, 0, j)),
            pl.BlockSpec((None, 1, tn), lambda l, j: (l, 0, j)),
        ],
        out_specs=pl.BlockSpec((None, SUBLANES, tn), lambda l, j: (l, 0, j)),
        out_shape=jax.ShapeDtypeStruct((DEPTH, SUBLANES, n6), F32),
        compiler_params=_cparams(("arbitrary", "arbitrary")),
        name="adaln",
    )(cond, w_ada, b_ada.reshape(DEPTH, 1, n6))


def _mod_row(i, tm):
    nctx = M_CTX // tm
    per = DEC_SEQ // tm
    return jnp.where(i < nctx, 0, 1 + (i - nctx) // per)


def _rope_blk(i, tm):
    nctx = M_CTX // tm
    per = DEC_SEQ // tm
    return jnp.where(i < nctx, 0, 1 + (i - nctx) % per)


def _finish_head(raw, raw_sw, gc, gs, out_ref, sl):
    ms = jnp.sum(raw * raw, axis=-1, keepdims=True) * (1.0 / D_QK)
    rs = lax.rsqrt(ms + EPS)
    val = raw * gc
    if gs is not None:
        val = val + raw_sw * gs
    out_ref[:, sl] = (val * rs).astype(BF16)


def _read_tokens(x_refs, nctx):
    if len(x_refs) == 1:
        return x_refs[0][...]
    return jnp.where(pl.program_id(0) < nctx, x_refs[0][...], x_refs[1][...])


def _token_specs(x, tm):
    nctx = M_CTX // tm
    if len(x) == 1:
        return [pl.BlockSpec((tm, D_MODEL), lambda i: (i, 0))]
    return [pl.BlockSpec((tm, D_MODEL), lambda i: (jnp.minimum(i, nctx - 1), 0)),
            pl.BlockSpec((tm, D_MODEL), lambda i: (jnp.maximum(i - nctx, 0), 0))]


def _stage1_kernel(*refs, nx, nctx):
    (mod_ref, g1_ref, w1_ref, gkv_ref, gqn_ref, wuq_ref, wuqs_ref, gq_ref, wuk_ref, gk_ref,
     wuv_ref, cos_ref, sin_ref, ulru_ref, uhy_ref, ckv_ref, krb_ref, q_ref, k_ref, v_ref) = refs[nx:]
    x = _read_tokens(refs[:nx], nctx)
    xm = _rms(x, g1_ref[...]) * (1.0 + mod_ref[1:2, :]) + mod_ref[0:1, :]
    xb = xm.astype(BF16)
    ulru_ref[...] = _dot(xb, w1_ref[:, 0:512])
    qc = _dot(xb, w1_ref[:, 512:768])
    ckv = _dot(xb, w1_ref[:, 768:896])
    krb = _dot(xb, w1_ref[:, 896:1024])
    krs = _dot(xb, w1_ref[:, 1024:1152])
    uhy_ref[...] = _dot(xb, w1_ref[:, 1152:2688])
    ckvn = _rms(ckv, gkv_ref[...])
    ckv_ref[...] = ckvn
    krb_ref[...] = krb
    qn = _rms(qc, gqn_ref[...]).astype(BF16)
    cb = ckvn.astype(BF16)
    v_ref[...] = _dot(cb, wuv_ref[...]).astype(BF16)
    cos = cos_ref[...]
    sin = sin_ref[...]
    gcq = cos * (gq_ref[0:1, :] * QK_SCALE)
    gsq = sin * (gq_ref[1:2, :] * QK_SCALE)
    gck = cos * gk_ref[0:1, :]
    gsk = sin * gk_ref[1:2, :]
    for pair in range(N_HEADS // 2):
        ps = slice(2 * HEAD_PAD * pair, 2 * HEAD_PAD * (pair + 1))
        qraw, qsw, kraw = _dot(qn, wuq_ref[:, ps]), _dot(qn, wuqs_ref[:, ps]), _dot(cb, wuk_ref[:, ps])
        for j in range(2):
            hs = slice(HEAD_PAD * j, HEAD_PAD * (j + 1))
            sl = slice(HEAD_PAD * (2 * pair + j), HEAD_PAD * (2 * pair + j + 1))
            _finish_head(qraw[:, hs], qsw[:, hs], gcq, gsq, q_ref, sl)
            _finish_head(kraw[:, hs] + krb, krs, gck, gsk, k_ref, sl)


def _stage1(x, modl, g1, w1, gkv, gqn, wuq, wuqs, gq, wuk, gk, wuv, cos_t, sin_t):
    tm = TM1
    full = lambda shape: pl.BlockSpec(shape, lambda i: (0,) * len(shape))
    row = lambda cols: pl.BlockSpec((tm, cols), lambda i: (i, 0))
    hw = N_HEADS * HEAD_PAD
    return pl.pallas_call(
        functools.partial(_stage1_kernel, nx=len(x), nctx=M_CTX // tm),
        grid=(M_TOK // tm,),
        in_specs=_token_specs(x, tm) + [
            pl.BlockSpec((None, SUBLANES, D_MODEL), lambda i: (_mod_row(i, tm), 0, 0)),
            full((1, D_MODEL)),
            full((D_MODEL, W1_COLS)),
            full((1, KV_RANK)),
            full((1, Q_RANK)),
            full((Q_RANK, hw)),
            full((Q_RANK, hw)),
            full((SUBLANES, HEAD_PAD)),
            full((KV_RANK, hw)),
            full((SUBLANES, HEAD_PAD)),
            full((KV_RANK, N_HEADS * D_V)),
            pl.BlockSpec((tm, LANES), lambda i: (_rope_blk(i, tm), 0)),
            pl.BlockSpec((tm, LANES), lambda i: (_rope_blk(i, tm), 0)),
        ],
        out_specs=[row(LRU_W), row(3 * HY_W), row(KV_RANK), row(LANES), row(hw), row(hw), row(N_HEADS * D_V)],
        out_shape=[
            jax.ShapeDtypeStruct((M_TOK, LRU_W), F32),
            jax.ShapeDtypeStruct((M_TOK, 3 * HY_W), F32),
            jax.ShapeDtypeStruct((M_TOK, KV_RANK), F32),
            jax.ShapeDtypeStruct((M_TOK, LANES), F32),
            jax.ShapeDtypeStruct((M_TOK, hw), BF16),
            jax.ShapeDtypeStruct((M_TOK, hw), BF16),
            jax.ShapeDtypeStruct((M_TOK, N_HEADS * D_V), BF16),
        ],
        compiler_params=_cparams(("arbitrary",)),
        name="stage1",
    )(*x, modl, g1, w1, gkv, gqn, wuq, wuqs, gq, wuk, gk, wuv, cos_t, sin_t)


def _kvprep_kernel(ckv_ref, krb_ref, wuk_ref, gk_ref, wuv_ref, k_ref, v_ref):
    cb = ckv_ref[...].astype(BF16)
    v_ref[...] = _dot(cb, wuv_ref[...]).astype(BF16)
    krb = krb_ref[...]
    for h in range(N_HEADS):
        sl = slice(HEAD_PAD * h, HEAD_PAD * (h + 1))
        _finish_head(_dot(cb, wuk_ref[:, sl]) + krb, None, gk_ref[0:1, :], None, k_ref, sl)


def _kvprep(ckv, krb, wuk, gk, wuv):
    rows = ckv.shape[0]
    tm = TM1
    hw = N_HEADS * HEAD_PAD
    full = lambda shape: pl.BlockSpec(shape, lambda i: (0,) * len(shape))
    row = lambda cols: pl.BlockSpec((tm, cols), lambda i: (i, 0))
    return pl.pallas_call(
        _kvprep_kernel,
        grid=(rows // tm,),
        in_specs=[row(KV_RANK), row(LANES), full((KV_RANK, hw)), full((SUBLANES, HEAD_PAD)),
                  full((KV_RANK, N_HEADS * D_V))],
        out_specs=[row(hw), row(N_HEADS * D_V)],
        out_shape=[jax.ShapeDtypeStruct((rows, hw), BF16), jax.ShapeDtypeStruct((rows, N_HEADS * D_V), BF16)],
        compiler_params=_cparams(("arbitrary",)),
        name="kvprep",
    )(ckv, krb, wuk, gk, wuv)


def _attn_kernel(*refs, heads, nseg, groups):
    q_ref = refs[0]
    k_refs = refs[1:1 + nseg]
    v_refs = refs[1 + nseg:1 + 2 * nseg]
    o_ref = refs[1 + 2 * nseg]
    tq = q_ref.shape[0]
    lane = lax.broadcasted_iota(jnp.int32, (tq, LANES), 1)
    low = lane < D_V
    for pair in range(heads // 2):
        outs = []
        for j in range(2):
            h = 2 * pair + j
            sl = slice(HEAD_PAD * h, HEAD_PAD * (h + 1))
            q = q_ref[:, sl]
            parts = []
            for group in groups:
                s = [lax.dot_general(q, k_refs[seg][r0:r1, sl], (((1,), (1,)), ((), ())),
                                     preferred_element_type=F32) for seg, r0, r1 in group]
                m = jnp.max(s[0], axis=-1, keepdims=True)
                for si in s[1:]:
                    m = jnp.maximum(m, jnp.max(si, axis=-1, keepdims=True))
                acc = None
                den = None
                for si, (seg, r0, r1) in zip(s, group):
                    p = jnp.exp2(si - m)
                    d = jnp.sum(p, axis=-1, keepdims=True)
                    o = _dot(p.astype(BF16), v_refs[seg][r0:r1, LANES * pair:LANES * (pair + 1)])
                    acc = o if acc is None else acc + o
                    den = d if den is None else den + d
                parts.append((m, acc, den))
            if len(parts) == 1:
                _, acc, den = parts[0]
            else:
                mall = parts[0][0]
                for m, _, _ in parts[1:]:
                    mall = jnp.maximum(mall, m)
                acc = None
                den = None
                for m, a, d in parts:
                    w = jnp.exp2(m - mall)
                    acc = w * a if acc is None else acc + w * a
                    den = w * d if den is None else den + w * d
            outs.append(acc / den)
        o_ref[:, LANES * pair:LANES * (pair + 1)] = jnp.where(low, outs[0], outs[1]).astype(BF16)


def _attention_ctx(q, k, v):
    hw = N_HEADS * HEAD_PAD
    vw = N_HEADS * D_V
    return pl.pallas_call(
        functools.partial(_attn_kernel, heads=N_HEADS, nseg=1, groups=(((0, 0, SEQ),),)),
        grid=(BATCH,),
        in_specs=[
            pl.BlockSpec((SEQ, hw), lambda b: (b, 0)),
            pl.BlockSpec((SEQ, hw), lambda b: (b, 0)),
            pl.BlockSpec((SEQ, vw), lambda b: (b, 0)),
        ],
        out_specs=pl.BlockSpec((SEQ, vw), lambda b: (b, 0)),
        out_shape=jax.ShapeDtypeStruct((M_CTX, vw), BF16),
        compiler_params=_cparams(("arbitrary",)),
        name="attn_ctx",
    )(q, k, v)


def _attention_lat(q, k, v, kc, vc):
    tq = ATTN_TQ
    nq = DEC_SEQ // tq
    qoff = M_CTX // tq
    koff = M_CTX // DEC_SEQ
    return pl.pallas_call(
        functools.partial(_attn_kernel, heads=2, nseg=2, groups=ATTN_LAT_GROUPS),
        grid=(DEC_BATCH, N_HEADS // 2, nq),
        in_specs=[
            pl.BlockSpec((tq, 2 * HEAD_PAD), lambda b, p, i: (qoff + b * nq + i, p)),
            pl.BlockSpec((DEC_SEQ, 2 * HEAD_PAD), lambda b, p, i: (koff + b, p)),
            pl.BlockSpec((PAST_LEN, 2 * HEAD_PAD), lambda b, p, i: (b, p)),
            pl.BlockSpec((DEC_SEQ, 2 * D_V), lambda b, p, i: (koff + b, p)),
            pl.BlockSpec((PAST_LEN, 2 * D_V), lambda b, p, i: (b, p)),
        ],
        out_specs=pl.BlockSpec((tq, 2 * D_V), lambda b, p, i: (b * nq + i, p)),
        out_shape=jax.ShapeDtypeStruct((M_LAT, N_HEADS * D_V), BF16),
        compiler_params=_cparams(("arbitrary", "arbitrary", "arbitrary")),
        name="attn_lat",
    )(q, k, kc, v, vc)


def _lru_kernel(*refs, reverse, tc, nchunks):
    if reverse:
        (up_ref, uc_ref, un_ref, hf_ref, cw_ref, cb_ref, wr_ref, wi_ref, br_ref, bi_ref, lam_ref, h0_ref,
         y_ref, st_ref, ext_sc, a_sc, b_sc, p_sc, h_sc, car_sc) = refs
    else:
        (up_ref, uc_ref, un_ref, cw_ref, cb_ref, wr_ref, wi_ref, br_ref, bi_ref, lam_ref, h0_ref,
         y_ref, st_ref, ext_sc, a_sc, b_sc, p_sc, h_sc, car_sc) = refs
    c = pl.program_id(1)
    chunk = (nchunks - 1 - c) if reverse else c
    prev = jnp.where(chunk == 0, 0.0, up_ref[...])
    nxt = jnp.where(chunk == nchunks - 1, 0.0, un_ref[...])
    ext_sc[0:SUBLANES, :] = prev
    ext_sc[SUBLANES:SUBLANES + tc, :] = uc_ref[...]
    ext_sc[SUBLANES + tc:2 * SUBLANES + tc, :] = nxt
    xc = cb_ref[...]
    for k in range(4):
        xc = xc + cw_ref[k:k + 1, :] * ext_sc[SUBLANES - 2 + k:SUBLANES - 2 + k + tc, :]
    xb = xc.astype(BF16)
    r = _sigmoid(_dot(xb, wr_ref[...]) + br_ref[...])
    gi = _sigmoid(_dot(xb, wi_ref[...]) + bi_ref[...])
    lam = lam_ref[...]
    logsig = -(jnp.maximum(-lam, 0.0) + jnp.log1p(jnp.exp(-jnp.abs(lam))))
    la = LRU_C * r * logsig
    a = jnp.exp(la)
    v = -jnp.tanh(la) * (a * a + 1.0)
    bc = jnp.where(v > 0.0, v * lax.rsqrt(v), 0.0) * (gi * xc)

    @pl.when(c == 0)
    def _():
        car_sc[...] = h0_ref[...]

    nseg = SUBLANES
    sl = tc // nseg
    sp = sl + SUBLANES
    nlb = LRU_W // LANES
    for j in range(nlb):
        for s in range(nseg):
            rows = slice(sl * s, sl * (s + 1))
            dst = slice((j * nseg + s) * sp, (j * nseg + s) * sp + sl)
            a_sc[dst, :] = a[rows, LANES * j:LANES * (j + 1)]
            b_sc[dst, :] = bc[rows, LANES * j:LANES * (j + 1)]

    def body(k, carry):
        i = (sl - 1 - k) if reverse else k
        hs, ps = carry
        hn, pn = [], []
        for j in range(nlb):
            idx = pl.ds(j * nseg * sp + i, nseg, stride=sp)
            av = a_sc[idx, :]
            h = av * hs[j] + b_sc[idx, :]
            p = av * ps[j]
            p_sc[idx, :] = p
            h_sc[idx, :] = h
            hn.append(h)
            pn.append(p)
        return tuple(hn), tuple(pn)

    zero = jnp.zeros((nseg, LANES), F32)
    one = jnp.ones((nseg, LANES), F32)
    hend, pend = lax.fori_loop(0, sl, body, ((zero,) * nlb, (one,) * nlb), unroll=4)

    order = range(nseg - 1, -1, -1) if reverse else range(nseg)
    for j in range(nlb):
        lanes = slice(LANES * j, LANES * (j + 1))
        cin = car_sc[0:1, lanes]
        for s in order:
            rows = slice(sl * s, sl * (s + 1))
            src = slice((j * nseg + s) * sp, (j * nseg + s) * sp + sl)
            h = h_sc[src, :] + p_sc[src, :] * cin
            if reverse:
                y_ref[rows, lanes] = (hf_ref[rows, lanes] + h).astype(BF16)
            else:
                y_ref[rows, lanes] = h
            cin = hend[j][s:s + 1, :] + pend[j][s:s + 1, :] * cin
        car_sc[0:1, lanes] = cin
        st_ref[:, lanes] = jnp.broadcast_to(cin, (SUBLANES, LANES))


def _lru_dir(u, hf, cw, cb, wr, wi, br, bi, lam, h0, *, reverse, row_off, nseq, seqlen, tc):
    nchunks = seqlen // tc
    hb = M_TOK // SUBLANES

    def chunk_of(c):
        return (nchunks - 1 - c) if reverse else c

    def cur(b, c):
        return ((row_off + b * seqlen) // tc + chunk_of(c), 0)

    def prv(b, c):
        return (jnp.maximum((row_off + b * seqlen + chunk_of(c) * tc) // SUBLANES - 1, 0), 0)

    def nxt(b, c):
        return (jnp.minimum((row_off + b * seqlen + (chunk_of(c) + 1) * tc) // SUBLANES, hb - 1), 0)

    def out_cur(b, c):
        return ((b * seqlen) // tc + chunk_of(c), 0)

    full = lambda shape: pl.BlockSpec(shape, lambda b, c: (0,) * len(shape))
    in_specs = [pl.BlockSpec((SUBLANES, LRU_W), prv), pl.BlockSpec((tc, LRU_W), cur),
                pl.BlockSpec((SUBLANES, LRU_W), nxt)]
    args = [u, u, u]
    if reverse:
        in_specs.append(pl.BlockSpec((tc, LRU_W), out_cur))
        args.append(hf)
    in_specs += [full((SUBLANES, LRU_W)), full((1, LRU_W)), full((LRU_W, LRU_W)), full((LRU_W, LRU_W)),
                 full((1, LRU_W)), full((1, LRU_W)), full((1, LRU_W)),
                 pl.BlockSpec((None, SUBLANES, LRU_W), lambda b, c: (b, 0, 0))]
    args += [cw, cb, wr, wi, br, bi, lam, h0]
    return pl.pallas_call(
        functools.partial(_lru_kernel, reverse=reverse, tc=tc, nchunks=nchunks),
        grid=(nseq, nchunks),
        in_specs=in_specs,
        out_specs=[pl.BlockSpec((tc, LRU_W), out_cur),
                   pl.BlockSpec((None, SUBLANES, LRU_W), lambda b, c: (b, 0, 0))],
        out_shape=[jax.ShapeDtypeStruct((nseq * seqlen, LRU_W), BF16 if reverse else F32),
                   jax.ShapeDtypeStruct((nseq, SUBLANES, LRU_W), F32)],
        scratch_shapes=[pltpu.VMEM((tc + 2 * SUBLANES, LRU_W), F32)]
        + [pltpu.VMEM(((LRU_W // LANES) * (tc + SUBLANES * SUBLANES), LANES), F32)] * 4
        + [pltpu.VMEM((SUBLANES, LRU_W), F32)],
        compiler_params=_cparams(("arbitrary", "arbitrary")),
        name="lru_bwd" if reverse else "lru_fwd",
    )(*args)


def _lru_mixer(u, p, h0, *, row_off, nseq, seqlen, tc):
    kw = dict(row_off=row_off, nseq=nseq, seqlen=seqlen, tc=tc)
    hf, stf = _lru_dir(u, None, p["cw"], p["cb"], p["wr"][0], p["wi"][0], p["br"][0], p["bi"][0], p["lam"][0],
                       h0[0], reverse=False, **kw)
    y, stb = _lru_dir(u, hf, p["cw"], p["cb"], p["wr"][1], p["wi"][1], p["br"][1], p["bi"][1], p["lam"][1],
                      h0[1], reverse=True, **kw)
    return y, stf[:, 0, :], stb[:, 0, :]


def _shortconv_kernel(up_ref, uc_ref, un_ref, cw_ref, cb_ref, v_ref, x1_ref, x2_ref, ext_sc, *, tc, nchunks):
    c = pl.program_id(1)
    prev = jnp.where(c == 0, 0.0, up_ref[...])
    nxt = jnp.where(c == nchunks - 1, 0.0, un_ref[...])
    ext_sc[0:SUBLANES, :] = prev
    ext_sc[SUBLANES:SUBLANES + tc, :] = uc_ref[...]
    ext_sc[SUBLANES + tc:2 * SUBLANES + tc, :] = nxt
    for part, o_ref in enumerate((v_ref, x1_ref, x2_ref)):
        cs = slice(HY_W * part, HY_W * (part + 1))
        acc = cb_ref[:, cs]
        for k in range(3):
            acc = acc + cw_ref[k:k + 1, cs] * ext_sc[SUBLANES - 1 + k:SUBLANES - 1 + k + tc, cs]
        o_ref[...] = acc


def _shortconv(u, cw, cb, *, row_off, nseq, seqlen, tc):
    nchunks = seqlen // tc
    w = 3 * HY_W
    hb = M_TOK // SUBLANES
    cur = lambda b, c: ((row_off + b * seqlen) // tc + c, 0)
    prv = lambda b, c: (jnp.maximum((row_off + b * seqlen + c * tc) // SUBLANES - 1, 0), 0)
    nxt = lambda b, c: (jnp.minimum((row_off + b * seqlen + (c + 1) * tc) // SUBLANES, hb - 1), 0)
    out = lambda b, c: ((b * seqlen) // tc + c, 0)
    full = lambda shape: pl.BlockSpec(shape, lambda b, c: (0,) * len(shape))
    rows = nseq * seqlen
    return pl.pallas_call(
        functools.partial(_shortconv_kernel, tc=tc, nchunks=nchunks),
        grid=(nseq, nchunks),
        in_specs=[pl.BlockSpec((SUBLANES, w), prv), pl.BlockSpec((tc, w), cur), pl.BlockSpec((SUBLANES, w), nxt),
                  full((SUBLANES, w)), full((1, w))],
        out_specs=[pl.BlockSpec((tc, HY_W), out)] * 3,
        out_shape=[jax.ShapeDtypeStruct((rows, HY_W), F32)] * 3,
        scratch_shapes=[pltpu.VMEM((tc + 2 * SUBLANES, w), F32)],
        compiler_params=_cparams(("arbitrary", "arbitrary")),
        name="hy_shortconv",
    )(u, u, u, cw, cb)


def _hyfilt_kernel(z_ref, t_ref, w1_ref, b1_ref, f1_ref, w2_ref, b2_ref, f2_ref, w3_ref, ad_ref, h_ref, s_ref):
    i = pl.program_id(0)
    z = z_ref[...].astype(BF16)
    h = jnp.sin(f1_ref[...] * (_dot(z, w1_ref[...]) + b1_ref[...]))
    h = jnp.sin(f2_ref[...] * (_dot(h.astype(BF16), w2_ref[...]) + b2_ref[...]))
    h = _dot(h.astype(BF16), w3_ref[...])
    t = t_ref[...]
    ncol = h.shape[1] // LANES
    win = jnp.concatenate([jnp.exp(-t * ad_ref[:, LANES * j:LANES * (j + 1)]) for j in range(ncol)], axis=1)
    h = h * win
    h_ref[...] = h

    @pl.when(i == 0)
    def _():
        s_ref[...] = jnp.zeros_like(s_ref)

    s_ref[0:1, :] = s_ref[0:1, :] + jnp.sum(jnp.abs(h), axis=0, keepdims=True)


def _hyfilt(feats, tcol, w1, b1, f1, w2, b2, f2, w3, absdelta):
    L = feats.shape[0]
    tl = min(L, 512)
    wcols = HY_ORDER * 2 * HY_W
    full = lambda shape: pl.BlockSpec(shape, lambda i: (0,) * len(shape))
    return pl.pallas_call(
        _hyfilt_kernel,
        grid=(L // tl,),
        in_specs=[pl.BlockSpec((tl, LANES), lambda i: (i, 0)), pl.BlockSpec((tl, LANES), lambda i: (i, 0)),
                  full((LANES, HY_HID)), full((1, HY_HID)), full((1, HY_HID)),
                  full((HY_HID, HY_HID)), full((1, HY_HID)), full((1, HY_HID)),
                  full((HY_HID, wcols)), full((1, wcols))],
        out_specs=[pl.BlockSpec((tl, wcols), lambda i: (i, 0)), full((SUBLANES, wcols))],
        out_shape=[jax.ShapeDtypeStruct((L, wcols), F32), jax.ShapeDtypeStruct((SUBLANES, wcols), F32)],
        compiler_params=_cparams(("arbitrary",)),
        name="hy_filter",
    )(feats, tcol, w1, b1, f1, w2, b2, f2, w3, absdelta)


def _combine_spectrum(zr, zi, s_ref, hr_out, hi_out):
    for o in range(HY_ORDER):
        f = slice(2 * HY_W * o, 2 * HY_W * o + HY_W)
        b = slice(2 * HY_W * o + HY_W, 2 * HY_W * (o + 1))
        den = s_ref[0:1, f] + s_ref[0:1, b] + EPS
        hr_out(o, (zr[:, f] + zr[:, b]) / den)
        hi_out(o, (zi[:, f] - zi[:, b]) / den)


def _ctx_spec_kernel(f_ref, h_ref, s_ref, o_ref):
    n = f_ref.shape[0] // 2
    z = _dot(f_ref[...], h_ref[...].astype(BF16))
    zr, zi = z[:n], z[n:]

    def put_r(o, val):
        o_ref[0, :, HY_W * o:HY_W * (o + 1)] = val

    def put_i(o, val):
        o_ref[1, :, HY_W * o:HY_W * (o + 1)] = val

    _combine_spectrum(zr, zi, s_ref, put_r, put_i)


def _ctx_spectrum(fmat, hdec, s):
    n = fmat.shape[0] // 2
    return pl.pallas_call(
        _ctx_spec_kernel,
        out_shape=jax.ShapeDtypeStruct((2, n, HY_ORDER * HY_W), F32),
        compiler_params=pltpu.CompilerParams(vmem_limit_bytes=VMEM_LIMIT),
        name="hy_ctx_spectrum",
    )(fmat, hdec, s)


def _ctx_conv_kernel(z_ref, x_ref, f_ref, fi_ref, h_ref, bias_ref, o_ref, *, nb, seqlen):
    n = f_ref.shape[0] // 2
    hr = h_ref[0]
    hi = h_ref[1]
    for b in range(nb):
        rs = slice(seqlen * b, seqlen * (b + 1))
        zt = z_ref[rs, :]
        zf = _dot(f_ref[...], zt.astype(BF16))
        zr, zi = zf[:n], zf[n:]
        y = jnp.concatenate([zr * hr - zi * hi, zr * hi + zi * hr], axis=0).astype(BF16)
        conv = _dot(fi_ref[...], y)
        o_ref[rs, :] = (x_ref[rs, :] * (conv + zt * bias_ref[...])).astype(o_ref.dtype)


def _ctx_conv(z, xg, fmat, finv, hspec, bias, order, out_dtype):
    nb = 4
    n = fmat.shape[0] // 2
    rows = nb * SEQ
    return pl.pallas_call(
        functools.partial(_ctx_conv_kernel, nb=nb, seqlen=SEQ),
        grid=(BATCH // nb,),
        in_specs=[pl.BlockSpec((rows, HY_W), lambda i: (i, 0)), pl.BlockSpec((rows, HY_W), lambda i: (i, 0)),
                  pl.BlockSpec(fmat.shape, lambda i: (0, 0)), pl.BlockSpec(finv.shape, lambda i: (0, 0)),
                  pl.BlockSpec((2, n, HY_W), lambda i: (0, 0, order)),
                  pl.BlockSpec((1, HY_W), lambda i: (0, 0))],
        out_specs=pl.BlockSpec((rows, HY_W), lambda i: (i, 0)),
        out_shape=jax.ShapeDtypeStruct((M_CTX, HY_W), out_dtype),
        compiler_params=_cparams(("arbitrary",)),
        name="hy_ctx_conv",
    )(z, xg, fmat, finv, hspec, bias)


K1U = FFT_N1 // 2 + 1
SLABS = 72
PITCH = FFT_N2 + SUBLANES
NROW1 = FFT_N1 // 2
LAT_UNROLL_R = 16
LAT_UNROLL_K = 11


def _pitch_copy_in(src_ref, col, dst_sc):
    for n1 in range(NROW1):
        dst_sc[PITCH * n1:PITCH * n1 + FFT_N2, :] = src_ref[FFT_N2 * n1:FFT_N2 * (n1 + 1), col]


def _dft_stage_a(zp_scs, wa_ref, a_scs):
    def body(r, c):
        for zp_sc, a_sc in zip(zp_scs, a_scs):
            x = zp_sc[pl.ds(r, NROW1, stride=PITCH), :].astype(BF16)
            a_sc[pl.ds(r, SLABS, stride=PITCH), :] = _dot(wa_ref[...], x)
        return c

    lax.fori_loop(0, FFT_N2, body, 0, unroll=LAT_UNROLL_R)


def _load_k1(a_sc, k1):
    base = pl.multiple_of(k1 * (2 * PITCH), SUBLANES)
    a = jnp.concatenate([a_sc[pl.ds(base, FFT_N2), :], a_sc[pl.ds(base + PITCH, FFT_N2), :]], axis=0)
    return base, a.astype(BF16)


def _lat_spec_kernel(h_ref, s_ref, wa_ref, g_ref, o_ref, hf_sc, hb_sc, af_sc, ab_sc):
    _pitch_copy_in(h_ref, slice(0, LANES), hf_sc)
    _pitch_copy_in(h_ref, slice(LANES, 2 * LANES), hb_sc)
    _dft_stage_a((hf_sc, hb_sc), wa_ref, (af_sc, ab_sc))
    den = s_ref[0:1, 0:LANES] + s_ref[0:1, LANES:2 * LANES] + EPS

    def kbody(k1, c):
        _, af = _load_k1(af_sc, k1)
        _, ab = _load_k1(ab_sc, k1)
        zf = _dot(g_ref[k1], af)
        zb = _dot(g_ref[k1], ab)
        o_ref[0, k1] = (zf[:FFT_N2] + zb[:FFT_N2]) / den
        o_ref[1, k1] = (zf[FFT_N2:] - zb[FFT_N2:]) / den
        return c

    lax.fori_loop(0, K1U, kbody, 0, unroll=LAT_UNROLL_K)


def _lat_spectrum(hdec, s, wa, gtab):
    nblk = HY_ORDER * HY_W // LANES
    slab = NROW1 * PITCH
    return pl.pallas_call(
        _lat_spec_kernel,
        grid=(nblk,),
        in_specs=[pl.BlockSpec((DEC_SEQ, 2 * LANES), lambda i: (0, i)),
                  pl.BlockSpec((SUBLANES, 2 * LANES), lambda i: (0, i)),
                  pl.BlockSpec(wa.shape, lambda i: (0, 0)),
                  pl.BlockSpec(gtab.shape, lambda i: (0, 0, 0))],
        out_specs=pl.BlockSpec((2, K1U, FFT_N2, LANES), lambda i: (0, 0, 0, i)),
        out_shape=jax.ShapeDtypeStruct((2, K1U, FFT_N2, HY_ORDER * HY_W), F32),
        scratch_shapes=[pltpu.VMEM((slab, LANES), F32), pltpu.VMEM((slab, LANES), F32),
                        pltpu.VMEM((SLABS * PITCH, LANES), F32), pltpu.VMEM((SLABS * PITCH, LANES), F32)],
        compiler_params=_cparams(("arbitrary",)),
        name="hy_lat_spectrum",
    )(hdec, s, wa, gtab)


def _lat_conv_kernel(z_ref, x_ref, wa_ref, wai_ref, g_ref, gi_ref, h_ref, bias_ref, o_ref,
                     zp_sc, xp_sc, op_sc, a_sc):
    full = slice(None)
    _pitch_copy_in(z_ref, full, zp_sc)
    _pitch_copy_in(x_ref, full, xp_sc)
    _dft_stage_a((zp_sc,), wa_ref, (a_sc,))

    def kbody(k1, c):
        base, a = _load_k1(a_sc, k1)
        z = _dot(g_ref[k1], a)
        zr, zi = z[:FFT_N2], z[FFT_N2:]
        hr = h_ref[0, k1]
        hi = h_ref[1, k1]
        y = jnp.concatenate([zr * hr - zi * hi, zr * hi + zi * hr], axis=0).astype(BF16)
        bp = _dot(gi_ref[k1], y)
        a_sc[pl.ds(base, FFT_N2), :] = bp[:FFT_N2]
        a_sc[pl.ds(base + PITCH, FFT_N2), :] = bp[FFT_N2:]
        return c

    lax.fori_loop(0, K1U, kbody, 0, unroll=LAT_UNROLL_K)
    bias = bias_ref[...]

    def rbody(r, c):
        yb = a_sc[pl.ds(r, SLABS, stride=PITCH), :].astype(BF16)
        conv = _dot(wai_ref[...], yb)
        zz = zp_sc[pl.ds(r, NROW1, stride=PITCH), :]
        xx = xp_sc[pl.ds(r, NROW1, stride=PITCH), :]
        op_sc[pl.ds(r, NROW1, stride=PITCH), :] = xx * (conv + zz * bias)
        return c

    lax.fori_loop(0, FFT_N2, rbody, 0, unroll=LAT_UNROLL_R)
    for n1 in range(NROW1):
        o_ref[FFT_N2 * n1:FFT_N2 * (n1 + 1), :] = op_sc[PITCH * n1:PITCH * n1 + FFT_N2, :]


def _lat_conv(z, xg, wa, wainv, gtab, gitab, hspec, bias, order):
    ncb = HY_W // LANES
    slab = NROW1 * PITCH
    blk = pl.BlockSpec((DEC_SEQ, LANES), lambda cb, b: (b, cb))
    const = lambda a: pl.BlockSpec(a.shape, lambda cb, b: (0,) * a.ndim)
    return pl.pallas_call(
        _lat_conv_kernel,
        grid=(ncb, DEC_BATCH),
        in_specs=[blk, blk, const(wa), const(wainv), const(gtab), const(gitab),
                  pl.BlockSpec((2, K1U, FFT_N2, LANES), lambda cb, b: (0, 0, 0, order * ncb + cb)),
                  pl.BlockSpec((1, LANES), lambda cb, b: (0, cb))],
        out_specs=blk,
        out_shape=jax.ShapeDtypeStruct((M_LAT, HY_W), F32),
        scratch_shapes=[pltpu.VMEM((slab, LANES), F32), pltpu.VMEM((slab, LANES), F32),
                        pltpu.VMEM((slab, LANES), F32), pltpu.VMEM((SLABS * PITCH, LANES), F32)],
        compiler_params=_cparams(("arbitrary", "arbitrary")),
        name="hy_lat_conv",
    )(z, xg, wa, wainv, gtab, gitab, hspec, bias)


def _angle(m, n):
    return (m % n).astype(F32) * (2.0 * math.pi / n)


def _ctx_tables():
    n = 2 * SEQ
    nf = SEQ + SUBLANES
    k = jnp.arange(nf, dtype=jnp.int32)[:, None]
    t = jnp.arange(SEQ, dtype=jnp.int32)[None, :]
    th = _angle(k * t, n)
    live = (k <= SEQ).astype(F32)
    wgt = jnp.where((k == 0) | (k == SEQ), 1.0, 2.0) * live / n
    fmat = jnp.concatenate([live * jnp.cos(th), -live * jnp.sin(th)], axis=0)
    finv = jnp.concatenate([(wgt * jnp.cos(th)).T, (-wgt * jnp.sin(th)).T], axis=1)
    return fmat.astype(BF16), finv.astype(BF16)


def _lat_tables():
    n1, n2 = FFT_N1, FFT_N2
    n = n1 * n2
    k1 = jnp.arange(K1U, dtype=jnp.int32)
    th1 = _angle(k1[:, None] * jnp.arange(NROW1, dtype=jnp.int32)[None, :], n1)
    wa = jnp.stack([jnp.cos(th1), -jnp.sin(th1)], axis=1).reshape(2 * K1U, NROW1)
    wa = jnp.pad(wa, ((0, SLABS - 2 * K1U), (0, 0)))
    wgt = jnp.where((k1 == 0) | (k1 == n1 // 2), 1.0, 2.0)[:, None] / n
    wainv = jnp.stack([wgt * jnp.cos(th1), -wgt * jnp.sin(th1)], axis=1).reshape(2 * K1U, NROW1).T
    wainv = jnp.pad(wainv, ((0, 0), (0, SLABS - 2 * K1U)))
    k2 = jnp.arange(n2, dtype=jnp.int32)
    nn2 = jnp.arange(n2, dtype=jnp.int32)
    kfull = k1[:, None, None] + n1 * k2[None, :, None]
    th = _angle(kfull * nn2[None, None, :], n)
    gr, gi = jnp.cos(th), -jnp.sin(th)
    g = jnp.concatenate([jnp.concatenate([gr, -gi], axis=2), jnp.concatenate([gi, gr], axis=2)], axis=1)
    thT = jnp.swapaxes(th, 1, 2)
    ir, ii = jnp.cos(thT), jnp.sin(thT)
    ginv = jnp.concatenate([jnp.concatenate([ir, -ii], axis=2), jnp.concatenate([ii, ir], axis=2)], axis=1)
    return wa.astype(BF16), wainv.astype(BF16), g.astype(BF16), ginv.astype(BF16)


def _filter_features(L):
    t = jnp.linspace(0.0, 1.0, L, dtype=F32)[:, None]
    bands = (HY_EMB - 1) // 2
    w = (2.0 * math.pi / L) * jnp.arange(L, dtype=F32)[:, None]
    f = jnp.linspace(1e-4, bands - 1, bands, dtype=F32)[None, :]
    z = jnp.concatenate([t, jnp.cos(f * w), -jnp.sin(f * w)], axis=-1)
    z = jnp.pad(z, ((0, 0), (0, LANES - HY_EMB)))
    return z, jnp.broadcast_to(t, (L, LANES))


def _hyena_filter(p, L, blocked):
    feats, tcol = _filter_features(L)
    deltas = jnp.linspace(math.log(1e-2) / HY_FAST_PCT, math.log(1e-2) / HY_SLOW_PCT, HY_W, dtype=F32)
    absdelta = jnp.tile(jnp.abs(deltas), HY_ORDER * 2)[None, :]
    w3 = p["w3"]
    if blocked:
        reorder = lambda a: a.reshape(a.shape[0], HY_ORDER, 2, HY_W // LANES, LANES).transpose(
            0, 1, 3, 2, 4).reshape(a.shape[0], HY_ORDER * 2 * HY_W)
        w3, absdelta = reorder(w3), reorder(absdelta)
    return _hyfilt(feats, tcol, p["w1"], p["b1"], p["f1"], p["w2"], p["b2"], p["f2"], w3, absdelta)


def _hyena_ctx(u_hy, p, tabs):
    fmat, finv = tabs
    v, x1, x2 = _shortconv(u_hy, p["cw"], p["cb"], row_off=0, nseq=BATCH, seqlen=SEQ, tc=SEQ)
    hdec, s = _hyena_filter(p, SEQ, False)
    hspec = _ctx_spectrum(fmat, hdec, s)
    z = _ctx_conv(v, x1, fmat, finv, hspec, p["bias"][0:1], 0, F32)
    return _ctx_conv(z, x2, fmat, finv, hspec, p["bias"][1:2], 1, F32)


def _hyena_lat(u_hy, p, tabs):
    wa, wainv, gtab, gitab = tabs
    v, x1, x2 = _shortconv(u_hy, p["cw"], p["cb"], row_off=M_CTX, nseq=DEC_BATCH, seqlen=DEC_SEQ, tc=512)
    hdec, s = _hyena_filter(p, DEC_SEQ, True)
    hspec = _lat_spectrum(hdec, s, wa, gtab)
    z = _lat_conv(v, x1, wa, wainv, gtab, gitab, hspec, p["bias"][0:1], 0)
    return _lat_conv(z, x2, wa, wainv, gtab, gitab, hspec, p["bias"][1:2], 1)


def _stage3_kernel(*refs, nx, nctx):
    (mod_ref, g1_ref, wg_ref, ylc_ref, yll_ref, ymc_ref, yml_ref, yhc_ref, yhl_ref,
     wl_ref, wm_ref, wh_ref, wo_ref, g2_ref, xo_ref, xm2_ref) = refs[nx:]
    is_ctx = pl.program_id(0) < nctx
    x = _read_tokens(refs[:nx], nctx)
    xm = _rms(x, g1_ref[...]) * (1.0 + mod_ref[1:2, :]) + mod_ref[0:1, :]
    xb = xm.astype(BF16)
    merged = None
    branches = ((ylc_ref, yll_ref, wl_ref), (ymc_ref, yml_ref, wm_ref), (yhc_ref, yhl_ref, wh_ref))
    for bidx, (yc_ref, yl_ref, w_ref) in enumerate(branches):
        gate = _sigmoid(_dot(xb, wg_ref[:, D_MODEL * bidx:D_MODEL * (bidx + 1)]))
        y = jnp.where(is_ctx, yc_ref[...], yl_ref[...]).astype(BF16)
        term = gate * _dot(y, w_ref[...])
        merged = term if merged is None else merged + term
    xo = x + mod_ref[2:3, :] * _dot(merged.astype(BF16), wo_ref[...])
    xo_ref[...] = xo
    xm2 = _rms(xo, g2_ref[...]) * (1.0 + mod_ref[4:5, :]) + mod_ref[3:4, :]
    xm2_ref[...] = xm2.astype(BF16)


def _stage3(x, modl, g1, wg, ylru, ymla, yhy, wl, wm, wh, wo, g2):
    tm = TM3
    nctx = M_CTX // tm
    full = lambda shape: pl.BlockSpec(shape, lambda i: (0,) * len(shape))
    row = lambda cols: pl.BlockSpec((tm, cols), lambda i: (i, 0))
    ctx = lambda cols: pl.BlockSpec((tm, cols), lambda i: (jnp.minimum(i, nctx - 1), 0))
    lat = lambda cols: pl.BlockSpec((tm, cols), lambda i: (jnp.maximum(i - nctx, 0), 0))
    return pl.pallas_call(
        functools.partial(_stage3_kernel, nx=len(x), nctx=nctx),
        grid=(M_TOK // tm,),
        in_specs=_token_specs(x, tm) + [
                  pl.BlockSpec((None, SUBLANES, D_MODEL), lambda i: (_mod_row(i, tm), 0, 0)),
                  full((1, D_MODEL)), full((D_MODEL, 3 * D_MODEL)),
                  ctx(LRU_W), lat(LRU_W), ctx(N_HEADS * D_V), lat(N_HEADS * D_V), ctx(HY_W), lat(HY_W),
                  full((LRU_W, D_MODEL)), full((N_HEADS * D_V, D_MODEL)), full((HY_W, D_MODEL)),
                  full((D_MODEL, D_MODEL)), full((1, D_MODEL))],
        out_specs=[row(D_MODEL), row(D_MODEL)],
        out_shape=[jax.ShapeDtypeStruct((M_TOK, D_MODEL), F32), jax.ShapeDtypeStruct((M_TOK, D_MODEL), BF16)],
        compiler_params=_cparams(("arbitrary",)),
        name="stage3",
    )(*x, modl, g1, wg, *ylru, *ymla, *yhy, wl, wm, wh, wo, g2)


def _ffn_kernel(xm_ref, x_ref, mod_ref, wg_ref, wu_ref, wd_ref, o_ref, acc_sc, *, nchunks):
    j = pl.program_id(1)
    xb = xm_ref[...]
    g = _dot(xb, wg_ref[...])
    u = _dot(xb, wu_ref[...])
    hid = (g * _sigmoid(g) * u).astype(BF16)
    part = _dot(hid, wd_ref[...])

    @pl.when(j == 0)
    def _():
        acc_sc[...] = part

    @pl.when(j > 0)
    def _():
        acc_sc[...] = acc_sc[...] + part

    @pl.when(j == nchunks - 1)
    def _():
        o_ref[...] = x_ref[...] + mod_ref[5:6, :] * acc_sc[...]


def _ffn_dense(xm2, x, modl, wg, wu, wd):
    tm = TM_FFN
    nchunks = 2
    cw = D_FF // nchunks
    return pl.pallas_call(
        functools.partial(_ffn_kernel, nchunks=nchunks),
        grid=(M_TOK // tm, nchunks),
        in_specs=[pl.BlockSpec((tm, D_MODEL), lambda i, j: (i, 0)),
                  pl.BlockSpec((tm, D_MODEL), lambda i, j: (i, 0)),
                  pl.BlockSpec((None, SUBLANES, D_MODEL), lambda i, j: (_mod_row(i, tm), 0, 0)),
                  pl.BlockSpec((D_MODEL, cw), lambda i, j: (0, j)),
                  pl.BlockSpec((D_MODEL, cw), lambda i, j: (0, j)),
                  pl.BlockSpec((cw, D_MODEL), lambda i, j: (j, 0))],
        out_specs=pl.BlockSpec((tm, D_MODEL), lambda i, j: (i, 0)),
        out_shape=jax.ShapeDtypeStruct((M_TOK, D_MODEL), F32),
        scratch_shapes=[pltpu.VMEM((tm, D_MODEL), F32)],
        compiler_params=_cparams(("arbitrary", "arbitrary")),
        name="ffn_dense",
    )(xm2, x, modl, wg, wu, wd)


MOE_TILE = 256
MOE_ROWS = 2 * M_TOK + N_EXPERTS * MOE_TILE
MOE_TILES = MOE_ROWS // MOE_TILE
MOE_PAD_ROWS = MOE_ROWS - 2 * M_TOK
MOE_SEG = D_MODEL // LANES


def _to_row_tiles(val, ref, base):
    n = val.shape[0]
    for j in range(MOE_SEG):
        ref[pl.ds(base + j, n, stride=MOE_SEG), :] = val[:, LANES * j:LANES * (j + 1)]


def _from_row_tiles(ref, base, n):
    return jnp.concatenate([ref[pl.ds(base + j, n, stride=MOE_SEG), :] for j in range(MOE_SEG)], axis=1)


def _router_kernel(xm_ref, wr_ref, tri_ref, sel_ref, xp_ref, cnt_ref, base_sc):
    i = pl.program_id(0)
    xb = xm_ref[...]
    tm = xb.shape[0]
    lane = lax.broadcasted_iota(jnp.int32, (tm, LANES), 1)

    @pl.when(i == 0)
    def _():
        base_sc[...] = jnp.zeros_like(base_sc)

    logits = jnp.where(lane < N_EXPERTS, _dot(xb, wr_ref[...]), -1e30)
    mx = jnp.max(logits, axis=-1, keepdims=True)
    ex = jnp.exp(logits - mx)
    probs = ex / jnp.sum(ex, axis=-1, keepdims=True)
    p1 = jnp.max(probs, axis=-1, keepdims=True)
    i1 = jnp.min(jnp.where(probs == p1, lane, LANES), axis=-1, keepdims=True)
    rest = jnp.where(lane == i1, -1.0, probs)
    p2 = jnp.max(rest, axis=-1, keepdims=True)
    i2 = jnp.min(jnp.where(rest == p2, lane, LANES), axis=-1, keepdims=True)
    tot = p1 + p2
    oh1 = lane == i1
    oh2 = lane == i2
    oh = jnp.where(oh1 | oh2, 1.0, 0.0)
    before = base_sc[0:1, :] + _dot(tri_ref[...], oh.astype(BF16))
    r1 = jnp.sum(jnp.where(oh1, before, 0.0), axis=-1, keepdims=True)
    r2 = jnp.sum(jnp.where(oh2, before, 0.0), axis=-1, keepdims=True)
    base_sc[0:1, :] = base_sc[0:1, :] + jnp.sum(oh, axis=0, keepdims=True)
    cnt_ref[...] = base_sc[...]
    sel = jnp.where(lane == 0, p1 / tot, 0.0) + jnp.where(lane == 1, p2 / tot, 0.0)
    sel = sel + jnp.where(lane == 2, i1.astype(F32), 0.0) + jnp.where(lane == 3, i2.astype(F32), 0.0)
    sel_ref[...] = sel + jnp.where(lane == 4, r1, 0.0) + jnp.where(lane == 5, r2, 0.0)
    _to_row_tiles(xb.astype(F32), xp_ref, 0)


def _moe_router(xm2, wr):
    tm = TM_FFN
    tri = jnp.tril(jnp.ones((tm, tm), F32), -1).astype(BF16)
    return pl.pallas_call(
        _router_kernel,
        grid=(M_TOK // tm,),
        in_specs=[pl.BlockSpec((tm, D_MODEL), lambda i: (i, 0)),
                  pl.BlockSpec((D_MODEL, LANES), lambda i: (0, 0)),
                  pl.BlockSpec((tm, tm), lambda i: (0, 0))],
        out_specs=[pl.BlockSpec((tm, LANES), lambda i: (i, 0)),
                   pl.BlockSpec((tm * MOE_SEG, LANES), lambda i: (i, 0)),
                   pl.BlockSpec((SUBLANES, LANES), lambda i: (0, 0))],
        out_shape=[jax.ShapeDtypeStruct((M_TOK, LANES), F32),
                   jax.ShapeDtypeStruct((M_TOK * MOE_SEG, LANES), F32),
                   jax.ShapeDtypeStruct((SUBLANES, LANES), F32)],
        scratch_shapes=[pltpu.VMEM((SUBLANES, LANES), F32)],
        compiler_params=_cparams(("arbitrary",)),
        name="moe_router",
    )(xm2, wr, tri)


def _row_copy(src, srow8, dst, drow8, sem):
    aligned = lambda r: r if isinstance(r, int) else pl.multiple_of(r, MOE_SEG)
    return pltpu.make_async_copy(src.at[pl.ds(aligned(srow8), MOE_SEG), :],
                                 dst.at[pl.ds(aligned(drow8), MOE_SEG), :], sem)


def _dispatch_kernel(pos_ref, pad_ref, xp_ref, xs_ref, ring_sc, zero_sc, sem, *, nsteps):
    i = pl.program_id(0)
    slot = i % 2
    nrow = MOE_TILE
    slot_rows = nrow * MOE_SEG

    def wait_slot(s):
        for _ in range(2):
            pltpu.make_async_copy(ring_sc.at[s], xs_ref.at[pl.ds(0, slot_rows), :], sem.at[s]).wait()

    @pl.when(i >= 2)
    def _():
        wait_slot(slot)

    ring_sc[slot] = xp_ref[...]

    def body(t, c):
        _row_copy(ring_sc.at[slot], t * MOE_SEG, xs_ref, pos_ref[0, 2 * t], sem.at[slot]).start(priority=0)
        _row_copy(ring_sc.at[slot], t * MOE_SEG, xs_ref, pos_ref[0, 2 * t + 1], sem.at[slot]).start(priority=1)
        return c

    lax.fori_loop(0, nrow, body, 0, unroll=4)

    @pl.when(i == nsteps - 1)
    def _():
        zero_sc[...] = jnp.zeros_like(zero_sc)

        def zbody(t, c):
            _row_copy(zero_sc, 0, xs_ref, pad_ref[t], sem.at[2]).start()
            return c

        lax.fori_loop(0, MOE_PAD_ROWS, zbody, 0, unroll=4)
        wait_slot(1 - slot)
        wait_slot(slot)

        for _ in range(MOE_PAD_ROWS // nrow):
            pltpu.make_async_copy(ring_sc.at[0], xs_ref.at[pl.ds(0, slot_rows), :], sem.at[2]).wait()


def _moe_dispatch(xp, pos, padrows):
    nsteps = M_TOK // MOE_TILE
    return pl.pallas_call(
        functools.partial(_dispatch_kernel, nsteps=nsteps),
        grid=(nsteps,),
        in_specs=[pl.BlockSpec((None, 1, 2 * MOE_TILE), lambda i: (i, 0, 0), memory_space=pltpu.SMEM),
                  pl.BlockSpec(memory_space=pltpu.SMEM),
                  pl.BlockSpec((MOE_TILE * MOE_SEG, LANES), lambda i: (i, 0))],
        out_specs=pl.BlockSpec(memory_space=pl.ANY),
        out_shape=jax.ShapeDtypeStruct((MOE_ROWS * MOE_SEG, LANES), F32),
        scratch_shapes=[pltpu.VMEM((2, MOE_TILE * MOE_SEG, LANES), F32), pltpu.VMEM((MOE_SEG, LANES), F32),
                        pltpu.SemaphoreType.DMA((3,))],
        compiler_params=_cparams(("arbitrary",)),
        name="moe_dispatch",
    )(pos, padrows, xp)


def _experts_kernel(te_ref, nu_ref, xs_ref, wg_ref, wu_ref, wd_ref, ys_ref):
    i = pl.program_id(0)

    @pl.when(i < nu_ref[0])
    def _():
        xb = _from_row_tiles(xs_ref, 0, MOE_TILE).astype(BF16)
        g = _dot(xb, wg_ref[...])
        u = _dot(xb, wu_ref[...])
        hid = (g * _sigmoid(g) * u).astype(BF16)
        _to_row_tiles(_dot(hid, wd_ref[...]), ys_ref, 0)

    @pl.when(i >= nu_ref[0])
    def _():
        ys_ref[...] = jnp.zeros_like(ys_ref)


def _moe_experts(tile_expert, n_used, xs, wg, wu, wd):
    grid_spec = pltpu.PrefetchScalarGridSpec(
        num_scalar_prefetch=2,
        grid=(MOE_TILES,),
        in_specs=[pl.BlockSpec((MOE_TILE * MOE_SEG, LANES), lambda i, te, nu: (i, 0)),
                  pl.BlockSpec((None, D_MODEL, D_FF_E), lambda i, te, nu: (te[i], 0, 0)),
                  pl.BlockSpec((None, D_MODEL, D_FF_E), lambda i, te, nu: (te[i], 0, 0)),
                  pl.BlockSpec((None, D_FF_E, D_MODEL), lambda i, te, nu: (te[i], 0, 0))],
        out_specs=pl.BlockSpec((MOE_TILE * MOE_SEG, LANES), lambda i, te, nu: (i, 0)),
    )
    return pl.pallas_call(
        _experts_kernel,
        grid_spec=grid_spec,
        out_shape=jax.ShapeDtypeStruct((MOE_ROWS * MOE_SEG, LANES), F32),
        compiler_params=_cparams(("arbitrary",)),
        name="moe_experts",
    )(tile_expert, n_used, xs, wg, wu, wd)


def _combine_kernel(pos_ref, x_ref, mod_ref, sel_ref, ys_ref, oc_ref, ol_ref, buf_sc, sem, *, nsteps, nctx):
    i = pl.program_id(0)
    slot = i % 2
    nrow = MOE_TILE
    part = nrow * MOE_SEG

    def start(s, off):
        def body(t, c):
            for k in range(2):
                _row_copy(ys_ref, pos_ref[0, off + 2 * t + k], buf_sc, (2 * s + k) * part + t * MOE_SEG,
                          sem.at[s]).start(priority=k)
            return c
        lax.fori_loop(0, nrow, body, 0, unroll=4)

    @pl.when(i == 0)
    def _():
        start(0, 0)

    @pl.when(i + 1 < nsteps)
    def _():
        start(1 - slot, 2 * nrow)

    for k in range(2):
        pltpu.make_async_copy(ys_ref.at[pl.ds(0, part), :], buf_sc.at[pl.ds(0, part), :], sem.at[slot]).wait()
    lane = lax.broadcasted_iota(jnp.int32, (nrow, LANES), 1)
    sel = sel_ref[...]
    w1 = jnp.sum(jnp.where(lane == 0, sel, 0.0), axis=-1, keepdims=True)
    w2 = jnp.sum(jnp.where(lane == 1, sel, 0.0), axis=-1, keepdims=True)
    y = (w1 * _from_row_tiles(buf_sc, 2 * slot * part, nrow)
         + w2 * _from_row_tiles(buf_sc, (2 * slot + 1) * part, nrow))
    val = x_ref[...] + mod_ref[5:6, :] * y

    @pl.when(i < nctx)
    def _():
        oc_ref[...] = val

    @pl.when(i >= nctx)
    def _():
        ol_ref[...] = val


def _moe_combine(pos2, x, modl, sel, ys):
    nsteps = M_TOK // MOE_TILE
    tm = MOE_TILE
    nctx = M_CTX // tm
    return pl.pallas_call(
        functools.partial(_combine_kernel, nsteps=nsteps, nctx=nctx),
        grid=(nsteps,),
        in_specs=[pl.BlockSpec((None, 1, 4 * MOE_TILE), lambda i: (i, 0, 0), memory_space=pltpu.SMEM),
                  pl.BlockSpec((tm, D_MODEL), lambda i: (i, 0)),
                  pl.BlockSpec((None, SUBLANES, D_MODEL), lambda i: (_mod_row(i, tm), 0, 0)),
                  pl.BlockSpec((tm, LANES), lambda i: (i, 0)),
                  pl.BlockSpec(memory_space=pl.ANY)],
        out_specs=[pl.BlockSpec((tm, D_MODEL), lambda i: (jnp.minimum(i, nctx - 1), 0)),
                   pl.BlockSpec((tm, D_MODEL), lambda i: (jnp.maximum(i - nctx, 0), 0))],
        out_shape=[jax.ShapeDtypeStruct((M_CTX, D_MODEL), F32), jax.ShapeDtypeStruct((M_LAT, D_MODEL), F32)],
        scratch_shapes=[pltpu.VMEM((2 * 2 * MOE_TILE * MOE_SEG, LANES), F32), pltpu.SemaphoreType.DMA((2,))],
        compiler_params=_cparams(("arbitrary",)),
        name="moe_combine",
    )(pos2, x, modl, sel, ys)


def _ffn_moe(xm2, x, modl, wr, wg, wu, wd):
    sel, xp, cnt = _moe_router(xm2, wr)
    counts = cnt[0, :N_EXPERTS].astype(jnp.int32)
    padded = ((counts + MOE_TILE - 1) // MOE_TILE) * MOE_TILE
    ends = jnp.cumsum(padded)
    offs = ends - padded
    experts = sel[:, 2:4].astype(jnp.int32)
    ranks = sel[:, 4:6].astype(jnp.int32)
    pos = (offs[experts] + ranks) * MOE_SEG
    tile_start = jnp.arange(MOE_TILES, dtype=jnp.int32) * MOE_TILE
    tile_expert = jnp.minimum(jnp.sum(tile_start[:, None] >= ends[None, :], axis=1), N_EXPERTS - 1)
    n_used = (ends[-1] // MOE_TILE).astype(jnp.int32)[None]
    rows = jnp.arange(MOE_ROWS, dtype=jnp.int32)
    row_expert = jnp.repeat(tile_expert, MOE_TILE)
    written = (rows < ends[-1]) & (rows - offs[row_expert] < counts[row_expert])
    padrows = jnp.nonzero(~written, size=MOE_PAD_ROWS)[0].astype(jnp.int32) * MOE_SEG
    pos_tiles = pos.reshape(M_TOK // MOE_TILE, 1, 2 * MOE_TILE)
    xs = _moe_dispatch(xp, pos_tiles, padrows)
    ys = _moe_experts(tile_expert.astype(jnp.int32), n_used, xs, wg, wu, wd)
    nxt = jnp.concatenate([pos_tiles[1:], pos_tiles[-1:]], axis=0)
    pos2 = jnp.concatenate([pos_tiles, nxt], axis=2)
    return tuple(_moe_combine(pos2, x, modl, sel, ys))


def _block_diag(w):
    nb, bs, _ = w.shape
    eye = jnp.eye(nb, dtype=w.dtype)
    return jnp.einsum("njk,nm->njmk", w, eye).reshape(nb * bs, nb * bs)


def _head_pad_cols(w, width):
    r = w.shape[0]
    return jnp.pad(w, ((0, 0), (0, 0), (0, HEAD_PAD - width))).reshape(r, N_HEADS * HEAD_PAD)


def _swap_rope_pairs(a):
    nope, rope = a[..., :D_NOPE], a[..., D_NOPE:]
    sw = rope.reshape(rope.shape[:-1] + (D_ROPE // 2, 2))[..., ::-1].reshape(rope.shape)
    return jnp.concatenate([nope, sw], axis=-1)


def _head_gain(g):
    rows = jnp.stack([g, _swap_rope_pairs(g)], axis=0)
    return jnp.pad(rows, ((0, SUBLANES - 2), (0, HEAD_PAD - D_QK)))


def _rope_tables(tm):
    rows = DEC_SEQ // GRID_W
    row = jnp.repeat(jnp.arange(rows, dtype=F32), GRID_W)
    col = jnp.tile(jnp.arange(GRID_W, dtype=F32), rows)
    half = D_ROPE // 2
    inv_freq = ROPE_BASE ** (-jnp.arange(0, half, 2, dtype=F32) / half)
    ang = jnp.concatenate([row[:, None] * inv_freq, col[:, None] * inv_freq], axis=-1)
    cos, sin = jnp.cos(ang), jnp.sin(ang)
    cos2 = jnp.repeat(cos, 2, axis=1)
    sin2 = jnp.stack([-sin, sin], axis=-1).reshape(DEC_SEQ, D_ROPE)
    cos_t = jnp.pad(cos2, ((0, 0), (D_NOPE, HEAD_PAD - D_QK)), constant_values=1.0)
    sin_t = jnp.pad(sin2, ((0, 0), (D_NOPE, HEAD_PAD - D_QK)))
    cos_t = jnp.concatenate([jnp.ones((tm, HEAD_PAD), F32), cos_t], axis=0)
    sin_t = jnp.concatenate([jnp.zeros((tm, HEAD_PAD), F32), sin_t], axis=0)
    return cos_t, sin_t


def kernel(x_prompt, x_sample, cache_ckv, cache_krope, state_lru, c, c_ctx, norm1, norm2, w_ada, b_ada, w_in, mla_q_norm, mla_kv_norm, mla_w_uq, mla_w_uk, mla_w_uv, mla_q_qknorm, mla_k_qknorm, lru_conv_w, lru_conv_b, lru_w_gate, lru_b_gate, lru_lambda, hy_conv_w, hy_conv_b, hy_w1, hy_b1, hy_freq1, hy_w2, hy_b2, hy_freq2, hy_w3, hy_bias, w_lru_out, w_mla_out, w_hy_out, w_out, ffn_w_gate, ffn_w_up, ffn_w_down, moe_w_router, moe_w_gate, moe_w_up, moe_w_down):
    x = (x_prompt.reshape(M_CTX, D_MODEL), x_sample.reshape(M_LAT, D_MODEL))

    cond = jnp.concatenate([c_ctx[None, :], c, jnp.zeros((SUBLANES - 1 - DEC_BATCH, D_MODEL), F32)], axis=0)
    mod = _adaln(cond, w_ada, b_ada).reshape(DEPTH, SUBLANES, 6, D_MODEL)
    mod = jnp.pad(mod, ((0, 0), (0, 0), (0, SUBLANES - 6), (0, 0)))

    cos_t, sin_t = _rope_tables(TM1)
    ctx_tabs = _ctx_tables()
    lat_tabs = _lat_tables()
    zero_state = jnp.zeros((BATCH, SUBLANES, LRU_W), F32)

    ckv_out, kr_out, st_out = [], [], []
    for l in range(DEPTH):
        wl = w_in[l]
        wkr = jnp.concatenate([jnp.zeros((D_MODEL, D_NOPE), F32), wl[:, 896:928]], axis=1)
        krblk = lambda w: jnp.pad(w, ((0, 0), (0, HEAD_PAD - D_QK)))
        w1 = jnp.concatenate([wl[:, :896], krblk(wkr), krblk(_swap_rope_pairs(wkr)), wl[:, 928:2464]],
                             axis=1).astype(BF16)
        wgates = wl[:, 2464:].astype(BF16)
        wuq = _head_pad_cols(mla_w_uq[l], D_QK).astype(BF16)
        wuqs = _head_pad_cols(_swap_rope_pairs(mla_w_uq[l]), D_QK).astype(BF16)
        wuk = _head_pad_cols(mla_w_uk[l], D_NOPE).astype(BF16)
        wuv = mla_w_uv[l].reshape(KV_RANK, N_HEADS * D_V).astype(BF16)
        gq = _head_gain(mla_q_qknorm[l])
        gk = _head_gain(mla_k_qknorm[l])

        ulru, uhy, ckv, krb, q, k, v = _stage1(
            x, mod[l], norm1[l][None, :], w1, mla_kv_norm[l][None, :], mla_q_norm[l][None, :],
            wuq, wuqs, gq, wuk, gk, wuv, cos_t, sin_t)
        ckv_out.append(ckv[:M_CTX].reshape(BATCH, SEQ, KV_RANK))
        kr_out.append(krb[:M_CTX, D_NOPE:D_QK].reshape(BATCH, SEQ, D_ROPE))

        kc, vc = _kvprep(cache_ckv[:, l].reshape(DEC_BATCH * PAST_LEN, KV_RANK),
                         jnp.pad(cache_krope[:, l].reshape(DEC_BATCH * PAST_LEN, D_ROPE),
                                 ((0, 0), (D_NOPE, HEAD_PAD - D_QK))),
                         wuk, gk, wuv)
        ymla = (_attention_ctx(q, k, v), _attention_lat(q, k, v, kc, vc))

        lp = dict(
            cw=jnp.pad(lru_conv_w[l], ((0, SUBLANES - 4), (0, 0))), cb=lru_conv_b[l][None, :],
            wr=[_block_diag(lru_w_gate[l, d, 0]).astype(BF16) for d in range(2)],
            wi=[_block_diag(lru_w_gate[l, d, 1]).astype(BF16) for d in range(2)],
            br=[lru_b_gate[l, d, 0][None, :] for d in range(2)],
            bi=[lru_b_gate[l, d, 1][None, :] for d in range(2)],
            lam=[lru_lambda[l, d][None, :] for d in range(2)])
        y_c, stf, stb = _lru_mixer(ulru, lp, (zero_state, zero_state), row_off=0, nseq=BATCH, seqlen=SEQ, tc=SEQ)
        st_out.append(jnp.stack([stf, stb], axis=1))
        h0 = [jnp.broadcast_to(state_lru[:, l, d][:, None, :], (DEC_BATCH, SUBLANES, LRU_W)) for d in range(2)]
        y_l, _, _ = _lru_mixer(ulru, lp, h0, row_off=M_CTX, nseq=DEC_BATCH, seqlen=DEC_SEQ, tc=512)
        ylru = (y_c, y_l)

        hp = dict(
            cw=jnp.pad(hy_conv_w[l], ((0, SUBLANES - 3), (0, 0))), cb=hy_conv_b[l][None, :],
            w1=jnp.pad(hy_w1[l], ((0, LANES - HY_EMB), (0, 0))).astype(BF16), b1=hy_b1[l][None, :],
            f1=hy_freq1[l][None, :], w2=hy_w2[l].astype(BF16), b2=hy_b2[l][None, :], f2=hy_freq2[l][None, :],
            w3=hy_w3[l].astype(BF16), bias=hy_bias[l])
        yhy = (_hyena_ctx(uhy, hp, ctx_tabs), _hyena_lat(uhy, hp, lat_tabs))

        xmid, xm2 = _stage3(x, mod[l], norm1[l][None, :], wgates, ylru, ymla, yhy,
                            w_lru_out[l].astype(BF16), w_mla_out[l].astype(BF16), w_hy_out[l].astype(BF16),
                            w_out[l].astype(BF16), norm2[l][None, :])
        j = l // 2
        if l % 2 == 0:
            x = (_ffn_dense(xm2, xmid, mod[l], ffn_w_gate[j].astype(BF16), ffn_w_up[j].astype(BF16),
                            ffn_w_down[j].astype(BF16)),)
        else:
            wr = jnp.pad(moe_w_router[j], ((0, 0), (0, LANES - N_EXPERTS))).astype(BF16)
            x = _ffn_moe(xm2, xmid, mod[l], wr, moe_w_gate[j].astype(BF16), moe_w_up[j].astype(BF16),
                         moe_w_down[j].astype(BF16))

    xc, xl = x if len(x) == 2 else (x[0][:M_CTX], x[0][M_CTX:])
    y_prompt = xc.reshape(BATCH, SEQ, D_MODEL)
    y_sample = xl.reshape(DEC_BATCH, DEC_SEQ, D_MODEL)
    return (y_prompt, y_sample, jnp.stack(ckv_out, axis=1), jnp.stack(kr_out, axis=1), jnp.stack(st_out, axis=1))
```

```python
import functools
import math

import jax
import jax.numpy as jnp
from jax import lax
from jax.experimental import pallas as pl
from jax.experimental.pallas import tpu as pltpu

F32 = jnp.float32
BF16 = jnp.bfloat16

D_MODEL = 1024
BATCH = 32
SEQ = 256
DEPTH = 2
DEC_BATCH = 2
DEC_SEQ = 4096
PAST_LEN = 512
GRID_W = 64
EPS = 1e-6
LRU_W = 512
LRU_BLOCKS = 8
LRU_C = 8.0
N_HEADS = 8
D_NOPE = 64
D_ROPE = 32
D_QK = D_NOPE + D_ROPE
D_V = 64
Q_RANK = 256
KV_RANK = 128
ROPE_BASE = 10000.0
HY_W = 512
HY_ORDER = 2
HY_EMB = 33
HY_HID = 64
HY_FAST_PCT = 0.3
HY_SLOW_PCT = 1.5
D_FF = 2816
N_EXPERTS = 8
D_FF_E = 1408

LANES = 128
SUBLANES = 8
VMEM_LIMIT = 56 * 1024 * 1024

M_CTX = BATCH * SEQ
M_LAT = DEC_BATCH * DEC_SEQ
M_TOK = M_CTX + M_LAT
TM1 = 512
TM3 = 512
TM_FFN = 512
W1_COLS = 2688
HEAD_PAD = LANES
QK_SCALE = math.log2(math.e) / math.sqrt(D_QK)
ATTN_TQ = 256
ATTN_LAT_HEADS = 4
ATTN_LAT_GROUPS = (((0, 0, DEC_SEQ // 2),), ((0, DEC_SEQ // 2, DEC_SEQ), (1, 0, PAST_LEN)))

FFT_N1 = 64
FFT_N2 = 128


def _cparams(sem, vmem=VMEM_LIMIT):
    return pltpu.CompilerParams(dimension_semantics=sem, vmem_limit_bytes=vmem)


def _dot(a, b):
    return jnp.dot(a, b, preferred_element_type=F32)


def _rms(x, g):
    ms = jnp.mean(x * x, axis=-1, keepdims=True)
    return x * lax.rsqrt(ms + EPS) * g


def _sigmoid(x):
    return 1.0 / (1.0 + jnp.exp(-x))


def _ada_kernel(c_ref, w_ref, b_ref, o_ref):
    c = c_ref[...]
    s = (c * _sigmoid(c)).astype(BF16)
    o_ref[...] = _dot(s, w_ref[...].astype(BF16)) + b_ref[...]


def _adaln(cond, w_ada, b_ada):
    tn = 1024
    n6 = 6 * D_MODEL
    return pl.pallas_call(
        _ada_kernel,
        grid=(DEPTH, n6 // tn),
        in_specs=[
            pl.BlockSpec((SUBLANES, D_MODEL), lambda l, j: (0, 0)),
            pl.BlockSpec((None, D_MODEL, tn), lambda l, j: (l, 0, j)),
            pl.BlockSpec((None, 1, tn), lambda l, j: (l, 0, j)),
        ],
        out_specs=pl.BlockSpec((None, SUBLANES, tn), lambda l, j: (l, 0, j)),
        out_shape=jax.ShapeDtypeStruct((DEPTH, SUBLANES, n6), F32),
        compiler_params=_cparams(("arbitrary", "arbitrary")),
        name="adaln",
    )(cond, w_ada, b_ada.reshape(DEPTH, 1, n6))


def _mod_row(i, tm):
    nctx = M_CTX // tm
    per = DEC_SEQ // tm
    return jnp.where(i < nctx, 0, 1 + (i - nctx) // per)


def _rope_blk(i, tm):
    nctx = M_CTX // tm
    per = DEC_SEQ // tm
    return jnp.where(i < nctx, 0, 1 + (i - nctx) % per)


def _finish_head(raw, raw_sw, gc, gs, out_ref, sl):
    ms = jnp.sum(raw * raw, axis=-1, keepdims=True) * (1.0 / D_QK)
    rs = lax.rsqrt(ms + EPS)
    val = raw * gc
    if gs is not None:
        val = val + raw_sw * gs
    out_ref[:, sl] = (val * rs).astype(BF16)


def _read_tokens(x_refs, nctx):
    if len(x_refs) == 1:
        return x_refs[0][...]
    return jnp.where(pl.program_id(0) < nctx, x_refs[0][...], x_refs[1][...])


def _token_specs(x, tm):
    nctx = M_CTX // tm
    if len(x) == 1:
        return [pl.BlockSpec((tm, D_MODEL), lambda i: (i, 0))]
    return [pl.BlockSpec((tm, D_MODEL), lambda i: (jnp.minimum(i, nctx - 1), 0)),
            pl.BlockSpec((tm, D_MODEL), lambda i: (jnp.maximum(i - nctx, 0), 0))]


def _stage1_kernel(*refs, nx, nctx):
    (mod_ref, g1_ref, w1_ref, gkv_ref, gqn_ref, wuq_ref, wuqs_ref, gq_ref, wuk_ref, gk_ref,
     wuv_ref, cos_ref, sin_ref, ulru_ref, uhy_ref, ckv_ref, krb_ref, q_ref, k_ref, v_ref) = refs[nx:]
    x = _read_tokens(refs[:nx], nctx)
    xm = _rms(x, g1_ref[...]) * (1.0 + mod_ref[1:2, :]) + mod_ref[0:1, :]
    xb = xm.astype(BF16)
    ulru_ref[...] = _dot(xb, w1_ref[:, 0:512])
    qc = _dot(xb, w1_ref[:, 512:768])
    ckv = _dot(xb, w1_ref[:, 768:896])
    krb = _dot(xb, w1_ref[:, 896:1024])
    krs = _dot(xb, w1_ref[:, 1024:1152])
    uhy_ref[...] = _dot(xb, w1_ref[:, 1152:2688])
    ckvn = _rms(ckv, gkv_ref[...])
    ckv_ref[...] = ckvn
    krb_ref[...] = krb
    qn = _rms(qc, gqn_ref[...]).astype(BF16)
    cb = ckvn.astype(BF16)
    v_ref[...] = _dot(cb, wuv_ref[...]).astype(BF16)
    cos = cos_ref[...]
    sin = sin_ref[...]
    gcq = cos * (gq_ref[0:1, :] * QK_SCALE)
    gsq = sin * (gq_ref[1:2, :] * QK_SCALE)
    gck = cos * gk_ref[0:1, :]
    gsk = sin * gk_ref[1:2, :]
    for pair in range(N_HEADS // 2):
        ps = slice(2 * HEAD_PAD * pair, 2 * HEAD_PAD * (pair + 1))
        qraw, qsw, kraw = _dot(qn, wuq_ref[:, ps]), _dot(qn, wuqs_ref[:, ps]), _dot(cb, wuk_ref[:, ps])
        for j in range(2):
            hs = slice(HEAD_PAD * j, HEAD_PAD * (j + 1))
            sl = slice(HEAD_PAD * (2 * pair + j), HEAD_PAD * (2 * pair + j + 1))
            _finish_head(qraw[:, hs], qsw[:, hs], gcq, gsq, q_ref, sl)
            _finish_head(kraw[:, hs] + krb, krs, gck, gsk, k_ref, sl)


def _stage1(x, modl, g1, w1, gkv, gqn, wuq, wuqs, gq, wuk, gk, wuv, cos_t, sin_t):
    tm = TM1
    full = lambda shape: pl.BlockSpec(shape, lambda i: (0,) * len(shape))
    row = lambda cols: pl.BlockSpec((tm, cols), lambda i: (i, 0))
    hw = N_HEADS * HEAD_PAD
    return pl.pallas_call(
        functools.partial(_stage1_kernel, nx=len(x), nctx=M_CTX // tm),
        grid=(M_TOK // tm,),
        in_specs=_token_specs(x, tm) + [
            pl.BlockSpec((None, SUBLANES, D_MODEL), lambda i: (_mod_row(i, tm), 0, 0)),
            full((1, D_MODEL)),
            full((D_MODEL, W1_COLS)),
            full((1, KV_RANK)),
            full((1, Q_RANK)),
            full((Q_RANK, hw)),
            full((Q_RANK, hw)),
            full((SUBLANES, HEAD_PAD)),
            full((KV_RANK, hw)),
            full((SUBLANES, HEAD_PAD)),
            full((KV_RANK, N_HEADS * D_V)),
            pl.BlockSpec((tm, LANES), lambda i: (_rope_blk(i, tm), 0)),
            pl.BlockSpec((tm, LANES), lambda i: (_rope_blk(i, tm), 0)),
        ],
        out_specs=[row(LRU_W), row(3 * HY_W), row(KV_RANK), row(LANES), row(hw), row(hw), row(N_HEADS * D_V)],
        out_shape=[
            jax.ShapeDtypeStruct((M_TOK, LRU_W), F32),
            jax.ShapeDtypeStruct((M_TOK, 3 * HY_W), F32),
            jax.ShapeDtypeStruct((M_TOK, KV_RANK), F32),
            jax.ShapeDtypeStruct((M_TOK, LANES), F32),
            jax.ShapeDtypeStruct((M_TOK, hw), BF16),
            jax.ShapeDtypeStruct((M_TOK, hw), BF16),
            jax.ShapeDtypeStruct((M_TOK, N_HEADS * D_V), BF16),
        ],
        compiler_params=_cparams(("arbitrary",)),
        name="stage1",
    )(*x, modl, g1, w1, gkv, gqn, wuq, wuqs, gq, wuk, gk, wuv, cos_t, sin_t)


def _kvprep_kernel(ckv_ref, krb_ref, wuk_ref, gk_ref, wuv_ref, k_ref, v_ref):
    cb = ckv_ref[...].astype(BF16)
    v_ref[...] = _dot(cb, wuv_ref[...]).astype(BF16)
    krb = krb_ref[...]
    for h in range(N_HEADS):
        sl = slice(HEAD_PAD * h, HEAD_PAD * (h + 1))
        _finish_head(_dot(cb, wuk_ref[:, sl]) + krb, None, gk_ref[0:1, :], None, k_ref, sl)


def _kvprep(ckv, krb, wuk, gk, wuv):
    rows = ckv.shape[0]
    tm = TM1
    hw = N_HEADS * HEAD_PAD
    full = lambda shape: pl.BlockSpec(shape, lambda i: (0,) * len(shape))
    row = lambda cols: pl.BlockSpec((tm, cols), lambda i: (i, 0))
    return pl.pallas_call(
        _kvprep_kernel,
        grid=(rows // tm,),
        in_specs=[row(KV_RANK), row(LANES), full((KV_RANK, hw)), full((SUBLANES, HEAD_PAD)),
                  full((KV_RANK, N_HEADS * D_V))],
        out_specs=[row(hw), row(N_HEADS * D_V)],
        out_shape=[jax.ShapeDtypeStruct((rows, hw), BF16), jax.ShapeDtypeStruct((rows, N_HEADS * D_V), BF16)],
        compiler_params=_cparams(("arbitrary",)),
        name="kvprep",
    )(ckv, krb, wuk, gk, wuv)


def _attn_kernel(*refs, heads, nseg, groups):
    q_ref = refs[0]
    k_refs = refs[1:1 + nseg]
    v_refs = refs[1 + nseg:1 + 2 * nseg]
    o_ref = refs[1 + 2 * nseg]
    tq = q_ref.shape[0]
    lane = lax.broadcasted_iota(jnp.int32, (tq, LANES), 1)
    low = lane < D_V
    for pair in range(heads // 2):
        outs = []
        for j in range(2):
            h = 2 * pair + j
            sl = slice(HEAD_PAD * h, HEAD_PAD * (h + 1))
            q = q_ref[:, sl]
            parts = []
            for group in groups:
                s = [lax.dot_general(q, k_refs[seg][r0:r1, sl], (((1,), (1,)), ((), ())),
                                     preferred_element_type=F32) for seg, r0, r1 in group]
                m = jnp.max(s[0], axis=-1, keepdims=True)
                for si in s[1:]:
                    m = jnp.maximum(m, jnp.max(si, axis=-1, keepdims=True))
                acc = None
                den = None
                for si, (seg, r0, r1) in zip(s, group):
                    p = jnp.exp2(si - m)
                    d = jnp.sum(p, axis=-1, keepdims=True)
                    o = _dot(p.astype(BF16), v_refs[seg][r0:r1, LANES * pair:LANES * (pair + 1)])
                    acc = o if acc is None else acc + o
                    den = d if den is None else den + d
                parts.append((m, acc, den))
            if len(parts) == 1:
                _, acc, den = parts[0]
            else:
                mall = parts[0][0]
                for m, _, _ in parts[1:]:
                    mall = jnp.maximum(mall, m)
                acc = None
                den = None
                for m, a, d in parts:
                    w = jnp.exp2(m - mall)
                    acc = w * a if acc is None else acc + w * a
                    den = w * d if den is None else den + w * d
            outs.append(acc / den)
        o_ref[:, LANES * pair:LANES * (pair + 1)] = jnp.where(low, outs[0], outs[1]).astype(BF16)


def _attention_ctx(q, k, v):
    hw = N_HEADS * HEAD_PAD
    vw = N_HEADS * D_V
    return pl.pallas_call(
        functools.partial(_attn_kernel, heads=N_HEADS, nseg=1, groups=(((0, 0, SEQ),),)),
        grid=(BATCH,),
        in_specs=[
            pl.BlockSpec((SEQ, hw), lambda b: (b, 0)),
            pl.BlockSpec((SEQ, hw), lambda b: (b, 0)),
            pl.BlockSpec((SEQ, vw), lambda b: (b, 0)),
        ],
        out_specs=pl.BlockSpec((SEQ, vw), lambda b: (b, 0)),
        out_shape=jax.ShapeDtypeStruct((M_CTX, vw), BF16),
        compiler_params=_cparams(("arbitrary",)),
        name="attn_ctx",
    )(q, k, v)


def _attention_lat(q, k, v, kc, vc):
    tq = ATTN_TQ
    nq = DEC_SEQ // tq
    qoff = M_CTX // tq
    koff = M_CTX // DEC_SEQ
    hp = ATTN_LAT_HEADS
    return pl.pallas_call(
        functools.partial(_attn_kernel, heads=hp, nseg=2, groups=ATTN_LAT_GROUPS),
        grid=(DEC_BATCH, N_HEADS // hp, nq),
        in_specs=[
            pl.BlockSpec((tq, hp * HEAD_PAD), lambda b, p, i: (qoff + b * nq + i, p)),
            pl.BlockSpec((DEC_SEQ, hp * HEAD_PAD), lambda b, p, i: (koff + b, p)),
            pl.BlockSpec((PAST_LEN, hp * HEAD_PAD), lambda b, p, i: (b, p)),
            pl.BlockSpec((DEC_SEQ, hp * D_V), lambda b, p, i: (koff + b, p)),
            pl.BlockSpec((PAST_LEN, hp * D_V), lambda b, p, i: (b, p)),
        ],
        out_specs=pl.BlockSpec((tq, hp * D_V), lambda b, p, i: (b * nq + i, p)),
        out_shape=jax.ShapeDtypeStruct((M_LAT, N_HEADS * D_V), BF16),
        compiler_params=_cparams(("arbitrary", "arbitrary", "arbitrary")),
        name="attn_lat",
    )(q, k, kc, v, vc)


def _lru_kernel(*refs, reverse, tc, nchunks):
    if reverse:
        (up_ref, uc_ref, un_ref, hf_ref, cw_ref, cb_ref, wr_ref, wi_ref, br_ref, bi_ref, lam_ref, h0_ref,
         y_ref, st_ref, ext_sc, a_sc, b_sc, p_sc, h_sc, car_sc) = refs
    else:
        (up_ref, uc_ref, un_ref, cw_ref, cb_ref, wr_ref, wi_ref, br_ref, bi_ref, lam_ref, h0_ref,
         y_ref, st_ref, ext_sc, a_sc, b_sc, p_sc, h_sc, car_sc) = refs
    c = pl.program_id(1)
    chunk = (nchunks - 1 - c) if reverse else c
    prev = jnp.where(chunk == 0, 0.0, up_ref[...])
    nxt = jnp.where(chunk == nchunks - 1, 0.0, un_ref[...])
    ext_sc[0:SUBLANES, :] = prev
    ext_sc[SUBLANES:SUBLANES + tc, :] = uc_ref[...]
    ext_sc[SUBLANES + tc:2 * SUBLANES + tc, :] = nxt
    xc = cb_ref[...]
    for k in range(4):
        xc = xc + cw_ref[k:k + 1, :] * ext_sc[SUBLANES - 2 + k:SUBLANES - 2 + k + tc, :]
    xb = xc.astype(BF16)
    r = _sigmoid(_dot(xb, wr_ref[...]) + br_ref[...])
    gi = _sigmoid(_dot(xb, wi_ref[...]) + bi_ref[...])
    lam = lam_ref[...]
    logsig = -(jnp.maximum(-lam, 0.0) + jnp.log1p(jnp.exp(-jnp.abs(lam))))
    la = LRU_C * r * logsig
    a = jnp.exp(la)
    v = -jnp.tanh(la) * (a * a + 1.0)
    bc = jnp.where(v > 0.0, v * lax.rsqrt(v), 0.0) * (gi * xc)

    @pl.when(c == 0)
    def _():
        car_sc[...] = h0_ref[...]

    nseg = SUBLANES
    sl = tc // nseg
    sp = sl + SUBLANES
    nlb = LRU_W // LANES
    for j in range(nlb):
        for s in range(nseg):
            rows = slice(sl * s, sl * (s + 1))
            dst = slice((j * nseg + s) * sp, (j * nseg + s) * sp + sl)
            a_sc[dst, :] = a[rows, LANES * j:LANES * (j + 1)]
            b_sc[dst, :] = bc[rows, LANES * j:LANES * (j + 1)]

    def body(k, carry):
        i = (sl - 1 - k) if reverse else k
        hs, ps = carry
        hn, pn = [], []
        for j in range(nlb):
            idx = pl.ds(j * nseg * sp + i, nseg, stride=sp)
            av = a_sc[idx, :]
            h = av * hs[j] + b_sc[idx, :]
            p = av * ps[j]
            p_sc[idx, :] = p
            h_sc[idx, :] = h
            hn.append(h)
            pn.append(p)
        return tuple(hn), tuple(pn)

    zero = jnp.zeros((nseg, LANES), F32)
    one = jnp.ones((nseg, LANES), F32)
    hend, pend = lax.fori_loop(0, sl, body, ((zero,) * nlb, (one,) * nlb), unroll=4)

    order = range(nseg - 1, -1, -1) if reverse else range(nseg)
    for j in range(nlb):
        lanes = slice(LANES * j, LANES * (j + 1))
        cin = car_sc[0:1, lanes]
        for s in order:
            rows = slice(sl * s, sl * (s + 1))
            src = slice((j * nseg + s) * sp, (j * nseg + s) * sp + sl)
            h = h_sc[src, :] + p_sc[src, :] * cin
            if reverse:
                y_ref[rows, lanes] = (hf_ref[rows, lanes] + h).astype(BF16)
            else:
                y_ref[rows, lanes] = h
            cin = hend[j][s:s + 1, :] + pend[j][s:s + 1, :] * cin
        car_sc[0:1, lanes] = cin
        st_ref[:, lanes] = jnp.broadcast_to(cin, (SUBLANES, LANES))


def _lru_dir(u, hf, cw, cb, wr, wi, br, bi, lam, h0, *, reverse, row_off, nseq, seqlen, tc):
    nchunks = seqlen // tc
    hb = M_TOK // SUBLANES

    def chunk_of(c):
        return (nchunks - 1 - c) if reverse else c

    def cur(b, c):
        return ((row_off + b * seqlen) // tc + chunk_of(c), 0)

    def prv(b, c):
        return (jnp.maximum((row_off + b * seqlen + chunk_of(c) * tc) // SUBLANES - 1, 0), 0)

    def nxt(b, c):
        return (jnp.minimum((row_off + b * seqlen + (chunk_of(c) + 1) * tc) // SUBLANES, hb - 1), 0)

    def out_cur(b, c):
        return ((b * seqlen) // tc + chunk_of(c), 0)

    full = lambda shape: pl.BlockSpec(shape, lambda b, c: (0,) * len(shape))
    in_specs = [pl.BlockSpec((SUBLANES, LRU_W), prv), pl.BlockSpec((tc, LRU_W), cur),
                pl.BlockSpec((SUBLANES, LRU_W), nxt)]
    args = [u, u, u]
    if reverse:
        in_specs.append(pl.BlockSpec((tc, LRU_W), out_cur))
        args.append(hf)
    in_specs += [full((SUBLANES, LRU_W)), full((1, LRU_W)), full((LRU_W, LRU_W)), full((LRU_W, LRU_W)),
                 full((1, LRU_W)), full((1, LRU_W)), full((1, LRU_W)),
                 pl.BlockSpec((None, SUBLANES, LRU_W), lambda b, c: (b, 0, 0))]
    args += [cw, cb, wr, wi, br, bi, lam, h0]
    return pl.pallas_call(
        functools.partial(_lru_kernel, reverse=reverse, tc=tc, nchunks=nchunks),
        grid=(nseq, nchunks),
        in_specs=in_specs,
        out_specs=[pl.BlockSpec((tc, LRU_W), out_cur),
                   pl.BlockSpec((None, SUBLANES, LRU_W), lambda b, c: (b, 0, 0))],
        out_shape=[jax.ShapeDtypeStruct((nseq * seqlen, LRU_W), BF16 if reverse else F32),
                   jax.ShapeDtypeStruct((nseq, SUBLANES, LRU_W), F32)],
        scratch_shapes=[pltpu.VMEM((tc + 2 * SUBLANES, LRU_W), F32)]
        + [pltpu.VMEM(((LRU_W // LANES) * (tc + SUBLANES * SUBLANES), LANES), F32)] * 4
        + [pltpu.VMEM((SUBLANES, LRU_W), F32)],
        compiler_params=_cparams(("arbitrary", "arbitrary")),
        name="lru_bwd" if reverse else "lru_fwd",
    )(*args)


def _lru_mixer(u, p, h0, *, row_off, nseq, seqlen, tc):
    kw = dict(row_off=row_off, nseq=nseq, seqlen=seqlen, tc=tc)
    hf, stf = _lru_dir(u, None, p["cw"], p["cb"], p["wr"][0], p["wi"][0], p["br"][0], p["bi"][0], p["lam"][0],
                       h0[0], reverse=False, **kw)
    y, stb = _lru_dir(u, hf, p["cw"], p["cb"], p["wr"][1], p["wi"][1], p["br"][1], p["bi"][1], p["lam"][1],
                      h0[1], reverse=True, **kw)
    return y, stf[:, 0, :], stb[:, 0, :]


def _shortconv_kernel(up_ref, uc_ref, un_ref, cw_ref, cb_ref, v_ref, x1_ref, x2_ref, ext_sc, *, tc, nchunks):
    c = pl.program_id(1)
    prev = jnp.where(c == 0, 0.0, up_ref[...])
    nxt = jnp.where(c == nchunks - 1, 0.0, un_ref[...])
    ext_sc[0:SUBLANES, :] = prev
    ext_sc[SUBLANES:SUBLANES + tc, :] = uc_ref[...]
    ext_sc[SUBLANES + tc:2 * SUBLANES + tc, :] = nxt
    for part, o_ref in enumerate((v_ref, x1_ref, x2_ref)):
        cs = slice(HY_W * part, HY_W * (part + 1))
        acc = cb_ref[:, cs]
        for k in range(3):
            acc = acc + cw_ref[k:k + 1, cs] * ext_sc[SUBLANES - 1 + k:SUBLANES - 1 + k + tc, cs]
        o_ref[...] = acc


def _shortconv(u, cw, cb, *, row_off, nseq, seqlen, tc):
    nchunks = seqlen // tc
    w = 3 * HY_W
    hb = M_TOK // SUBLANES
    cur = lambda b, c: ((row_off + b * seqlen) // tc + c, 0)
    prv = lambda b, c: (jnp.maximum((row_off + b * seqlen + c * tc) // SUBLANES - 1, 0), 0)
    nxt = lambda b, c: (jnp.minimum((row_off + b * seqlen + (c + 1) * tc) // SUBLANES, hb - 1), 0)
    out = lambda b, c: ((b * seqlen) // tc + c, 0)
    full = lambda shape: pl.BlockSpec(shape, lambda b, c: (0,) * len(shape))
    rows = nseq * seqlen
    return pl.pallas_call(
        functools.partial(_shortconv_kernel, tc=tc, nchunks=nchunks),
        grid=(nseq, nchunks),
        in_specs=[pl.BlockSpec((SUBLANES, w), prv), pl.BlockSpec((tc, w), cur), pl.BlockSpec((SUBLANES, w), nxt),
                  full((SUBLANES, w)), full((1, w))],
        out_specs=[pl.BlockSpec((tc, HY_W), out)] * 3,
        out_shape=[jax.ShapeDtypeStruct((rows, HY_W), F32)] * 3,
        scratch_shapes=[pltpu.VMEM((tc + 2 * SUBLANES, w), F32)],
        compiler_params=_cparams(("arbitrary", "arbitrary")),
        name="hy_shortconv",
    )(u, u, u, cw, cb)


def _hyfilt_kernel(z_ref, t_ref, w1_ref, b1_ref, f1_ref, w2_ref, b2_ref, f2_ref, w3_ref, ad_ref, h_ref, s_ref):
    i = pl.program_id(0)
    z = z_ref[...].astype(BF16)
    h = jnp.sin(f1_ref[...] * (_dot(z, w1_ref[...]) + b1_ref[...]))
    h = jnp.sin(f2_ref[...] * (_dot(h.astype(BF16), w2_ref[...]) + b2_ref[...]))
    h = _dot(h.astype(BF16), w3_ref[...])
    t = t_ref[...]
    ncol = h.shape[1] // LANES
    win = jnp.concatenate([jnp.exp(-t * ad_ref[:, LANES * j:LANES * (j + 1)]) for j in range(ncol)], axis=1)
    h = h * win
    h_ref[...] = h

    @pl.when(i == 0)
    def _():
        s_ref[...] = jnp.zeros_like(s_ref)

    s_ref[0:1, :] = s_ref[0:1, :] + jnp.sum(jnp.abs(h), axis=0, keepdims=True)


def _hyfilt(feats, tcol, w1, b1, f1, w2, b2, f2, w3, absdelta):
    L = feats.shape[0]
    tl = min(L, 512)
    wcols = HY_ORDER * 2 * HY_W
    full = lambda shape: pl.BlockSpec(shape, lambda i: (0,) * len(shape))
    return pl.pallas_call(
        _hyfilt_kernel,
        grid=(L // tl,),
        in_specs=[pl.BlockSpec((tl, LANES), lambda i: (i, 0)), pl.BlockSpec((tl, LANES), lambda i: (i, 0)),
                  full((LANES, HY_HID)), full((1, HY_HID)), full((1, HY_HID)),
                  full((HY_HID, HY_HID)), full((1, HY_HID)), full((1, HY_HID)),
                  full((HY_HID, wcols)), full((1, wcols))],
        out_specs=[pl.BlockSpec((tl, wcols), lambda i: (i, 0)), full((SUBLANES, wcols))],
        out_shape=[jax.ShapeDtypeStruct((L, wcols), F32), jax.ShapeDtypeStruct((SUBLANES, wcols), F32)],
        compiler_params=_cparams(("arbitrary",)),
        name="hy_filter",
    )(feats, tcol, w1, b1, f1, w2, b2, f2, w3, absdelta)


def _combine_spectrum(zr, zi, s_ref, hr_out, hi_out):
    for o in range(HY_ORDER):
        f = slice(2 * HY_W * o, 2 * HY_W * o + HY_W)
        b = slice(2 * HY_W * o + HY_W, 2 * HY_W * (o + 1))
        den = s_ref[0:1, f] + s_ref[0:1, b] + EPS
        hr_out(o, (zr[:, f] + zr[:, b]) / den)
        hi_out(o, (zi[:, f] - zi[:, b]) / den)


def _ctx_spec_kernel(f_ref, h_ref, s_ref, o_ref):
    n = f_ref.shape[0] // 2
    z = _dot(f_ref[...], h_ref[...].astype(BF16))
    zr, zi = z[:n], z[n:]

    def put_r(o, val):
        o_ref[0, :, HY_W * o:HY_W * (o + 1)] = val

    def put_i(o, val):
        o_ref[1, :, HY_W * o:HY_W * (o + 1)] = val

    _combine_spectrum(zr, zi, s_ref, put_r, put_i)


def _ctx_spectrum(fmat, hdec, s):
    n = fmat.shape[0] // 2
    return pl.pallas_call(
        _ctx_spec_kernel,
        out_shape=jax.ShapeDtypeStruct((2, n, HY_ORDER * HY_W), F32),
        compiler_params=pltpu.CompilerParams(vmem_limit_bytes=VMEM_LIMIT),
        name="hy_ctx_spectrum",
    )(fmat, hdec, s)


def _ctx_conv_kernel(z_ref, x_ref, f_ref, fi_ref, h_ref, bias_ref, o_ref, *, nb, seqlen):
    n = f_ref.shape[0] // 2
    hr = h_ref[0]
    hi = h_ref[1]
    for b in range(nb):
        rs = slice(seqlen * b, seqlen * (b + 1))
        zt = z_ref[rs, :]
        zf = _dot(f_ref[...], zt.astype(BF16))
        zr, zi = zf[:n], zf[n:]
        y = jnp.concatenate([zr * hr - zi * hi, zr * hi + zi * hr], axis=0).astype(BF16)
        conv = _dot(fi_ref[...], y)
        o_ref[rs, :] = (x_ref[rs, :] * (conv + zt * bias_ref[...])).astype(o_ref.dtype)


def _ctx_conv(z, xg, fmat, finv, hspec, bias, order, out_dtype):
    nb = 4
    n = fmat.shape[0] // 2
    rows = nb * SEQ
    return pl.pallas_call(
        functools.partial(_ctx_conv_kernel, nb=nb, seqlen=SEQ),
        grid=(BATCH // nb,),
        in_specs=[pl.BlockSpec((rows, HY_W), lambda i: (i, 0)), pl.BlockSpec((rows, HY_W), lambda i: (i, 0)),
                  pl.BlockSpec(fmat.shape, lambda i: (0, 0)), pl.BlockSpec(finv.shape, lambda i: (0, 0)),
                  pl.BlockSpec((2, n, HY_W), lambda i: (0, 0, order)),
                  pl.BlockSpec((1, HY_W), lambda i: (0, 0))],
        out_specs=pl.BlockSpec((rows, HY_W), lambda i: (i, 0)),
        out_shape=jax.ShapeDtypeStruct((M_CTX, HY_W), out_dtype),
        compiler_params=_cparams(("arbitrary",)),
        name="hy_ctx_conv",
    )(z, xg, fmat, finv, hspec, bias)


K1U = FFT_N1 // 2 + 1
SLABS = 72
PITCH = FFT_N2 + SUBLANES
NROW1 = FFT_N1 // 2
LAT_UNROLL_R = 16
LAT_UNROLL_K = 11


def _pitch_copy_in(src_ref, col, dst_sc):
    for n1 in range(NROW1):
        dst_sc[PITCH * n1:PITCH * n1 + FFT_N2, :] = src_ref[FFT_N2 * n1:FFT_N2 * (n1 + 1), col]


def _dft_stage_a(zp_scs, wa_ref, a_scs):
    def body(r, c):
        for zp_sc, a_sc in zip(zp_scs, a_scs):
            x = zp_sc[pl.ds(r, NROW1, stride=PITCH), :].astype(BF16)
            a_sc[pl.ds(r, SLABS, stride=PITCH), :] = _dot(wa_ref[...], x)
        return c

    lax.fori_loop(0, FFT_N2, body, 0, unroll=LAT_UNROLL_R)


def _load_k1(a_sc, k1):
    base = pl.multiple_of(k1 * (2 * PITCH), SUBLANES)
    a = jnp.concatenate([a_sc[pl.ds(base, FFT_N2), :], a_sc[pl.ds(base + PITCH, FFT_N2), :]], axis=0)
    return base, a.astype(BF16)


def _lat_spec_kernel(h_ref, s_ref, wa_ref, g_ref, o_ref, hf_sc, hb_sc, af_sc, ab_sc):
    _pitch_copy_in(h_ref, slice(0, LANES), hf_sc)
    _pitch_copy_in(h_ref, slice(LANES, 2 * LANES), hb_sc)
    _dft_stage_a((hf_sc, hb_sc), wa_ref, (af_sc, ab_sc))
    den = s_ref[0:1, 0:LANES] + s_ref[0:1, LANES:2 * LANES] + EPS

    def kbody(k1, c):
        _, af = _load_k1(af_sc, k1)
        _, ab = _load_k1(ab_sc, k1)
        zf = _dot(g_ref[k1], af)
        zb = _dot(g_ref[k1], ab)
        o_ref[0, k1] = (zf[:FFT_N2] + zb[:FFT_N2]) / den
        o_ref[1, k1] = (zf[FFT_N2:] - zb[FFT_N2:]) / den
        return c

    lax.fori_loop(0, K1U, kbody, 0, unroll=LAT_UNROLL_K)


def _lat_spectrum(hdec, s, wa, gtab):
    nblk = HY_ORDER * HY_W // LANES
    slab = NROW1 * PITCH
    return pl.pallas_call(
        _lat_spec_kernel,
        grid=(nblk,),
        in_specs=[pl.BlockSpec((DEC_SEQ, 2 * LANES), lambda i: (0, i)),
                  pl.BlockSpec((SUBLANES, 2 * LANES), lambda i: (0, i)),
                  pl.BlockSpec(wa.shape, lambda i: (0, 0)),
                  pl.BlockSpec(gtab.shape, lambda i: (0, 0, 0))],
        out_specs=pl.BlockSpec((2, K1U, FFT_N2, LANES), lambda i: (0, 0, 0, i)),
        out_shape=jax.ShapeDtypeStruct((2, K1U, FFT_N2, HY_ORDER * HY_W), F32),
        scratch_shapes=[pltpu.VMEM((slab, LANES), F32), pltpu.VMEM((slab, LANES), F32),
                        pltpu.VMEM((SLABS * PITCH, LANES), F32), pltpu.VMEM((SLABS * PITCH, LANES), F32)],
        compiler_params=_cparams(("arbitrary",)),
        name="hy_lat_spectrum",
    )(hdec, s, wa, gtab)


def _lat_conv_kernel(z_ref, x_ref, wa_ref, wai_ref, g_ref, gi_ref, h_ref, bias_ref, o_ref,
                     zp_sc, xp_sc, op_sc, a_sc):
    full = slice(None)
    _pitch_copy_in(z_ref, full, zp_sc)
    _pitch_copy_in(x_ref, full, xp_sc)
    _dft_stage_a((zp_sc,), wa_ref, (a_sc,))

    def kbody(k1, c):
        base, a = _load_k1(a_sc, k1)
        z = _dot(g_ref[k1], a)
        zr, zi = z[:FFT_N2], z[FFT_N2:]
        hr = h_ref[0, k1]
        hi = h_ref[1, k1]
        y = jnp.concatenate([zr * hr - zi * hi, zr * hi + zi * hr], axis=0).astype(BF16)
        bp = _dot(gi_ref[k1], y)
        a_sc[pl.ds(base, FFT_N2), :] = bp[:FFT_N2]
        a_sc[pl.ds(base + PITCH, FFT_N2), :] = bp[FFT_N2:]
        return c

    lax.fori_loop(0, K1U, kbody, 0, unroll=LAT_UNROLL_K)
    bias = bias_ref[...]

    def rbody(r, c):
        yb = a_sc[pl.ds(r, SLABS, stride=PITCH), :].astype(BF16)
        conv = _dot(wai_ref[...], yb)
        zz = zp_sc[pl.ds(r, NROW1, stride=PITCH), :]
        xx = xp_sc[pl.ds(r, NROW1, stride=PITCH), :]
        op_sc[pl.ds(r, NROW1, stride=PITCH), :] = xx * (conv + zz * bias)
        return c

    lax.fori_loop(0, FFT_N2, rbody, 0, unroll=LAT_UNROLL_R)
    for n1 in range(NROW1):
        o_ref[FFT_N2 * n1:FFT_N2 * (n1 + 1), :] = op_sc[PITCH * n1:PITCH * n1 + FFT_N2, :]


def _lat_conv(z, xg, wa, wainv, gtab, gitab, hspec, bias, order):
    ncb = HY_W // LANES
    slab = NROW1 * PITCH
    blk = pl.BlockSpec((DEC_SEQ, LANES), lambda cb, b: (b, cb))
    const = lambda a: pl.BlockSpec(a.shape, lambda cb, b: (0,) * a.ndim)
    return pl.pallas_call(
        _lat_conv_kernel,
        grid=(ncb, DEC_BATCH),
        in_specs=[blk, blk, const(wa), const(wainv), const(gtab), const(gitab),
                  pl.BlockSpec((2, K1U, FFT_N2, LANES), lambda cb, b: (0, 0, 0, order * ncb + cb)),
                  pl.BlockSpec((1, LANES), lambda cb, b: (0, cb))],
        out_specs=blk,
        out_shape=jax.ShapeDtypeStruct((M_LAT, HY_W), F32),
        scratch_shapes=[pltpu.VMEM((slab, LANES), F32), pltpu.VMEM((slab, LANES), F32),
                        pltpu.VMEM((slab, LANES), F32), pltpu.VMEM((SLABS * PITCH, LANES), F32)],
        compiler_params=_cparams(("arbitrary", "arbitrary")),
        name="hy_lat_conv",
    )(z, xg, wa, wainv, gtab, gitab, hspec, bias)


def _angle(m, n):
    return (m % n).astype(F32) * (2.0 * math.pi / n)


def _ctx_tables():
    n = 2 * SEQ
    nf = SEQ + SUBLANES
    k = jnp.arange(nf, dtype=jnp.int32)[:, None]
    t = jnp.arange(SEQ, dtype=jnp.int32)[None, :]
    th = _angle(k * t, n)
    live = (k <= SEQ).astype(F32)
    wgt = jnp.where((k == 0) | (k == SEQ), 1.0, 2.0) * live / n
    fmat = jnp.concatenate([live * jnp.cos(th), -live * jnp.sin(th)], axis=0)
    finv = jnp.concatenate([(wgt * jnp.cos(th)).T, (-wgt * jnp.sin(th)).T], axis=1)
    return fmat.astype(BF16), finv.astype(BF16)


def _lat_tables():
    n1, n2 = FFT_N1, FFT_N2
    n = n1 * n2
    k1 = jnp.arange(K1U, dtype=jnp.int32)
    th1 = _angle(k1[:, None] * jnp.arange(NROW1, dtype=jnp.int32)[None, :], n1)
    wa = jnp.stack([jnp.cos(th1), -jnp.sin(th1)], axis=1).reshape(2 * K1U, NROW1)
    wa = jnp.pad(wa, ((0, SLABS - 2 * K1U), (0, 0)))
    wgt = jnp.where((k1 == 0) | (k1 == n1 // 2), 1.0, 2.0)[:, None] / n
    wainv = jnp.stack([wgt * jnp.cos(th1), -wgt * jnp.sin(th1)], axis=1).reshape(2 * K1U, NROW1).T
    wainv = jnp.pad(wainv, ((0, 0), (0, SLABS - 2 * K1U)))
    k2 = jnp.arange(n2, dtype=jnp.int32)
    nn2 = jnp.arange(n2, dtype=jnp.int32)
    kfull = k1[:, None, None] + n1 * k2[None, :, None]
    th = _angle(kfull * nn2[None, None, :], n)
    gr, gi = jnp.cos(th), -jnp.sin(th)
    g = jnp.concatenate([jnp.concatenate([gr, -gi], axis=2), jnp.concatenate([gi, gr], axis=2)],
                        axis=1).astype(BF16)
    ginv = jnp.swapaxes(g, 1, 2)
    return wa.astype(BF16), wainv.astype(BF16), g, ginv


def _filter_features(L):
    t = jnp.linspace(0.0, 1.0, L, dtype=F32)[:, None]
    bands = (HY_EMB - 1) // 2
    w = (2.0 * math.pi / L) * jnp.arange(L, dtype=F32)[:, None]
    f = jnp.linspace(1e-4, bands - 1, bands, dtype=F32)[None, :]
    z = jnp.concatenate([t, jnp.cos(f * w), -jnp.sin(f * w)], axis=-1)
    z = jnp.pad(z, ((0, 0), (0, LANES - HY_EMB)))
    return z, jnp.broadcast_to(t, (L, LANES))


def _hyena_filter(p, L, blocked):
    feats, tcol = _filter_features(L)
    deltas = jnp.linspace(math.log(1e-2) / HY_FAST_PCT, math.log(1e-2) / HY_SLOW_PCT, HY_W, dtype=F32)
    absdelta = jnp.tile(jnp.abs(deltas), HY_ORDER * 2)[None, :]
    w3 = p["w3"]
    if blocked:
        reorder = lambda a: a.reshape(a.shape[0], HY_ORDER, 2, HY_W // LANES, LANES).transpose(
            0, 1, 3, 2, 4).reshape(a.shape[0], HY_ORDER * 2 * HY_W)
        w3, absdelta = reorder(w3), reorder(absdelta)
    return _hyfilt(feats, tcol, p["w1"], p["b1"], p["f1"], p["w2"], p["b2"], p["f2"], w3, absdelta)


def _hyena_ctx(u_hy, p, tabs):
    fmat, finv = tabs
    v, x1, x2 = _shortconv(u_hy, p["cw"], p["cb"], row_off=0, nseq=BATCH, seqlen=SEQ, tc=SEQ)
    hdec, s = _hyena_filter(p, SEQ, False)
    hspec = _ctx_spectrum(fmat, hdec, s)
    z = _ctx_conv(v, x1, fmat, finv, hspec, p["bias"][0:1], 0, F32)
    return _ctx_conv(z, x2, fmat, finv, hspec, p["bias"][1:2], 1, F32)


def _hyena_lat(u_hy, p, tabs):
    wa, wainv, gtab, gitab = tabs
    v, x1, x2 = _shortconv(u_hy, p["cw"], p["cb"], row_off=M_CTX, nseq=DEC_BATCH, seqlen=DEC_SEQ, tc=512)
    hdec, s = _hyena_filter(p, DEC_SEQ, True)
    hspec = _lat_spectrum(hdec, s, wa, gtab)
    z = _lat_conv(v, x1, wa, wainv, gtab, gitab, hspec, p["bias"][0:1], 0)
    return _lat_conv(z, x2, wa, wainv, gtab, gitab, hspec, p["bias"][1:2], 1)


def _stage3_kernel(*refs, nx, nctx):
    (mod_ref, g1_ref, wg_ref, ylc_ref, yll_ref, ymc_ref, yml_ref, yhc_ref, yhl_ref,
     wl_ref, wm_ref, wh_ref, wo_ref, g2_ref, xo_ref, xm2_ref) = refs[nx:]
    is_ctx = pl.program_id(0) < nctx
    x = _read_tokens(refs[:nx], nctx)
    xm = _rms(x, g1_ref[...]) * (1.0 + mod_ref[1:2, :]) + mod_ref[0:1, :]
    xb = xm.astype(BF16)
    merged = None
    branches = ((ylc_ref, yll_ref, wl_ref), (ymc_ref, yml_ref, wm_ref), (yhc_ref, yhl_ref, wh_ref))
    for bidx, (yc_ref, yl_ref, w_ref) in enumerate(branches):
        gate = _sigmoid(_dot(xb, wg_ref[:, D_MODEL * bidx:D_MODEL * (bidx + 1)]))
        y = jnp.where(is_ctx, yc_ref[...], yl_ref[...]).astype(BF16)
        term = gate * _dot(y, w_ref[...])
        merged = term if merged is None else merged + term
    xo = x + mod_ref[2:3, :] * _dot(merged.astype(BF16), wo_ref[...])
    xo_ref[...] = xo
    xm2 = _rms(xo, g2_ref[...]) * (1.0 + mod_ref[4:5, :]) + mod_ref[3:4, :]
    xm2_ref[...] = xm2.astype(BF16)


def _stage3(x, modl, g1, wg, ylru, ymla, yhy, wl, wm, wh, wo, g2):
    tm = TM3
    nctx = M_CTX // tm
    full = lambda shape: pl.BlockSpec(shape, lambda i: (0,) * len(shape))
    row = lambda cols: pl.BlockSpec((tm, cols), lambda i: (i, 0))
    ctx = lambda cols: pl.BlockSpec((tm, cols), lambda i: (jnp.minimum(i, nctx - 1), 0))
    lat = lambda cols: pl.BlockSpec((tm, cols), lambda i: (jnp.maximum(i - nctx, 0), 0))
    return pl.pallas_call(
        functools.partial(_stage3_kernel, nx=len(x), nctx=nctx),
        grid=(M_TOK // tm,),
        in_specs=_token_specs(x, tm) + [
                  pl.BlockSpec((None, SUBLANES, D_MODEL), lambda i: (_mod_row(i, tm), 0, 0)),
                  full((1, D_MODEL)), full((D_MODEL, 3 * D_MODEL)),
                  ctx(LRU_W), lat(LRU_W), ctx(N_HEADS * D_V), lat(N_HEADS * D_V), ctx(HY_W), lat(HY_W),
                  full((LRU_W, D_MODEL)), full((N_HEADS * D_V, D_MODEL)), full((HY_W, D_MODEL)),
                  full((D_MODEL, D_MODEL)), full((1, D_MODEL))],
        out_specs=[row(D_MODEL), row(D_MODEL)],
        out_shape=[jax.ShapeDtypeStruct((M_TOK, D_MODEL), F32), jax.ShapeDtypeStruct((M_TOK, D_MODEL), BF16)],
        compiler_params=_cparams(("arbitrary",)),
        name="stage3",
    )(*x, modl, g1, wg, *ylru, *ymla, *yhy, wl, wm, wh, wo, g2)


def _ffn_kernel(xm_ref, x_ref, mod_ref, wg_ref, wu_ref, wd_ref, o_ref, acc_sc, *, nchunks):
    j = pl.program_id(1)
    xb = xm_ref[...]
    g = _dot(xb, wg_ref[...])
    u = _dot(xb, wu_ref[...])
    hid = (g * _sigmoid(g) * u).astype(BF16)
    part = _dot(hid, wd_ref[...])

    @pl.when(j == 0)
    def _():
        acc_sc[...] = part

    @pl.when(j > 0)
    def _():
        acc_sc[...] = acc_sc[...] + part

    @pl.when(j == nchunks - 1)
    def _():
        o_ref[...] = x_ref[...] + mod_ref[5:6, :] * acc_sc[...]


def _ffn_dense(xm2, x, modl, wg, wu, wd):
    tm = TM_FFN
    nchunks = 2
    cw = D_FF // nchunks
    return pl.pallas_call(
        functools.partial(_ffn_kernel, nchunks=nchunks),
        grid=(M_TOK // tm, nchunks),
        in_specs=[pl.BlockSpec((tm, D_MODEL), lambda i, j: (i, 0)),
                  pl.BlockSpec((tm, D_MODEL), lambda i, j: (i, 0)),
                  pl.BlockSpec((None, SUBLANES, D_MODEL), lambda i, j: (_mod_row(i, tm), 0, 0)),
                  pl.BlockSpec((D_MODEL, cw), lambda i, j: (0, j)),
                  pl.BlockSpec((D_MODEL, cw), lambda i, j: (0, j)),
                  pl.BlockSpec((cw, D_MODEL), lambda i, j: (j, 0))],
        out_specs=pl.BlockSpec((tm, D_MODEL), lambda i, j: (i, 0)),
        out_shape=jax.ShapeDtypeStruct((M_TOK, D_MODEL), F32),
        scratch_shapes=[pltpu.VMEM((tm, D_MODEL), F32)],
        compiler_params=_cparams(("arbitrary", "arbitrary")),
        name="ffn_dense",
    )(xm2, x, modl, wg, wu, wd)


MOE_TILE = 256
MOE_ROWS = 2 * M_TOK + N_EXPERTS * MOE_TILE
MOE_TILES = MOE_ROWS // MOE_TILE
MOE_PAD_ROWS = MOE_ROWS - 2 * M_TOK
MOE_SEG = D_MODEL // LANES


def _to_row_tiles(val, ref, base):
    n = val.shape[0]
    for j in range(MOE_SEG):
        ref[pl.ds(base + j, n, stride=MOE_SEG), :] = val[:, LANES * j:LANES * (j + 1)]


def _from_row_tiles(ref, base, n):
    return jnp.concatenate([ref[pl.ds(base + j, n, stride=MOE_SEG), :] for j in range(MOE_SEG)], axis=1)


def _router_kernel(xm_ref, wr_ref, tri_ref, sel_ref, xp_ref, cnt_ref, base_sc):
    i = pl.program_id(0)
    xb = xm_ref[...]
    tm = xb.shape[0]
    lane = lax.broadcasted_iota(jnp.int32, (tm, LANES), 1)

    @pl.when(i == 0)
    def _():
        base_sc[...] = jnp.zeros_like(base_sc)

    logits = jnp.where(lane < N_EXPERTS, _dot(xb, wr_ref[...]), -1e30)
    mx = jnp.max(logits, axis=-1, keepdims=True)
    ex = jnp.exp(logits - mx)
    probs = ex / jnp.sum(ex, axis=-1, keepdims=True)
    p1 = jnp.max(probs, axis=-1, keepdims=True)
    i1 = jnp.min(jnp.where(probs == p1, lane, LANES), axis=-1, keepdims=True)
    rest = jnp.where(lane == i1, -1.0, probs)
    p2 = jnp.max(rest, axis=-1, keepdims=True)
    i2 = jnp.min(jnp.where(rest == p2, lane, LANES), axis=-1, keepdims=True)
    tot = p1 + p2
    oh1 = lane == i1
    oh2 = lane == i2
    oh = jnp.where(oh1 | oh2, 1.0, 0.0)
    before = base_sc[0:1, :] + _dot(tri_ref[...], oh.astype(BF16))
    r1 = jnp.sum(jnp.where(oh1, before, 0.0), axis=-1, keepdims=True)
    r2 = jnp.sum(jnp.where(oh2, before, 0.0), axis=-1, keepdims=True)
    base_sc[0:1, :] = base_sc[0:1, :] + jnp.sum(oh, axis=0, keepdims=True)
    cnt_ref[...] = base_sc[...]
    sel = jnp.where(lane == 0, p1 / tot, 0.0) + jnp.where(lane == 1, p2 / tot, 0.0)
    sel = sel + jnp.where(lane == 2, i1.astype(F32), 0.0) + jnp.where(lane == 3, i2.astype(F32), 0.0)
    sel_ref[...] = sel + jnp.where(lane == 4, r1, 0.0) + jnp.where(lane == 5, r2, 0.0)
    _to_row_tiles(xb.astype(F32), xp_ref, 0)


def _moe_router(xm2, wr):
    tm = TM_FFN
    tri = jnp.tril(jnp.ones((tm, tm), F32), -1).astype(BF16)
    return pl.pallas_call(
        _router_kernel,
        grid=(M_TOK // tm,),
        in_specs=[pl.BlockSpec((tm, D_MODEL), lambda i: (i, 0)),
                  pl.BlockSpec((D_MODEL, LANES), lambda i: (0, 0)),
                  pl.BlockSpec((tm, tm), lambda i: (0, 0))],
        out_specs=[pl.BlockSpec((tm, LANES), lambda i: (i, 0)),
                   pl.BlockSpec((tm * MOE_SEG, LANES), lambda i: (i, 0)),
                   pl.BlockSpec((SUBLANES, LANES), lambda i: (0, 0))],
        out_shape=[jax.ShapeDtypeStruct((M_TOK, LANES), F32),
                   jax.ShapeDtypeStruct((M_TOK * MOE_SEG, LANES), F32),
                   jax.ShapeDtypeStruct((SUBLANES, LANES), F32)],
        scratch_shapes=[pltpu.VMEM((SUBLANES, LANES), F32)],
        compiler_params=_cparams(("arbitrary",)),
        name="moe_router",
    )(xm2, wr, tri)


def _row_copy(src, srow8, dst, drow8, sem):
    aligned = lambda r: r if isinstance(r, int) else pl.multiple_of(r, MOE_SEG)
    return pltpu.make_async_copy(src.at[pl.ds(aligned(srow8), MOE_SEG), :],
                                 dst.at[pl.ds(aligned(drow8), MOE_SEG), :], sem)


def _dispatch_kernel(pos_ref, pad_ref, xp_ref, xs_ref, ring_sc, zero_sc, sem, *, nsteps):
    i = pl.program_id(0)
    slot = i % 2
    nrow = MOE_TILE
    slot_rows = nrow * MOE_SEG

    def wait_slot(s):
        for _ in range(2):
            pltpu.make_async_copy(ring_sc.at[s], xs_ref.at[pl.ds(0, slot_rows), :], sem.at[s]).wait()

    @pl.when(i >= 2)
    def _():
        wait_slot(slot)

    ring_sc[slot] = xp_ref[...]

    def body(t, c):
        _row_copy(ring_sc.at[slot], t * MOE_SEG, xs_ref, pos_ref[0, 2 * t], sem.at[slot]).start(priority=0)
        _row_copy(ring_sc.at[slot], t * MOE_SEG, xs_ref, pos_ref[0, 2 * t + 1], sem.at[slot]).start(priority=1)
        return c

    lax.fori_loop(0, nrow, body, 0, unroll=4)

    @pl.when(i == nsteps - 1)
    def _():
        zero_sc[...] = jnp.zeros_like(zero_sc)

        def zbody(t, c):
            _row_copy(zero_sc, 0, xs_ref, pad_ref[t], sem.at[2]).start()
            return c

        lax.fori_loop(0, MOE_PAD_ROWS, zbody, 0, unroll=4)
        wait_slot(1 - slot)
        wait_slot(slot)

        for _ in range(MOE_PAD_ROWS // nrow):
            pltpu.make_async_copy(ring_sc.at[0], xs_ref.at[pl.ds(0, slot_rows), :], sem.at[2]).wait()


def _moe_dispatch(xp, pos, padrows):
    nsteps = M_TOK // MOE_TILE
    return pl.pallas_call(
        functools.partial(_dispatch_kernel, nsteps=nsteps),
        grid=(nsteps,),
        in_specs=[pl.BlockSpec((None, 1, 2 * MOE_TILE), lambda i: (i, 0, 0), memory_space=pltpu.SMEM),
                  pl.BlockSpec(memory_space=pltpu.SMEM),
                  pl.BlockSpec((MOE_TILE * MOE_SEG, LANES), lambda i: (i, 0))],
        out_specs=pl.BlockSpec(memory_space=pl.ANY),
        out_shape=jax.ShapeDtypeStruct((MOE_ROWS * MOE_SEG, LANES), F32),
        scratch_shapes=[pltpu.VMEM((2, MOE_TILE * MOE_SEG, LANES), F32), pltpu.VMEM((MOE_SEG, LANES), F32),
                        pltpu.SemaphoreType.DMA((3,))],
        compiler_params=_cparams(("arbitrary",)),
        name="moe_dispatch",
    )(pos, padrows, xp)


def _experts_kernel(te_ref, nu_ref, xs_ref, wg_ref, wu_ref, wd_ref, ys_ref):
    i = pl.program_id(0)

    @pl.when(i < nu_ref[0])
    def _():
        xb = _from_row_tiles(xs_ref, 0, MOE_TILE).astype(BF16)
        g = _dot(xb, wg_ref[...])
        u = _dot(xb, wu_ref[...])
        hid = (g * _sigmoid(g) * u).astype(BF16)
        _to_row_tiles(_dot(hid, wd_ref[...]), ys_ref, 0)

    @pl.when(i >= nu_ref[0])
    def _():
        ys_ref[...] = jnp.zeros_like(ys_ref)


def _moe_experts(tile_expert, n_used, xs, wg, wu, wd):
    grid_spec = pltpu.PrefetchScalarGridSpec(
        num_scalar_prefetch=2,
        grid=(MOE_TILES,),
        in_specs=[pl.BlockSpec((MOE_TILE * MOE_SEG, LANES), lambda i, te, nu: (i, 0)),
                  pl.BlockSpec((None, D_MODEL, D_FF_E), lambda i, te, nu: (te[i], 0, 0)),
                  pl.BlockSpec((None, D_MODEL, D_FF_E), lambda i, te, nu: (te[i], 0, 0)),
                  pl.BlockSpec((None, D_FF_E, D_MODEL), lambda i, te, nu: (te[i], 0, 0))],
        out_specs=pl.BlockSpec((MOE_TILE * MOE_SEG, LANES), lambda i, te, nu: (i, 0)),
    )
    return pl.pallas_call(
        _experts_kernel,
        grid_spec=grid_spec,
        out_shape=jax.ShapeDtypeStruct((MOE_ROWS * MOE_SEG, LANES), F32),
        compiler_params=_cparams(("arbitrary",)),
        name="moe_experts",
    )(tile_expert, n_used, xs, wg, wu, wd)


def _combine_kernel(pos_ref, x_ref, mod_ref, sel_ref, ys_ref, oc_ref, ol_ref, buf_sc, sem, *, nsteps, nctx):
    i = pl.program_id(0)
    slot = i % 2
    nrow = MOE_TILE
    part = nrow * MOE_SEG

    def start(s, off):
        def body(t, c):
            for k in range(2):
                _row_copy(ys_ref, pos_ref[0, off + 2 * t + k], buf_sc, (2 * s + k) * part + t * MOE_SEG,
                          sem.at[s]).start(priority=k)
            return c
        lax.fori_loop(0, nrow, body, 0, unroll=4)

    @pl.when(i == 0)
    def _():
        start(0, 0)

    @pl.when(i + 1 < nsteps)
    def _():
        start(1 - slot, 2 * nrow)

    for k in range(2):
        pltpu.make_async_copy(ys_ref.at[pl.ds(0, part), :], buf_sc.at[pl.ds(0, part), :], sem.at[slot]).wait()
    lane = lax.broadcasted_iota(jnp.int32, (nrow, LANES), 1)
    sel = sel_ref[...]
    w1 = jnp.sum(jnp.where(lane == 0, sel, 0.0), axis=-1, keepdims=True)
    w2 = jnp.sum(jnp.where(lane == 1, sel, 0.0), axis=-1, keepdims=True)
    y = (w1 * _from_row_tiles(buf_sc, 2 * slot * part, nrow)
         + w2 * _from_row_tiles(buf_sc, (2 * slot + 1) * part, nrow))
    val = x_ref[...] + mod_ref[5:6, :] * y

    @pl.when(i < nctx)
    def _():
        oc_ref[...] = val

    @pl.when(i >= nctx)
    def _():
        ol_ref[...] = val


def _moe_combine(pos2, x, modl, sel, ys):
    nsteps = M_TOK // MOE_TILE
    tm = MOE_TILE
    nctx = M_CTX // tm
    return pl.pallas_call(
        functools.partial(_combine_kernel, nsteps=nsteps, nctx=nctx),
        grid=(nsteps,),
        in_specs=[pl.BlockSpec((None, 1, 4 * MOE_TILE), lambda i: (i, 0, 0), memory_space=pltpu.SMEM),
                  pl.BlockSpec((tm, D_MODEL), lambda i: (i, 0)),
                  pl.BlockSpec((None, SUBLANES, D_MODEL), lambda i: (_mod_row(i, tm), 0, 0)),
                  pl.BlockSpec((tm, LANES), lambda i: (i, 0)),
                  pl.BlockSpec(memory_space=pl.ANY)],
        out_specs=[pl.BlockSpec((tm, D_MODEL), lambda i: (jnp.minimum(i, nctx - 1), 0)),
                   pl.BlockSpec((tm, D_MODEL), lambda i: (jnp.maximum(i - nctx, 0), 0))],
        out_shape=[jax.ShapeDtypeStruct((M_CTX, D_MODEL), F32), jax.ShapeDtypeStruct((M_LAT, D_MODEL), F32)],
        scratch_shapes=[pltpu.VMEM((2 * 2 * MOE_TILE * MOE_SEG, LANES), F32), pltpu.SemaphoreType.DMA((2,))],
        compiler_params=_cparams(("arbitrary",)),
        name="moe_combine",
    )(pos2, x, modl, sel, ys)


def _ffn_moe(xm2, x, modl, wr, wg, wu, wd):
    sel, xp, cnt = _moe_router(xm2, wr)
    counts = cnt[0, :N_EXPERTS].astype(jnp.int32)
    padded = ((counts + MOE_TILE - 1) // MOE_TILE) * MOE_TILE
    ends = jnp.cumsum(padded)
    offs = ends - padded
    experts = sel[:, 2:4].astype(jnp.int32)
    ranks = sel[:, 4:6].astype(jnp.int32)
    pos = (offs[experts] + ranks) * MOE_SEG
    tile_start = jnp.arange(MOE_TILES, dtype=jnp.int32) * MOE_TILE
    tile_expert = jnp.minimum(jnp.sum(tile_start[:, None] >= ends[None, :], axis=1), N_EXPERTS - 1)
    n_used = (ends[-1] // MOE_TILE).astype(jnp.int32)[None]
    rows = jnp.arange(MOE_ROWS, dtype=jnp.int32)
    row_expert = jnp.repeat(tile_expert, MOE_TILE)
    written = (rows < ends[-1]) & (rows - offs[row_expert] < counts[row_expert])
    padrows = jnp.nonzero(~written, size=MOE_PAD_ROWS)[0].astype(jnp.int32) * MOE_SEG
    pos_tiles = pos.reshape(M_TOK // MOE_TILE, 1, 2 * MOE_TILE)
    xs = _moe_dispatch(xp, pos_tiles, padrows)
    ys = _moe_experts(tile_expert.astype(jnp.int32), n_used, xs, wg, wu, wd)
    nxt = jnp.concatenate([pos_tiles[1:], pos_tiles[-1:]], axis=0)
    pos2 = jnp.concatenate([pos_tiles, nxt], axis=2)
    return tuple(_moe_combine(pos2, x, modl, sel, ys))


def _block_diag(w):
    nb, bs, _ = w.shape
    eye = jnp.eye(nb, dtype=w.dtype)
    return jnp.einsum("njk,nm->njmk", w, eye).reshape(nb * bs, nb * bs)


def _head_pad_cols(w, width):
    r = w.shape[0]
    return jnp.pad(w, ((0, 0), (0, 0), (0, HEAD_PAD - width))).reshape(r, N_HEADS * HEAD_PAD)


def _swap_rope_pairs(a):
    nope, rope = a[..., :D_NOPE], a[..., D_NOPE:]
    sw = rope.reshape(rope.shape[:-1] + (D_ROPE // 2, 2))[..., ::-1].reshape(rope.shape)
    return jnp.concatenate([nope, sw], axis=-1)


def _head_gain(g):
    rows = jnp.stack([g, _swap_rope_pairs(g)], axis=0)
    return jnp.pad(rows, ((0, SUBLANES - 2), (0, HEAD_PAD - D_QK)))


def _rope_tables(tm):
    rows = DEC_SEQ // GRID_W
    row = jnp.repeat(jnp.arange(rows, dtype=F32), GRID_W)
    col = jnp.tile(jnp.arange(GRID_W, dtype=F32), rows)
    half = D_ROPE // 2
    inv_freq = ROPE_BASE ** (-jnp.arange(0, half, 2, dtype=F32) / half)
    ang = jnp.concatenate([row[:, None] * inv_freq, col[:, None] * inv_freq], axis=-1)
    cos, sin = jnp.cos(ang), jnp.sin(ang)
    cos2 = jnp.repeat(cos, 2, axis=1)
    sin2 = jnp.stack([-sin, sin], axis=-1).reshape(DEC_SEQ, D_ROPE)
    cos_t = jnp.pad(cos2, ((0, 0), (D_NOPE, HEAD_PAD - D_QK)), constant_values=1.0)
    sin_t = jnp.pad(sin2, ((0, 0), (D_NOPE, HEAD_PAD - D_QK)))
    cos_t = jnp.concatenate([jnp.ones((tm, HEAD_PAD), F32), cos_t], axis=0)
    sin_t = jnp.concatenate([jnp.zeros((tm, HEAD_PAD), F32), sin_t], axis=0)
    return cos_t, sin_t


def kernel(x_prompt, x_sample, cache_ckv, cache_krope, state_lru, c, c_ctx, norm1, norm2, w_ada, b_ada, w_in, mla_q_norm, mla_kv_norm, mla_w_uq, mla_w_uk, mla_w_uv, mla_q_qknorm, mla_k_qknorm, lru_conv_w, lru_conv_b, lru_w_gate, lru_b_gate, lru_lambda, hy_conv_w, hy_conv_b, hy_w1, hy_b1, hy_freq1, hy_w2, hy_b2, hy_freq2, hy_w3, hy_bias, w_lru_out, w_mla_out, w_hy_out, w_out, ffn_w_gate, ffn_w_up, ffn_w_down, moe_w_router, moe_w_gate, moe_w_up, moe_w_down):
    x = (x_prompt.reshape(M_CTX, D_MODEL), x_sample.reshape(M_LAT, D_MODEL))

    cond = jnp.concatenate([c_ctx[None, :], c, jnp.zeros((SUBLANES - 1 - DEC_BATCH, D_MODEL), F32)], axis=0)
    mod = _adaln(cond, w_ada, b_ada).reshape(DEPTH, SUBLANES, 6, D_MODEL)
    mod = jnp.pad(mod, ((0, 0), (0, 0), (0, SUBLANES - 6), (0, 0)))

    cos_t, sin_t = _rope_tables(TM1)
    ctx_tabs = _ctx_tables()
    lat_tabs = _lat_tables()
    zero_state = jnp.zeros((BATCH, SUBLANES, LRU_W), F32)

    ckv_out, kr_out, st_out = [], [], []
    for l in range(DEPTH):
        wl = w_in[l]
        wkr = jnp.concatenate([jnp.zeros((D_MODEL, D_NOPE), F32), wl[:, 896:928]], axis=1)
        krblk = lambda w: jnp.pad(w, ((0, 0), (0, HEAD_PAD - D_QK)))
        w1 = jnp.concatenate([wl[:, :896], krblk(wkr), krblk(_swap_rope_pairs(wkr)), wl[:, 928:2464]],
                             axis=1).astype(BF16)
        wgates = wl[:, 2464:].astype(BF16)
        wuq = _head_pad_cols(mla_w_uq[l], D_QK).astype(BF16)
        wuqs = _head_pad_cols(_swap_rope_pairs(mla_w_uq[l]), D_QK).astype(BF16)
        wuk = _head_pad_cols(mla_w_uk[l], D_NOPE).astype(BF16)
        wuv = mla_w_uv[l].reshape(KV_RANK, N_HEADS * D_V).astype(BF16)
        gq = _head_gain(mla_q_qknorm[l])
        gk = _head_gain(mla_k_qknorm[l])

        ulru, uhy, ckv, krb, q, k, v = _stage1(
            x, mod[l], norm1[l][None, :], w1, mla_kv_norm[l][None, :], mla_q_norm[l][None, :],
            wuq, wuqs, gq, wuk, gk, wuv, cos_t, sin_t)
        ckv_out.append(ckv[:M_CTX].reshape(BATCH, SEQ, KV_RANK))
        kr_out.append(krb[:M_CTX, D_NOPE:D_QK].reshape(BATCH, SEQ, D_ROPE))

        kc, vc = _kvprep(cache_ckv[:, l].reshape(DEC_BATCH * PAST_LEN, KV_RANK),
                         jnp.pad(cache_krope[:, l].reshape(DEC_BATCH * PAST_LEN, D_ROPE),
                                 ((0, 0), (D_NOPE, HEAD_PAD - D_QK))),
                         wuk, gk, wuv)
        ymla = (_attention_ctx(q, k, v), _attention_lat(q, k, v, kc, vc))

        lp = dict(
            cw=jnp.pad(lru_conv_w[l], ((0, SUBLANES - 4), (0, 0))), cb=lru_conv_b[l][None, :],
            wr=[_block_diag(lru_w_gate[l, d, 0]).astype(BF16) for d in range(2)],
            wi=[_block_diag(lru_w_gate[l, d, 1]).astype(BF16) for d in range(2)],
            br=[lru_b_gate[l, d, 0][None, :] for d in range(2)],
            bi=[lru_b_gate[l, d, 1][None, :] for d in range(2)],
            lam=[lru_lambda[l, d][None, :] for d in range(2)])
        y_c, stf, stb = _lru_mixer(ulru, lp, (zero_state, zero_state), row_off=0, nseq=BATCH, seqlen=SEQ, tc=SEQ)
        st_out.append(jnp.stack([stf, stb], axis=1))
        h0 = [jnp.broadcast_to(state_lru[:, l, d][:, None, :], (DEC_BATCH, SUBLANES, LRU_W)) for d in range(2)]
        y_l, _, _ = _lru_mixer(ulru, lp, h0, row_off=M_CTX, nseq=DEC_BATCH, seqlen=DEC_SEQ, tc=512)
        ylru = (y_c, y_l)

        hp = dict(
            cw=jnp.pad(hy_conv_w[l], ((0, SUBLANES - 3), (0, 0))), cb=hy_conv_b[l][None, :],
            w1=jnp.pad(hy_w1[l], ((0, LANES - HY_EMB), (0, 0))).astype(BF16), b1=hy_b1[l][None, :],
            f1=hy_freq1[l][None, :], w2=hy_w2[l].astype(BF16), b2=hy_b2[l][None, :], f2=hy_freq2[l][None, :],
            w3=hy_w3[l].astype(BF16), bias=hy_bias[l])
        yhy = (_hyena_ctx(uhy, hp, ctx_tabs), _hyena_lat(uhy, hp, lat_tabs))

        xmid, xm2 = _stage3(x, mod[l], norm1[l][None, :], wgates, ylru, ymla, yhy,
                            w_lru_out[l].astype(BF16), w_mla_out[l].astype(BF16), w_hy_out[l].astype(BF16),
                            w_out[l].astype(BF16), norm2[l][None, :])
        j = l // 2
        if l % 2 == 0:
            x = (_ffn_dense(xm2, xmid, mod[l], ffn_w_gate[j].astype(BF16), ffn_w_up[j].astype(BF16),
                            ffn_w_down[j].astype(BF16)),)
        else:
            wr = jnp.pad(moe_w_router[j], ((0, 0), (0, LANES - N_EXPERTS))).astype(BF16)
            x = _ffn_moe(xm2, xmid, mod[l], wr, moe_w_gate[j].astype(BF16), moe_w_up[j].astype(BF16),
                         moe_w_down[j].astype(BF16))

    xc, xl = x if len(x) == 2 else (x[0][:M_CTX], x[0][M_CTX:])
    y_prompt = xc.reshape(BATCH, SEQ, D_MODEL)
    y_sample = xl.reshape(DEC_BATCH, DEC_SEQ, D_MODEL)
    return (y_prompt, y_sample, jnp.stack(ckv_out, axis=1), jnp.stack(kr_out, axis=1), jnp.stack(st_out, axis=1))
```

```python
import functools
import math

import jax
import jax.numpy as jnp
from jax import lax
from jax.experimental import pallas as pl
from jax.experimental.pallas import tpu as pltpu

F32 = jnp.float32
BF16 = jnp.bfloat16

D_MODEL = 1024
BATCH = 32
SEQ = 256
DEPTH = 2
DEC_BATCH = 2
DEC_SEQ = 4096
PAST_LEN = 512
GRID_W = 64
EPS = 1e-6
LRU_W = 512
LRU_BLOCKS = 8
LRU_C = 8.0
N_HEADS = 8
D_NOPE = 64
D_ROPE = 32
D_QK = D_NOPE + D_ROPE
D_V = 64
Q_RANK = 256
KV_RANK = 128
ROPE_BASE = 10000.0
HY_W = 512
HY_ORDER = 2
HY_EMB = 33
HY_HID = 64
HY_FAST_PCT = 0.3
HY_SLOW_PCT = 1.5
D_FF = 2816
N_EXPERTS = 8
D_FF_E = 1408

LANES = 128
SUBLANES = 8
VMEM_LIMIT = 56 * 1024 * 1024

M_CTX = BATCH * SEQ
M_LAT = DEC_BATCH * DEC_SEQ
M_TOK = M_CTX + M_LAT
TM1 = 512
TM3 = 512
TM_FFN = 512
W1_COLS = 2688
HEAD_PAD = LANES
QK_SCALE = math.log2(math.e) / math.sqrt(D_QK)
ATTN_TQ = 256
ATTN_LAT_HEADS = 4
ATTN_LAT_GROUPS = (((0, 0, DEC_SEQ // 2),), ((0, DEC_SEQ // 2, DEC_SEQ), (1, 0, PAST_LEN)))

FFT_N1 = 64
FFT_N2 = 128


def _cparams(sem, vmem=VMEM_LIMIT):
    return pltpu.CompilerParams(dimension_semantics=sem, vmem_limit_bytes=vmem)


def _dot(a, b):
    return jnp.dot(a, b, preferred_element_type=F32)


def _rms(x, g):
    ms = jnp.mean(x * x, axis=-1, keepdims=True)
    return x * lax.rsqrt(ms + EPS) * g


def _sigmoid(x):
    return 1.0 / (1.0 + jnp.exp(-x))


def _ada_kernel(c_ref, w_ref, b_ref, o_ref):
    c = c_ref[...]
    s = (c * _sigmoid(c)).astype(BF16)
    o_ref[...] = _dot(s, w_ref[...].astype(BF16)) + b_ref[...]


def _adaln(cond, w_ada, b_ada):
    tn = 1024
    n6 = 6 * D_MODEL
    return pl.pallas_call(
        _ada_kernel,
        grid=(DEPTH, n6 // tn),
        in_specs=[
            pl.BlockSpec((SUBLANES, D_MODEL), lambda l, j: (0, 0)),
            pl.BlockSpec((None, D_MODEL, tn), lambda l, j: (l, 0, j)),
            pl.BlockSpec((None, 1, tn), lambda l, j: (l, 0, j)),
        ],
        out_specs=pl.BlockSpec((None, SUBLANES, tn), lambda l, j: (l, 0, j)),
        out_shape=jax.ShapeDtypeStruct((DEPTH, SUBLANES, n6), F32),
        compiler_params=_cparams(("arbitrary", "arbitrary")),
        name="adaln",
    )(cond, w_ada, b_ada.reshape(DEPTH, 1, n6))


def _mod_row(i, tm):
    nctx = M_CTX // tm
    per = DEC_SEQ // tm
    return jnp.where(i < nctx, 0, 1 + (i - nctx) // per)


def _rope_blk(i, tm):
    nctx = M_CTX // tm
    per = DEC_SEQ // tm
    return jnp.where(i < nctx, 0, 1 + (i - nctx) % per)


def _finish_head(raw, raw_sw, gc, gs, out_ref, sl):
    ms = jnp.sum(raw * raw, axis=-1, keepdims=True) * (1.0 / D_QK)
    rs = lax.rsqrt(ms + EPS)
    val = raw * gc
    if gs is not None:
        val = val + raw_sw * gs
    out_ref[:, sl] = (val * rs).astype(BF16)


def _read_tokens(x_refs, nctx):
    if len(x_refs) == 1:
        return x_refs[0][...]
    return jnp.where(pl.program_id(0) < nctx, x_refs[0][...], x_refs[1][...])


def _token_specs(x, tm):
    nctx = M_CTX // tm
    if len(x) == 1:
        return [pl.BlockSpec((tm, D_MODEL), lambda i: (i, 0))]
    return [pl.BlockSpec((tm, D_MODEL), lambda i: (jnp.minimum(i, nctx - 1), 0)),
            pl.BlockSpec((tm, D_MODEL), lambda i: (jnp.maximum(i - nctx, 0), 0))]


def _stage1_kernel(*refs, nx, nctx):
    (mod_ref, g1_ref, w1_ref, gkv_ref, gqn_ref, wuq_ref, wuqs_ref, gq_ref, wuk_ref, gk_ref,
     wuv_ref, cos_ref, sin_ref, ulru_ref, uhy_ref, ckv_ref, krb_ref, q_ref, k_ref, v_ref) = refs[nx:]
    x = _read_tokens(refs[:nx], nctx)
    xm = _rms(x, g1_ref[...]) * (1.0 + mod_ref[1:2, :]) + mod_ref[0:1, :]
    xb = xm.astype(BF16)
    ulru_ref[...] = _dot(xb, w1_ref[:, 0:512])
    qc = _dot(xb, w1_ref[:, 512:768])
    ckv = _dot(xb, w1_ref[:, 768:896])
    krb = _dot(xb, w1_ref[:, 896:1024])
    krs = _dot(xb, w1_ref[:, 1024:1152])
    uhy_ref[...] = _dot(xb, w1_ref[:, 1152:2688])
    ckvn = _rms(ckv, gkv_ref[...])
    ckv_ref[...] = ckvn
    krb_ref[...] = krb
    qn = _rms(qc, gqn_ref[...]).astype(BF16)
    cb = ckvn.astype(BF16)
    v_ref[...] = _dot(cb, wuv_ref[...]).astype(BF16)
    cos = cos_ref[...]
    sin = sin_ref[...]
    gcq = cos * (gq_ref[0:1, :] * QK_SCALE)
    gsq = sin * (gq_ref[1:2, :] * QK_SCALE)
    gck = cos * gk_ref[0:1, :]
    gsk = sin * gk_ref[1:2, :]
    for pair in range(N_HEADS // 2):
        ps = slice(2 * HEAD_PAD * pair, 2 * HEAD_PAD * (pair + 1))
        qraw, qsw, kraw = _dot(qn, wuq_ref[:, ps]), _dot(qn, wuqs_ref[:, ps]), _dot(cb, wuk_ref[:, ps])
        for j in range(2):
            hs = slice(HEAD_PAD * j, HEAD_PAD * (j + 1))
            sl = slice(HEAD_PAD * (2 * pair + j), HEAD_PAD * (2 * pair + j + 1))
            _finish_head(qraw[:, hs], qsw[:, hs], gcq, gsq, q_ref, sl)
            _finish_head(kraw[:, hs] + krb, krs, gck, gsk, k_ref, sl)


def _stage1(x, modl, g1, w1, gkv, gqn, wuq, wuqs, gq, wuk, gk, wuv, cos_t, sin_t):
    tm = TM1
    full = lambda shape: pl.BlockSpec(shape, lambda i: (0,) * len(shape))
    row = lambda cols: pl.BlockSpec((tm, cols), lambda i: (i, 0))
    hw = N_HEADS * HEAD_PAD
    return pl.pallas_call(
        functools.partial(_stage1_kernel, nx=len(x), nctx=M_CTX // tm),
        grid=(M_TOK // tm,),
        in_specs=_token_specs(x, tm) + [
            pl.BlockSpec((None, SUBLANES, D_MODEL), lambda i: (_mod_row(i, tm), 0, 0)),
            full((1, D_MODEL)),
            full((D_MODEL, W1_COLS)),
            full((1, KV_RANK)),
            full((1, Q_RANK)),
            full((Q_RANK, hw)),
            full((Q_RANK, hw)),
            full((SUBLANES, HEAD_PAD)),
            full((KV_RANK, hw)),
            full((SUBLANES, HEAD_PAD)),
            full((KV_RANK, N_HEADS * D_V)),
            pl.BlockSpec((tm, LANES), lambda i: (_rope_blk(i, tm), 0)),
            pl.BlockSpec((tm, LANES), lambda i: (_rope_blk(i, tm), 0)),
        ],
        out_specs=[row(LRU_W), row(3 * HY_W), row(KV_RANK), row(LANES), row(hw), row(hw), row(N_HEADS * D_V)],
        out_shape=[
            jax.ShapeDtypeStruct((M_TOK, LRU_W), F32),
            jax.ShapeDtypeStruct((M_TOK, 3 * HY_W), F32),
            jax.ShapeDtypeStruct((M_TOK, KV_RANK), F32),
            jax.ShapeDtypeStruct((M_TOK, LANES), F32),
            jax.ShapeDtypeStruct((M_TOK, hw), BF16),
            jax.ShapeDtypeStruct((M_TOK, hw), BF16),
            jax.ShapeDtypeStruct((M_TOK, N_HEADS * D_V), BF16),
        ],
        compiler_params=_cparams(("arbitrary",)),
        name="stage1",
    )(*x, modl, g1, w1, gkv, gqn, wuq, wuqs, gq, wuk, gk, wuv, cos_t, sin_t)


def _kvprep_kernel(ckv_ref, krb_ref, wuk_ref, gk_ref, wuv_ref, k_ref, v_ref):
    cb = ckv_ref[...].astype(BF16)
    v_ref[...] = _dot(cb, wuv_ref[...]).astype(BF16)
    krb = krb_ref[...]
    for h in range(N_HEADS):
        sl = slice(HEAD_PAD * h, HEAD_PAD * (h + 1))
        _finish_head(_dot(cb, wuk_ref[:, sl]) + krb, None, gk_ref[0:1, :], None, k_ref, sl)


def _kvprep(ckv, krb, wuk, gk, wuv):
    rows = ckv.shape[0]
    tm = TM1
    hw = N_HEADS * HEAD_PAD
    full = lambda shape: pl.BlockSpec(shape, lambda i: (0,) * len(shape))
    row = lambda cols: pl.BlockSpec((tm, cols), lambda i: (i, 0))
    return pl.pallas_call(
        _kvprep_kernel,
        grid=(rows // tm,),
        in_specs=[row(KV_RANK), row(LANES), full((KV_RANK, hw)), full((SUBLANES, HEAD_PAD)),
                  full((KV_RANK, N_HEADS * D_V))],
        out_specs=[row(hw), row(N_HEADS * D_V)],
        out_shape=[jax.ShapeDtypeStruct((rows, hw), BF16), jax.ShapeDtypeStruct((rows, N_HEADS * D_V), BF16)],
        compiler_params=_cparams(("arbitrary",)),
        name="kvprep",
    )(ckv, krb, wuk, gk, wuv)


def _attn_kernel(*refs, heads, nseg, groups):
    q_ref = refs[0]
    k_refs = refs[1:1 + nseg]
    v_refs = refs[1 + nseg:1 + 2 * nseg]
    o_ref = refs[1 + 2 * nseg]
    tq = q_ref.shape[0]
    lane = lax.broadcasted_iota(jnp.int32, (tq, LANES), 1)
    low = lane < D_V
    for pair in range(heads // 2):
        outs = []
        for j in range(2):
            h = 2 * pair + j
            sl = slice(HEAD_PAD * h, HEAD_PAD * (h + 1))
            q = q_ref[:, sl]
            parts = []
            for group in groups:
                s = [lax.dot_general(q, k_refs[seg][r0:r1, sl], (((1,), (1,)), ((), ())),
                                     preferred_element_type=F32) for seg, r0, r1 in group]
                m = jnp.max(s[0], axis=-1, keepdims=True)
                for si in s[1:]:
                    m = jnp.maximum(m, jnp.max(si, axis=-1, keepdims=True))
                acc = None
                den = None
                for si, (seg, r0, r1) in zip(s, group):
                    p = jnp.exp2(si - m)
                    d = jnp.sum(p, axis=-1, keepdims=True)
                    o = _dot(p.astype(BF16), v_refs[seg][r0:r1, LANES * pair:LANES * (pair + 1)])
                    acc = o if acc is None else acc + o
                    den = d if den is None else den + d
                parts.append((m, acc, den))
            if len(parts) == 1:
                _, acc, den = parts[0]
            else:
                mall = parts[0][0]
                for m, _, _ in parts[1:]:
                    mall = jnp.maximum(mall, m)
                acc = None
                den = None
                for m, a, d in parts:
                    w = jnp.exp2(m - mall)
                    acc = w * a if acc is None else acc + w * a
                    den = w * d if den is None else den + w * d
            outs.append(acc / den)
        o_ref[:, LANES * pair:LANES * (pair + 1)] = jnp.where(low, outs[0], outs[1]).astype(BF16)


def _attention_ctx(q, k, v):
    hw = N_HEADS * HEAD_PAD
    vw = N_HEADS * D_V
    return pl.pallas_call(
        functools.partial(_attn_kernel, heads=N_HEADS, nseg=1, groups=(((0, 0, SEQ),),)),
        grid=(BATCH,),
        in_specs=[
            pl.BlockSpec((SEQ, hw), lambda b: (b, 0)),
            pl.BlockSpec((SEQ, hw), lambda b: (b, 0)),
            pl.BlockSpec((SEQ, vw), lambda b: (b, 0)),
        ],
        out_specs=pl.BlockSpec((SEQ, vw), lambda b: (b, 0)),
        out_shape=jax.ShapeDtypeStruct((M_CTX, vw), BF16),
        compiler_params=_cparams(("arbitrary",)),
        name="attn_ctx",
    )(q, k, v)


def _attention_lat(q, k, v, kc, vc):
    tq = ATTN_TQ
    nq = DEC_SEQ // tq
    qoff = M_CTX // tq
    koff = M_CTX // DEC_SEQ
    hp = ATTN_LAT_HEADS
    return pl.pallas_call(
        functools.partial(_attn_kernel, heads=hp, nseg=2, groups=ATTN_LAT_GROUPS),
        grid=(DEC_BATCH, N_HEADS // hp, nq),
        in_specs=[
            pl.BlockSpec((tq, hp * HEAD_PAD), lambda b, p, i: (qoff + b * nq + i, p)),
            pl.BlockSpec((DEC_SEQ, hp * HEAD_PAD), lambda b, p, i: (koff + b, p)),
            pl.BlockSpec((PAST_LEN, hp * HEAD_PAD), lambda b, p, i: (b, p)),
            pl.BlockSpec((DEC_SEQ, hp * D_V), lambda b, p, i: (koff + b, p)),
            pl.BlockSpec((PAST_LEN, hp * D_V), lambda b, p, i: (b, p)),
        ],
        out_specs=pl.BlockSpec((tq, hp * D_V), lambda b, p, i: (b * nq + i, p)),
        out_shape=jax.ShapeDtypeStruct((M_LAT, N_HEADS * D_V), BF16),
        compiler_params=_cparams(("arbitrary", "arbitrary", "arbitrary")),
        name="attn_lat",
    )(q, k, kc, v, vc)


def _lru_kernel(*refs, reverse, tc, nchunks):
    if reverse:
        (up_ref, uc_ref, un_ref, hf_ref, cw_ref, cb_ref, wr_ref, wi_ref, br_ref, bi_ref, lam_ref, h0_ref,
         y_ref, st_ref, ext_sc, a_sc, b_sc, p_sc, h_sc, car_sc) = refs
    else:
        (up_ref, uc_ref, un_ref, cw_ref, cb_ref, wr_ref, wi_ref, br_ref, bi_ref, lam_ref, h0_ref,
         y_ref, st_ref, ext_sc, a_sc, b_sc, p_sc, h_sc, car_sc) = refs
    c = pl.program_id(1)
    chunk = (nchunks - 1 - c) if reverse else c
    prev = jnp.where(chunk == 0, 0.0, up_ref[...])
    nxt = jnp.where(chunk == nchunks - 1, 0.0, un_ref[...])
    ext_sc[0:SUBLANES, :] = prev
    ext_sc[SUBLANES:SUBLANES + tc, :] = uc_ref[...]
    ext_sc[SUBLANES + tc:2 * SUBLANES + tc, :] = nxt
    xc = cb_ref[...]
    for k in range(4):
        xc = xc + cw_ref[k:k + 1, :] * ext_sc[SUBLANES - 2 + k:SUBLANES - 2 + k + tc, :]
    xb = xc.astype(BF16)
    r = _sigmoid(_dot(xb, wr_ref[...]) + br_ref[...])
    gi = _sigmoid(_dot(xb, wi_ref[...]) + bi_ref[...])
    lam = lam_ref[...]
    logsig = -(jnp.maximum(-lam, 0.0) + jnp.log1p(jnp.exp(-jnp.abs(lam))))
    la = LRU_C * r * logsig
    a = jnp.exp(la)
    v = -jnp.tanh(la) * (a * a + 1.0)
    bc = jnp.where(v > 0.0, v * lax.rsqrt(v), 0.0) * (gi * xc)

    @pl.when(c == 0)
    def _():
        car_sc[...] = h0_ref[...]

    nseg = SUBLANES
    sl = tc // nseg
    sp = sl + SUBLANES
    nlb = LRU_W // LANES
    for j in range(nlb):
        for s in range(nseg):
            rows = slice(sl * s, sl * (s + 1))
            dst = slice((j * nseg + s) * sp, (j * nseg + s) * sp + sl)
            a_sc[dst, :] = a[rows, LANES * j:LANES * (j + 1)]
            b_sc[dst, :] = bc[rows, LANES * j:LANES * (j + 1)]

    def body(k, carry):
        i = (sl - 1 - k) if reverse else k
        hs, ps = carry
        hn, pn = [], []
        for j in range(nlb):
            idx = pl.ds(j * nseg * sp + i, nseg, stride=sp)
            av = a_sc[idx, :]
            h = av * hs[j] + b_sc[idx, :]
            p = av * ps[j]
            p_sc[idx, :] = p
            h_sc[idx, :] = h
            hn.append(h)
            pn.append(p)
        return tuple(hn), tuple(pn)

    zero = jnp.zeros((nseg, LANES), F32)
    one = jnp.ones((nseg, LANES), F32)
    hend, pend = lax.fori_loop(0, sl, body, ((zero,) * nlb, (one,) * nlb), unroll=4)

    order = range(nseg - 1, -1, -1) if reverse else range(nseg)
    for j in range(nlb):
        lanes = slice(LANES * j, LANES * (j + 1))
        cin = car_sc[0:1, lanes]
        for s in order:
            rows = slice(sl * s, sl * (s + 1))
            src = slice((j * nseg + s) * sp, (j * nseg + s) * sp + sl)
            h = h_sc[src, :] + p_sc[src, :] * cin
            if reverse:
                y_ref[rows, lanes] = (hf_ref[rows, lanes] + h).astype(BF16)
            else:
                y_ref[rows, lanes] = h
            cin = hend[j][s:s + 1, :] + pend[j][s:s + 1, :] * cin
        car_sc[0:1, lanes] = cin
        st_ref[:, lanes] = jnp.broadcast_to(cin, (SUBLANES, LANES))


def _lru_dir(u, hf, cw, cb, wr, wi, br, bi, lam, h0, *, reverse, row_off, nseq, seqlen, tc):
    nchunks = seqlen // tc
    hb = M_TOK // SUBLANES

    def chunk_of(c):
        return (nchunks - 1 - c) if reverse else c

    def cur(b, c):
        return ((row_off + b * seqlen) // tc + chunk_of(c), 0)

    def prv(b, c):
        return (jnp.maximum((row_off + b * seqlen + chunk_of(c) * tc) // SUBLANES - 1, 0), 0)

    def nxt(b, c):
        return (jnp.minimum((row_off + b * seqlen + (chunk_of(c) + 1) * tc) // SUBLANES, hb - 1), 0)

    def out_cur(b, c):
        return ((b * seqlen) // tc + chunk_of(c), 0)

    full = lambda shape: pl.BlockSpec(shape, lambda b, c: (0,) * len(shape))
    in_specs = [pl.BlockSpec((SUBLANES, LRU_W), prv), pl.BlockSpec((tc, LRU_W), cur),
                pl.BlockSpec((SUBLANES, LRU_W), nxt)]
    args = [u, u, u]
    if reverse:
        in_specs.append(pl.BlockSpec((tc, LRU_W), out_cur))
        args.append(hf)
    in_specs += [full((SUBLANES, LRU_W)), full((1, LRU_W)), full((LRU_W, LRU_W)), full((LRU_W, LRU_W)),
                 full((1, LRU_W)), full((1, LRU_W)), full((1, LRU_W)),
                 pl.BlockSpec((None, SUBLANES, LRU_W), lambda b, c: (b, 0, 0))]
    args += [cw, cb, wr, wi, br, bi, lam, h0]
    return pl.pallas_call(
        functools.partial(_lru_kernel, reverse=reverse, tc=tc, nchunks=nchunks),
        grid=(nseq, nchunks),
        in_specs=in_specs,
        out_specs=[pl.BlockSpec((tc, LRU_W), out_cur),
                   pl.BlockSpec((None, SUBLANES, LRU_W), lambda b, c: (b, 0, 0))],
        out_shape=[jax.ShapeDtypeStruct((nseq * seqlen, LRU_W), BF16 if reverse else F32),
                   jax.ShapeDtypeStruct((nseq, SUBLANES, LRU_W), F32)],
        scratch_shapes=[pltpu.VMEM((tc + 2 * SUBLANES, LRU_W), F32)]
        + [pltpu.VMEM(((LRU_W // LANES) * (tc + SUBLANES * SUBLANES), LANES), F32)] * 4
        + [pltpu.VMEM((SUBLANES, LRU_W), F32)],
        compiler_params=_cparams(("arbitrary", "arbitrary")),
        name="lru_bwd" if reverse else "lru_fwd",
    )(*args)


def _lru_mixer(u, p, h0, *, row_off, nseq, seqlen, tc):
    kw = dict(row_off=row_off, nseq=nseq, seqlen=seqlen, tc=tc)
    hf, stf = _lru_dir(u, None, p["cw"], p["cb"], p["wr"][0], p["wi"][0], p["br"][0], p["bi"][0], p["lam"][0],
                       h0[0], reverse=False, **kw)
    y, stb = _lru_dir(u, hf, p["cw"], p["cb"], p["wr"][1], p["wi"][1], p["br"][1], p["bi"][1], p["lam"][1],
                      h0[1], reverse=True, **kw)
    return y, stf[:, 0, :], stb[:, 0, :]


def _shortconv_kernel(up_ref, uc_ref, un_ref, cw_ref, cb_ref, v_ref, x1_ref, x2_ref, ext_sc, *, tc, nchunks):
    c = pl.program_id(1)
    prev = jnp.where(c == 0, 0.0, up_ref[...])
    nxt = jnp.where(c == nchunks - 1, 0.0, un_ref[...])
    ext_sc[0:SUBLANES, :] = prev
    ext_sc[SUBLANES:SUBLANES + tc, :] = uc_ref[...]
    ext_sc[SUBLANES + tc:2 * SUBLANES + tc, :] = nxt
    for part, o_ref in enumerate((v_ref, x1_ref, x2_ref)):
        cs = slice(HY_W * part, HY_W * (part + 1))
        acc = cb_ref[:, cs]
        for k in range(3):
            acc = acc + cw_ref[k:k + 1, cs] * ext_sc[SUBLANES - 1 + k:SUBLANES - 1 + k + tc, cs]
        o_ref[...] = acc


def _shortconv(u, cw, cb, *, row_off, nseq, seqlen, tc):
    nchunks = seqlen // tc
    w = 3 * HY_W
    hb = M_TOK // SUBLANES
    cur = lambda b, c: ((row_off + b * seqlen) // tc + c, 0)
    prv = lambda b, c: (jnp.maximum((row_off + b * seqlen + c * tc) // SUBLANES - 1, 0), 0)
    nxt = lambda b, c: (jnp.minimum((row_off + b * seqlen + (c + 1) * tc) // SUBLANES, hb - 1), 0)
    out = lambda b, c: ((b * seqlen) // tc + c, 0)
    full = lambda shape: pl.BlockSpec(shape, lambda b, c: (0,) * len(shape))
    rows = nseq * seqlen
    return pl.pallas_call(
        functools.partial(_shortconv_kernel, tc=tc, nchunks=nchunks),
        grid=(nseq, nchunks),
        in_specs=[pl.BlockSpec((SUBLANES, w), prv), pl.BlockSpec((tc, w), cur), pl.BlockSpec((SUBLANES, w), nxt),
                  full((SUBLANES, w)), full((1, w))],
        out_specs=[pl.BlockSpec((tc, HY_W), out)] * 3,
        out_shape=[jax.ShapeDtypeStruct((rows, HY_W), F32)] * 3,
        scratch_shapes=[pltpu.VMEM((tc + 2 * SUBLANES, w), F32)],
        compiler_params=_cparams(("arbitrary", "arbitrary")),
        name="hy_shortconv",
    )(u, u, u, cw, cb)


def _hyfilt_kernel(z_ref, t_ref, w1_ref, b1_ref, f1_ref, w2_ref, b2_ref, f2_ref, w3_ref, ad_ref, h_ref, s_ref):
    i = pl.program_id(0)
    z = z_ref[...].astype(BF16)
    h = jnp.sin(f1_ref[...] * (_dot(z, w1_ref[...]) + b1_ref[...]))
    h = jnp.sin(f2_ref[...] * (_dot(h.astype(BF16), w2_ref[...]) + b2_ref[...]))
    h = _dot(h.astype(BF16), w3_ref[...])
    t = t_ref[...]
    ncol = h.shape[1] // LANES
    win = jnp.concatenate([jnp.exp(-t * ad_ref[:, LANES * j:LANES * (j + 1)]) for j in range(ncol)], axis=1)
    h = h * win
    h_ref[...] = h

    @pl.when(i == 0)
    def _():
        s_ref[...] = jnp.zeros_like(s_ref)

    s_ref[0:1, :] = s_ref[0:1, :] + jnp.sum(jnp.abs(h), axis=0, keepdims=True)


def _hyfilt(feats, tcol, w1, b1, f1, w2, b2, f2, w3, absdelta):
    L = feats.shape[0]
    tl = min(L, 512)
    wcols = HY_ORDER * 2 * HY_W
    full = lambda shape: pl.BlockSpec(shape, lambda i: (0,) * len(shape))
    return pl.pallas_call(
        _hyfilt_kernel,
        grid=(L // tl,),
        in_specs=[pl.BlockSpec((tl, LANES), lambda i: (i, 0)), pl.BlockSpec((tl, LANES), lambda i: (i, 0)),
                  full((LANES, HY_HID)), full((1, HY_HID)), full((1, HY_HID)),
                  full((HY_HID, HY_HID)), full((1, HY_HID)), full((1, HY_HID)),
                  full((HY_HID, wcols)), full((1, wcols))],
        out_specs=[pl.BlockSpec((tl, wcols), lambda i: (i, 0)), full((SUBLANES, wcols))],
        out_shape=[jax.ShapeDtypeStruct((L, wcols), F32), jax.ShapeDtypeStruct((SUBLANES, wcols), F32)],
        compiler_params=_cparams(("arbitrary",)),
        name="hy_filter",
    )(feats, tcol, w1, b1, f1, w2, b2, f2, w3, absdelta)


def _combine_spectrum(zr, zi, s_ref, hr_out, hi_out):
    for o in range(HY_ORDER):
        f = slice(2 * HY_W * o, 2 * HY_W * o + HY_W)
        b = slice(2 * HY_W * o + HY_W, 2 * HY_W * (o + 1))
        den = s_ref[0:1, f] + s_ref[0:1, b] + EPS
        hr_out(o, (zr[:, f] + zr[:, b]) / den)
        hi_out(o, (zi[:, f] - zi[:, b]) / den)


def _ctx_spec_kernel(f_ref, h_ref, s_ref, o_ref):
    n = f_ref.shape[0] // 2
    z = _dot(f_ref[...], h_ref[...].astype(BF16))
    zr, zi = z[:n], z[n:]

    def put_r(o, val):
        o_ref[0, :, HY_W * o:HY_W * (o + 1)] = val

    def put_i(o, val):
        o_ref[1, :, HY_W * o:HY_W * (o + 1)] = val

    _combine_spectrum(zr, zi, s_ref, put_r, put_i)


def _ctx_spectrum(fmat, hdec, s):
    n = fmat.shape[0] // 2
    return pl.pallas_call(
        _ctx_spec_kernel,
        out_shape=jax.ShapeDtypeStruct((2, n, HY_ORDER * HY_W), F32),
        compiler_params=pltpu.CompilerParams(vmem_limit_bytes=VMEM_LIMIT),
        name="hy_ctx_spectrum",
    )(fmat, hdec, s)


def _ctx_conv_kernel(z_ref, x_ref, f_ref, fi_ref, h_ref, bias_ref, o_ref, *, nb, seqlen):
    n = f_ref.shape[0] // 2
    hr = h_ref[0]
    hi = h_ref[1]
    for b in range(nb):
        rs = slice(seqlen * b, seqlen * (b + 1))
        zt = z_ref[rs, :]
        zf = _dot(f_ref[...], zt.astype(BF16))
        zr, zi = zf[:n], zf[n:]
        y = jnp.concatenate([zr * hr - zi * hi, zr * hi + zi * hr], axis=0).astype(BF16)
        conv = _dot(fi_ref[...], y)
        o_ref[rs, :] = (x_ref[rs, :] * (conv + zt * bias_ref[...])).astype(o_ref.dtype)


def _ctx_conv(z, xg, fmat, finv, hspec, bias, order, out_dtype):
    nb = 4
    n = fmat.shape[0] // 2
    rows = nb * SEQ
    return pl.pallas_call(
        functools.partial(_ctx_conv_kernel, nb=nb, seqlen=SEQ),
        grid=(BATCH // nb,),
        in_specs=[pl.BlockSpec((rows, HY_W), lambda i: (i, 0)), pl.BlockSpec((rows, HY_W), lambda i: (i, 0)),
                  pl.BlockSpec(fmat.shape, lambda i: (0, 0)), pl.BlockSpec(finv.shape, lambda i: (0, 0)),
                  pl.BlockSpec((2, n, HY_W), lambda i: (0, 0, order)),
                  pl.BlockSpec((1, HY_W), lambda i: (0, 0))],
        out_specs=pl.BlockSpec((rows, HY_W), lambda i: (i, 0)),
        out_shape=jax.ShapeDtypeStruct((M_CTX, HY_W), out_dtype),
        compiler_params=_cparams(("arbitrary",)),
        name="hy_ctx_conv",
    )(z, xg, fmat, finv, hspec, bias)


K1U = FFT_N1 // 2 + 1
SLABS = 72
PITCH = FFT_N2 + SUBLANES
NROW1 = FFT_N1 // 2
LAT_UNROLL_R = 16
LAT_UNROLL_K = 11


def _pitch_copy_in(src_ref, col, dst_sc):
    for n1 in range(NROW1):
        dst_sc[PITCH * n1:PITCH * n1 + FFT_N2, :] = src_ref[FFT_N2 * n1:FFT_N2 * (n1 + 1), col]


def _dft_stage_a(zp_scs, wa_ref, a_scs):
    def body(r, c):
        x = jnp.concatenate([zp_sc[pl.ds(r, NROW1, stride=PITCH), :] for zp_sc in zp_scs], axis=1)
        a = _dot(wa_ref[...], x.astype(BF16))
        for j, a_sc in enumerate(a_scs):
            a_sc[pl.ds(r, SLABS, stride=PITCH), :] = a[:, LANES * j:LANES * (j + 1)]
        return c

    lax.fori_loop(0, FFT_N2, body, 0, unroll=LAT_UNROLL_R)


def _load_k1(a_sc, k1):
    base = pl.multiple_of(k1 * (2 * PITCH), SUBLANES)
    a = jnp.concatenate([a_sc[pl.ds(base, FFT_N2), :], a_sc[pl.ds(base + PITCH, FFT_N2), :]], axis=0)
    return base, a.astype(BF16)


def _lat_spec_kernel(h_ref, s_ref, wa_ref, g_ref, o_ref, hf_sc, hb_sc, af_sc, ab_sc):
    _pitch_copy_in(h_ref, slice(0, LANES), hf_sc)
    _pitch_copy_in(h_ref, slice(LANES, 2 * LANES), hb_sc)
    _dft_stage_a((hf_sc, hb_sc), wa_ref, (af_sc, ab_sc))
    den = s_ref[0:1, 0:LANES] + s_ref[0:1, LANES:2 * LANES] + EPS

    def kbody(k1, c):
        _, af = _load_k1(af_sc, k1)
        _, ab = _load_k1(ab_sc, k1)
        z = _dot(g_ref[k1], jnp.concatenate([af, ab], axis=1))
        zf, zb = z[:, :LANES], z[:, LANES:]
        o_ref[0, k1] = (zf[:FFT_N2] + zb[:FFT_N2]) / den
        o_ref[1, k1] = (zf[FFT_N2:] - zb[FFT_N2:]) / den
        return c

    lax.fori_loop(0, K1U, kbody, 0, unroll=LAT_UNROLL_K)


def _lat_spectrum(hdec, s, wa, gtab):
    nblk = HY_ORDER * HY_W // LANES
    slab = NROW1 * PITCH
    return pl.pallas_call(
        _lat_spec_kernel,
        grid=(nblk,),
        in_specs=[pl.BlockSpec((DEC_SEQ, 2 * LANES), lambda i: (0, i)),
                  pl.BlockSpec((SUBLANES, 2 * LANES), lambda i: (0, i)),
                  pl.BlockSpec(wa.shape, lambda i: (0, 0)),
                  pl.BlockSpec(gtab.shape, lambda i: (0, 0, 0))],
        out_specs=pl.BlockSpec((2, K1U, FFT_N2, LANES), lambda i: (0, 0, 0, i)),
        out_shape=jax.ShapeDtypeStruct((2, K1U, FFT_N2, HY_ORDER * HY_W), F32),
        scratch_shapes=[pltpu.VMEM((slab, LANES), F32), pltpu.VMEM((slab, LANES), F32),
                        pltpu.VMEM((SLABS * PITCH, LANES), F32), pltpu.VMEM((SLABS * PITCH, LANES), F32)],
        compiler_params=_cparams(("arbitrary",)),
        name="hy_lat_spectrum",
    )(hdec, s, wa, gtab)


def _lat_conv_kernel(z_ref, x_ref, wa_ref, wai_ref, g_ref, gi_ref, h_ref, bias_ref, o_ref,
                     zp_sc, xp_sc, op_sc, a_sc):
    full = slice(None)
    _pitch_copy_in(z_ref, full, zp_sc)
    _pitch_copy_in(x_ref, full, xp_sc)
    _dft_stage_a((zp_sc,), wa_ref, (a_sc,))

    def kbody(k1, c):
        base, a = _load_k1(a_sc, k1)
        z = _dot(g_ref[k1], a)
        zr, zi = z[:FFT_N2], z[FFT_N2:]
        hr = h_ref[0, k1]
        hi = h_ref[1, k1]
        y = jnp.concatenate([zr * hr - zi * hi, zr * hi + zi * hr], axis=0).astype(BF16)
        bp = _dot(gi_ref[k1], y)
        a_sc[pl.ds(base, FFT_N2), :] = bp[:FFT_N2]
        a_sc[pl.ds(base + PITCH, FFT_N2), :] = bp[FFT_N2:]
        return c

    lax.fori_loop(0, K1U, kbody, 0, unroll=LAT_UNROLL_K)
    bias = bias_ref[...]

    def rbody(r, c):
        yb = a_sc[pl.ds(r, SLABS, stride=PITCH), :].astype(BF16)
        conv = _dot(wai_ref[...], yb)
        zz = zp_sc[pl.ds(r, NROW1, stride=PITCH), :]
        xx = xp_sc[pl.ds(r, NROW1, stride=PITCH), :]
        op_sc[pl.ds(r, NROW1, stride=PITCH), :] = xx * (conv + zz * bias)
        return c

    lax.fori_loop(0, FFT_N2, rbody, 0, unroll=LAT_UNROLL_R)
    for n1 in range(NROW1):
        o_ref[FFT_N2 * n1:FFT_N2 * (n1 + 1), :] = op_sc[PITCH * n1:PITCH * n1 + FFT_N2, :]


def _lat_conv(z, xg, wa, wainv, gtab, gitab, hspec, bias, order):
    ncb = HY_W // LANES
    slab = NROW1 * PITCH
    blk = pl.BlockSpec((DEC_SEQ, LANES), lambda cb, b: (b, cb))
    const = lambda a: pl.BlockSpec(a.shape, lambda cb, b: (0,) * a.ndim)
    return pl.pallas_call(
        _lat_conv_kernel,
        grid=(ncb, DEC_BATCH),
        in_specs=[blk, blk, const(wa), const(wainv), const(gtab), const(gitab),
                  pl.BlockSpec((2, K1U, FFT_N2, LANES), lambda cb, b: (0, 0, 0, order * ncb + cb)),
                  pl.BlockSpec((1, LANES), lambda cb, b: (0, cb))],
        out_specs=blk,
        out_shape=jax.ShapeDtypeStruct((M_LAT, HY_W), F32),
        scratch_shapes=[pltpu.VMEM((slab, LANES), F32), pltpu.VMEM((slab, LANES), F32),
                        pltpu.VMEM((slab, LANES), F32), pltpu.VMEM((SLABS * PITCH, LANES), F32)],
        compiler_params=_cparams(("arbitrary", "arbitrary")),
        name="hy_lat_conv",
    )(z, xg, wa, wainv, gtab, gitab, hspec, bias)


def _angle(m, n):
    return (m % n).astype(F32) * (2.0 * math.pi / n)


def _ctx_tables():
    n = 2 * SEQ
    nf = SEQ + SUBLANES
    k = jnp.arange(nf, dtype=jnp.int32)[:, None]
    t = jnp.arange(SEQ, dtype=jnp.int32)[None, :]
    th = _angle(k * t, n)
    live = (k <= SEQ).astype(F32)
    wgt = jnp.where((k == 0) | (k == SEQ), 1.0, 2.0) * live / n
    fmat = jnp.concatenate([live * jnp.cos(th), -live * jnp.sin(th)], axis=0)
    finv = jnp.concatenate([(wgt * jnp.cos(th)).T, (-wgt * jnp.sin(th)).T], axis=1)
    return fmat.astype(BF16), finv.astype(BF16)


def _lat_tables():
    n1, n2 = FFT_N1, FFT_N2
    n = n1 * n2
    k1 = jnp.arange(K1U, dtype=jnp.int32)
    th1 = _angle(k1[:, None] * jnp.arange(NROW1, dtype=jnp.int32)[None, :], n1)
    wa = jnp.stack([jnp.cos(th1), -jnp.sin(th1)], axis=1).reshape(2 * K1U, NROW1)
    wa = jnp.pad(wa, ((0, SLABS - 2 * K1U), (0, 0)))
    wgt = jnp.where((k1 == 0) | (k1 == n1 // 2), 1.0, 2.0)[:, None] / n
    wainv = jnp.stack([wgt * jnp.cos(th1), -wgt * jnp.sin(th1)], axis=1).reshape(2 * K1U, NROW1).T
    wainv = jnp.pad(wainv, ((0, 0), (0, SLABS - 2 * K1U)))
    k2 = jnp.arange(n2, dtype=jnp.int32)
    nn2 = jnp.arange(n2, dtype=jnp.int32)
    ta = _angle(k1[:, None] * nn2[None, :], n)[:, None, :]
    tb = _angle(k2[:, None] * nn2[None, :], n2)[None, :, :]
    ca, sa, cb_, sb = jnp.cos(ta), jnp.sin(ta), jnp.cos(tb), jnp.sin(tb)
    gr = ca * cb_ - sa * sb
    gi = -(sa * cb_ + ca * sb)
    g = jnp.concatenate([jnp.concatenate([gr, -gi], axis=2), jnp.concatenate([gi, gr], axis=2)],
                        axis=1).astype(BF16)
    ginv = jnp.swapaxes(g, 1, 2)
    return wa.astype(BF16), wainv.astype(BF16), g, ginv


def _filter_features(L):
    t = jnp.linspace(0.0, 1.0, L, dtype=F32)[:, None]
    bands = (HY_EMB - 1) // 2
    w = (2.0 * math.pi / L) * jnp.arange(L, dtype=F32)[:, None]
    f = jnp.linspace(1e-4, bands - 1, bands, dtype=F32)[None, :]
    z = jnp.concatenate([t, jnp.cos(f * w), -jnp.sin(f * w)], axis=-1)
    z = jnp.pad(z, ((0, 0), (0, LANES - HY_EMB)))
    return z, jnp.broadcast_to(t, (L, LANES))


def _hyena_filter(p, L, blocked):
    feats, tcol = _filter_features(L)
    deltas = jnp.linspace(math.log(1e-2) / HY_FAST_PCT, math.log(1e-2) / HY_SLOW_PCT, HY_W, dtype=F32)
    absdelta = jnp.tile(jnp.abs(deltas), HY_ORDER * 2)[None, :]
    w3 = p["w3"]
    if blocked:
        reorder = lambda a: a.reshape(a.shape[0], HY_ORDER, 2, HY_W // LANES, LANES).transpose(
            0, 1, 3, 2, 4).reshape(a.shape[0], HY_ORDER * 2 * HY_W)
        w3, absdelta = reorder(w3), reorder(absdelta)
    return _hyfilt(feats, tcol, p["w1"], p["b1"], p["f1"], p["w2"], p["b2"], p["f2"], w3, absdelta)


def _hyena_ctx(u_hy, p, tabs):
    fmat, finv = tabs
    v, x1, x2 = _shortconv(u_hy, p["cw"], p["cb"], row_off=0, nseq=BATCH, seqlen=SEQ, tc=SEQ)
    hdec, s = _hyena_filter(p, SEQ, False)
    hspec = _ctx_spectrum(fmat, hdec, s)
    z = _ctx_conv(v, x1, fmat, finv, hspec, p["bias"][0:1], 0, F32)
    return _ctx_conv(z, x2, fmat, finv, hspec, p["bias"][1:2], 1, F32)


def _hyena_lat(u_hy, p, tabs):
    wa, wainv, gtab, gitab = tabs
    v, x1, x2 = _shortconv(u_hy, p["cw"], p["cb"], row_off=M_CTX, nseq=DEC_BATCH, seqlen=DEC_SEQ, tc=512)
    hdec, s = _hyena_filter(p, DEC_SEQ, True)
    hspec = _lat_spectrum(hdec, s, wa, gtab)
    z = _lat_conv(v, x1, wa, wainv, gtab, gitab, hspec, p["bias"][0:1], 0)
    return _lat_conv(z, x2, wa, wainv, gtab, gitab, hspec, p["bias"][1:2], 1)


def _stage3_kernel(*refs, nx, nctx):
    (mod_ref, g1_ref, wg_ref, ylc_ref, yll_ref, ymc_ref, yml_ref, yhc_ref, yhl_ref,
     wl_ref, wm_ref, wh_ref, wo_ref, g2_ref, xo_ref, xm2_ref) = refs[nx:]
    is_ctx = pl.program_id(0) < nctx
    x = _read_tokens(refs[:nx], nctx)
    xm = _rms(x, g1_ref[...]) * (1.0 + mod_ref[1:2, :]) + mod_ref[0:1, :]
    xb = xm.astype(BF16)
    merged = None
    branches = ((ylc_ref, yll_ref, wl_ref), (ymc_ref, yml_ref, wm_ref), (yhc_ref, yhl_ref, wh_ref))
    for bidx, (yc_ref, yl_ref, w_ref) in enumerate(branches):
        gate = _sigmoid(_dot(xb, wg_ref[:, D_MODEL * bidx:D_MODEL * (bidx + 1)]))
        y = jnp.where(is_ctx, yc_ref[...], yl_ref[...]).astype(BF16)
        term = gate * _dot(y, w_ref[...])
        merged = term if merged is None else merged + term
    xo = x + mod_ref[2:3, :] * _dot(merged.astype(BF16), wo_ref[...])
    xo_ref[...] = xo
    xm2 = _rms(xo, g2_ref[...]) * (1.0 + mod_ref[4:5, :]) + mod_ref[3:4, :]
    xm2_ref[...] = xm2.astype(BF16)


def _stage3(x, modl, g1, wg, ylru, ymla, yhy, wl, wm, wh, wo, g2):
    tm = TM3
    nctx = M_CTX // tm
    full = lambda shape: pl.BlockSpec(shape, lambda i: (0,) * len(shape))
    row = lambda cols: pl.BlockSpec((tm, cols), lambda i: (i, 0))
    ctx = lambda cols: pl.BlockSpec((tm, cols), lambda i: (jnp.minimum(i, nctx - 1), 0))
    lat = lambda cols: pl.BlockSpec((tm, cols), lambda i: (jnp.maximum(i - nctx, 0), 0))
    return pl.pallas_call(
        functools.partial(_stage3_kernel, nx=len(x), nctx=nctx),
        grid=(M_TOK // tm,),
        in_specs=_token_specs(x, tm) + [
                  pl.BlockSpec((None, SUBLANES, D_MODEL), lambda i: (_mod_row(i, tm), 0, 0)),
                  full((1, D_MODEL)), full((D_MODEL, 3 * D_MODEL)),
                  ctx(LRU_W), lat(LRU_W), ctx(N_HEADS * D_V), lat(N_HEADS * D_V), ctx(HY_W), lat(HY_W),
                  full((LRU_W, D_MODEL)), full((N_HEADS * D_V, D_MODEL)), full((HY_W, D_MODEL)),
                  full((D_MODEL, D_MODEL)), full((1, D_MODEL))],
        out_specs=[row(D_MODEL), row(D_MODEL)],
        out_shape=[jax.ShapeDtypeStruct((M_TOK, D_MODEL), F32), jax.ShapeDtypeStruct((M_TOK, D_MODEL), BF16)],
        compiler_params=_cparams(("arbitrary",)),
        name="stage3",
    )(*x, modl, g1, wg, *ylru, *ymla, *yhy, wl, wm, wh, wo, g2)


def _ffn_kernel(xm_ref, x_ref, mod_ref, wg_ref, wu_ref, wd_ref, o_ref, acc_sc, *, nchunks):
    j = pl.program_id(1)
    xb = xm_ref[...]
    g = _dot(xb, wg_ref[...])
    u = _dot(xb, wu_ref[...])
    hid = (g * _sigmoid(g) * u).astype(BF16)
    part = _dot(hid, wd_ref[...])

    @pl.when(j == 0)
    def _():
        acc_sc[...] = part

    @pl.when(j > 0)
    def _():
        acc_sc[...] = acc_sc[...] + part

    @pl.when(j == nchunks - 1)
    def _():
        o_ref[...] = x_ref[...] + mod_ref[5:6, :] * acc_sc[...]


def _ffn_dense(xm2, x, modl, wg, wu, wd):
    tm = TM_FFN
    nchunks = 2
    cw = D_FF // nchunks
    return pl.pallas_call(
        functools.partial(_ffn_kernel, nchunks=nchunks),
        grid=(M_TOK // tm, nchunks),
        in_specs=[pl.BlockSpec((tm, D_MODEL), lambda i, j: (i, 0)),
                  pl.BlockSpec((tm, D_MODEL), lambda i, j: (i, 0)),
                  pl.BlockSpec((None, SUBLANES, D_MODEL), lambda i, j: (_mod_row(i, tm), 0, 0)),
                  pl.BlockSpec((D_MODEL, cw), lambda i, j: (0, j)),
                  pl.BlockSpec((D_MODEL, cw), lambda i, j: (0, j)),
                  pl.BlockSpec((cw, D_MODEL), lambda i, j: (j, 0))],
        out_specs=pl.BlockSpec((tm, D_MODEL), lambda i, j: (i, 0)),
        out_shape=jax.ShapeDtypeStruct((M_TOK, D_MODEL), F32),
        scratch_shapes=[pltpu.VMEM((tm, D_MODEL), F32)],
        compiler_params=_cparams(("arbitrary", "arbitrary")),
        name="ffn_dense",
    )(xm2, x, modl, wg, wu, wd)


MOE_TILE = 256
MOE_ROWS = 2 * M_TOK + N_EXPERTS * MOE_TILE
MOE_TILES = MOE_ROWS // MOE_TILE
MOE_PAD_ROWS = MOE_ROWS - 2 * M_TOK
MOE_SEG = D_MODEL // LANES


def _to_row_tiles(val, ref, base):
    n = val.shape[0]
    for j in range(MOE_SEG):
        ref[pl.ds(base + j, n, stride=MOE_SEG), :] = val[:, LANES * j:LANES * (j + 1)]


def _from_row_tiles(ref, base, n):
    return jnp.concatenate([ref[pl.ds(base + j, n, stride=MOE_SEG), :] for j in range(MOE_SEG)], axis=1)


def _router_kernel(xm_ref, wr_ref, tri_ref, sel_ref, xp_ref, cnt_ref, base_sc):
    i = pl.program_id(0)
    xb = xm_ref[...]
    tm = xb.shape[0]
    lane = lax.broadcasted_iota(jnp.int32, (tm, LANES), 1)

    @pl.when(i == 0)
    def _():
        base_sc[...] = jnp.zeros_like(base_sc)

    logits = jnp.where(lane < N_EXPERTS, _dot(xb, wr_ref[...]), -1e30)
    mx = jnp.max(logits, axis=-1, keepdims=True)
    ex = jnp.exp(logits - mx)
    probs = ex / jnp.sum(ex, axis=-1, keepdims=True)
    p1 = jnp.max(probs, axis=-1, keepdims=True)
    i1 = jnp.min(jnp.where(probs == p1, lane, LANES), axis=-1, keepdims=True)
    rest = jnp.where(lane == i1, -1.0, probs)
    p2 = jnp.max(rest, axis=-1, keepdims=True)
    i2 = jnp.min(jnp.where(rest == p2, lane, LANES), axis=-1, keepdims=True)
    tot = p1 + p2
    oh1 = lane == i1
    oh2 = lane == i2
    oh = jnp.where(oh1 | oh2, 1.0, 0.0)
    before = base_sc[0:1, :] + _dot(tri_ref[...], oh.astype(BF16))
    r1 = jnp.sum(jnp.where(oh1, before, 0.0), axis=-1, keepdims=True)
    r2 = jnp.sum(jnp.where(oh2, before, 0.0), axis=-1, keepdims=True)
    base_sc[0:1, :] = base_sc[0:1, :] + jnp.sum(oh, axis=0, keepdims=True)
    cnt_ref[...] = base_sc[...]
    sel = jnp.where(lane == 0, p1 / tot, 0.0) + jnp.where(lane == 1, p2 / tot, 0.0)
    sel = sel + jnp.where(lane == 2, i1.astype(F32), 0.0) + jnp.where(lane == 3, i2.astype(F32), 0.0)
    sel_ref[...] = sel + jnp.where(lane == 4, r1, 0.0) + jnp.where(lane == 5, r2, 0.0)
    _to_row_tiles(xb.astype(F32), xp_ref, 0)


def _moe_router(xm2, wr):
    tm = TM_FFN
    tri = jnp.tril(jnp.ones((tm, tm), F32), -1).astype(BF16)
    return pl.pallas_call(
        _router_kernel,
        grid=(M_TOK // tm,),
        in_specs=[pl.BlockSpec((tm, D_MODEL), lambda i: (i, 0)),
                  pl.BlockSpec((D_MODEL, LANES), lambda i: (0, 0)),
                  pl.BlockSpec((tm, tm), lambda i: (0, 0))],
        out_specs=[pl.BlockSpec((tm, LANES), lambda i: (i, 0)),
                   pl.BlockSpec((tm * MOE_SEG, LANES), lambda i: (i, 0)),
                   pl.BlockSpec((SUBLANES, LANES), lambda i: (0, 0))],
        out_shape=[jax.ShapeDtypeStruct((M_TOK, LANES), F32),
                   jax.ShapeDtypeStruct((M_TOK * MOE_SEG, LANES), F32),
                   jax.ShapeDtypeStruct((SUBLANES, LANES), F32)],
        scratch_shapes=[pltpu.VMEM((SUBLANES, LANES), F32)],
        compiler_params=_cparams(("arbitrary",)),
        name="moe_router",
    )(xm2, wr, tri)


def _row_copy(src, srow8, dst, drow8, sem):
    aligned = lambda r: r if isinstance(r, int) else pl.multiple_of(r, MOE_SEG)
    return pltpu.make_async_copy(src.at[pl.ds(aligned(srow8), MOE_SEG), :],
                                 dst.at[pl.ds(aligned(drow8), MOE_SEG), :], sem)


def _dispatch_kernel(pos_ref, pad_ref, xp_ref, xs_ref, ring_sc, zero_sc, sem, *, nsteps):
    i = pl.program_id(0)
    slot = i % 2
    nrow = MOE_TILE
    slot_rows = nrow * MOE_SEG

    def wait_slot(s):
        for _ in range(2):
            pltpu.make_async_copy(ring_sc.at[s], xs_ref.at[pl.ds(0, slot_rows), :], sem.at[s]).wait()

    @pl.when(i >= 2)
    def _():
        wait_slot(slot)

    ring_sc[slot] = xp_ref[...]

    def body(t, c):
        _row_copy(ring_sc.at[slot], t * MOE_SEG, xs_ref, pos_ref[0, 2 * t], sem.at[slot]).start(priority=0)
        _row_copy(ring_sc.at[slot], t * MOE_SEG, xs_ref, pos_ref[0, 2 * t + 1], sem.at[slot]).start(priority=1)
        return c

    lax.fori_loop(0, nrow, body, 0, unroll=4)

    @pl.when(i == nsteps - 1)
    def _():
        zero_sc[...] = jnp.zeros_like(zero_sc)

        def zbody(t, c):
            _row_copy(zero_sc, 0, xs_ref, pad_ref[t], sem.at[2]).start()
            return c

        lax.fori_loop(0, MOE_PAD_ROWS, zbody, 0, unroll=4)
        wait_slot(1 - slot)
        wait_slot(slot)

        for _ in range(MOE_PAD_ROWS // nrow):
            pltpu.make_async_copy(ring_sc.at[0], xs_ref.at[pl.ds(0, slot_rows), :], sem.at[2]).wait()


def _moe_dispatch(xp, pos, padrows):
    nsteps = M_TOK // MOE_TILE
    return pl.pallas_call(
        functools.partial(_dispatch_kernel, nsteps=nsteps),
        grid=(nsteps,),
        in_specs=[pl.BlockSpec((None, 1, 2 * MOE_TILE), lambda i: (i, 0, 0), memory_space=pltpu.SMEM),
                  pl.BlockSpec(memory_space=pltpu.SMEM),
                  pl.BlockSpec((MOE_TILE * MOE_SEG, LANES), lambda i: (i, 0))],
        out_specs=pl.BlockSpec(memory_space=pl.ANY),
        out_shape=jax.ShapeDtypeStruct((MOE_ROWS * MOE_SEG, LANES), F32),
        scratch_shapes=[pltpu.VMEM((2, MOE_TILE * MOE_SEG, LANES), F32), pltpu.VMEM((MOE_SEG, LANES), F32),
                        pltpu.SemaphoreType.DMA((3,))],
        compiler_params=_cparams(("arbitrary",)),
        name="moe_dispatch",
    )(pos, padrows, xp)


def _experts_kernel(te_ref, nu_ref, xs_ref, wg_ref, wu_ref, wd_ref, ys_ref):
    i = pl.program_id(0)

    @pl.when(i < nu_ref[0])
    def _():
        xb = _from_row_tiles(xs_ref, 0, MOE_TILE).astype(BF16)
        g = _dot(xb, wg_ref[...])
        u = _dot(xb, wu_ref[...])
        hid = (g * _sigmoid(g) * u).astype(BF16)
        _to_row_tiles(_dot(hid, wd_ref[...]), ys_ref, 0)

    @pl.when(i >= nu_ref[0])
    def _():
        ys_ref[...] = jnp.zeros_like(ys_ref)


def _moe_experts(tile_expert, n_used, xs, wg, wu, wd):
    grid_spec = pltpu.PrefetchScalarGridSpec(
        num_scalar_prefetch=2,
        grid=(MOE_TILES,),
        in_specs=[pl.BlockSpec((MOE_TILE * MOE_SEG, LANES), lambda i, te, nu: (i, 0)),
                  pl.BlockSpec((None, D_MODEL, D_FF_E), lambda i, te, nu: (te[i], 0, 0)),
                  pl.BlockSpec((None, D_MODEL, D_FF_E), lambda i, te, nu: (te[i], 0, 0)),
                  pl.BlockSpec((None, D_FF_E, D_MODEL), lambda i, te, nu: (te[i], 0, 0))],
        out_specs=pl.BlockSpec((MOE_TILE * MOE_SEG, LANES), lambda i, te, nu: (i, 0)),
    )
    return pl.pallas_call(
        _experts_kernel,
        grid_spec=grid_spec,
        out_shape=jax.ShapeDtypeStruct((MOE_ROWS * MOE_SEG, LANES), F32),
        compiler_params=_cparams(("arbitrary",)),
        name="moe_experts",
    )(tile_expert, n_used, xs, wg, wu, wd)


def _combine_kernel(pos_ref, x_ref, mod_ref, sel_ref, ys_ref, oc_ref, ol_ref, buf_sc, sem, *, nsteps, nctx):
    i = pl.program_id(0)
    slot = i % 2
    nrow = MOE_TILE
    part = nrow * MOE_SEG

    def start(s, off):
        def body(t, c):
            for k in range(2):
                _row_copy(ys_ref, pos_ref[0, off + 2 * t + k], buf_sc, (2 * s + k) * part + t * MOE_SEG,
                          sem.at[s]).start(priority=k)
            return c
        lax.fori_loop(0, nrow, body, 0, unroll=4)

    @pl.when(i == 0)
    def _():
        start(0, 0)

    @pl.when(i + 1 < nsteps)
    def _():
        start(1 - slot, 2 * nrow)

    for k in range(2):
        pltpu.make_async_copy(ys_ref.at[pl.ds(0, part), :], buf_sc.at[pl.ds(0, part), :], sem.at[slot]).wait()
    lane = lax.broadcasted_iota(jnp.int32, (nrow, LANES), 1)
    sel = sel_ref[...]
    w1 = jnp.sum(jnp.where(lane == 0, sel, 0.0), axis=-1, keepdims=True)
    w2 = jnp.sum(jnp.where(lane == 1, sel, 0.0), axis=-1, keepdims=True)
    y = (w1 * _from_row_tiles(buf_sc, 2 * slot * part, nrow)
         + w2 * _from_row_tiles(buf_sc, (2 * slot + 1) * part, nrow))
    val = x_ref[...] + mod_ref[5:6, :] * y

    @pl.when(i < nctx)
    def _():
        oc_ref[...] = val

    @pl.when(i >= nctx)
    def _():
        ol_ref[...] = val


def _moe_combine(pos2, x, modl, sel, ys):
    nsteps = M_TOK // MOE_TILE
    tm = MOE_TILE
    nctx = M_CTX // tm
    return pl.pallas_call(
        functools.partial(_combine_kernel, nsteps=nsteps, nctx=nctx),
        grid=(nsteps,),
        in_specs=[pl.BlockSpec((None, 1, 4 * MOE_TILE), lambda i: (i, 0, 0), memory_space=pltpu.SMEM),
                  pl.BlockSpec((tm, D_MODEL), lambda i: (i, 0)),
                  pl.BlockSpec((None, SUBLANES, D_MODEL), lambda i: (_mod_row(i, tm), 0, 0)),
                  pl.BlockSpec((tm, LANES), lambda i: (i, 0)),
                  pl.BlockSpec(memory_space=pl.ANY)],
        out_specs=[pl.BlockSpec((tm, D_MODEL), lambda i: (jnp.minimum(i, nctx - 1), 0)),
                   pl.BlockSpec((tm, D_MODEL), lambda i: (jnp.maximum(i - nctx, 0), 0))],
        out_shape=[jax.ShapeDtypeStruct((M_CTX, D_MODEL), F32), jax.ShapeDtypeStruct((M_LAT, D_MODEL), F32)],
        scratch_shapes=[pltpu.VMEM((2 * 2 * MOE_TILE * MOE_SEG, LANES), F32), pltpu.SemaphoreType.DMA((2,))],
        compiler_params=_cparams(("arbitrary",)),
        name="moe_combine",
    )(pos2, x, modl, sel, ys)


def _ffn_moe(xm2, x, modl, wr, wg, wu, wd):
    sel, xp, cnt = _moe_router(xm2, wr)
    counts = cnt[0, :N_EXPERTS].astype(jnp.int32)
    padded = ((counts + MOE_TILE - 1) // MOE_TILE) * MOE_TILE
    ends = jnp.cumsum(padded)
    offs = ends - padded
    experts = sel[:, 2:4].astype(jnp.int32)
    ranks = sel[:, 4:6].astype(jnp.int32)
    pos = (offs[experts] + ranks) * MOE_SEG
    tile_start = jnp.arange(MOE_TILES, dtype=jnp.int32) * MOE_TILE
    tile_expert = jnp.minimum(jnp.sum(tile_start[:, None] >= ends[None, :], axis=1), N_EXPERTS - 1)
    n_used = (ends[-1] // MOE_TILE).astype(jnp.int32)[None]
    rows = jnp.arange(MOE_ROWS, dtype=jnp.int32)
    row_expert = jnp.repeat(tile_expert, MOE_TILE)
    written = (rows < ends[-1]) & (rows - offs[row_expert] < counts[row_expert])
    padrows = jnp.nonzero(~written, size=MOE_PAD_ROWS)[0].astype(jnp.int32) * MOE_SEG
    pos_tiles = pos.reshape(M_TOK // MOE_TILE, 1, 2 * MOE_TILE)
    xs = _moe_dispatch(xp, pos_tiles, padrows)
    ys = _moe_experts(tile_expert.astype(jnp.int32), n_used, xs, wg, wu, wd)
    nxt = jnp.concatenate([pos_tiles[1:], pos_tiles[-1:]], axis=0)
    pos2 = jnp.concatenate([pos_tiles, nxt], axis=2)
    return tuple(_moe_combine(pos2, x, modl, sel, ys))


def _block_diag(w):
    nb, bs, _ = w.shape
    eye = jnp.eye(nb, dtype=w.dtype)
    return jnp.einsum("njk,nm->njmk", w, eye).reshape(nb * bs, nb * bs)


def _head_pad_cols(w, width):
    r = w.shape[0]
    return jnp.pad(w, ((0, 0), (0, 0), (0, HEAD_PAD - width))).reshape(r, N_HEADS * HEAD_PAD)


def _swap_rope_pairs(a):
    nope, rope = a[..., :D_NOPE], a[..., D_NOPE:]
    sw = rope.reshape(rope.shape[:-1] + (D_ROPE // 2, 2))[..., ::-1].reshape(rope.shape)
    return jnp.concatenate([nope, sw], axis=-1)


def _head_gain(g):
    rows = jnp.stack([g, _swap_rope_pairs(g)], axis=0)
    return jnp.pad(rows, ((0, SUBLANES - 2), (0, HEAD_PAD - D_QK)))


def _rope_tables(tm):
    rows = DEC_SEQ // GRID_W
    row = jnp.repeat(jnp.arange(rows, dtype=F32), GRID_W)
    col = jnp.tile(jnp.arange(GRID_W, dtype=F32), rows)
    half = D_ROPE // 2
    inv_freq = ROPE_BASE ** (-jnp.arange(0, half, 2, dtype=F32) / half)
    ang = jnp.concatenate([row[:, None] * inv_freq, col[:, None] * inv_freq], axis=-1)
    cos, sin = jnp.cos(ang), jnp.sin(ang)
    cos2 = jnp.repeat(cos, 2, axis=1)
    sin2 = jnp.stack([-sin, sin], axis=-1).reshape(DEC_SEQ, D_ROPE)
    cos_t = jnp.pad(cos2, ((0, 0), (D_NOPE, HEAD_PAD - D_QK)), constant_values=1.0)
    sin_t = jnp.pad(sin2, ((0, 0), (D_NOPE, HEAD_PAD - D_QK)))
    cos_t = jnp.concatenate([jnp.ones((tm, HEAD_PAD), F32), cos_t], axis=0)
    sin_t = jnp.concatenate([jnp.zeros((tm, HEAD_PAD), F32), sin_t], axis=0)
    return cos_t, sin_t


def kernel(x_prompt, x_sample, cache_ckv, cache_krope, state_lru, c, c_ctx, norm1, norm2, w_ada, b_ada, w_in, mla_q_norm, mla_kv_norm, mla_w_uq, mla_w_uk, mla_w_uv, mla_q_qknorm, mla_k_qknorm, lru_conv_w, lru_conv_b, lru_w_gate, lru_b_gate, lru_lambda, hy_conv_w, hy_conv_b, hy_w1, hy_b1, hy_freq1, hy_w2, hy_b2, hy_freq2, hy_w3, hy_bias, w_lru_out, w_mla_out, w_hy_out, w_out, ffn_w_gate, ffn_w_up, ffn_w_down, moe_w_router, moe_w_gate, moe_w_up, moe_w_down):
    x = (x_prompt.reshape(M_CTX, D_MODEL), x_sample.reshape(M_LAT, D_MODEL))

    cond = jnp.concatenate([c_ctx[None, :], c, jnp.zeros((SUBLANES - 1 - DEC_BATCH, D_MODEL), F32)], axis=0)
    mod = _adaln(cond, w_ada, b_ada).reshape(DEPTH, SUBLANES, 6, D_MODEL)
    mod = jnp.pad(mod, ((0, 0), (0, 0), (0, SUBLANES - 6), (0, 0)))

    cos_t, sin_t = _rope_tables(TM1)
    ctx_tabs = _ctx_tables()
    lat_tabs = _lat_tables()
    zero_state = jnp.zeros((BATCH, SUBLANES, LRU_W), F32)

    ckv_out, kr_out, st_out = [], [], []
    for l in range(DEPTH):
        wl = w_in[l]
        wkr = jnp.concatenate([jnp.zeros((D_MODEL, D_NOPE), F32), wl[:, 896:928]], axis=1)
        krblk = lambda w: jnp.pad(w, ((0, 0), (0, HEAD_PAD - D_QK)))
        w1 = jnp.concatenate([wl[:, :896], krblk(wkr), krblk(_swap_rope_pairs(wkr)), wl[:, 928:2464]],
                             axis=1).astype(BF16)
        wgates = wl[:, 2464:].astype(BF16)
        wuq = _head_pad_cols(mla_w_uq[l], D_QK).astype(BF16)
        wuqs = _head_pad_cols(_swap_rope_pairs(mla_w_uq[l]), D_QK).astype(BF16)
        wuk = _head_pad_cols(mla_w_uk[l], D_NOPE).astype(BF16)
        wuv = mla_w_uv[l].reshape(KV_RANK, N_HEADS * D_V).astype(BF16)
        gq = _head_gain(mla_q_qknorm[l])
        gk = _head_gain(mla_k_qknorm[l])

        ulru, uhy, ckv, krb, q, k, v = _stage1(
            x, mod[l], norm1[l][None, :], w1, mla_kv_norm[l][None, :], mla_q_norm[l][None, :],
            wuq, wuqs, gq, wuk, gk, wuv, cos_t, sin_t)
        ckv_out.append(ckv[:M_CTX].reshape(BATCH, SEQ, KV_RANK))
        kr_out.append(krb[:M_CTX, D_NOPE:D_QK].reshape(BATCH, SEQ, D_ROPE))

        kc, vc = _kvprep(cache_ckv[:, l].reshape(DEC_BATCH * PAST_LEN, KV_RANK),
                         jnp.pad(cache_krope[:, l].reshape(DEC_BATCH * PAST_LEN, D_ROPE),
                                 ((0, 0), (D_NOPE, HEAD_PAD - D_QK))),
                         wuk, gk, wuv)
        ymla = (_attention_ctx(q, k, v), _attention_lat(q, k, v, kc, vc))

        lp = dict(
            cw=jnp.pad(lru_conv_w[l], ((0, SUBLANES - 4), (0, 0))), cb=lru_conv_b[l][None, :],
            wr=[_block_diag(lru_w_gate[l, d, 0]).astype(BF16) for d in range(2)],
            wi=[_block_diag(lru_w_gate[l, d, 1]).astype(BF16) for d in range(2)],
            br=[lru_b_gate[l, d, 0][None, :] for d in range(2)],
            bi=[lru_b_gate[l, d, 1][None, :] for d in range(2)],
            lam=[lru_lambda[l, d][None, :] for d in range(2)])
        y_c, stf, stb = _lru_mixer(ulru, lp, (zero_state, zero_state), row_off=0, nseq=BATCH, seqlen=SEQ, tc=SEQ)
        st_out.append(jnp.stack([stf, stb], axis=1))
        h0 = [jnp.broadcast_to(state_lru[:, l, d][:, None, :], (DEC_BATCH, SUBLANES, LRU_W)) for d in range(2)]
        y_l, _, _ = _lru_mixer(ulru, lp, h0, row_off=M_CTX, nseq=DEC_BATCH, seqlen=DEC_SEQ, tc=512)
        ylru = (y_c, y_l)

        hp = dict(
            cw=jnp.pad(hy_conv_w[l], ((0, SUBLANES - 3), (0, 0))), cb=hy_conv_b[l][None, :],
            w1=jnp.pad(hy_w1[l], ((0, LANES - HY_EMB), (0, 0))).astype(BF16), b1=hy_b1[l][None, :],
            f1=hy_freq1[l][None, :], w2=hy_w2[l].astype(BF16), b2=hy_b2[l][None, :], f2=hy_freq2[l][None, :],
            w3=hy_w3[l].astype(BF16), bias=hy_bias[l])
        yhy = (_hyena_ctx(uhy, hp, ctx_tabs), _hyena_lat(uhy, hp, lat_tabs))

        xmid, xm2 = _stage3(x, mod[l], norm1[l][None, :], wgates, ylru, ymla, yhy,
                            w_lru_out[l].astype(BF16), w_mla_out[l].astype(BF16), w_hy_out[l].astype(BF16),
                            w_out[l].astype(BF16), norm2[l][None, :])
        j = l // 2
        if l % 2 == 0:
            x = (_ffn_dense(xm2, xmid, mod[l], ffn_w_gate[j].astype(BF16), ffn_w_up[j].astype(BF16),
                            ffn_w_down[j].astype(BF16)),)
        else:
            wr = jnp.pad(moe_w_router[j], ((0, 0), (0, LANES - N_EXPERTS))).astype(BF16)
            x = _ffn_moe(xm2, xmid, mod[l], wr, moe_w_gate[j].astype(BF16), moe_w_up[j].astype(BF16),
                         moe_w_down[j].astype(BF16))

    xc, xl = x if len(x) == 2 else (x[0][:M_CTX], x[0][M_CTX:])
    y_prompt = xc.reshape(BATCH, SEQ, D_MODEL)
    y_sample = xl.reshape(DEC_BATCH, DEC_SEQ, D_MODEL)
    return (y_prompt, y_sample, jnp.stack(ckv_out, axis=1), jnp.stack(kr_out, axis=1), jnp.stack(st_out, axis=1))
```

```python
import functools
import math

import jax
import jax.numpy as jnp
from jax import lax
from jax.experimental import pallas as pl
from jax.experimental.pallas import tpu as pltpu

F32 = jnp.float32
BF16 = jnp.bfloat16

D_MODEL = 1024
BATCH = 32
SEQ = 256
DEPTH = 2
DEC_BATCH = 2
DEC_SEQ = 4096
PAST_LEN = 512
GRID_W = 64
EPS = 1e-6
LRU_W = 512
LRU_BLOCKS = 8
LRU_C = 8.0
N_HEADS = 8
D_NOPE = 64
D_ROPE = 32
D_QK = D_NOPE + D_ROPE
D_V = 64
Q_RANK = 256
KV_RANK = 128
ROPE_BASE = 10000.0
HY_W = 512
HY_ORDER = 2
HY_EMB = 33
HY_HID = 64
HY_FAST_PCT = 0.3
HY_SLOW_PCT = 1.5
D_FF = 2816
N_EXPERTS = 8
D_FF_E = 1408

LANES = 128
SUBLANES = 8
VMEM_LIMIT = 56 * 1024 * 1024

M_CTX = BATCH * SEQ
M_LAT = DEC_BATCH * DEC_SEQ
M_TOK = M_CTX + M_LAT
TM1 = 512
TM3 = 512
TM_FFN = 512
W1_COLS = 2688
HEAD_PAD = LANES
QK_SCALE = math.log2(math.e) / math.sqrt(D_QK)
ATTN_TQ = 256
ATTN_LAT_HEADS = 4
ATTN_LAT_GROUPS = (((0, 0, DEC_SEQ // 2),), ((0, DEC_SEQ // 2, DEC_SEQ), (1, 0, PAST_LEN)))

FFT_N1 = 64
FFT_N2 = 128


def _cparams(sem, vmem=VMEM_LIMIT):
    return pltpu.CompilerParams(dimension_semantics=sem, vmem_limit_bytes=vmem)


def _dot(a, b):
    return jnp.dot(a, b, preferred_element_type=F32)


def _rms(x, g):
    ms = jnp.mean(x * x, axis=-1, keepdims=True)
    return x * lax.rsqrt(ms + EPS) * g


def _sigmoid(x):
    return 1.0 / (1.0 + jnp.exp(-x))


def _ada_kernel(c_ref, w_ref, b_ref, o_ref):
    c = c_ref[...]
    s = (c * _sigmoid(c)).astype(BF16)
    o_ref[...] = _dot(s, w_ref[...].astype(BF16)) + b_ref[...]


def _adaln(cond, w_ada, b_ada):
    tn = 1024
    n6 = 6 * D_MODEL
    return pl.pallas_call(
        _ada_kernel,
        grid=(DEPTH, n6 // tn),
        in_specs=[
            pl.BlockSpec((SUBLANES, D_MODEL), lambda l, j: (0, 0)),
            pl.BlockSpec((None, D_MODEL, tn), lambda l, j: (l, 0, j)),
            pl.BlockSpec((None, 1, tn), lambda l, j: (l, 0, j)),
        ],
        out_specs=pl.BlockSpec((None, SUBLANES, tn), lambda l, j: (l, 0, j)),
        out_shape=jax.ShapeDtypeStruct((DEPTH, SUBLANES, n6), F32),
        compiler_params=_cparams(("arbitrary", "arbitrary")),
        name="adaln",
    )(cond, w_ada, b_ada.reshape(DEPTH, 1, n6))


def _mod_row(i, tm):
    nctx = M_CTX // tm
    per = DEC_SEQ // tm
    return jnp.where(i < nctx, 0, 1 + (i - nctx) // per)


def _rope_blk(i, tm):
    nctx = M_CTX // tm
    per = DEC_SEQ // tm
    return jnp.where(i < nctx, 0, 1 + (i - nctx) % per)


def _finish_head(raw, raw_sw, gc, gs, out_ref, sl):
    ms = jnp.sum(raw * raw, axis=-1, keepdims=True) * (1.0 / D_QK)
    rs = lax.rsqrt(ms + EPS)
    val = raw * gc
    if gs is not None:
        val = val + raw_sw * gs
    out_ref[:, sl] = (val * rs).astype(BF16)


def _read_tokens(x_refs, nctx):
    if len(x_refs) == 1:
        return x_refs[0][...]
    return jnp.where(pl.program_id(0) < nctx, x_refs[0][...], x_refs[1][...])


def _token_specs(x, tm):
    nctx = M_CTX // tm
    if len(x) == 1:
        return [pl.BlockSpec((tm, D_MODEL), lambda i: (i, 0))]
    return [pl.BlockSpec((tm, D_MODEL), lambda i: (jnp.minimum(i, nctx - 1), 0)),
            pl.BlockSpec((tm, D_MODEL), lambda i: (jnp.maximum(i - nctx, 0), 0))]


def _stage1_kernel(*refs, nx, nctx):
    (mod_ref, g1_ref, w1_ref, gkv_ref, gqn_ref, wuq_ref, wuqs_ref, gq_ref, wuk_ref, gk_ref,
     wuv_ref, cos_ref, sin_ref, ulru_ref, uhy_ref, ckv_ref, krb_ref, q_ref, k_ref, v_ref) = refs[nx:]
    x = _read_tokens(refs[:nx], nctx)
    xm = _rms(x, g1_ref[...]) * (1.0 + mod_ref[1:2, :]) + mod_ref[0:1, :]
    xb = xm.astype(BF16)
    ulru_ref[...] = _dot(xb, w1_ref[:, 0:512])
    qc = _dot(xb, w1_ref[:, 512:768])
    ckv = _dot(xb, w1_ref[:, 768:896])
    krb = _dot(xb, w1_ref[:, 896:1024])
    krs = _dot(xb, w1_ref[:, 1024:1152])
    uhy_ref[...] = _dot(xb, w1_ref[:, 1152:2688])
    ckvn = _rms(ckv, gkv_ref[...])
    ckv_ref[...] = ckvn
    krb_ref[...] = krb
    qn = _rms(qc, gqn_ref[...]).astype(BF16)
    cb = ckvn.astype(BF16)
    v_ref[...] = _dot(cb, wuv_ref[...]).astype(BF16)
    cos = cos_ref[...]
    sin = sin_ref[...]
    gcq = cos * (gq_ref[0:1, :] * QK_SCALE)
    gsq = sin * (gq_ref[1:2, :] * QK_SCALE)
    gck = cos * gk_ref[0:1, :]
    gsk = sin * gk_ref[1:2, :]
    for pair in range(N_HEADS // 2):
        ps = slice(2 * HEAD_PAD * pair, 2 * HEAD_PAD * (pair + 1))
        qraw, qsw, kraw = _dot(qn, wuq_ref[:, ps]), _dot(qn, wuqs_ref[:, ps]), _dot(cb, wuk_ref[:, ps])
        for j in range(2):
            hs = slice(HEAD_PAD * j, HEAD_PAD * (j + 1))
            sl = slice(HEAD_PAD * (2 * pair + j), HEAD_PAD * (2 * pair + j + 1))
            _finish_head(qraw[:, hs], qsw[:, hs], gcq, gsq, q_ref, sl)
            _finish_head(kraw[:, hs] + krb, krs, gck, gsk, k_ref, sl)


def _stage1(x, modl, g1, w1, gkv, gqn, wuq, wuqs, gq, wuk, gk, wuv, cos_t, sin_t):
    tm = TM1
    full = lambda shape: pl.BlockSpec(shape, lambda i: (0,) * len(shape))
    row = lambda cols: pl.BlockSpec((tm, cols), lambda i: (i, 0))
    hw = N_HEADS * HEAD_PAD
    return pl.pallas_call(
        functools.partial(_stage1_kernel, nx=len(x), nctx=M_CTX // tm),
        grid=(M_TOK // tm,),
        in_specs=_token_specs(x, tm) + [
            pl.BlockSpec((None, SUBLANES, D_MODEL), lambda i: (_mod_row(i, tm), 0, 0)),
            full((1, D_MODEL)),
            full((D_MODEL, W1_COLS)),
            full((1, KV_RANK)),
            full((1, Q_RANK)),
            full((Q_RANK, hw)),
            full((Q_RANK, hw)),
            full((SUBLANES, HEAD_PAD)),
            full((KV_RANK, hw)),
            full((SUBLANES, HEAD_PAD)),
            full((KV_RANK, N_HEADS * D_V)),
            pl.BlockSpec((tm, LANES), lambda i: (_rope_blk(i, tm), 0)),
            pl.BlockSpec((tm, LANES), lambda i: (_rope_blk(i, tm), 0)),
        ],
        out_specs=[row(LRU_W), row(3 * HY_W), row(KV_RANK), row(LANES), row(hw), row(hw), row(N_HEADS * D_V)],
        out_shape=[
            jax.ShapeDtypeStruct((M_TOK, LRU_W), F32),
            jax.ShapeDtypeStruct((M_TOK, 3 * HY_W), F32),
            jax.ShapeDtypeStruct((M_TOK, KV_RANK), F32),
            jax.ShapeDtypeStruct((M_TOK, LANES), F32),
            jax.ShapeDtypeStruct((M_TOK, hw), BF16),
            jax.ShapeDtypeStruct((M_TOK, hw), BF16),
            jax.ShapeDtypeStruct((M_TOK, N_HEADS * D_V), BF16),
        ],
        compiler_params=_cparams(("arbitrary",)),
        name="stage1",
    )(*x, modl, g1, w1, gkv, gqn, wuq, wuqs, gq, wuk, gk, wuv, cos_t, sin_t)


def _kvprep_kernel(ckv_ref, krb_ref, wuk_ref, gk_ref, wuv_ref, k_ref, v_ref):
    cb = ckv_ref[...].astype(BF16)
    v_ref[...] = _dot(cb, wuv_ref[...]).astype(BF16)
    krb = krb_ref[...]
    for h in range(N_HEADS):
        sl = slice(HEAD_PAD * h, HEAD_PAD * (h + 1))
        _finish_head(_dot(cb, wuk_ref[:, sl]) + krb, None, gk_ref[0:1, :], None, k_ref, sl)


def _kvprep(ckv, krb, wuk, gk, wuv):
    rows = ckv.shape[0]
    tm = TM1
    hw = N_HEADS * HEAD_PAD
    full = lambda shape: pl.BlockSpec(shape, lambda i: (0,) * len(shape))
    row = lambda cols: pl.BlockSpec((tm, cols), lambda i: (i, 0))
    return pl.pallas_call(
        _kvprep_kernel,
        grid=(rows // tm,),
        in_specs=[row(KV_RANK), row(LANES), full((KV_RANK, hw)), full((SUBLANES, HEAD_PAD)),
                  full((KV_RANK, N_HEADS * D_V))],
        out_specs=[row(hw), row(N_HEADS * D_V)],
        out_shape=[jax.ShapeDtypeStruct((rows, hw), BF16), jax.ShapeDtypeStruct((rows, N_HEADS * D_V), BF16)],
        compiler_params=_cparams(("arbitrary",)),
        name="kvprep",
    )(ckv, krb, wuk, gk, wuv)


def _attn_kernel(*refs, heads, nseg, groups):
    q_ref = refs[0]
    k_refs = refs[1:1 + nseg]
    v_refs = refs[1 + nseg:1 + 2 * nseg]
    o_ref = refs[1 + 2 * nseg]
    tq = q_ref.shape[0]
    lane = lax.broadcasted_iota(jnp.int32, (tq, LANES), 1)
    low = lane < D_V
    for pair in range(heads // 2):
        outs = []
        for j in range(2):
            h = 2 * pair + j
            sl = slice(HEAD_PAD * h, HEAD_PAD * (h + 1))
            q = q_ref[:, sl]
            parts = []
            for group in groups:
                s = [lax.dot_general(q, k_refs[seg][r0:r1, sl], (((1,), (1,)), ((), ())),
                                     preferred_element_type=F32) for seg, r0, r1 in group]
                m = jnp.max(s[0], axis=-1, keepdims=True)
                for si in s[1:]:
                    m = jnp.maximum(m, jnp.max(si, axis=-1, keepdims=True))
                acc = None
                den = None
                for si, (seg, r0, r1) in zip(s, group):
                    p = jnp.exp2(si - m)
                    d = jnp.sum(p, axis=-1, keepdims=True)
                    o = _dot(p.astype(BF16), v_refs[seg][r0:r1, LANES * pair:LANES * (pair + 1)])
                    acc = o if acc is None else acc + o
                    den = d if den is None else den + d
                parts.append((m, acc, den))
            if len(parts) == 1:
                _, acc, den = parts[0]
            else:
                mall = parts[0][0]
                for m, _, _ in parts[1:]:
                    mall = jnp.maximum(mall, m)
                acc = None
                den = None
                for m, a, d in parts:
                    w = jnp.exp2(m - mall)
                    acc = w * a if acc is None else acc + w * a
                    den = w * d if den is None else den + w * d
            outs.append(acc / den)
        o_ref[:, LANES * pair:LANES * (pair + 1)] = jnp.where(low, outs[0], outs[1]).astype(BF16)


def _attention_ctx(q, k, v):
    hw = N_HEADS * HEAD_PAD
    vw = N_HEADS * D_V
    return pl.pallas_call(
        functools.partial(_attn_kernel, heads=N_HEADS, nseg=1, groups=(((0, 0, SEQ),),)),
        grid=(BATCH,),
        in_specs=[
            pl.BlockSpec((SEQ, hw), lambda b: (b, 0)),
            pl.BlockSpec((SEQ, hw), lambda b: (b, 0)),
            pl.BlockSpec((SEQ, vw), lambda b: (b, 0)),
        ],
        out_specs=pl.BlockSpec((SEQ, vw), lambda b: (b, 0)),
        out_shape=jax.ShapeDtypeStruct((M_CTX, vw), BF16),
        compiler_params=_cparams(("arbitrary",)),
        name="attn_ctx",
    )(q, k, v)


def _attention_lat(q, k, v, kc, vc):
    tq = ATTN_TQ
    nq = DEC_SEQ // tq
    qoff = M_CTX // tq
    koff = M_CTX // DEC_SEQ
    hp = ATTN_LAT_HEADS
    return pl.pallas_call(
        functools.partial(_attn_kernel, heads=hp, nseg=2, groups=ATTN_LAT_GROUPS),
        grid=(DEC_BATCH, N_HEADS // hp, nq),
        in_specs=[
            pl.BlockSpec((tq, hp * HEAD_PAD), lambda b, p, i: (qoff + b * nq + i, p)),
            pl.BlockSpec((DEC_SEQ, hp * HEAD_PAD), lambda b, p, i: (koff + b, p)),
            pl.BlockSpec((PAST_LEN, hp * HEAD_PAD), lambda b, p, i: (b, p)),
            pl.BlockSpec((DEC_SEQ, hp * D_V), lambda b, p, i: (koff + b, p)),
            pl.BlockSpec((PAST_LEN, hp * D_V), lambda b, p, i: (b, p)),
        ],
        out_specs=pl.BlockSpec((tq, hp * D_V), lambda b, p, i: (b * nq + i, p)),
        out_shape=jax.ShapeDtypeStruct((M_LAT, N_HEADS * D_V), BF16),
        compiler_params=_cparams(("arbitrary", "arbitrary", "arbitrary")),
        name="attn_lat",
    )(q, k, kc, v, vc)


def _lru_kernel(*refs, reverse, tc, nchunks):
    if reverse:
        (up_ref, uc_ref, un_ref, hf_ref, cw_ref, cb_ref, wr_ref, wi_ref, br_ref, bi_ref, lam_ref, h0_ref,
         y_ref, st_ref, ext_sc, a_sc, b_sc, p_sc, h_sc, car_sc) = refs
    else:
        (up_ref, uc_ref, un_ref, cw_ref, cb_ref, wr_ref, wi_ref, br_ref, bi_ref, lam_ref, h0_ref,
         y_ref, st_ref, ext_sc, a_sc, b_sc, p_sc, h_sc, car_sc) = refs
    c = pl.program_id(1)
    chunk = (nchunks - 1 - c) if reverse else c
    prev = jnp.where(chunk == 0, 0.0, up_ref[...])
    nxt = jnp.where(chunk == nchunks - 1, 0.0, un_ref[...])
    ext_sc[0:SUBLANES, :] = prev
    ext_sc[SUBLANES:SUBLANES + tc, :] = uc_ref[...]
    ext_sc[SUBLANES + tc:2 * SUBLANES + tc, :] = nxt
    xc = cb_ref[...]
    for k in range(4):
        xc = xc + cw_ref[k:k + 1, :] * ext_sc[SUBLANES - 2 + k:SUBLANES - 2 + k + tc, :]
    xb = xc.astype(BF16)
    r = _sigmoid(_dot(xb, wr_ref[...]) + br_ref[...])
    gi = _sigmoid(_dot(xb, wi_ref[...]) + bi_ref[...])
    lam = lam_ref[...]
    logsig = -(jnp.maximum(-lam, 0.0) + jnp.log1p(jnp.exp(-jnp.abs(lam))))
    la = LRU_C * r * logsig
    a = jnp.exp(la)
    v = -jnp.tanh(la) * (a * a + 1.0)
    bc = jnp.where(v > 0.0, v * lax.rsqrt(v), 0.0) * (gi * xc)

    @pl.when(c == 0)
    def _():
        car_sc[...] = h0_ref[...]

    nseg = SUBLANES
    sl = tc // nseg
    sp = sl + SUBLANES
    nlb = LRU_W // LANES
    for j in range(nlb):
        for s in range(nseg):
            rows = slice(sl * s, sl * (s + 1))
            dst = slice((j * nseg + s) * sp, (j * nseg + s) * sp + sl)
            a_sc[dst, :] = a[rows, LANES * j:LANES * (j + 1)]
            b_sc[dst, :] = bc[rows, LANES * j:LANES * (j + 1)]

    def body(k, carry):
        i = (sl - 1 - k) if reverse else k
        hs, ps = carry
        hn, pn = [], []
        for j in range(nlb):
            idx = pl.ds(j * nseg * sp + i, nseg, stride=sp)
            av = a_sc[idx, :]
            h = av * hs[j] + b_sc[idx, :]
            p = av * ps[j]
            p_sc[idx, :] = p
            h_sc[idx, :] = h
            hn.append(h)
            pn.append(p)
        return tuple(hn), tuple(pn)

    zero = jnp.zeros((nseg, LANES), F32)
    one = jnp.ones((nseg, LANES), F32)
    hend, pend = lax.fori_loop(0, sl, body, ((zero,) * nlb, (one,) * nlb), unroll=4)

    order = range(nseg - 1, -1, -1) if reverse else range(nseg)
    for j in range(nlb):
        lanes = slice(LANES * j, LANES * (j + 1))
        cin = car_sc[0:1, lanes]
        for s in order:
            rows = slice(sl * s, sl * (s + 1))
            src = slice((j * nseg + s) * sp, (j * nseg + s) * sp + sl)
            h = h_sc[src, :] + p_sc[src, :] * cin
            if reverse:
                y_ref[rows, lanes] = (hf_ref[rows, lanes] + h).astype(BF16)
            else:
                y_ref[rows, lanes] = h
            cin = hend[j][s:s + 1, :] + pend[j][s:s + 1, :] * cin
        car_sc[0:1, lanes] = cin
        st_ref[:, lanes] = jnp.broadcast_to(cin, (SUBLANES, LANES))


def _lru_dir(u, hf, cw, cb, wr, wi, br, bi, lam, h0, *, reverse, row_off, nseq, seqlen, tc):
    nchunks = seqlen // tc
    hb = M_TOK // SUBLANES

    def chunk_of(c):
        return (nchunks - 1 - c) if reverse else c

    def cur(b, c):
        return ((row_off + b * seqlen) // tc + chunk_of(c), 0)

    def prv(b, c):
        return (jnp.maximum((row_off + b * seqlen + chunk_of(c) * tc) // SUBLANES - 1, 0), 0)

    def nxt(b, c):
        return (jnp.minimum((row_off + b * seqlen + (chunk_of(c) + 1) * tc) // SUBLANES, hb - 1), 0)

    def out_cur(b, c):
        return ((b * seqlen) // tc + chunk_of(c), 0)

    full = lambda shape: pl.BlockSpec(shape, lambda b, c: (0,) * len(shape))
    in_specs = [pl.BlockSpec((SUBLANES, LRU_W), prv), pl.BlockSpec((tc, LRU_W), cur),
                pl.BlockSpec((SUBLANES, LRU_W), nxt)]
    args = [u, u, u]
    if reverse:
        in_specs.append(pl.BlockSpec((tc, LRU_W), out_cur))
        args.append(hf)
    in_specs += [full((SUBLANES, LRU_W)), full((1, LRU_W)), full((LRU_W, LRU_W)), full((LRU_W, LRU_W)),
                 full((1, LRU_W)), full((1, LRU_W)), full((1, LRU_W)),
                 pl.BlockSpec((None, SUBLANES, LRU_W), lambda b, c: (b, 0, 0))]
    args += [cw, cb, wr, wi, br, bi, lam, h0]
    return pl.pallas_call(
        functools.partial(_lru_kernel, reverse=reverse, tc=tc, nchunks=nchunks),
        grid=(nseq, nchunks),
        in_specs=in_specs,
        out_specs=[pl.BlockSpec((tc, LRU_W), out_cur),
                   pl.BlockSpec((None, SUBLANES, LRU_W), lambda b, c: (b, 0, 0))],
        out_shape=[jax.ShapeDtypeStruct((nseq * seqlen, LRU_W), BF16 if reverse else F32),
                   jax.ShapeDtypeStruct((nseq, SUBLANES, LRU_W), F32)],
        scratch_shapes=[pltpu.VMEM((tc + 2 * SUBLANES, LRU_W), F32)]
        + [pltpu.VMEM(((LRU_W // LANES) * (tc + SUBLANES * SUBLANES), LANES), F32)] * 4
        + [pltpu.VMEM((SUBLANES, LRU_W), F32)],
        compiler_params=_cparams(("arbitrary", "arbitrary")),
        name="lru_bwd" if reverse else "lru_fwd",
    )(*args)


def _lru_mixer(u, p, h0, *, row_off, nseq, seqlen, tc):
    kw = dict(row_off=row_off, nseq=nseq, seqlen=seqlen, tc=tc)
    hf, stf = _lru_dir(u, None, p["cw"], p["cb"], p["wr"][0], p["wi"][0], p["br"][0], p["bi"][0], p["lam"][0],
                       h0[0], reverse=False, **kw)
    y, stb = _lru_dir(u, hf, p["cw"], p["cb"], p["wr"][1], p["wi"][1], p["br"][1], p["bi"][1], p["lam"][1],
                      h0[1], reverse=True, **kw)
    return y, stf[:, 0, :], stb[:, 0, :]


def _shortconv_kernel(up_ref, uc_ref, un_ref, cw_ref, cb_ref, v_ref, x1_ref, x2_ref, ext_sc, *, tc, nchunks):
    c = pl.program_id(1)
    prev = jnp.where(c == 0, 0.0, up_ref[...])
    nxt = jnp.where(c == nchunks - 1, 0.0, un_ref[...])
    ext_sc[0:SUBLANES, :] = prev
    ext_sc[SUBLANES:SUBLANES + tc, :] = uc_ref[...]
    ext_sc[SUBLANES + tc:2 * SUBLANES + tc, :] = nxt
    for part, o_ref in enumerate((v_ref, x1_ref, x2_ref)):
        cs = slice(HY_W * part, HY_W * (part + 1))
        acc = cb_ref[:, cs]
        for k in range(3):
            acc = acc + cw_ref[k:k + 1, cs] * ext_sc[SUBLANES - 1 + k:SUBLANES - 1 + k + tc, cs]
        o_ref[...] = acc


def _shortconv(u, cw, cb, *, row_off, nseq, seqlen, tc):
    nchunks = seqlen // tc
    w = 3 * HY_W
    hb = M_TOK // SUBLANES
    cur = lambda b, c: ((row_off + b * seqlen) // tc + c, 0)
    prv = lambda b, c: (jnp.maximum((row_off + b * seqlen + c * tc) // SUBLANES - 1, 0), 0)
    nxt = lambda b, c: (jnp.minimum((row_off + b * seqlen + (c + 1) * tc) // SUBLANES, hb - 1), 0)
    out = lambda b, c: ((b * seqlen) // tc + c, 0)
    full = lambda shape: pl.BlockSpec(shape, lambda b, c: (0,) * len(shape))
    rows = nseq * seqlen
    return pl.pallas_call(
        functools.partial(_shortconv_kernel, tc=tc, nchunks=nchunks),
        grid=(nseq, nchunks),
        in_specs=[pl.BlockSpec((SUBLANES, w), prv), pl.BlockSpec((tc, w), cur), pl.BlockSpec((SUBLANES, w), nxt),
                  full((SUBLANES, w)), full((1, w))],
        out_specs=[pl.BlockSpec((tc, HY_W), out)] * 3,
        out_shape=[jax.ShapeDtypeStruct((rows, HY_W), F32)] * 3,
        scratch_shapes=[pltpu.VMEM((tc + 2 * SUBLANES, w), F32)],
        compiler_params=_cparams(("arbitrary", "arbitrary")),
        name="hy_shortconv",
    )(u, u, u, cw, cb)


def _hyfilt_kernel(z_ref, t_ref, w1_ref, b1_ref, f1_ref, w2_ref, b2_ref, f2_ref, w3_ref, ad_ref, h_ref, s_ref):
    i = pl.program_id(0)
    z = z_ref[...].astype(BF16)
    h = jnp.sin(f1_ref[...] * (_dot(z, w1_ref[...]) + b1_ref[...]))
    h = jnp.sin(f2_ref[...] * (_dot(h.astype(BF16), w2_ref[...]) + b2_ref[...]))
    h = _dot(h.astype(BF16), w3_ref[...])
    t = t_ref[...]
    ncol = h.shape[1] // LANES
    win = jnp.concatenate([jnp.exp(-t * ad_ref[:, LANES * j:LANES * (j + 1)]) for j in range(ncol)], axis=1)
    h = h * win
    h_ref[...] = h

    @pl.when(i == 0)
    def _():
        s_ref[...] = jnp.zeros_like(s_ref)

    s_ref[0:1, :] = s_ref[0:1, :] + jnp.sum(jnp.abs(h), axis=0, keepdims=True)


def _hyfilt(feats, tcol, w1, b1, f1, w2, b2, f2, w3, absdelta):
    L = feats.shape[0]
    tl = min(L, 512)
    wcols = HY_ORDER * 2 * HY_W
    full = lambda shape: pl.BlockSpec(shape, lambda i: (0,) * len(shape))
    return pl.pallas_call(
        _hyfilt_kernel,
        grid=(L // tl,),
        in_specs=[pl.BlockSpec((tl, LANES), lambda i: (i, 0)), pl.BlockSpec((tl, LANES), lambda i: (i, 0)),
                  full((LANES, HY_HID)), full((1, HY_HID)), full((1, HY_HID)),
                  full((HY_HID, HY_HID)), full((1, HY_HID)), full((1, HY_HID)),
                  full((HY_HID, wcols)), full((1, wcols))],
        out_specs=[pl.BlockSpec((tl, wcols), lambda i: (i, 0)), full((SUBLANES, wcols))],
        out_shape=[jax.ShapeDtypeStruct((L, wcols), F32), jax.ShapeDtypeStruct((SUBLANES, wcols), F32)],
        compiler_params=_cparams(("arbitrary",)),
        name="hy_filter",
    )(feats, tcol, w1, b1, f1, w2, b2, f2, w3, absdelta)


def _combine_spectrum(zr, zi, s_ref, hr_out, hi_out):
    for o in range(HY_ORDER):
        f = slice(2 * HY_W * o, 2 * HY_W * o + HY_W)
        b = slice(2 * HY_W * o + HY_W, 2 * HY_W * (o + 1))
        den = s_ref[0:1, f] + s_ref[0:1, b] + EPS
        hr_out(o, (zr[:, f] + zr[:, b]) / den)
        hi_out(o, (zi[:, f] - zi[:, b]) / den)


def _ctx_spec_kernel(f_ref, h_ref, s_ref, o_ref):
    n = f_ref.shape[0] // 2
    z = _dot(f_ref[...], h_ref[...].astype(BF16))
    zr, zi = z[:n], z[n:]

    def put_r(o, val):
        o_ref[0, :, HY_W * o:HY_W * (o + 1)] = val

    def put_i(o, val):
        o_ref[1, :, HY_W * o:HY_W * (o + 1)] = val

    _combine_spectrum(zr, zi, s_ref, put_r, put_i)


def _ctx_spectrum(fmat, hdec, s):
    n = fmat.shape[0] // 2
    return pl.pallas_call(
        _ctx_spec_kernel,
        out_shape=jax.ShapeDtypeStruct((2, n, HY_ORDER * HY_W), F32),
        compiler_params=pltpu.CompilerParams(vmem_limit_bytes=VMEM_LIMIT),
        name="hy_ctx_spectrum",
    )(fmat, hdec, s)


def _ctx_conv_kernel(z_ref, x_ref, f_ref, fi_ref, h_ref, bias_ref, o_ref, *, nb, seqlen):
    n = f_ref.shape[0] // 2
    hr = h_ref[0]
    hi = h_ref[1]
    for b in range(nb):
        rs = slice(seqlen * b, seqlen * (b + 1))
        zt = z_ref[rs, :]
        zf = _dot(f_ref[...], zt.astype(BF16))
        zr, zi = zf[:n], zf[n:]
        y = jnp.concatenate([zr * hr - zi * hi, zr * hi + zi * hr], axis=0).astype(BF16)
        conv = _dot(fi_ref[...], y)
        o_ref[rs, :] = (x_ref[rs, :] * (conv + zt * bias_ref[...])).astype(o_ref.dtype)


def _ctx_conv(z, xg, fmat, finv, hspec, bias, order, out_dtype):
    nb = 4
    n = fmat.shape[0] // 2
    rows = nb * SEQ
    return pl.pallas_call(
        functools.partial(_ctx_conv_kernel, nb=nb, seqlen=SEQ),
        grid=(BATCH // nb,),
        in_specs=[pl.BlockSpec((rows, HY_W), lambda i: (i, 0)), pl.BlockSpec((rows, HY_W), lambda i: (i, 0)),
                  pl.BlockSpec(fmat.shape, lambda i: (0, 0)), pl.BlockSpec(finv.shape, lambda i: (0, 0)),
                  pl.BlockSpec((2, n, HY_W), lambda i: (0, 0, order)),
                  pl.BlockSpec((1, HY_W), lambda i: (0, 0))],
        out_specs=pl.BlockSpec((rows, HY_W), lambda i: (i, 0)),
        out_shape=jax.ShapeDtypeStruct((M_CTX, HY_W), out_dtype),
        compiler_params=_cparams(("arbitrary",)),
        name="hy_ctx_conv",
    )(z, xg, fmat, finv, hspec, bias)


K1U = FFT_N1 // 2 + 1
SLABS = 72
PITCH = FFT_N2 + SUBLANES
NROW1 = FFT_N1 // 2
LAT_UNROLL_R = 16
LAT_UNROLL_K = 11


def _pitch_copy_in(src_ref, col, dst_sc):
    for n1 in range(NROW1):
        dst_sc[PITCH * n1:PITCH * n1 + FFT_N2, :] = src_ref[FFT_N2 * n1:FFT_N2 * (n1 + 1), col]


def _dft_stage_a(zp_scs, wa_ref, a_scs):
    def body(r, c):
        x = jnp.concatenate([zp_sc[pl.ds(r, NROW1, stride=PITCH), :] for zp_sc in zp_scs], axis=1)
        a = _dot(wa_ref[...], x.astype(BF16))
        for j, a_sc in enumerate(a_scs):
            a_sc[pl.ds(r, SLABS, stride=PITCH), :] = a[:, LANES * j:LANES * (j + 1)]
        return c

    lax.fori_loop(0, FFT_N2, body, 0, unroll=LAT_UNROLL_R)


def _load_k1(a_sc, k1):
    base = pl.multiple_of(k1 * (2 * PITCH), SUBLANES)
    a = jnp.concatenate([a_sc[pl.ds(base, FFT_N2), :], a_sc[pl.ds(base + PITCH, FFT_N2), :]], axis=0)
    return base, a.astype(BF16)


def _lat_spec_kernel(h_ref, s_ref, wa_ref, g_ref, o_ref, hf_sc, hb_sc, af_sc, ab_sc):
    _pitch_copy_in(h_ref, slice(0, LANES), hf_sc)
    _pitch_copy_in(h_ref, slice(LANES, 2 * LANES), hb_sc)
    _dft_stage_a((hf_sc, hb_sc), wa_ref, (af_sc, ab_sc))
    den = s_ref[0:1, 0:LANES] + s_ref[0:1, LANES:2 * LANES] + EPS

    def kbody(k1, c):
        _, af = _load_k1(af_sc, k1)
        _, ab = _load_k1(ab_sc, k1)
        z = _dot(g_ref[k1], jnp.concatenate([af, ab], axis=1))
        zf, zb = z[:, :LANES], z[:, LANES:]
        o_ref[0, k1] = (zf[:FFT_N2] + zb[:FFT_N2]) / den
        o_ref[1, k1] = (zf[FFT_N2:] - zb[FFT_N2:]) / den
        return c

    lax.fori_loop(0, K1U, kbody, 0, unroll=LAT_UNROLL_K)


def _lat_spectrum(hdec, s, wa, gtab):
    nblk = HY_ORDER * HY_W // LANES
    slab = NROW1 * PITCH
    return pl.pallas_call(
        _lat_spec_kernel,
        grid=(nblk,),
        in_specs=[pl.BlockSpec((DEC_SEQ, 2 * LANES), lambda i: (0, i)),
                  pl.BlockSpec((SUBLANES, 2 * LANES), lambda i: (0, i)),
                  pl.BlockSpec(wa.shape, lambda i: (0, 0)),
                  pl.BlockSpec(gtab.shape, lambda i: (0, 0, 0))],
        out_specs=pl.BlockSpec((2, K1U, FFT_N2, LANES), lambda i: (0, 0, 0, i)),
        out_shape=jax.ShapeDtypeStruct((2, K1U, FFT_N2, HY_ORDER * HY_W), F32),
        scratch_shapes=[pltpu.VMEM((slab, LANES), F32), pltpu.VMEM((slab, LANES), F32),
                        pltpu.VMEM((SLABS * PITCH, LANES), F32), pltpu.VMEM((SLABS * PITCH, LANES), F32)],
        compiler_params=_cparams(("arbitrary",)),
        name="hy_lat_spectrum",
    )(hdec, s, wa, gtab)


def _lat_conv_kernel(z_ref, x_ref, wa_ref, wai_ref, g_ref, gi_ref, h_ref, bias_ref, o_ref,
                     zp_sc, xp_sc, op_sc, a_sc):
    full = slice(None)
    _pitch_copy_in(z_ref, full, zp_sc)
    _pitch_copy_in(x_ref, full, xp_sc)
    _dft_stage_a((zp_sc,), wa_ref, (a_sc,))

    def kbody(k1, c):
        base, a = _load_k1(a_sc, k1)
        z = _dot(g_ref[k1], a)
        zr, zi = z[:FFT_N2], z[FFT_N2:]
        hr = h_ref[0, k1]
        hi = h_ref[1, k1]
        y = jnp.concatenate([zr * hr - zi * hi, zr * hi + zi * hr], axis=0).astype(BF16)
        bp = _dot(gi_ref[k1], y)
        a_sc[pl.ds(base, FFT_N2), :] = bp[:FFT_N2]
        a_sc[pl.ds(base + PITCH, FFT_N2), :] = bp[FFT_N2:]
        return c

    lax.fori_loop(0, K1U, kbody, 0, unroll=LAT_UNROLL_K)
    bias = bias_ref[...]

    def rbody(r, c):
        yb = a_sc[pl.ds(r, SLABS, stride=PITCH), :].astype(BF16)
        conv = _dot(wai_ref[...], yb)
        zz = zp_sc[pl.ds(r, NROW1, stride=PITCH), :]
        xx = xp_sc[pl.ds(r, NROW1, stride=PITCH), :]
        op_sc[pl.ds(r, NROW1, stride=PITCH), :] = xx * (conv + zz * bias)
        return c

    lax.fori_loop(0, FFT_N2, rbody, 0, unroll=LAT_UNROLL_R)
    for n1 in range(NROW1):
        o_ref[FFT_N2 * n1:FFT_N2 * (n1 + 1), :] = op_sc[PITCH * n1:PITCH * n1 + FFT_N2, :]


def _lat_conv(z, xg, wa, wainv, gtab, gitab, hspec, bias, order):
    ncb = HY_W // LANES
    slab = NROW1 * PITCH
    blk = pl.BlockSpec((DEC_SEQ, LANES), lambda cb, b: (b, cb))
    const = lambda a: pl.BlockSpec(a.shape, lambda cb, b: (0,) * a.ndim)
    return pl.pallas_call(
        _lat_conv_kernel,
        grid=(ncb, DEC_BATCH),
        in_specs=[blk, blk, const(wa), const(wainv), const(gtab), const(gitab),
                  pl.BlockSpec((2, K1U, FFT_N2, LANES), lambda cb, b: (0, 0, 0, order * ncb + cb)),
                  pl.BlockSpec((1, LANES), lambda cb, b: (0, cb))],
        out_specs=blk,
        out_shape=jax.ShapeDtypeStruct((M_LAT, HY_W), F32),
        scratch_shapes=[pltpu.VMEM((slab, LANES), F32), pltpu.VMEM((slab, LANES), F32),
                        pltpu.VMEM((slab, LANES), F32), pltpu.VMEM((SLABS * PITCH, LANES), F32)],
        compiler_params=_cparams(("arbitrary", "arbitrary")),
        name="hy_lat_conv",
    )(z, xg, wa, wainv, gtab, gitab, hspec, bias)


def _angle(m, n):
    return (m % n).astype(F32) * (2.0 * math.pi / n)


def _ctx_tables():
    n = 2 * SEQ
    nf = SEQ + SUBLANES
    k = jnp.arange(nf, dtype=jnp.int32)[:, None]
    t = jnp.arange(SEQ, dtype=jnp.int32)[None, :]
    th = _angle(k * t, n)
    live = (k <= SEQ).astype(F32)
    wgt = jnp.where((k == 0) | (k == SEQ), 1.0, 2.0) * live / n
    fmat = jnp.concatenate([live * jnp.cos(th), -live * jnp.sin(th)], axis=0)
    finv = jnp.concatenate([(wgt * jnp.cos(th)).T, (-wgt * jnp.sin(th)).T], axis=1)
    return fmat.astype(BF16), finv.astype(BF16)


def _lat_tables():
    n1, n2 = FFT_N1, FFT_N2
    n = n1 * n2
    k1 = jnp.arange(K1U, dtype=jnp.int32)
    th1 = _angle(k1[:, None] * jnp.arange(NROW1, dtype=jnp.int32)[None, :], n1)
    wa = jnp.stack([jnp.cos(th1), -jnp.sin(th1)], axis=1).reshape(2 * K1U, NROW1)
    wa = jnp.pad(wa, ((0, SLABS - 2 * K1U), (0, 0)))
    wgt = jnp.where((k1 == 0) | (k1 == n1 // 2), 1.0, 2.0)[:, None] / n
    wainv = jnp.stack([wgt * jnp.cos(th1), -wgt * jnp.sin(th1)], axis=1).reshape(2 * K1U, NROW1).T
    wainv = jnp.pad(wainv, ((0, 0), (0, SLABS - 2 * K1U)))
    k2 = jnp.arange(n2, dtype=jnp.int32)
    nn2 = jnp.arange(n2, dtype=jnp.int32)
    ta = _angle(k1[:, None] * nn2[None, :], n)[:, None, :]
    tb = _angle(k2[:, None] * nn2[None, :], n2)[None, :, :]
    ca, sa, cb_, sb = jnp.cos(ta), jnp.sin(ta), jnp.cos(tb), jnp.sin(tb)
    gr = ca * cb_ - sa * sb
    gi = -(sa * cb_ + ca * sb)
    g = jnp.concatenate([jnp.concatenate([gr, -gi], axis=2), jnp.concatenate([gi, gr], axis=2)],
                        axis=1).astype(BF16)
    ginv = jnp.swapaxes(g, 1, 2)
    return wa.astype(BF16), wainv.astype(BF16), g, ginv


def _filter_features(L):
    t = jnp.linspace(0.0, 1.0, L, dtype=F32)[:, None]
    bands = (HY_EMB - 1) // 2
    w = (2.0 * math.pi / L) * jnp.arange(L, dtype=F32)[:, None]
    f = jnp.linspace(1e-4, bands - 1, bands, dtype=F32)[None, :]
    z = jnp.concatenate([t, jnp.cos(f * w), -jnp.sin(f * w)], axis=-1)
    z = jnp.pad(z, ((0, 0), (0, LANES - HY_EMB)))
    return z, jnp.broadcast_to(t, (L, LANES))


def _hyena_filter(p, L, blocked):
    feats, tcol = _filter_features(L)
    deltas = jnp.linspace(math.log(1e-2) / HY_FAST_PCT, math.log(1e-2) / HY_SLOW_PCT, HY_W, dtype=F32)
    absdelta = jnp.tile(jnp.abs(deltas), HY_ORDER * 2)[None, :]
    w3 = p["w3"]
    if blocked:
        reorder = lambda a: a.reshape(a.shape[0], HY_ORDER, 2, HY_W // LANES, LANES).transpose(
            0, 1, 3, 2, 4).reshape(a.shape[0], HY_ORDER * 2 * HY_W)
        w3, absdelta = reorder(w3), reorder(absdelta)
    return _hyfilt(feats, tcol, p["w1"], p["b1"], p["f1"], p["w2"], p["b2"], p["f2"], w3, absdelta)


def _hyena_ctx(u_hy, p, tabs):
    fmat, finv = tabs
    v, x1, x2 = _shortconv(u_hy, p["cw"], p["cb"], row_off=0, nseq=BATCH, seqlen=SEQ, tc=SEQ)
    hdec, s = _hyena_filter(p, SEQ, False)
    hspec = _ctx_spectrum(fmat, hdec, s)
    z = _ctx_conv(v, x1, fmat, finv, hspec, p["bias"][0:1], 0, F32)
    return _ctx_conv(z, x2, fmat, finv, hspec, p["bias"][1:2], 1, F32)


def _hyena_lat(u_hy, p, tabs):
    wa, wainv, gtab, gitab = tabs
    v, x1, x2 = _shortconv(u_hy, p["cw"], p["cb"], row_off=M_CTX, nseq=DEC_BATCH, seqlen=DEC_SEQ, tc=512)
    hdec, s = _hyena_filter(p, DEC_SEQ, True)
    hspec = _lat_spectrum(hdec, s, wa, gtab)
    z = _lat_conv(v, x1, wa, wainv, gtab, gitab, hspec, p["bias"][0:1], 0)
    return _lat_conv(z, x2, wa, wainv, gtab, gitab, hspec, p["bias"][1:2], 1)


def _stage3_kernel(*refs, nx, nctx):
    (mod_ref, g1_ref, wg_ref, ylc_ref, yll_ref, ymc_ref, yml_ref, yhc_ref, yhl_ref,
     wl_ref, wm_ref, wh_ref, wo_ref, g2_ref, xo_ref, xm2_ref) = refs[nx:]
    is_ctx = pl.program_id(0) < nctx
    x = _read_tokens(refs[:nx], nctx)
    xm = _rms(x, g1_ref[...]) * (1.0 + mod_ref[1:2, :]) + mod_ref[0:1, :]
    xb = xm.astype(BF16)
    merged = None
    branches = ((ylc_ref, yll_ref, wl_ref), (ymc_ref, yml_ref, wm_ref), (yhc_ref, yhl_ref, wh_ref))
    for bidx, (yc_ref, yl_ref, w_ref) in enumerate(branches):
        gate = _sigmoid(_dot(xb, wg_ref[:, D_MODEL * bidx:D_MODEL * (bidx + 1)]))
        y = jnp.where(is_ctx, yc_ref[...], yl_ref[...]).astype(BF16)
        term = gate * _dot(y, w_ref[...])
        merged = term if merged is None else merged + term
    xo = x + mod_ref[2:3, :] * _dot(merged.astype(BF16), wo_ref[...])
    xo_ref[...] = xo
    xm2 = _rms(xo, g2_ref[...]) * (1.0 + mod_ref[4:5, :]) + mod_ref[3:4, :]
    xm2_ref[...] = xm2.astype(BF16)


def _stage3(x, modl, g1, wg, ylru, ymla, yhy, wl, wm, wh, wo, g2):
    tm = TM3
    nctx = M_CTX // tm
    full = lambda shape: pl.BlockSpec(shape, lambda i: (0,) * len(shape))
    row = lambda cols: pl.BlockSpec((tm, cols), lambda i: (i, 0))
    ctx = lambda cols: pl.BlockSpec((tm, cols), lambda i: (jnp.minimum(i, nctx - 1), 0))
    lat = lambda cols: pl.BlockSpec((tm, cols), lambda i: (jnp.maximum(i - nctx, 0), 0))
    return pl.pallas_call(
        functools.partial(_stage3_kernel, nx=len(x), nctx=nctx),
        grid=(M_TOK // tm,),
        in_specs=_token_specs(x, tm) + [
                  pl.BlockSpec((None, SUBLANES, D_MODEL), lambda i: (_mod_row(i, tm), 0, 0)),
                  full((1, D_MODEL)), full((D_MODEL, 3 * D_MODEL)),
                  ctx(LRU_W), lat(LRU_W), ctx(N_HEADS * D_V), lat(N_HEADS * D_V), ctx(HY_W), lat(HY_W),
                  full((LRU_W, D_MODEL)), full((N_HEADS * D_V, D_MODEL)), full((HY_W, D_MODEL)),
                  full((D_MODEL, D_MODEL)), full((1, D_MODEL))],
        out_specs=[row(D_MODEL), row(D_MODEL)],
        out_shape=[jax.ShapeDtypeStruct((M_TOK, D_MODEL), F32), jax.ShapeDtypeStruct((M_TOK, D_MODEL), BF16)],
        compiler_params=_cparams(("arbitrary",)),
        name="stage3",
    )(*x, modl, g1, wg, *ylru, *ymla, *yhy, wl, wm, wh, wo, g2)


def _ffn_kernel(xm_ref, x_ref, mod_ref, wg_ref, wu_ref, wd_ref, o_ref, acc_sc, *, nchunks):
    j = pl.program_id(1)
    xb = xm_ref[...]
    g = _dot(xb, wg_ref[...])
    u = _dot(xb, wu_ref[...])
    hid = (g * _sigmoid(g) * u).astype(BF16)
    part = _dot(hid, wd_ref[...])

    @pl.when(j == 0)
    def _():
        acc_sc[...] = part

    @pl.when(j > 0)
    def _():
        acc_sc[...] = acc_sc[...] + part

    @pl.when(j == nchunks - 1)
    def _():
        o_ref[...] = x_ref[...] + mod_ref[5:6, :] * acc_sc[...]


def _ffn_dense(xm2, x, modl, wg, wu, wd):
    tm = TM_FFN
    nchunks = 1
    cw = D_FF // nchunks
    return pl.pallas_call(
        functools.partial(_ffn_kernel, nchunks=nchunks),
        grid=(M_TOK // tm, nchunks),
        in_specs=[pl.BlockSpec((tm, D_MODEL), lambda i, j: (i, 0)),
                  pl.BlockSpec((tm, D_MODEL), lambda i, j: (i, 0)),
                  pl.BlockSpec((None, SUBLANES, D_MODEL), lambda i, j: (_mod_row(i, tm), 0, 0)),
                  pl.BlockSpec((D_MODEL, cw), lambda i, j: (0, j)),
                  pl.BlockSpec((D_MODEL, cw), lambda i, j: (0, j)),
                  pl.BlockSpec((cw, D_MODEL), lambda i, j: (j, 0))],
        out_specs=pl.BlockSpec((tm, D_MODEL), lambda i, j: (i, 0)),
        out_shape=jax.ShapeDtypeStruct((M_TOK, D_MODEL), F32),
        scratch_shapes=[pltpu.VMEM((tm, D_MODEL), F32)],
        compiler_params=_cparams(("arbitrary", "arbitrary")),
        name="ffn_dense",
    )(xm2, x, modl, wg, wu, wd)


MOE_TILE = 256
MOE_ROWS = 2 * M_TOK + N_EXPERTS * MOE_TILE
MOE_TILES = MOE_ROWS // MOE_TILE
MOE_PAD_ROWS = MOE_ROWS - 2 * M_TOK
MOE_SEG = D_MODEL // LANES


def _to_row_tiles(val, ref, base):
    n = val.shape[0]
    for j in range(MOE_SEG):
        ref[pl.ds(base + j, n, stride=MOE_SEG), :] = val[:, LANES * j:LANES * (j + 1)]


def _from_row_tiles(ref, base, n):
    return jnp.concatenate([ref[pl.ds(base + j, n, stride=MOE_SEG), :] for j in range(MOE_SEG)], axis=1)


def _router_kernel(xm_ref, wr_ref, tri_ref, sel_ref, xp_ref, cnt_ref, base_sc):
    i = pl.program_id(0)
    xb = xm_ref[...]
    tm = xb.shape[0]
    lane = lax.broadcasted_iota(jnp.int32, (tm, LANES), 1)

    @pl.when(i == 0)
    def _():
        base_sc[...] = jnp.zeros_like(base_sc)

    logits = jnp.where(lane < N_EXPERTS, _dot(xb, wr_ref[...]), -1e30)
    mx = jnp.max(logits, axis=-1, keepdims=True)
    ex = jnp.exp(logits - mx)
    probs = ex / jnp.sum(ex, axis=-1, keepdims=True)
    p1 = jnp.max(probs, axis=-1, keepdims=True)
    i1 = jnp.min(jnp.where(probs == p1, lane, LANES), axis=-1, keepdims=True)
    rest = jnp.where(lane == i1, -1.0, probs)
    p2 = jnp.max(rest, axis=-1, keepdims=True)
    i2 = jnp.min(jnp.where(rest == p2, lane, LANES), axis=-1, keepdims=True)
    tot = p1 + p2
    oh1 = lane == i1
    oh2 = lane == i2
    oh = jnp.where(oh1 | oh2, 1.0, 0.0)
    before = base_sc[0:1, :] + _dot(tri_ref[...], oh.astype(BF16))
    r1 = jnp.sum(jnp.where(oh1, before, 0.0), axis=-1, keepdims=True)
    r2 = jnp.sum(jnp.where(oh2, before, 0.0), axis=-1, keepdims=True)
    base_sc[0:1, :] = base_sc[0:1, :] + jnp.sum(oh, axis=0, keepdims=True)
    cnt_ref[...] = base_sc[...]
    sel = jnp.where(lane == 0, p1 / tot, 0.0) + jnp.where(lane == 1, p2 / tot, 0.0)
    sel = sel + jnp.where(lane == 2, i1.astype(F32), 0.0) + jnp.where(lane == 3, i2.astype(F32), 0.0)
    sel_ref[...] = sel + jnp.where(lane == 4, r1, 0.0) + jnp.where(lane == 5, r2, 0.0)
    _to_row_tiles(xb.astype(F32), xp_ref, 0)


def _moe_router(xm2, wr):
    tm = TM_FFN
    tri = jnp.tril(jnp.ones((tm, tm), F32), -1).astype(BF16)
    return pl.pallas_call(
        _router_kernel,
        grid=(M_TOK // tm,),
        in_specs=[pl.BlockSpec((tm, D_MODEL), lambda i: (i, 0)),
                  pl.BlockSpec((D_MODEL, LANES), lambda i: (0, 0)),
                  pl.BlockSpec((tm, tm), lambda i: (0, 0))],
        out_specs=[pl.BlockSpec((tm, LANES), lambda i: (i, 0)),
                   pl.BlockSpec((tm * MOE_SEG, LANES), lambda i: (i, 0)),
                   pl.BlockSpec((SUBLANES, LANES), lambda i: (0, 0))],
        out_shape=[jax.ShapeDtypeStruct((M_TOK, LANES), F32),
                   jax.ShapeDtypeStruct((M_TOK * MOE_SEG, LANES), F32),
                   jax.ShapeDtypeStruct((SUBLANES, LANES), F32)],
        scratch_shapes=[pltpu.VMEM((SUBLANES, LANES), F32)],
        compiler_params=_cparams(("arbitrary",)),
        name="moe_router",
    )(xm2, wr, tri)


def _row_copy(src, srow8, dst, drow8, sem):
    aligned = lambda r: r if isinstance(r, int) else pl.multiple_of(r, MOE_SEG)
    return pltpu.make_async_copy(src.at[pl.ds(aligned(srow8), MOE_SEG), :],
                                 dst.at[pl.ds(aligned(drow8), MOE_SEG), :], sem)


def _dispatch_kernel(pos_ref, pad_ref, xp_ref, xs_ref, ring_sc, zero_sc, sem, *, nsteps):
    i = pl.program_id(0)
    slot = i % 2
    nrow = MOE_TILE
    slot_rows = nrow * MOE_SEG

    def wait_slot(s):
        for _ in range(2):
            pltpu.make_async_copy(ring_sc.at[s], xs_ref.at[pl.ds(0, slot_rows), :], sem.at[s]).wait()

    @pl.when(i >= 2)
    def _():
        wait_slot(slot)

    ring_sc[slot] = xp_ref[...]

    def body(t, c):
        _row_copy(ring_sc.at[slot], t * MOE_SEG, xs_ref, pos_ref[0, 2 * t], sem.at[slot]).start(priority=0)
        _row_copy(ring_sc.at[slot], t * MOE_SEG, xs_ref, pos_ref[0, 2 * t + 1], sem.at[slot]).start(priority=1)
        return c

    lax.fori_loop(0, nrow, body, 0, unroll=4)

    @pl.when(i == nsteps - 1)
    def _():
        zero_sc[...] = jnp.zeros_like(zero_sc)

        def zbody(t, c):
            _row_copy(zero_sc, 0, xs_ref, pad_ref[t], sem.at[2]).start()
            return c

        lax.fori_loop(0, MOE_PAD_ROWS, zbody, 0, unroll=4)
        wait_slot(1 - slot)
        wait_slot(slot)

        for _ in range(MOE_PAD_ROWS // nrow):
            pltpu.make_async_copy(ring_sc.at[0], xs_ref.at[pl.ds(0, slot_rows), :], sem.at[2]).wait()


def _moe_dispatch(xp, pos, padrows):
    nsteps = M_TOK // MOE_TILE
    return pl.pallas_call(
        functools.partial(_dispatch_kernel, nsteps=nsteps),
        grid=(nsteps,),
        in_specs=[pl.BlockSpec((None, 1, 2 * MOE_TILE), lambda i: (i, 0, 0), memory_space=pltpu.SMEM),
                  pl.BlockSpec(memory_space=pltpu.SMEM),
                  pl.BlockSpec((MOE_TILE * MOE_SEG, LANES), lambda i: (i, 0))],
        out_specs=pl.BlockSpec(memory_space=pl.ANY),
        out_shape=jax.ShapeDtypeStruct((MOE_ROWS * MOE_SEG, LANES), F32),
        scratch_shapes=[pltpu.VMEM((2, MOE_TILE * MOE_SEG, LANES), F32), pltpu.VMEM((MOE_SEG, LANES), F32),
                        pltpu.SemaphoreType.DMA((3,))],
        compiler_params=_cparams(("arbitrary",)),
        name="moe_dispatch",
    )(pos, padrows, xp)


def _experts_kernel(te_ref, nu_ref, xs_ref, wg_ref, wu_ref, wd_ref, ys_ref):
    i = pl.program_id(0)

    @pl.when(i < nu_ref[0])
    def _():
        xb = _from_row_tiles(xs_ref, 0, MOE_TILE).astype(BF16)
        g = _dot(xb, wg_ref[...])
        u = _dot(xb, wu_ref[...])
        hid = (g * _sigmoid(g) * u).astype(BF16)
        _to_row_tiles(_dot(hid, wd_ref[...]), ys_ref, 0)

    @pl.when(i >= nu_ref[0])
    def _():
        ys_ref[...] = jnp.zeros_like(ys_ref)


def _moe_experts(tile_expert, n_used, xs, wg, wu, wd):
    grid_spec = pltpu.PrefetchScalarGridSpec(
        num_scalar_prefetch=2,
        grid=(MOE_TILES,),
        in_specs=[pl.BlockSpec((MOE_TILE * MOE_SEG, LANES), lambda i, te, nu: (i, 0)),
                  pl.BlockSpec((None, D_MODEL, D_FF_E), lambda i, te, nu: (te[i], 0, 0)),
                  pl.BlockSpec((None, D_MODEL, D_FF_E), lambda i, te, nu: (te[i], 0, 0)),
                  pl.BlockSpec((None, D_FF_E, D_MODEL), lambda i, te, nu: (te[i], 0, 0))],
        out_specs=pl.BlockSpec((MOE_TILE * MOE_SEG, LANES), lambda i, te, nu: (i, 0)),
    )
    return pl.pallas_call(
        _experts_kernel,
        grid_spec=grid_spec,
        out_shape=jax.ShapeDtypeStruct((MOE_ROWS * MOE_SEG, LANES), F32),
        compiler_params=_cparams(("arbitrary",)),
        name="moe_experts",
    )(tile_expert, n_used, xs, wg, wu, wd)


def _combine_kernel(pos_ref, x_ref, mod_ref, sel_ref, ys_ref, oc_ref, ol_ref, buf_sc, sem, *, nsteps, nctx):
    i = pl.program_id(0)
    slot = i % 2
    nrow = MOE_TILE
    part = nrow * MOE_SEG

    def start(s, off):
        def body(t, c):
            for k in range(2):
                _row_copy(ys_ref, pos_ref[0, off + 2 * t + k], buf_sc, (2 * s + k) * part + t * MOE_SEG,
                          sem.at[s]).start(priority=k)
            return c
        lax.fori_loop(0, nrow, body, 0, unroll=4)

    @pl.when(i == 0)
    def _():
        start(0, 0)

    @pl.when(i + 1 < nsteps)
    def _():
        start(1 - slot, 2 * nrow)

    for k in range(2):
        pltpu.make_async_copy(ys_ref.at[pl.ds(0, part), :], buf_sc.at[pl.ds(0, part), :], sem.at[slot]).wait()
    lane = lax.broadcasted_iota(jnp.int32, (nrow, LANES), 1)
    sel = sel_ref[...]
    w1 = jnp.sum(jnp.where(lane == 0, sel, 0.0), axis=-1, keepdims=True)
    w2 = jnp.sum(jnp.where(lane == 1, sel, 0.0), axis=-1, keepdims=True)
    y = (w1 * _from_row_tiles(buf_sc, 2 * slot * part, nrow)
         + w2 * _from_row_tiles(buf_sc, (2 * slot + 1) * part, nrow))
    val = x_ref[...] + mod_ref[5:6, :] * y

    @pl.when(i < nctx)
    def _():
        oc_ref[...] = val

    @pl.when(i >= nctx)
    def _():
        ol_ref[...] = val


def _moe_combine(pos2, x, modl, sel, ys):
    nsteps = M_TOK // MOE_TILE
    tm = MOE_TILE
    nctx = M_CTX // tm
    return pl.pallas_call(
        functools.partial(_combine_kernel, nsteps=nsteps, nctx=nctx),
        grid=(nsteps,),
        in_specs=[pl.BlockSpec((None, 1, 4 * MOE_TILE), lambda i: (i, 0, 0), memory_space=pltpu.SMEM),
                  pl.BlockSpec((tm, D_MODEL), lambda i: (i, 0)),
                  pl.BlockSpec((None, SUBLANES, D_MODEL), lambda i: (_mod_row(i, tm), 0, 0)),
                  pl.BlockSpec((tm, LANES), lambda i: (i, 0)),
                  pl.BlockSpec(memory_space=pl.ANY)],
        out_specs=[pl.BlockSpec((tm, D_MODEL), lambda i: (jnp.minimum(i, nctx - 1), 0)),
                   pl.BlockSpec((tm, D_MODEL), lambda i: (jnp.maximum(i - nctx, 0), 0))],
        out_shape=[jax.ShapeDtypeStruct((M_CTX, D_MODEL), F32), jax.ShapeDtypeStruct((M_LAT, D_MODEL), F32)],
        scratch_shapes=[pltpu.VMEM((2 * 2 * MOE_TILE * MOE_SEG, LANES), F32), pltpu.SemaphoreType.DMA((2,))],
        compiler_params=_cparams(("arbitrary",)),
        name="moe_combine",
    )(pos2, x, modl, sel, ys)


def _ffn_moe(xm2, x, modl, wr, wg, wu, wd):
    sel, xp, cnt = _moe_router(xm2, wr)
    counts = cnt[0, :N_EXPERTS].astype(jnp.int32)
    padded = ((counts + MOE_TILE - 1) // MOE_TILE) * MOE_TILE
    ends = jnp.cumsum(padded)
    offs = ends - padded
    experts = sel[:, 2:4].astype(jnp.int32)
    ranks = sel[:, 4:6].astype(jnp.int32)
    pos = (offs[experts] + ranks) * MOE_SEG
    tile_start = jnp.arange(MOE_TILES, dtype=jnp.int32) * MOE_TILE
    tile_expert = jnp.minimum(jnp.sum(tile_start[:, None] >= ends[None, :], axis=1), N_EXPERTS - 1)
    n_used = (ends[-1] // MOE_TILE).astype(jnp.int32)[None]
    rows = jnp.arange(MOE_ROWS, dtype=jnp.int32)
    row_expert = jnp.repeat(tile_expert, MOE_TILE)
    written = (rows < ends[-1]) & (rows - offs[row_expert] < counts[row_expert])
    padrows = jnp.nonzero(~written, size=MOE_PAD_ROWS)[0].astype(jnp.int32) * MOE_SEG
    pos_tiles = pos.reshape(M_TOK // MOE_TILE, 1, 2 * MOE_TILE)
    xs = _moe_dispatch(xp, pos_tiles, padrows)
    ys = _moe_experts(tile_expert.astype(jnp.int32), n_used, xs, wg, wu, wd)
    nxt = jnp.concatenate([pos_tiles[1:], pos_tiles[-1:]], axis=0)
    pos2 = jnp.concatenate([pos_tiles, nxt], axis=2)
    return tuple(_moe_combine(pos2, x, modl, sel, ys))


def _block_diag(w):
    nb, bs, _ = w.shape
    eye = jnp.eye(nb, dtype=w.dtype)
    return jnp.einsum("njk,nm->njmk", w, eye).reshape(nb * bs, nb * bs)


def _head_pad_cols(w, width):
    r = w.shape[0]
    return jnp.pad(w, ((0, 0), (0, 0), (0, HEAD_PAD - width))).reshape(r, N_HEADS * HEAD_PAD)


def _swap_rope_pairs(a):
    nope, rope = a[..., :D_NOPE], a[..., D_NOPE:]
    sw = rope.reshape(rope.shape[:-1] + (D_ROPE // 2, 2))[..., ::-1].reshape(rope.shape)
    return jnp.concatenate([nope, sw], axis=-1)


def _head_gain(g):
    rows = jnp.stack([g, _swap_rope_pairs(g)], axis=0)
    return jnp.pad(rows, ((0, SUBLANES - 2), (0, HEAD_PAD - D_QK)))


def _rope_tables(tm):
    rows = DEC_SEQ // GRID_W
    row = jnp.repeat(jnp.arange(rows, dtype=F32), GRID_W)
    col = jnp.tile(jnp.arange(GRID_W, dtype=F32), rows)
    half = D_ROPE // 2
    inv_freq = ROPE_BASE ** (-jnp.arange(0, half, 2, dtype=F32) / half)
    ang = jnp.concatenate([row[:, None] * inv_freq, col[:, None] * inv_freq], axis=-1)
    cos, sin = jnp.cos(ang), jnp.sin(ang)
    cos2 = jnp.repeat(cos, 2, axis=1)
    sin2 = jnp.stack([-sin, sin], axis=-1).reshape(DEC_SEQ, D_ROPE)
    cos_t = jnp.pad(cos2, ((0, 0), (D_NOPE, HEAD_PAD - D_QK)), constant_values=1.0)
    sin_t = jnp.pad(sin2, ((0, 0), (D_NOPE, HEAD_PAD - D_QK)))
    cos_t = jnp.concatenate([jnp.ones((tm, HEAD_PAD), F32), cos_t], axis=0)
    sin_t = jnp.concatenate([jnp.zeros((tm, HEAD_PAD), F32), sin_t], axis=0)
    return cos_t, sin_t


def kernel(x_prompt, x_sample, cache_ckv, cache_krope, state_lru, c, c_ctx, norm1, norm2, w_ada, b_ada, w_in, mla_q_norm, mla_kv_norm, mla_w_uq, mla_w_uk, mla_w_uv, mla_q_qknorm, mla_k_qknorm, lru_conv_w, lru_conv_b, lru_w_gate, lru_b_gate, lru_lambda, hy_conv_w, hy_conv_b, hy_w1, hy_b1, hy_freq1, hy_w2, hy_b2, hy_freq2, hy_w3, hy_bias, w_lru_out, w_mla_out, w_hy_out, w_out, ffn_w_gate, ffn_w_up, ffn_w_down, moe_w_router, moe_w_gate, moe_w_up, moe_w_down):
    x = (x_prompt.reshape(M_CTX, D_MODEL), x_sample.reshape(M_LAT, D_MODEL))

    cond = jnp.concatenate([c_ctx[None, :], c, jnp.zeros((SUBLANES - 1 - DEC_BATCH, D_MODEL), F32)], axis=0)
    mod = _adaln(cond, w_ada, b_ada).reshape(DEPTH, SUBLANES, 6, D_MODEL)
    mod = jnp.pad(mod, ((0, 0), (0, 0), (0, SUBLANES - 6), (0, 0)))

    cos_t, sin_t = _rope_tables(TM1)
    ctx_tabs = _ctx_tables()
    lat_tabs = _lat_tables()
    zero_state = jnp.zeros((BATCH, SUBLANES, LRU_W), F32)

    ckv_out, kr_out, st_out = [], [], []
    for l in range(DEPTH):
        wl = w_in[l]
        wkr = jnp.concatenate([jnp.zeros((D_MODEL, D_NOPE), F32), wl[:, 896:928]], axis=1)
        krblk = lambda w: jnp.pad(w, ((0, 0), (0, HEAD_PAD - D_QK)))
        w1 = jnp.concatenate([wl[:, :896], krblk(wkr), krblk(_swap_rope_pairs(wkr)), wl[:, 928:2464]],
                             axis=1).astype(BF16)
        wgates = wl[:, 2464:].astype(BF16)
        wuq = _head_pad_cols(mla_w_uq[l], D_QK).astype(BF16)
        wuqs = _head_pad_cols(_swap_rope_pairs(mla_w_uq[l]), D_QK).astype(BF16)
        wuk = _head_pad_cols(mla_w_uk[l], D_NOPE).astype(BF16)
        wuv = mla_w_uv[l].reshape(KV_RANK, N_HEADS * D_V).astype(BF16)
        gq = _head_gain(mla_q_qknorm[l])
        gk = _head_gain(mla_k_qknorm[l])

        ulru, uhy, ckv, krb, q, k, v = _stage1(
            x, mod[l], norm1[l][None, :], w1, mla_kv_norm[l][None, :], mla_q_norm[l][None, :],
            wuq, wuqs, gq, wuk, gk, wuv, cos_t, sin_t)
        ckv_out.append(ckv[:M_CTX].reshape(BATCH, SEQ, KV_RANK))
        kr_out.append(krb[:M_CTX, D_NOPE:D_QK].reshape(BATCH, SEQ, D_ROPE))

        kc, vc = _kvprep(cache_ckv[:, l].reshape(DEC_BATCH * PAST_LEN, KV_RANK),
                         jnp.pad(cache_krope[:, l].reshape(DEC_BATCH * PAST_LEN, D_ROPE),
                                 ((0, 0), (D_NOPE, HEAD_PAD - D_QK))),
                         wuk, gk, wuv)
        ymla = (_attention_ctx(q, k, v), _attention_lat(q, k, v, kc, vc))

        lp = dict(
            cw=jnp.pad(lru_conv_w[l], ((0, SUBLANES - 4), (0, 0))), cb=lru_conv_b[l][None, :],
            wr=[_block_diag(lru_w_gate[l, d, 0]).astype(BF16) for d in range(2)],
            wi=[_block_diag(lru_w_gate[l, d, 1]).astype(BF16) for d in range(2)],
            br=[lru_b_gate[l, d, 0][None, :] for d in range(2)],
            bi=[lru_b_gate[l, d, 1][None, :] for d in range(2)],
            lam=[lru_lambda[l, d][None, :] for d in range(2)])
        y_c, stf, stb = _lru_mixer(ulru, lp, (zero_state, zero_state), row_off=0, nseq=BATCH, seqlen=SEQ, tc=SEQ)
        st_out.append(jnp.stack([stf, stb], axis=1))
        h0 = [jnp.broadcast_to(state_lru[:, l, d][:, None, :], (DEC_BATCH, SUBLANES, LRU_W)) for d in range(2)]
        y_l, _, _ = _lru_mixer(ulru, lp, h0, row_off=M_CTX, nseq=DEC_BATCH, seqlen=DEC_SEQ, tc=512)
        ylru = (y_c, y_l)

        hp = dict(
            cw=jnp.pad(hy_conv_w[l], ((0, SUBLANES - 3), (0, 0))), cb=hy_conv_b[l][None, :],
            w1=jnp.pad(hy_w1[l], ((0, LANES - HY_EMB), (0, 0))).astype(BF16), b1=hy_b1[l][None, :],
            f1=hy_freq1[l][None, :], w2=hy_w2[l].astype(BF16), b2=hy_b2[l][None, :], f2=hy_freq2[l][None, :],
            w3=hy_w3[l].astype(BF16), bias=hy_bias[l])
        yhy = (_hyena_ctx(uhy, hp, ctx_tabs), _hyena_lat(uhy, hp, lat_tabs))

        xmid, xm2 = _stage3(x, mod[l], norm1[l][None, :], wgates, ylru, ymla, yhy,
                            w_lru_out[l].astype(BF16), w_mla_out[l].astype(BF16), w_hy_out[l].astype(BF16),
                            w_out[l].astype(BF16), norm2[l][None, :])
        j = l // 2
        if l % 2 == 0:
            x = (_ffn_dense(xm2, xmid, mod[l], ffn_w_gate[j].astype(BF16), ffn_w_up[j].astype(BF16),
                            ffn_w_down[j].astype(BF16)),)
        else:
            wr = jnp.pad(moe_w_router[j], ((0, 0), (0, LANES - N_EXPERTS))).astype(BF16)
            x = _ffn_moe(xm2, xmid, mod[l], wr, moe_w_gate[j].astype(BF16), moe_w_up[j].astype(BF16),
                         moe_w_down[j].astype(BF16))

    xc, xl = x if len(x) == 2 else (x[0][:M_CTX], x[0][M_CTX:])
    y_prompt = xc.reshape(BATCH, SEQ, D_MODEL)
    y_sample = xl.reshape(DEC_BATCH, DEC_SEQ, D_MODEL)
    return (y_prompt, y_sample, jnp.stack(ckv_out, axis=1), jnp.stack(kr_out, axis=1), jnp.stack(st_out, axis=1))
```

```python
import functools
import math

import jax
import jax.numpy as jnp
from jax import lax
from jax.experimental import pallas as pl
from jax.experimental.pallas import tpu as pltpu

F32 = jnp.float32
BF16 = jnp.bfloat16

D_MODEL = 1024
BATCH = 32
SEQ = 256
DEPTH = 2
DEC_BATCH = 2
DEC_SEQ = 4096
PAST_LEN = 512
GRID_W = 64
EPS = 1e-6
LRU_W = 512
LRU_BLOCKS = 8
LRU_C = 8.0
N_HEADS = 8
D_NOPE = 64
D_ROPE = 32
D_QK = D_NOPE + D_ROPE
D_V = 64
Q_RANK = 256
KV_RANK = 128
ROPE_BASE = 10000.0
HY_W = 512
HY_ORDER = 2
HY_EMB = 33
HY_HID = 64
HY_FAST_PCT = 0.3
HY_SLOW_PCT = 1.5
D_FF = 2816
N_EXPERTS = 8
D_FF_E = 1408

LANES = 128
SUBLANES = 8
VMEM_LIMIT = 56 * 1024 * 1024

M_CTX = BATCH * SEQ
M_LAT = DEC_BATCH * DEC_SEQ
M_TOK = M_CTX + M_LAT
TM1 = 512
TM3 = 512
TM_FFN = 512
W1_COLS = 2688
HEAD_PAD = LANES
QK_SCALE = math.log2(math.e) / math.sqrt(D_QK)
ATTN_TQ = 256
ATTN_LAT_HEADS = 4
ATTN_LAT_GROUPS = (((0, 0, DEC_SEQ // 2),), ((0, DEC_SEQ // 2, DEC_SEQ), (1, 0, PAST_LEN)))

FFT_N1 = 64
FFT_N2 = 128


def _cparams(sem, vmem=VMEM_LIMIT):
    return pltpu.CompilerParams(dimension_semantics=sem, vmem_limit_bytes=vmem)


def _dot(a, b):
    return jnp.dot(a, b, preferred_element_type=F32)


def _rms(x, g):
    ms = jnp.mean(x * x, axis=-1, keepdims=True)
    return x * lax.rsqrt(ms + EPS) * g


def _sigmoid(x):
    return 1.0 / (1.0 + jnp.exp(-x))


def _ada_kernel(c_ref, w_ref, b_ref, o_ref):
    c = c_ref[...]
    s = (c * _sigmoid(c)).astype(BF16)
    o_ref[...] = _dot(s, w_ref[...].astype(BF16)) + b_ref[...]


def _adaln(cond, w_ada, b_ada):
    tn = 1024
    n6 = 6 * D_MODEL
    return pl.pallas_call(
        _ada_kernel,
        grid=(DEPTH, n6 // tn),
        in_specs=[
            pl.BlockSpec((SUBLANES, D_MODEL), lambda l, j: (0, 0)),
            pl.BlockSpec((None, D_MODEL, tn), lambda l, j: (l, 0, j)),
            pl.BlockSpec((None, 1, tn), lambda l, j: (l, 0, j)),
        ],
        out_specs=pl.BlockSpec((None, SUBLANES, tn), lambda l, j: (l, 0, j)),
        out_shape=jax.ShapeDtypeStruct((DEPTH, SUBLANES, n6), F32),
        compiler_params=_cparams(("arbitrary", "arbitrary")),
        name="adaln",
    )(cond, w_ada, b_ada.reshape(DEPTH, 1, n6))


def _mod_row(i, tm):
    nctx = M_CTX // tm
    per = DEC_SEQ // tm
    return jnp.where(i < nctx, 0, 1 + (i - nctx) // per)


def _rope_blk(i, tm):
    nctx = M_CTX // tm
    per = DEC_SEQ // tm
    return jnp.where(i < nctx, 0, 1 + (i - nctx) % per)


def _finish_head(raw, raw_sw, gc, gs, out_ref, sl):
    ms = jnp.sum(raw * raw, axis=-1, keepdims=True) * (1.0 / D_QK)
    rs = lax.rsqrt(ms + EPS)
    val = raw * gc
    if gs is not None:
        val = val + raw_sw * gs
    out_ref[:, sl] = (val * rs).astype(BF16)


def _read_tokens(x_refs, nctx):
    if len(x_refs) == 1:
        return x_refs[0][...]
    return jnp.where(pl.program_id(0) < nctx, x_refs[0][...], x_refs[1][...])


def _token_specs(x, tm):
    nctx = M_CTX // tm
    if len(x) == 1:
        return [pl.BlockSpec((tm, D_MODEL), lambda i: (i, 0))]
    return [pl.BlockSpec((tm, D_MODEL), lambda i: (jnp.minimum(i, nctx - 1), 0)),
            pl.BlockSpec((tm, D_MODEL), lambda i: (jnp.maximum(i - nctx, 0), 0))]


def _stage1_kernel(*refs, nx, nctx):
    (mod_ref, g1_ref, w1_ref, gkv_ref, gqn_ref, wuq_ref, wuqs_ref, gq_ref, wuk_ref, gk_ref,
     wuv_ref, cos_ref, sin_ref, ulru_ref, uhy_ref, ckv_ref, krb_ref, q_ref, k_ref, v_ref) = refs[nx:]
    x = _read_tokens(refs[:nx], nctx)
    xm = _rms(x, g1_ref[...]) * (1.0 + mod_ref[1:2, :]) + mod_ref[0:1, :]
    xb = xm.astype(BF16)
    ulru_ref[...] = _dot(xb, w1_ref[:, 0:512])
    qc = _dot(xb, w1_ref[:, 512:768])
    ckv = _dot(xb, w1_ref[:, 768:896])
    krb = _dot(xb, w1_ref[:, 896:1024])
    krs = _dot(xb, w1_ref[:, 1024:1152])
    uhy_ref[...] = _dot(xb, w1_ref[:, 1152:2688])
    ckvn = _rms(ckv, gkv_ref[...])
    ckv_ref[...] = ckvn
    krb_ref[...] = krb
    qn = _rms(qc, gqn_ref[...]).astype(BF16)
    cb = ckvn.astype(BF16)
    v_ref[...] = _dot(cb, wuv_ref[...]).astype(BF16)
    cos = cos_ref[...]
    sin = sin_ref[...]
    gcq = cos * (gq_ref[0:1, :] * QK_SCALE)
    gsq = sin * (gq_ref[1:2, :] * QK_SCALE)
    gck = cos * gk_ref[0:1, :]
    gsk = sin * gk_ref[1:2, :]
    for pair in range(N_HEADS // 2):
        ps = slice(2 * HEAD_PAD * pair, 2 * HEAD_PAD * (pair + 1))
        qraw, qsw, kraw = _dot(qn, wuq_ref[:, ps]), _dot(qn, wuqs_ref[:, ps]), _dot(cb, wuk_ref[:, ps])
        for j in range(2):
            hs = slice(HEAD_PAD * j, HEAD_PAD * (j + 1))
            sl = slice(HEAD_PAD * (2 * pair + j), HEAD_PAD * (2 * pair + j + 1))
            _finish_head(qraw[:, hs], qsw[:, hs], gcq, gsq, q_ref, sl)
            _finish_head(kraw[:, hs] + krb, krs, gck, gsk, k_ref, sl)


def _stage1(x, modl, g1, w1, gkv, gqn, wuq, wuqs, gq, wuk, gk, wuv, cos_t, sin_t):
    tm = TM1
    full = lambda shape: pl.BlockSpec(shape, lambda i: (0,) * len(shape))
    row = lambda cols: pl.BlockSpec((tm, cols), lambda i: (i, 0))
    hw = N_HEADS * HEAD_PAD
    return pl.pallas_call(
        functools.partial(_stage1_kernel, nx=len(x), nctx=M_CTX // tm),
        grid=(M_TOK // tm,),
        in_specs=_token_specs(x, tm) + [
            pl.BlockSpec((None, SUBLANES, D_MODEL), lambda i: (_mod_row(i, tm), 0, 0)),
            full((1, D_MODEL)),
            full((D_MODEL, W1_COLS)),
            full((1, KV_RANK)),
            full((1, Q_RANK)),
            full((Q_RANK, hw)),
            full((Q_RANK, hw)),
            full((SUBLANES, HEAD_PAD)),
            full((KV_RANK, hw)),
            full((SUBLANES, HEAD_PAD)),
            full((KV_RANK, N_HEADS * D_V)),
            pl.BlockSpec((tm, LANES), lambda i: (_rope_blk(i, tm), 0)),
            pl.BlockSpec((tm, LANES), lambda i: (_rope_blk(i, tm), 0)),
        ],
        out_specs=[row(LRU_W), row(3 * HY_W), row(KV_RANK), row(LANES), row(hw), row(hw), row(N_HEADS * D_V)],
        out_shape=[
            jax.ShapeDtypeStruct((M_TOK, LRU_W), F32),
            jax.ShapeDtypeStruct((M_TOK, 3 * HY_W), F32),
            jax.ShapeDtypeStruct((M_TOK, KV_RANK), F32),
            jax.ShapeDtypeStruct((M_TOK, LANES), F32),
            jax.ShapeDtypeStruct((M_TOK, hw), BF16),
            jax.ShapeDtypeStruct((M_TOK, hw), BF16),
            jax.ShapeDtypeStruct((M_TOK, N_HEADS * D_V), BF16),
        ],
        compiler_params=_cparams(("arbitrary",)),
        name="stage1",
    )(*x, modl, g1, w1, gkv, gqn, wuq, wuqs, gq, wuk, gk, wuv, cos_t, sin_t)


def _kvprep_kernel(ckv_ref, krb_ref, wuk_ref, gk_ref, wuv_ref, k_ref, v_ref):
    cb = ckv_ref[...].astype(BF16)
    v_ref[...] = _dot(cb, wuv_ref[...]).astype(BF16)
    krb = krb_ref[...]
    for h in range(N_HEADS):
        sl = slice(HEAD_PAD * h, HEAD_PAD * (h + 1))
        _finish_head(_dot(cb, wuk_ref[:, sl]) + krb, None, gk_ref[0:1, :], None, k_ref, sl)


def _kvprep(ckv, krb, wuk, gk, wuv):
    rows = ckv.shape[0]
    tm = TM1
    hw = N_HEADS * HEAD_PAD
    full = lambda shape: pl.BlockSpec(shape, lambda i: (0,) * len(shape))
    row = lambda cols: pl.BlockSpec((tm, cols), lambda i: (i, 0))
    return pl.pallas_call(
        _kvprep_kernel,
        grid=(rows // tm,),
        in_specs=[row(KV_RANK), row(LANES), full((KV_RANK, hw)), full((SUBLANES, HEAD_PAD)),
                  full((KV_RANK, N_HEADS * D_V))],
        out_specs=[row(hw), row(N_HEADS * D_V)],
        out_shape=[jax.ShapeDtypeStruct((rows, hw), BF16), jax.ShapeDtypeStruct((rows, N_HEADS * D_V), BF16)],
        compiler_params=_cparams(("arbitrary",)),
        name="kvprep",
    )(ckv, krb, wuk, gk, wuv)


def _attn_kernel(*refs, heads, nseg, groups):
    q_ref = refs[0]
    k_refs = refs[1:1 + nseg]
    v_refs = refs[1 + nseg:1 + 2 * nseg]
    o_ref = refs[1 + 2 * nseg]
    tq = q_ref.shape[0]
    lane = lax.broadcasted_iota(jnp.int32, (tq, LANES), 1)
    low = lane < D_V
    for pair in range(heads // 2):
        outs = []
        for j in range(2):
            h = 2 * pair + j
            sl = slice(HEAD_PAD * h, HEAD_PAD * (h + 1))
            q = q_ref[:, sl]
            parts = []
            for group in groups:
                s = [lax.dot_general(q, k_refs[seg][r0:r1, sl], (((1,), (1,)), ((), ())),
                                     preferred_element_type=F32) for seg, r0, r1 in group]
                m = jnp.max(s[0], axis=-1, keepdims=True)
                for si in s[1:]:
                    m = jnp.maximum(m, jnp.max(si, axis=-1, keepdims=True))
                acc = None
                den = None
                for si, (seg, r0, r1) in zip(s, group):
                    p = jnp.exp2(si - m)
                    d = jnp.sum(p, axis=-1, keepdims=True)
                    o = _dot(p.astype(BF16), v_refs[seg][r0:r1, LANES * pair:LANES * (pair + 1)])
                    acc = o if acc is None else acc + o
                    den = d if den is None else den + d
                parts.append((m, acc, den))
            if len(parts) == 1:
                _, acc, den = parts[0]
            else:
                mall = parts[0][0]
                for m, _, _ in parts[1:]:
                    mall = jnp.maximum(mall, m)
                acc = None
                den = None
                for m, a, d in parts:
                    w = jnp.exp2(m - mall)
                    acc = w * a if acc is None else acc + w * a
                    den = w * d if den is None else den + w * d
            outs.append(acc / den)
        o_ref[:, LANES * pair:LANES * (pair + 1)] = jnp.where(low, outs[0], outs[1]).astype(BF16)


def _attention_ctx(q, k, v):
    hw = N_HEADS * HEAD_PAD
    vw = N_HEADS * D_V
    return pl.pallas_call(
        functools.partial(_attn_kernel, heads=N_HEADS, nseg=1, groups=(((0, 0, SEQ),),)),
        grid=(BATCH,),
        in_specs=[
            pl.BlockSpec((SEQ, hw), lambda b: (b, 0)),
            pl.BlockSpec((SEQ, hw), lambda b: (b, 0)),
            pl.BlockSpec((SEQ, vw), lambda b: (b, 0)),
        ],
        out_specs=pl.BlockSpec((SEQ, vw), lambda b: (b, 0)),
        out_shape=jax.ShapeDtypeStruct((M_CTX, vw), BF16),
        compiler_params=_cparams(("arbitrary",)),
        name="attn_ctx",
    )(q, k, v)


def _attention_lat(q, k, v, kc, vc):
    tq = ATTN_TQ
    nq = DEC_SEQ // tq
    qoff = M_CTX // tq
    koff = M_CTX // DEC_SEQ
    hp = ATTN_LAT_HEADS
    return pl.pallas_call(
        functools.partial(_attn_kernel, heads=hp, nseg=2, groups=ATTN_LAT_GROUPS),
        grid=(DEC_BATCH, N_HEADS // hp, nq),
        in_specs=[
            pl.BlockSpec((tq, hp * HEAD_PAD), lambda b, p, i: (qoff + b * nq + i, p)),
            pl.BlockSpec((DEC_SEQ, hp * HEAD_PAD), lambda b, p, i: (koff + b, p)),
            pl.BlockSpec((PAST_LEN, hp * HEAD_PAD), lambda b, p, i: (b, p)),
            pl.BlockSpec((DEC_SEQ, hp * D_V), lambda b, p, i: (koff + b, p)),
            pl.BlockSpec((PAST_LEN, hp * D_V), lambda b, p, i: (b, p)),
        ],
        out_specs=pl.BlockSpec((tq, hp * D_V), lambda b, p, i: (b * nq + i, p)),
        out_shape=jax.ShapeDtypeStruct((M_LAT, N_HEADS * D_V), BF16),
        compiler_params=_cparams(("arbitrary", "arbitrary", "arbitrary")),
        name="attn_lat",
    )(q, k, kc, v, vc)


def _lru_kernel(*refs, reverse, tc, nchunks):
    if reverse:
        (up_ref, uc_ref, un_ref, hf_ref, cw_ref, cb_ref, wr_ref, wi_ref, br_ref, bi_ref, lam_ref, h0_ref,
         y_ref, st_ref, ext_sc, a_sc, b_sc, p_sc, h_sc, car_sc) = refs
    else:
        (up_ref, uc_ref, un_ref, cw_ref, cb_ref, wr_ref, wi_ref, br_ref, bi_ref, lam_ref, h0_ref,
         y_ref, st_ref, ext_sc, a_sc, b_sc, p_sc, h_sc, car_sc) = refs
    c = pl.program_id(1)
    chunk = (nchunks - 1 - c) if reverse else c
    prev = jnp.where(chunk == 0, 0.0, up_ref[...])
    nxt = jnp.where(chunk == nchunks - 1, 0.0, un_ref[...])
    ext_sc[0:SUBLANES, :] = prev
    ext_sc[SUBLANES:SUBLANES + tc, :] = uc_ref[...]
    ext_sc[SUBLANES + tc:2 * SUBLANES + tc, :] = nxt
    xc = cb_ref[...]
    for k in range(4):
        xc = xc + cw_ref[k:k + 1, :] * ext_sc[SUBLANES - 2 + k:SUBLANES - 2 + k + tc, :]
    xb = xc.astype(BF16)
    r = _sigmoid(_dot(xb, wr_ref[...]) + br_ref[...])
    gi = _sigmoid(_dot(xb, wi_ref[...]) + bi_ref[...])
    lam = lam_ref[...]
    logsig = -(jnp.maximum(-lam, 0.0) + jnp.log1p(jnp.exp(-jnp.abs(lam))))
    la = LRU_C * r * logsig
    a = jnp.exp(la)
    v = -jnp.tanh(la) * (a * a + 1.0)
    bc = jnp.where(v > 0.0, v * lax.rsqrt(v), 0.0) * (gi * xc)

    @pl.when(c == 0)
    def _():
        car_sc[...] = h0_ref[...]

    nseg = SUBLANES
    sl = tc // nseg
    sp = sl + SUBLANES
    nlb = LRU_W // LANES
    for j in range(nlb):
        for s in range(nseg):
            rows = slice(sl * s, sl * (s + 1))
            dst = slice((j * nseg + s) * sp, (j * nseg + s) * sp + sl)
            a_sc[dst, :] = a[rows, LANES * j:LANES * (j + 1)]
            b_sc[dst, :] = bc[rows, LANES * j:LANES * (j + 1)]

    def body(k, carry):
        i = (sl - 1 - k) if reverse else k
        hs, ps = carry
        hn, pn = [], []
        for j in range(nlb):
            idx = pl.ds(j * nseg * sp + i, nseg, stride=sp)
            av = a_sc[idx, :]
            h = av * hs[j] + b_sc[idx, :]
            p = av * ps[j]
            p_sc[idx, :] = p
            h_sc[idx, :] = h
            hn.append(h)
            pn.append(p)
        return tuple(hn), tuple(pn)

    zero = jnp.zeros((nseg, LANES), F32)
    one = jnp.ones((nseg, LANES), F32)
    hend, pend = lax.fori_loop(0, sl, body, ((zero,) * nlb, (one,) * nlb), unroll=4)

    order = range(nseg - 1, -1, -1) if reverse else range(nseg)
    for j in range(nlb):
        lanes = slice(LANES * j, LANES * (j + 1))
        cin = car_sc[0:1, lanes]
        for s in order:
            rows = slice(sl * s, sl * (s + 1))
            src = slice((j * nseg + s) * sp, (j * nseg + s) * sp + sl)
            h = h_sc[src, :] + p_sc[src, :] * cin
            if reverse:
                y_ref[rows, lanes] = (hf_ref[rows, lanes] + h).astype(BF16)
            else:
                y_ref[rows, lanes] = h
            cin = hend[j][s:s + 1, :] + pend[j][s:s + 1, :] * cin
        car_sc[0:1, lanes] = cin
        st_ref[:, lanes] = jnp.broadcast_to(cin, (SUBLANES, LANES))


def _lru_dir(u, hf, cw, cb, wr, wi, br, bi, lam, h0, *, reverse, row_off, nseq, seqlen, tc):
    nchunks = seqlen // tc
    hb = M_TOK // SUBLANES

    def chunk_of(c):
        return (nchunks - 1 - c) if reverse else c

    def cur(b, c):
        return ((row_off + b * seqlen) // tc + chunk_of(c), 0)

    def prv(b, c):
        return (jnp.maximum((row_off + b * seqlen + chunk_of(c) * tc) // SUBLANES - 1, 0), 0)

    def nxt(b, c):
        return (jnp.minimum((row_off + b * seqlen + (chunk_of(c) + 1) * tc) // SUBLANES, hb - 1), 0)

    def out_cur(b, c):
        return ((b * seqlen) // tc + chunk_of(c), 0)

    full = lambda shape: pl.BlockSpec(shape, lambda b, c: (0,) * len(shape))
    in_specs = [pl.BlockSpec((SUBLANES, LRU_W), prv), pl.BlockSpec((tc, LRU_W), cur),
                pl.BlockSpec((SUBLANES, LRU_W), nxt)]
    args = [u, u, u]
    if reverse:
        in_specs.append(pl.BlockSpec((tc, LRU_W), out_cur))
        args.append(hf)
    in_specs += [full((SUBLANES, LRU_W)), full((1, LRU_W)), full((LRU_W, LRU_W)), full((LRU_W, LRU_W)),
                 full((1, LRU_W)), full((1, LRU_W)), full((1, LRU_W)),
                 pl.BlockSpec((None, SUBLANES, LRU_W), lambda b, c: (b, 0, 0))]
    args += [cw, cb, wr, wi, br, bi, lam, h0]
    return pl.pallas_call(
        functools.partial(_lru_kernel, reverse=reverse, tc=tc, nchunks=nchunks),
        grid=(nseq, nchunks),
        in_specs=in_specs,
        out_specs=[pl.BlockSpec((tc, LRU_W), out_cur),
                   pl.BlockSpec((None, SUBLANES, LRU_W), lambda b, c: (b, 0, 0))],
        out_shape=[jax.ShapeDtypeStruct((nseq * seqlen, LRU_W), BF16 if reverse else F32),
                   jax.ShapeDtypeStruct((nseq, SUBLANES, LRU_W), F32)],
        scratch_shapes=[pltpu.VMEM((tc + 2 * SUBLANES, LRU_W), F32)]
        + [pltpu.VMEM(((LRU_W // LANES) * (tc + SUBLANES * SUBLANES), LANES), F32)] * 4
        + [pltpu.VMEM((SUBLANES, LRU_W), F32)],
        compiler_params=_cparams(("arbitrary", "arbitrary")),
        name="lru_bwd" if reverse else "lru_fwd",
    )(*args)


def _lru_mixer(u, p, h0, *, row_off, nseq, seqlen, tc):
    kw = dict(row_off=row_off, nseq=nseq, seqlen=seqlen, tc=tc)
    hf, stf = _lru_dir(u, None, p["cw"], p["cb"], p["wr"][0], p["wi"][0], p["br"][0], p["bi"][0], p["lam"][0],
                       h0[0], reverse=False, **kw)
    y, stb = _lru_dir(u, hf, p["cw"], p["cb"], p["wr"][1], p["wi"][1], p["br"][1], p["bi"][1], p["lam"][1],
                      h0[1], reverse=True, **kw)
    return y, stf[:, 0, :], stb[:, 0, :]


def _shortconv_kernel(up_ref, uc_ref, un_ref, cw_ref, cb_ref, v_ref, x1_ref, x2_ref, ext_sc, *, tc, nchunks):
    c = pl.program_id(1)
    prev = jnp.where(c == 0, 0.0, up_ref[...])
    nxt = jnp.where(c == nchunks - 1, 0.0, un_ref[...])
    ext_sc[0:SUBLANES, :] = prev
    ext_sc[SUBLANES:SUBLANES + tc, :] = uc_ref[...]
    ext_sc[SUBLANES + tc:2 * SUBLANES + tc, :] = nxt
    for part, o_ref in enumerate((v_ref, x1_ref, x2_ref)):
        cs = slice(HY_W * part, HY_W * (part + 1))
        acc = cb_ref[:, cs]
        for k in range(3):
            acc = acc + cw_ref[k:k + 1, cs] * ext_sc[SUBLANES - 1 + k:SUBLANES - 1 + k + tc, cs]
        o_ref[...] = acc


def _shortconv(u, cw, cb, *, row_off, nseq, seqlen, tc):
    nchunks = seqlen // tc
    w = 3 * HY_W
    hb = M_TOK // SUBLANES
    cur = lambda b, c: ((row_off + b * seqlen) // tc + c, 0)
    prv = lambda b, c: (jnp.maximum((row_off + b * seqlen + c * tc) // SUBLANES - 1, 0), 0)
    nxt = lambda b, c: (jnp.minimum((row_off + b * seqlen + (c + 1) * tc) // SUBLANES, hb - 1), 0)
    out = lambda b, c: ((b * seqlen) // tc + c, 0)
    full = lambda shape: pl.BlockSpec(shape, lambda b, c: (0,) * len(shape))
    rows = nseq * seqlen
    return pl.pallas_call(
        functools.partial(_shortconv_kernel, tc=tc, nchunks=nchunks),
        grid=(nseq, nchunks),
        in_specs=[pl.BlockSpec((SUBLANES, w), prv), pl.BlockSpec((tc, w), cur), pl.BlockSpec((SUBLANES, w), nxt),
                  full((SUBLANES, w)), full((1, w))],
        out_specs=[pl.BlockSpec((tc, HY_W), out)] * 3,
        out_shape=[jax.ShapeDtypeStruct((rows, HY_W), F32)] * 3,
        scratch_shapes=[pltpu.VMEM((tc + 2 * SUBLANES, w), F32)],
        compiler_params=_cparams(("arbitrary", "arbitrary")),
        name="hy_shortconv",
    )(u, u, u, cw, cb)


def _hyfilt_kernel(z_ref, t_ref, w1_ref, b1_ref, f1_ref, w2_ref, b2_ref, f2_ref, w3_ref, ad_ref, h_ref, s_ref):
    i = pl.program_id(0)
    z = z_ref[...].astype(BF16)
    h = jnp.sin(f1_ref[...] * (_dot(z, w1_ref[...]) + b1_ref[...]))
    h = jnp.sin(f2_ref[...] * (_dot(h.astype(BF16), w2_ref[...]) + b2_ref[...]))
    h = _dot(h.astype(BF16), w3_ref[...])
    t = t_ref[...]
    ncol = h.shape[1] // LANES
    win = jnp.concatenate([jnp.exp(-t * ad_ref[:, LANES * j:LANES * (j + 1)]) for j in range(ncol)], axis=1)
    h = h * win
    h_ref[...] = h

    @pl.when(i == 0)
    def _():
        s_ref[...] = jnp.zeros_like(s_ref)

    s_ref[0:1, :] = s_ref[0:1, :] + jnp.sum(jnp.abs(h), axis=0, keepdims=True)


def _hyfilt(feats, tcol, w1, b1, f1, w2, b2, f2, w3, absdelta):
    L = feats.shape[0]
    tl = min(L, 512)
    wcols = HY_ORDER * 2 * HY_W
    full = lambda shape: pl.BlockSpec(shape, lambda i: (0,) * len(shape))
    return pl.pallas_call(
        _hyfilt_kernel,
        grid=(L // tl,),
        in_specs=[pl.BlockSpec((tl, LANES), lambda i: (i, 0)), pl.BlockSpec((tl, LANES), lambda i: (i, 0)),
                  full((LANES, HY_HID)), full((1, HY_HID)), full((1, HY_HID)),
                  full((HY_HID, HY_HID)), full((1, HY_HID)), full((1, HY_HID)),
                  full((HY_HID, wcols)), full((1, wcols))],
        out_specs=[pl.BlockSpec((tl, wcols), lambda i: (i, 0)), full((SUBLANES, wcols))],
        out_shape=[jax.ShapeDtypeStruct((L, wcols), F32), jax.ShapeDtypeStruct((SUBLANES, wcols), F32)],
        compiler_params=_cparams(("arbitrary",)),
        name="hy_filter",
    )(feats, tcol, w1, b1, f1, w2, b2, f2, w3, absdelta)


def _combine_spectrum(zr, zi, s_ref, hr_out, hi_out):
    for o in range(HY_ORDER):
        f = slice(2 * HY_W * o, 2 * HY_W * o + HY_W)
        b = slice(2 * HY_W * o + HY_W, 2 * HY_W * (o + 1))
        den = s_ref[0:1, f] + s_ref[0:1, b] + EPS
        hr_out(o, (zr[:, f] + zr[:, b]) / den)
        hi_out(o, (zi[:, f] - zi[:, b]) / den)


def _ctx_spec_kernel(f_ref, h_ref, s_ref, o_ref):
    n = f_ref.shape[0] // 2
    z = _dot(f_ref[...], h_ref[...].astype(BF16))
    zr, zi = z[:n], z[n:]

    def put_r(o, val):
        o_ref[0, :, HY_W * o:HY_W * (o + 1)] = val

    def put_i(o, val):
        o_ref[1, :, HY_W * o:HY_W * (o + 1)] = val

    _combine_spectrum(zr, zi, s_ref, put_r, put_i)


def _ctx_spectrum(fmat, hdec, s):
    n = fmat.shape[0] // 2
    return pl.pallas_call(
        _ctx_spec_kernel,
        out_shape=jax.ShapeDtypeStruct((2, n, HY_ORDER * HY_W), F32),
        compiler_params=pltpu.CompilerParams(vmem_limit_bytes=VMEM_LIMIT),
        name="hy_ctx_spectrum",
    )(fmat, hdec, s)


def _ctx_conv_kernel(z_ref, x_ref, f_ref, fi_ref, h_ref, bias_ref, o_ref, *, nb, seqlen):
    n = f_ref.shape[0] // 2
    hr = h_ref[0]
    hi = h_ref[1]
    for b in range(nb):
        rs = slice(seqlen * b, seqlen * (b + 1))
        zt = z_ref[rs, :]
        zf = _dot(f_ref[...], zt.astype(BF16))
        zr, zi = zf[:n], zf[n:]
        y = jnp.concatenate([zr * hr - zi * hi, zr * hi + zi * hr], axis=0).astype(BF16)
        conv = _dot(fi_ref[...], y)
        o_ref[rs, :] = (x_ref[rs, :] * (conv + zt * bias_ref[...])).astype(o_ref.dtype)


def _ctx_conv(z, xg, fmat, finv, hspec, bias, order, out_dtype):
    nb = 4
    n = fmat.shape[0] // 2
    rows = nb * SEQ
    return pl.pallas_call(
        functools.partial(_ctx_conv_kernel, nb=nb, seqlen=SEQ),
        grid=(BATCH // nb,),
        in_specs=[pl.BlockSpec((rows, HY_W), lambda i: (i, 0)), pl.BlockSpec((rows, HY_W), lambda i: (i, 0)),
                  pl.BlockSpec(fmat.shape, lambda i: (0, 0)), pl.BlockSpec(finv.shape, lambda i: (0, 0)),
                  pl.BlockSpec((2, n, HY_W), lambda i: (0, 0, order)),
                  pl.BlockSpec((1, HY_W), lambda i: (0, 0))],
        out_specs=pl.BlockSpec((rows, HY_W), lambda i: (i, 0)),
        out_shape=jax.ShapeDtypeStruct((M_CTX, HY_W), out_dtype),
        compiler_params=_cparams(("arbitrary",)),
        name="hy_ctx_conv",
    )(z, xg, fmat, finv, hspec, bias)


K1U = FFT_N1 // 2 + 1
SLABS = 72
PITCH = FFT_N2 + SUBLANES
NROW1 = FFT_N1 // 2
LAT_UNROLL_R = 32
LAT_UNROLL_K = 33


def _pitch_copy_in(src_ref, col, dst_sc):
    for n1 in range(NROW1):
        dst_sc[PITCH * n1:PITCH * n1 + FFT_N2, :] = src_ref[FFT_N2 * n1:FFT_N2 * (n1 + 1), col]


def _dft_stage_a(zp_scs, wa_ref, a_scs):
    def body(r, c):
        x = jnp.concatenate([zp_sc[pl.ds(r, NROW1, stride=PITCH), :] for zp_sc in zp_scs], axis=1)
        a = _dot(wa_ref[...], x.astype(BF16))
        for j, a_sc in enumerate(a_scs):
            a_sc[pl.ds(r, SLABS, stride=PITCH), :] = a[:, LANES * j:LANES * (j + 1)]
        return c

    lax.fori_loop(0, FFT_N2, body, 0, unroll=LAT_UNROLL_R)


def _load_k1(a_sc, k1):
    base = pl.multiple_of(k1 * (2 * PITCH), SUBLANES)
    a = jnp.concatenate([a_sc[pl.ds(base, FFT_N2), :], a_sc[pl.ds(base + PITCH, FFT_N2), :]], axis=0)
    return base, a.astype(BF16)


def _lat_spec_kernel(h_ref, s_ref, wa_ref, g_ref, o_ref, hf_sc, hb_sc, af_sc, ab_sc):
    _pitch_copy_in(h_ref, slice(0, LANES), hf_sc)
    _pitch_copy_in(h_ref, slice(LANES, 2 * LANES), hb_sc)
    _dft_stage_a((hf_sc, hb_sc), wa_ref, (af_sc, ab_sc))
    den = s_ref[0:1, 0:LANES] + s_ref[0:1, LANES:2 * LANES] + EPS

    def kbody(k1, c):
        _, af = _load_k1(af_sc, k1)
        _, ab = _load_k1(ab_sc, k1)
        z = _dot(g_ref[k1], jnp.concatenate([af, ab], axis=1))
        zf, zb = z[:, :LANES], z[:, LANES:]
        o_ref[0, k1] = (zf[:FFT_N2] + zb[:FFT_N2]) / den
        o_ref[1, k1] = (zf[FFT_N2:] - zb[FFT_N2:]) / den
        return c

    lax.fori_loop(0, K1U, kbody, 0, unroll=LAT_UNROLL_K)


def _lat_spectrum(hdec, s, wa, gtab):
    nblk = HY_ORDER * HY_W // LANES
    slab = NROW1 * PITCH
    return pl.pallas_call(
        _lat_spec_kernel,
        grid=(nblk,),
        in_specs=[pl.BlockSpec((DEC_SEQ, 2 * LANES), lambda i: (0, i)),
                  pl.BlockSpec((SUBLANES, 2 * LANES), lambda i: (0, i)),
                  pl.BlockSpec(wa.shape, lambda i: (0, 0)),
                  pl.BlockSpec(gtab.shape, lambda i: (0, 0, 0))],
        out_specs=pl.BlockSpec((2, K1U, FFT_N2, LANES), lambda i: (0, 0, 0, i)),
        out_shape=jax.ShapeDtypeStruct((2, K1U, FFT_N2, HY_ORDER * HY_W), F32),
        scratch_shapes=[pltpu.VMEM((slab, LANES), F32), pltpu.VMEM((slab, LANES), F32),
                        pltpu.VMEM((SLABS * PITCH, LANES), F32), pltpu.VMEM((SLABS * PITCH, LANES), F32)],
        compiler_params=_cparams(("arbitrary",)),
        name="hy_lat_spectrum",
    )(hdec, s, wa, gtab)


def _lat_conv_kernel(z_ref, x_ref, wa_ref, wai_ref, g_ref, gi_ref, h_ref, bias_ref, o_ref,
                     zp_sc, xp_sc, op_sc, a_sc):
    full = slice(None)
    _pitch_copy_in(z_ref, full, zp_sc)
    _pitch_copy_in(x_ref, full, xp_sc)
    _dft_stage_a((zp_sc,), wa_ref, (a_sc,))

    def kbody(k1, c):
        base, a = _load_k1(a_sc, k1)
        z = _dot(g_ref[k1], a)
        zr, zi = z[:FFT_N2], z[FFT_N2:]
        hr = h_ref[0, k1]
        hi = h_ref[1, k1]
        y = jnp.concatenate([zr * hr - zi * hi, zr * hi + zi * hr], axis=0).astype(BF16)
        bp = _dot(gi_ref[k1], y)
        a_sc[pl.ds(base, FFT_N2), :] = bp[:FFT_N2]
        a_sc[pl.ds(base + PITCH, FFT_N2), :] = bp[FFT_N2:]
        return c

    lax.fori_loop(0, K1U, kbody, 0, unroll=LAT_UNROLL_K)
    bias = bias_ref[...]

    def rbody(r, c):
        yb = a_sc[pl.ds(r, SLABS, stride=PITCH), :].astype(BF16)
        conv = _dot(wai_ref[...], yb)
        zz = zp_sc[pl.ds(r, NROW1, stride=PITCH), :]
        xx = xp_sc[pl.ds(r, NROW1, stride=PITCH), :]
        op_sc[pl.ds(r, NROW1, stride=PITCH), :] = xx * (conv + zz * bias)
        return c

    lax.fori_loop(0, FFT_N2, rbody, 0, unroll=LAT_UNROLL_R)
    for n1 in range(NROW1):
        o_ref[FFT_N2 * n1:FFT_N2 * (n1 + 1), :] = op_sc[PITCH * n1:PITCH * n1 + FFT_N2, :]


def _lat_conv(z, xg, wa, wainv, gtab, gitab, hspec, bias, order):
    ncb = HY_W // LANES
    slab = NROW1 * PITCH
    blk = pl.BlockSpec((DEC_SEQ, LANES), lambda cb, b: (b, cb))
    const = lambda a: pl.BlockSpec(a.shape, lambda cb, b: (0,) * a.ndim)
    return pl.pallas_call(
        _lat_conv_kernel,
        grid=(ncb, DEC_BATCH),
        in_specs=[blk, blk, const(wa), const(wainv), const(gtab), const(gitab),
                  pl.BlockSpec((2, K1U, FFT_N2, LANES), lambda cb, b: (0, 0, 0, order * ncb + cb)),
                  pl.BlockSpec((1, LANES), lambda cb, b: (0, cb))],
        out_specs=blk,
        out_shape=jax.ShapeDtypeStruct((M_LAT, HY_W), F32),
        scratch_shapes=[pltpu.VMEM((slab, LANES), F32), pltpu.VMEM((slab, LANES), F32),
                        pltpu.VMEM((slab, LANES), F32), pltpu.VMEM((SLABS * PITCH, LANES), F32)],
        compiler_params=_cparams(("arbitrary", "arbitrary")),
        name="hy_lat_conv",
    )(z, xg, wa, wainv, gtab, gitab, hspec, bias)


def _angle(m, n):
    return (m % n).astype(F32) * (2.0 * math.pi / n)


def _ctx_tables():
    n = 2 * SEQ
    nf = SEQ + SUBLANES
    k = jnp.arange(nf, dtype=jnp.int32)[:, None]
    t = jnp.arange(SEQ, dtype=jnp.int32)[None, :]
    th = _angle(k * t, n)
    live = (k <= SEQ).astype(F32)
    wgt = jnp.where((k == 0) | (k == SEQ), 1.0, 2.0) * live / n
    fmat = jnp.concatenate([live * jnp.cos(th), -live * jnp.sin(th)], axis=0)
    finv = jnp.concatenate([(wgt * jnp.cos(th)).T, (-wgt * jnp.sin(th)).T], axis=1)
    return fmat.astype(BF16), finv.astype(BF16)


def _lat_tables():
    n1, n2 = FFT_N1, FFT_N2
    n = n1 * n2
    k1 = jnp.arange(K1U, dtype=jnp.int32)
    th1 = _angle(k1[:, None] * jnp.arange(NROW1, dtype=jnp.int32)[None, :], n1)
    wa = jnp.stack([jnp.cos(th1), -jnp.sin(th1)], axis=1).reshape(2 * K1U, NROW1)
    wa = jnp.pad(wa, ((0, SLABS - 2 * K1U), (0, 0)))
    wgt = jnp.where((k1 == 0) | (k1 == n1 // 2), 1.0, 2.0)[:, None] / n
    wainv = jnp.stack([wgt * jnp.cos(th1), -wgt * jnp.sin(th1)], axis=1).reshape(2 * K1U, NROW1).T
    wainv = jnp.pad(wainv, ((0, 0), (0, SLABS - 2 * K1U)))
    k2 = jnp.arange(n2, dtype=jnp.int32)
    nn2 = jnp.arange(n2, dtype=jnp.int32)
    ta = _angle(k1[:, None] * nn2[None, :], n)[:, None, :]
    tb = _angle(k2[:, None] * nn2[None, :], n2)[None, :, :]
    ca, sa, cb_, sb = jnp.cos(ta), jnp.sin(ta), jnp.cos(tb), jnp.sin(tb)
    gr = ca * cb_ - sa * sb
    gi = -(sa * cb_ + ca * sb)
    g = jnp.concatenate([jnp.concatenate([gr, -gi], axis=2), jnp.concatenate([gi, gr], axis=2)],
                        axis=1).astype(BF16)
    ginv = jnp.swapaxes(g, 1, 2)
    return wa.astype(BF16), wainv.astype(BF16), g, ginv


def _filter_features(L):
    t = jnp.linspace(0.0, 1.0, L, dtype=F32)[:, None]
    bands = (HY_EMB - 1) // 2
    w = (2.0 * math.pi / L) * jnp.arange(L, dtype=F32)[:, None]
    f = jnp.linspace(1e-4, bands - 1, bands, dtype=F32)[None, :]
    z = jnp.concatenate([t, jnp.cos(f * w), -jnp.sin(f * w)], axis=-1)
    z = jnp.pad(z, ((0, 0), (0, LANES - HY_EMB)))
    return z, jnp.broadcast_to(t, (L, LANES))


def _hyena_filter(p, L, blocked):
    feats, tcol = _filter_features(L)
    deltas = jnp.linspace(math.log(1e-2) / HY_FAST_PCT, math.log(1e-2) / HY_SLOW_PCT, HY_W, dtype=F32)
    absdelta = jnp.tile(jnp.abs(deltas), HY_ORDER * 2)[None, :]
    w3 = p["w3"]
    if blocked:
        reorder = lambda a: a.reshape(a.shape[0], HY_ORDER, 2, HY_W // LANES, LANES).transpose(
            0, 1, 3, 2, 4).reshape(a.shape[0], HY_ORDER * 2 * HY_W)
        w3, absdelta = reorder(w3), reorder(absdelta)
    return _hyfilt(feats, tcol, p["w1"], p["b1"], p["f1"], p["w2"], p["b2"], p["f2"], w3, absdelta)


def _hyena_ctx(u_hy, p, tabs):
    fmat, finv = tabs
    v, x1, x2 = _shortconv(u_hy, p["cw"], p["cb"], row_off=0, nseq=BATCH, seqlen=SEQ, tc=SEQ)
    hdec, s = _hyena_filter(p, SEQ, False)
    hspec = _ctx_spectrum(fmat, hdec, s)
    z = _ctx_conv(v, x1, fmat, finv, hspec, p["bias"][0:1], 0, F32)
    return _ctx_conv(z, x2, fmat, finv, hspec, p["bias"][1:2], 1, F32)


def _hyena_lat(u_hy, p, tabs):
    wa, wainv, gtab, gitab = tabs
    v, x1, x2 = _shortconv(u_hy, p["cw"], p["cb"], row_off=M_CTX, nseq=DEC_BATCH, seqlen=DEC_SEQ, tc=512)
    hdec, s = _hyena_filter(p, DEC_SEQ, True)
    hspec = _lat_spectrum(hdec, s, wa, gtab)
    z = _lat_conv(v, x1, wa, wainv, gtab, gitab, hspec, p["bias"][0:1], 0)
    return _lat_conv(z, x2, wa, wainv, gtab, gitab, hspec, p["bias"][1:2], 1)


def _stage3_kernel(*refs, nx, nctx):
    (mod_ref, g1_ref, wg_ref, ylc_ref, yll_ref, ymc_ref, yml_ref, yhc_ref, yhl_ref,
     wl_ref, wm_ref, wh_ref, wo_ref, g2_ref, xo_ref, xm2_ref) = refs[nx:]
    is_ctx = pl.program_id(0) < nctx
    x = _read_tokens(refs[:nx], nctx)
    xm = _rms(x, g1_ref[...]) * (1.0 + mod_ref[1:2, :]) + mod_ref[0:1, :]
    xb = xm.astype(BF16)
    merged = None
    branches = ((ylc_ref, yll_ref, wl_ref), (ymc_ref, yml_ref, wm_ref), (yhc_ref, yhl_ref, wh_ref))
    for bidx, (yc_ref, yl_ref, w_ref) in enumerate(branches):
        gate = _sigmoid(_dot(xb, wg_ref[:, D_MODEL * bidx:D_MODEL * (bidx + 1)]))
        y = jnp.where(is_ctx, yc_ref[...], yl_ref[...]).astype(BF16)
        term = gate * _dot(y, w_ref[...])
        merged = term if merged is None else merged + term
    xo = x + mod_ref[2:3, :] * _dot(merged.astype(BF16), wo_ref[...])
    xo_ref[...] = xo
    xm2 = _rms(xo, g2_ref[...]) * (1.0 + mod_ref[4:5, :]) + mod_ref[3:4, :]
    xm2_ref[...] = xm2.astype(BF16)


def _stage3(x, modl, g1, wg, ylru, ymla, yhy, wl, wm, wh, wo, g2):
    tm = TM3
    nctx = M_CTX // tm
    full = lambda shape: pl.BlockSpec(shape, lambda i: (0,) * len(shape))
    row = lambda cols: pl.BlockSpec((tm, cols), lambda i: (i, 0))
    ctx = lambda cols: pl.BlockSpec((tm, cols), lambda i: (jnp.minimum(i, nctx - 1), 0))
    lat = lambda cols: pl.BlockSpec((tm, cols), lambda i: (jnp.maximum(i - nctx, 0), 0))
    return pl.pallas_call(
        functools.partial(_stage3_kernel, nx=len(x), nctx=nctx),
        grid=(M_TOK // tm,),
        in_specs=_token_specs(x, tm) + [
                  pl.BlockSpec((None, SUBLANES, D_MODEL), lambda i: (_mod_row(i, tm), 0, 0)),
                  full((1, D_MODEL)), full((D_MODEL, 3 * D_MODEL)),
                  ctx(LRU_W), lat(LRU_W), ctx(N_HEADS * D_V), lat(N_HEADS * D_V), ctx(HY_W), lat(HY_W),
                  full((LRU_W, D_MODEL)), full((N_HEADS * D_V, D_MODEL)), full((HY_W, D_MODEL)),
                  full((D_MODEL, D_MODEL)), full((1, D_MODEL))],
        out_specs=[row(D_MODEL), row(D_MODEL)],
        out_shape=[jax.ShapeDtypeStruct((M_TOK, D_MODEL), F32), jax.ShapeDtypeStruct((M_TOK, D_MODEL), BF16)],
        compiler_params=_cparams(("arbitrary",)),
        name="stage3",
    )(*x, modl, g1, wg, *ylru, *ymla, *yhy, wl, wm, wh, wo, g2)


def _ffn_kernel(xm_ref, x_ref, mod_ref, wg_ref, wu_ref, wd_ref, o_ref, acc_sc, *, nchunks):
    j = pl.program_id(1)
    xb = xm_ref[...]
    g = _dot(xb, wg_ref[...])
    u = _dot(xb, wu_ref[...])
    hid = (g * _sigmoid(g) * u).astype(BF16)
    part = _dot(hid, wd_ref[...])

    @pl.when(j == 0)
    def _():
        acc_sc[...] = part

    @pl.when(j > 0)
    def _():
        acc_sc[...] = acc_sc[...] + part

    @pl.when(j == nchunks - 1)
    def _():
        o_ref[...] = x_ref[...] + mod_ref[5:6, :] * acc_sc[...]


def _ffn_dense(xm2, x, modl, wg, wu, wd):
    tm = TM_FFN
    nchunks = 1
    cw = D_FF // nchunks
    return pl.pallas_call(
        functools.partial(_ffn_kernel, nchunks=nchunks),
        grid=(M_TOK // tm, nchunks),
        in_specs=[pl.BlockSpec((tm, D_MODEL), lambda i, j: (i, 0)),
                  pl.BlockSpec((tm, D_MODEL), lambda i, j: (i, 0)),
                  pl.BlockSpec((None, SUBLANES, D_MODEL), lambda i, j: (_mod_row(i, tm), 0, 0)),
                  pl.BlockSpec((D_MODEL, cw), lambda i, j: (0, j)),
                  pl.BlockSpec((D_MODEL, cw), lambda i, j: (0, j)),
                  pl.BlockSpec((cw, D_MODEL), lambda i, j: (j, 0))],
        out_specs=pl.BlockSpec((tm, D_MODEL), lambda i, j: (i, 0)),
        out_shape=jax.ShapeDtypeStruct((M_TOK, D_MODEL), F32),
        scratch_shapes=[pltpu.VMEM((tm, D_MODEL), F32)],
        compiler_params=_cparams(("arbitrary", "arbitrary")),
        name="ffn_dense",
    )(xm2, x, modl, wg, wu, wd)


MOE_TILE = 256
MOE_ROWS = 2 * M_TOK + N_EXPERTS * MOE_TILE
MOE_TILES = MOE_ROWS // MOE_TILE
MOE_PAD_ROWS = MOE_ROWS - 2 * M_TOK
MOE_SEG = D_MODEL // LANES


def _to_row_tiles(val, ref, base):
    n = val.shape[0]
    for j in range(MOE_SEG):
        ref[pl.ds(base + j, n, stride=MOE_SEG), :] = val[:, LANES * j:LANES * (j + 1)]


def _from_row_tiles(ref, base, n):
    return jnp.concatenate([ref[pl.ds(base + j, n, stride=MOE_SEG), :] for j in range(MOE_SEG)], axis=1)


def _router_kernel(xm_ref, wr_ref, tri_ref, sel_ref, xp_ref, cnt_ref, base_sc):
    i = pl.program_id(0)
    xb = xm_ref[...]
    tm = xb.shape[0]
    lane = lax.broadcasted_iota(jnp.int32, (tm, LANES), 1)

    @pl.when(i == 0)
    def _():
        base_sc[...] = jnp.zeros_like(base_sc)

    logits = jnp.where(lane < N_EXPERTS, _dot(xb, wr_ref[...]), -1e30)
    mx = jnp.max(logits, axis=-1, keepdims=True)
    ex = jnp.exp(logits - mx)
    probs = ex / jnp.sum(ex, axis=-1, keepdims=True)
    p1 = jnp.max(probs, axis=-1, keepdims=True)
    i1 = jnp.min(jnp.where(probs == p1, lane, LANES), axis=-1, keepdims=True)
    rest = jnp.where(lane == i1, -1.0, probs)
    p2 = jnp.max(rest, axis=-1, keepdims=True)
    i2 = jnp.min(jnp.where(rest == p2, lane, LANES), axis=-1, keepdims=True)
    tot = p1 + p2
    oh1 = lane == i1
    oh2 = lane == i2
    oh = jnp.where(oh1 | oh2, 1.0, 0.0)
    before = base_sc[0:1, :] + _dot(tri_ref[...], oh.astype(BF16))
    r1 = jnp.sum(jnp.where(oh1, before, 0.0), axis=-1, keepdims=True)
    r2 = jnp.sum(jnp.where(oh2, before, 0.0), axis=-1, keepdims=True)
    base_sc[0:1, :] = base_sc[0:1, :] + jnp.sum(oh, axis=0, keepdims=True)
    cnt_ref[...] = base_sc[...]
    sel = jnp.where(lane == 0, p1 / tot, 0.0) + jnp.where(lane == 1, p2 / tot, 0.0)
    sel = sel + jnp.where(lane == 2, i1.astype(F32), 0.0) + jnp.where(lane == 3, i2.astype(F32), 0.0)
    sel_ref[...] = sel + jnp.where(lane == 4, r1, 0.0) + jnp.where(lane == 5, r2, 0.0)
    _to_row_tiles(xb.astype(F32), xp_ref, 0)


def _moe_router(xm2, wr):
    tm = TM_FFN
    tri = jnp.tril(jnp.ones((tm, tm), F32), -1).astype(BF16)
    return pl.pallas_call(
        _router_kernel,
        grid=(M_TOK // tm,),
        in_specs=[pl.BlockSpec((tm, D_MODEL), lambda i: (i, 0)),
                  pl.BlockSpec((D_MODEL, LANES), lambda i: (0, 0)),
                  pl.BlockSpec((tm, tm), lambda i: (0, 0))],
        out_specs=[pl.BlockSpec((tm, LANES), lambda i: (i, 0)),
                   pl.BlockSpec((tm * MOE_SEG, LANES), lambda i: (i, 0)),
                   pl.BlockSpec((SUBLANES, LANES), lambda i: (0, 0))],
        out_shape=[jax.ShapeDtypeStruct((M_TOK, LANES), F32),
                   jax.ShapeDtypeStruct((M_TOK * MOE_SEG, LANES), F32),
                   jax.ShapeDtypeStruct((SUBLANES, LANES), F32)],
        scratch_shapes=[pltpu.VMEM((SUBLANES, LANES), F32)],
        compiler_params=_cparams(("arbitrary",)),
        name="moe_router",
    )(xm2, wr, tri)


def _row_copy(src, srow8, dst, drow8, sem):
    aligned = lambda r: r if isinstance(r, int) else pl.multiple_of(r, MOE_SEG)
    return pltpu.make_async_copy(src.at[pl.ds(aligned(srow8), MOE_SEG), :],
                                 dst.at[pl.ds(aligned(drow8), MOE_SEG), :], sem)


def _dispatch_kernel(pos_ref, pad_ref, xp_ref, xs_ref, ring_sc, zero_sc, sem, *, nsteps):
    i = pl.program_id(0)
    slot = i % 2
    nrow = MOE_TILE
    slot_rows = nrow * MOE_SEG

    def wait_slot(s):
        for _ in range(2):
            pltpu.make_async_copy(ring_sc.at[s], xs_ref.at[pl.ds(0, slot_rows), :], sem.at[s]).wait()

    @pl.when(i >= 2)
    def _():
        wait_slot(slot)

    ring_sc[slot] = xp_ref[...]

    def body(t, c):
        _row_copy(ring_sc.at[slot], t * MOE_SEG, xs_ref, pos_ref[0, 2 * t], sem.at[slot]).start(priority=0)
        _row_copy(ring_sc.at[slot], t * MOE_SEG, xs_ref, pos_ref[0, 2 * t + 1], sem.at[slot]).start(priority=1)
        return c

    lax.fori_loop(0, nrow, body, 0, unroll=4)

    @pl.when(i == nsteps - 1)
    def _():
        zero_sc[...] = jnp.zeros_like(zero_sc)

        def zbody(t, c):
            _row_copy(zero_sc, 0, xs_ref, pad_ref[t], sem.at[2]).start()
            return c

        lax.fori_loop(0, MOE_PAD_ROWS, zbody, 0, unroll=4)
        wait_slot(1 - slot)
        wait_slot(slot)

        for _ in range(MOE_PAD_ROWS // nrow):
            pltpu.make_async_copy(ring_sc.at[0], xs_ref.at[pl.ds(0, slot_rows), :], sem.at[2]).wait()


def _moe_dispatch(xp, pos, padrows):
    nsteps = M_TOK // MOE_TILE
    return pl.pallas_call(
        functools.partial(_dispatch_kernel, nsteps=nsteps),
        grid=(nsteps,),
        in_specs=[pl.BlockSpec((None, 1, 2 * MOE_TILE), lambda i: (i, 0, 0), memory_space=pltpu.SMEM),
                  pl.BlockSpec(memory_space=pltpu.SMEM),
                  pl.BlockSpec((MOE_TILE * MOE_SEG, LANES), lambda i: (i, 0))],
        out_specs=pl.BlockSpec(memory_space=pl.ANY),
        out_shape=jax.ShapeDtypeStruct((MOE_ROWS * MOE_SEG, LANES), F32),
        scratch_shapes=[pltpu.VMEM((2, MOE_TILE * MOE_SEG, LANES), F32), pltpu.VMEM((MOE_SEG, LANES), F32),
                        pltpu.SemaphoreType.DMA((3,))],
        compiler_params=_cparams(("arbitrary",)),
        name="moe_dispatch",
    )(pos, padrows, xp)


def _experts_kernel(te_ref, nu_ref, xs_ref, wg_ref, wu_ref, wd_ref, ys_ref):
    i = pl.program_id(0)

    @pl.when(i < nu_ref[0])
    def _():
        xb = _from_row_tiles(xs_ref, 0, MOE_TILE).astype(BF16)
        g = _dot(xb, wg_ref[...])
        u = _dot(xb, wu_ref[...])
        hid = (g * _sigmoid(g) * u).astype(BF16)
        _to_row_tiles(_dot(hid, wd_ref[...]), ys_ref, 0)

    @pl.when(i >= nu_ref[0])
    def _():
        ys_ref[...] = jnp.zeros_like(ys_ref)


def _moe_experts(tile_expert, n_used, xs, wg, wu, wd):
    grid_spec = pltpu.PrefetchScalarGridSpec(
        num_scalar_prefetch=2,
        grid=(MOE_TILES,),
        in_specs=[pl.BlockSpec((MOE_TILE * MOE_SEG, LANES), lambda i, te, nu: (i, 0)),
                  pl.BlockSpec((None, D_MODEL, D_FF_E), lambda i, te, nu: (te[i], 0, 0)),
                  pl.BlockSpec((None, D_MODEL, D_FF_E), lambda i, te, nu: (te[i], 0, 0)),
                  pl.BlockSpec((None, D_FF_E, D_MODEL), lambda i, te, nu: (te[i], 0, 0))],
        out_specs=pl.BlockSpec((MOE_TILE * MOE_SEG, LANES), lambda i, te, nu: (i, 0)),
    )
    return pl.pallas_call(
        _experts_kernel,
        grid_spec=grid_spec,
        out_shape=jax.ShapeDtypeStruct((MOE_ROWS * MOE_SEG, LANES), F32),
        compiler_params=_cparams(("arbitrary",)),
        name="moe_experts",
    )(tile_expert, n_used, xs, wg, wu, wd)


def _combine_kernel(pos_ref, x_ref, mod_ref, sel_ref, ys_ref, oc_ref, ol_ref, buf_sc, sem, *, nsteps, nctx):
    i = pl.program_id(0)
    slot = i % 2
    nrow = MOE_TILE
    part = nrow * MOE_SEG

    def start(s, off):
        def body(t, c):
            for k in range(2):
                _row_copy(ys_ref, pos_ref[0, off + 2 * t + k], buf_sc, (2 * s + k) * part + t * MOE_SEG,
                          sem.at[s]).start(priority=k)
            return c
        lax.fori_loop(0, nrow, body, 0, unroll=4)

    @pl.when(i == 0)
    def _():
        start(0, 0)

    @pl.when(i + 1 < nsteps)
    def _():
        start(1 - slot, 2 * nrow)

    for k in range(2):
        pltpu.make_async_copy(ys_ref.at[pl.ds(0, part), :], buf_sc.at[pl.ds(0, part), :], sem.at[slot]).wait()
    lane = lax.broadcasted_iota(jnp.int32, (nrow, LANES), 1)
    sel = sel_ref[...]
    w1 = jnp.sum(jnp.where(lane == 0, sel, 0.0), axis=-1, keepdims=True)
    w2 = jnp.sum(jnp.where(lane == 1, sel, 0.0), axis=-1, keepdims=True)
    y = (w1 * _from_row_tiles(buf_sc, 2 * slot * part, nrow)
         + w2 * _from_row_tiles(buf_sc, (2 * slot + 1) * part, nrow))
    val = x_ref[...] + mod_ref[5:6, :] * y

    @pl.when(i < nctx)
    def _():
        oc_ref[...] = val

    @pl.when(i >= nctx)
    def _():
        ol_ref[...] = val


def _moe_combine(pos2, x, modl, sel, ys):
    nsteps = M_TOK // MOE_TILE
    tm = MOE_TILE
    nctx = M_CTX // tm
    return pl.pallas_call(
        functools.partial(_combine_kernel, nsteps=nsteps, nctx=nctx),
        grid=(nsteps,),
        in_specs=[pl.BlockSpec((None, 1, 4 * MOE_TILE), lambda i: (i, 0, 0), memory_space=pltpu.SMEM),
                  pl.BlockSpec((tm, D_MODEL), lambda i: (i, 0)),
                  pl.BlockSpec((None, SUBLANES, D_MODEL), lambda i: (_mod_row(i, tm), 0, 0)),
                  pl.BlockSpec((tm, LANES), lambda i: (i, 0)),
                  pl.BlockSpec(memory_space=pl.ANY)],
        out_specs=[pl.BlockSpec((tm, D_MODEL), lambda i: (jnp.minimum(i, nctx - 1), 0)),
                   pl.BlockSpec((tm, D_MODEL), lambda i: (jnp.maximum(i - nctx, 0), 0))],
        out_shape=[jax.ShapeDtypeStruct((M_CTX, D_MODEL), F32), jax.ShapeDtypeStruct((M_LAT, D_MODEL), F32)],
        scratch_shapes=[pltpu.VMEM((2 * 2 * MOE_TILE * MOE_SEG, LANES), F32), pltpu.SemaphoreType.DMA((2,))],
        compiler_params=_cparams(("arbitrary",)),
        name="moe_combine",
    )(pos2, x, modl, sel, ys)


def _ffn_moe(xm2, x, modl, wr, wg, wu, wd):
    sel, xp, cnt = _moe_router(xm2, wr)
    counts = cnt[0, :N_EXPERTS].astype(jnp.int32)
    padded = ((counts + MOE_TILE - 1) // MOE_TILE) * MOE_TILE
    ends = jnp.cumsum(padded)
    offs = ends - padded
    experts = sel[:, 2:4].astype(jnp.int32)
    ranks = sel[:, 4:6].astype(jnp.int32)
    pos = (offs[experts] + ranks) * MOE_SEG
    tile_start = jnp.arange(MOE_TILES, dtype=jnp.int32) * MOE_TILE
    tile_expert = jnp.minimum(jnp.sum(tile_start[:, None] >= ends[None, :], axis=1), N_EXPERTS - 1)
    n_used = (ends[-1] // MOE_TILE).astype(jnp.int32)[None]
    rows = jnp.arange(MOE_ROWS, dtype=jnp.int32)
    row_expert = jnp.repeat(tile_expert, MOE_TILE)
    written = (rows < ends[-1]) & (rows - offs[row_expert] < counts[row_expert])
    padrows = jnp.nonzero(~written, size=MOE_PAD_ROWS)[0].astype(jnp.int32) * MOE_SEG
    pos_tiles = pos.reshape(M_TOK // MOE_TILE, 1, 2 * MOE_TILE)
    xs = _moe_dispatch(xp, pos_tiles, padrows)
    ys = _moe_experts(tile_expert.astype(jnp.int32), n_used, xs, wg, wu, wd)
    nxt = jnp.concatenate([pos_tiles[1:], pos_tiles[-1:]], axis=0)
    pos2 = jnp.concatenate([pos_tiles, nxt], axis=2)
    return tuple(_moe_combine(pos2, x, modl, sel, ys))


def _block_diag(w):
    nb, bs, _ = w.shape
    eye = jnp.eye(nb, dtype=w.dtype)
    return jnp.einsum("njk,nm->njmk", w, eye).reshape(nb * bs, nb * bs)


def _head_pad_cols(w, width):
    r = w.shape[0]
    return jnp.pad(w, ((0, 0), (0, 0), (0, HEAD_PAD - width))).reshape(r, N_HEADS * HEAD_PAD)


def _swap_rope_pairs(a):
    nope, rope = a[..., :D_NOPE], a[..., D_NOPE:]
    sw = rope.reshape(rope.shape[:-1] + (D_ROPE // 2, 2))[..., ::-1].reshape(rope.shape)
    return jnp.concatenate([nope, sw], axis=-1)


def _head_gain(g):
    rows = jnp.stack([g, _swap_rope_pairs(g)], axis=0)
    return jnp.pad(rows, ((0, SUBLANES - 2), (0, HEAD_PAD - D_QK)))


def _rope_tables(tm):
    rows = DEC_SEQ // GRID_W
    row = jnp.repeat(jnp.arange(rows, dtype=F32), GRID_W)
    col = jnp.tile(jnp.arange(GRID_W, dtype=F32), rows)
    half = D_ROPE // 2
    inv_freq = ROPE_BASE ** (-jnp.arange(0, half, 2, dtype=F32) / half)
    ang = jnp.concatenate([row[:, None] * inv_freq, col[:, None] * inv_freq], axis=-1)
    cos, sin = jnp.cos(ang), jnp.sin(ang)
    cos2 = jnp.repeat(cos, 2, axis=1)
    sin2 = jnp.stack([-sin, sin], axis=-1).reshape(DEC_SEQ, D_ROPE)
    cos_t = jnp.pad(cos2, ((0, 0), (D_NOPE, HEAD_PAD - D_QK)), constant_values=1.0)
    sin_t = jnp.pad(sin2, ((0, 0), (D_NOPE, HEAD_PAD - D_QK)))
    cos_t = jnp.concatenate([jnp.ones((tm, HEAD_PAD), F32), cos_t], axis=0)
    sin_t = jnp.concatenate([jnp.zeros((tm, HEAD_PAD), F32), sin_t], axis=0)
    return cos_t, sin_t


def kernel(x_prompt, x_sample, cache_ckv, cache_krope, state_lru, c, c_ctx, norm1, norm2, w_ada, b_ada, w_in, mla_q_norm, mla_kv_norm, mla_w_uq, mla_w_uk, mla_w_uv, mla_q_qknorm, mla_k_qknorm, lru_conv_w, lru_conv_b, lru_w_gate, lru_b_gate, lru_lambda, hy_conv_w, hy_conv_b, hy_w1, hy_b1, hy_freq1, hy_w2, hy_b2, hy_freq2, hy_w3, hy_bias, w_lru_out, w_mla_out, w_hy_out, w_out, ffn_w_gate, ffn_w_up, ffn_w_down, moe_w_router, moe_w_gate, moe_w_up, moe_w_down):
    x = (x_prompt.reshape(M_CTX, D_MODEL), x_sample.reshape(M_LAT, D_MODEL))

    cond = jnp.concatenate([c_ctx[None, :], c, jnp.zeros((SUBLANES - 1 - DEC_BATCH, D_MODEL), F32)], axis=0)
    mod = _adaln(cond, w_ada, b_ada).reshape(DEPTH, SUBLANES, 6, D_MODEL)
    mod = jnp.pad(mod, ((0, 0), (0, 0), (0, SUBLANES - 6), (0, 0)))

    cos_t, sin_t = _rope_tables(TM1)
    ctx_tabs = _ctx_tables()
    lat_tabs = _lat_tables()
    zero_state = jnp.zeros((BATCH, SUBLANES, LRU_W), F32)

    ckv_out, kr_out, st_out = [], [], []
    for l in range(DEPTH):
        wl = w_in[l]
        wkr = jnp.concatenate([jnp.zeros((D_MODEL, D_NOPE), F32), wl[:, 896:928]], axis=1)
        krblk = lambda w: jnp.pad(w, ((0, 0), (0, HEAD_PAD - D_QK)))
        w1 = jnp.concatenate([wl[:, :896], krblk(wkr), krblk(_swap_rope_pairs(wkr)), wl[:, 928:2464]],
                             axis=1).astype(BF16)
        wgates = wl[:, 2464:].astype(BF16)
        wuq = _head_pad_cols(mla_w_uq[l], D_QK).astype(BF16)
        wuqs = _head_pad_cols(_swap_rope_pairs(mla_w_uq[l]), D_QK).astype(BF16)
        wuk = _head_pad_cols(mla_w_uk[l], D_NOPE).astype(BF16)
        wuv = mla_w_uv[l].reshape(KV_RANK, N_HEADS * D_V).astype(BF16)
        gq = _head_gain(mla_q_qknorm[l])
        gk = _head_gain(mla_k_qknorm[l])

        ulru, uhy, ckv, krb, q, k, v = _stage1(
            x, mod[l], norm1[l][None, :], w1, mla_kv_norm[l][None, :], mla_q_norm[l][None, :],
            wuq, wuqs, gq, wuk, gk, wuv, cos_t, sin_t)
        ckv_out.append(ckv[:M_CTX].reshape(BATCH, SEQ, KV_RANK))
        kr_out.append(krb[:M_CTX, D_NOPE:D_QK].reshape(BATCH, SEQ, D_ROPE))

        kc, vc = _kvprep(cache_ckv[:, l].reshape(DEC_BATCH * PAST_LEN, KV_RANK),
                         jnp.pad(cache_krope[:, l].reshape(DEC_BATCH * PAST_LEN, D_ROPE),
                                 ((0, 0), (D_NOPE, HEAD_PAD - D_QK))),
                         wuk, gk, wuv)
        ymla = (_attention_ctx(q, k, v), _attention_lat(q, k, v, kc, vc))

        lp = dict(
            cw=jnp.pad(lru_conv_w[l], ((0, SUBLANES - 4), (0, 0))), cb=lru_conv_b[l][None, :],
            wr=[_block_diag(lru_w_gate[l, d, 0]).astype(BF16) for d in range(2)],
            wi=[_block_diag(lru_w_gate[l, d, 1]).astype(BF16) for d in range(2)],
            br=[lru_b_gate[l, d, 0][None, :] for d in range(2)],
            bi=[lru_b_gate[l, d, 1][None, :] for d in range(2)],
            lam=[lru_lambda[l, d][None, :] for d in range(2)])
        y_c, stf, stb = _lru_mixer(ulru, lp, (zero_state, zero_state), row_off=0, nseq=BATCH, seqlen=SEQ, tc=SEQ)
        st_out.append(jnp.stack([stf, stb], axis=1))
        h0 = [jnp.broadcast_to(state_lru[:, l, d][:, None, :], (DEC_BATCH, SUBLANES, LRU_W)) for d in range(2)]
        y_l, _, _ = _lru_mixer(ulru, lp, h0, row_off=M_CTX, nseq=DEC_BATCH, seqlen=DEC_SEQ, tc=512)
        ylru = (y_c, y_l)

        hp = dict(
            cw=jnp.pad(hy_conv_w[l], ((0, SUBLANES - 3), (0, 0))), cb=hy_conv_b[l][None, :],
            w1=jnp.pad(hy_w1[l], ((0, LANES - HY_EMB), (0, 0))).astype(BF16), b1=hy_b1[l][None, :],
            f1=hy_freq1[l][None, :], w2=hy_w2[l].astype(BF16), b2=hy_b2[l][None, :], f2=hy_freq2[l][None, :],
            w3=hy_w3[l].astype(BF16), bias=hy_bias[l])
        yhy = (_hyena_ctx(uhy, hp, ctx_tabs), _hyena_lat(uhy, hp, lat_tabs))

        xmid, xm2 = _stage3(x, mod[l], norm1[l][None, :], wgates, ylru, ymla, yhy,
                            w_lru_out[l].astype(BF16), w_mla_out[l].astype(BF16), w_hy_out[l].astype(BF16),
                            w_out[l].astype(BF16), norm2[l][None, :])
        j = l // 2
        if l % 2 == 0:
            x = (_ffn_dense(xm2, xmid, mod[l], ffn_w_gate[j].astype(BF16), ffn_w_up[j].astype(BF16),
                            ffn_w_down[j].astype(BF16)),)
        else:
            wr = jnp.pad(moe_w_router[j], ((0, 0), (0, LANES - N_EXPERTS))).astype(BF16)
            x = _ffn_moe(xm2, xmid, mod[l], wr, moe_w_gate[j].astype(BF16), moe_w_up[j].astype(BF16),
                         moe_w_down[j].astype(BF16))

    xc, xl = x if len(x) == 2 else (x[0][:M_CTX], x[0][M_CTX:])
    y_prompt = xc.reshape(BATCH, SEQ, D_MODEL)
    y_sample = xl.reshape(DEC_BATCH, DEC_SEQ, D_MODEL)
    return (y_prompt, y_sample, jnp.stack(ckv_out, axis=1), jnp.stack(kr_out, axis=1), jnp.stack(st_out, axis=1))
```

```python
import functools
import math

import jax
import jax.numpy as jnp
from jax import lax
from jax.experimental import pallas as pl
from jax.experimental.pallas import tpu as pltpu

F32 = jnp.float32
BF16 = jnp.bfloat16

D_MODEL = 1024
BATCH = 32
SEQ = 256
DEPTH = 2
DEC_BATCH = 2
DEC_SEQ = 4096
PAST_LEN = 512
GRID_W = 64
EPS = 1e-6
LRU_W = 512
LRU_BLOCKS = 8
LRU_C = 8.0
N_HEADS = 8
D_NOPE = 64
D_ROPE = 32
D_QK = D_NOPE + D_ROPE
D_V = 64
Q_RANK = 256
KV_RANK = 128
ROPE_BASE = 10000.0
HY_W = 512
HY_ORDER = 2
HY_EMB = 33
HY_HID = 64
HY_FAST_PCT = 0.3
HY_SLOW_PCT = 1.5
D_FF = 2816
N_EXPERTS = 8
D_FF_E = 1408

LANES = 128
SUBLANES = 8
VMEM_LIMIT = 56 * 1024 * 1024

M_CTX = BATCH * SEQ
M_LAT = DEC_BATCH * DEC_SEQ
M_TOK = M_CTX + M_LAT
TM1 = 512
TM3 = 512
TM_FFN = 512
W1_COLS = 2688
HEAD_PAD = LANES
QK_SCALE = math.log2(math.e) / math.sqrt(D_QK)
ATTN_TQ = 256
ATTN_LAT_HEADS = 4
ATTN_LAT_GROUPS = (((0, 0, DEC_SEQ // 2),), ((0, DEC_SEQ // 2, DEC_SEQ), (1, 0, PAST_LEN)))

FFT_N1 = 64
FFT_N2 = 128


def _cparams(sem, vmem=VMEM_LIMIT):
    return pltpu.CompilerParams(dimension_semantics=sem, vmem_limit_bytes=vmem)


def _dot(a, b):
    return jnp.dot(a, b, preferred_element_type=F32)


def _rms(x, g):
    ms = jnp.mean(x * x, axis=-1, keepdims=True)
    return x * lax.rsqrt(ms + EPS) * g


def _sigmoid(x):
    return 1.0 / (1.0 + jnp.exp(-x))


def _ada_kernel(c_ref, w_ref, b_ref, o_ref):
    c = c_ref[...]
    s = (c * _sigmoid(c)).astype(BF16)
    o_ref[...] = _dot(s, w_ref[...].astype(BF16)) + b_ref[...]


def _adaln(cond, w_ada, b_ada):
    tn = 1024
    n6 = 6 * D_MODEL
    return pl.pallas_call(
        _ada_kernel,
        grid=(DEPTH, n6 // tn),
        in_specs=[
            pl.BlockSpec((SUBLANES, D_MODEL), lambda l, j: (0, 0)),
            pl.BlockSpec((None, D_MODEL, tn), lambda l, j: (l, 0, j)),
            pl.BlockSpec((None, 1, tn), lambda l, j: (l, 0, j)),
        ],
        out_specs=pl.BlockSpec((None, SUBLANES, tn), lambda l, j: (l, 0, j)),
        out_shape=jax.ShapeDtypeStruct((DEPTH, SUBLANES, n6), F32),
        compiler_params=_cparams(("arbitrary", "arbitrary")),
        name="adaln",
    )(cond, w_ada, b_ada.reshape(DEPTH, 1, n6))


def _mod_row(i, tm):
    nctx = M_CTX // tm
    per = DEC_SEQ // tm
    return jnp.where(i < nctx, 0, 1 + (i - nctx) // per)


def _rope_blk(i, tm):
    nctx = M_CTX // tm
    per = DEC_SEQ // tm
    return jnp.where(i < nctx, 0, 1 + (i - nctx) % per)


def _finish_head(raw, raw_sw, gc, gs, out_ref, sl):
    ms = jnp.sum(raw * raw, axis=-1, keepdims=True) * (1.0 / D_QK)
    rs = lax.rsqrt(ms + EPS)
    val = raw * gc
    if gs is not None:
        val = val + raw_sw * gs
    out_ref[:, sl] = (val * rs).astype(BF16)


def _read_tokens(x_refs, nctx):
    if len(x_refs) == 1:
        return x_refs[0][...]
    return jnp.where(pl.program_id(0) < nctx, x_refs[0][...], x_refs[1][...])


def _token_specs(x, tm):
    nctx = M_CTX // tm
    if len(x) == 1:
        return [pl.BlockSpec((tm, D_MODEL), lambda i: (i, 0))]
    return [pl.BlockSpec((tm, D_MODEL), lambda i: (jnp.minimum(i, nctx - 1), 0)),
            pl.BlockSpec((tm, D_MODEL), lambda i: (jnp.maximum(i - nctx, 0), 0))]


def _stage1_kernel(*refs, nx, nctx):
    (mod_ref, g1_ref, w1_ref, gkv_ref, gqn_ref, wuq_ref, wuqs_ref, gq_ref, wuk_ref, gk_ref,
     wuv_ref, cos_ref, sin_ref, ulru_ref, uhy_ref, ckv_ref, krb_ref, q_ref, k_ref, v_ref) = refs[nx:]
    x = _read_tokens(refs[:nx], nctx)
    xm = _rms(x, g1_ref[...]) * (1.0 + mod_ref[1:2, :]) + mod_ref[0:1, :]
    xb = xm.astype(BF16)
    ulru_ref[...] = _dot(xb, w1_ref[:, 0:512])
    qc = _dot(xb, w1_ref[:, 512:768])
    ckv = _dot(xb, w1_ref[:, 768:896])
    krb = _dot(xb, w1_ref[:, 896:1024])
    krs = _dot(xb, w1_ref[:, 1024:1152])
    uhy_ref[...] = _dot(xb, w1_ref[:, 1152:2688])
    ckvn = _rms(ckv, gkv_ref[...])
    ckv_ref[...] = ckvn
    krb_ref[...] = krb
    qn = _rms(qc, gqn_ref[...]).astype(BF16)
    cb = ckvn.astype(BF16)
    v_ref[...] = _dot(cb, wuv_ref[...]).astype(BF16)
    cos = cos_ref[...]
    sin = sin_ref[...]
    gcq = cos * (gq_ref[0:1, :] * QK_SCALE)
    gsq = sin * (gq_ref[1:2, :] * QK_SCALE)
    gck = cos * gk_ref[0:1, :]
    gsk = sin * gk_ref[1:2, :]
    for pair in range(N_HEADS // 2):
        ps = slice(2 * HEAD_PAD * pair, 2 * HEAD_PAD * (pair + 1))
        qraw, qsw, kraw = _dot(qn, wuq_ref[:, ps]), _dot(qn, wuqs_ref[:, ps]), _dot(cb, wuk_ref[:, ps])
        for j in range(2):
            hs = slice(HEAD_PAD * j, HEAD_PAD * (j + 1))
            sl = slice(HEAD_PAD * (2 * pair + j), HEAD_PAD * (2 * pair + j + 1))
            _finish_head(qraw[:, hs], qsw[:, hs], gcq, gsq, q_ref, sl)
            _finish_head(kraw[:, hs] + krb, krs, gck, gsk, k_ref, sl)


def _stage1(x, modl, g1, w1, gkv, gqn, wuq, wuqs, gq, wuk, gk, wuv, cos_t, sin_t):
    tm = TM1
    full = lambda shape: pl.BlockSpec(shape, lambda i: (0,) * len(shape))
    row = lambda cols: pl.BlockSpec((tm, cols), lambda i: (i, 0))
    hw = N_HEADS * HEAD_PAD
    return pl.pallas_call(
        functools.partial(_stage1_kernel, nx=len(x), nctx=M_CTX // tm),
        grid=(M_TOK // tm,),
        in_specs=_token_specs(x, tm) + [
            pl.BlockSpec((None, SUBLANES, D_MODEL), lambda i: (_mod_row(i, tm), 0, 0)),
            full((1, D_MODEL)),
            full((D_MODEL, W1_COLS)),
            full((1, KV_RANK)),
            full((1, Q_RANK)),
            full((Q_RANK, hw)),
            full((Q_RANK, hw)),
            full((SUBLANES, HEAD_PAD)),
            full((KV_RANK, hw)),
            full((SUBLANES, HEAD_PAD)),
            full((KV_RANK, N_HEADS * D_V)),
            pl.BlockSpec((tm, LANES), lambda i: (_rope_blk(i, tm), 0)),
            pl.BlockSpec((tm, LANES), lambda i: (_rope_blk(i, tm), 0)),
        ],
        out_specs=[row(LRU_W), row(3 * HY_W), row(KV_RANK), row(LANES), row(hw), row(hw), row(N_HEADS * D_V)],
        out_shape=[
            jax.ShapeDtypeStruct((M_TOK, LRU_W), F32),
            jax.ShapeDtypeStruct((M_TOK, 3 * HY_W), F32),
            jax.ShapeDtypeStruct((M_TOK, KV_RANK), F32),
            jax.ShapeDtypeStruct((M_TOK, LANES), F32),
            jax.ShapeDtypeStruct((M_TOK, hw), BF16),
            jax.ShapeDtypeStruct((M_TOK, hw), BF16),
            jax.ShapeDtypeStruct((M_TOK, N_HEADS * D_V), BF16),
        ],
        compiler_params=_cparams(("arbitrary",)),
        name="stage1",
    )(*x, modl, g1, w1, gkv, gqn, wuq, wuqs, gq, wuk, gk, wuv, cos_t, sin_t)


def _kvprep_kernel(ckv_ref, krb_ref, wuk_ref, gk_ref, wuv_ref, k_ref, v_ref):
    cb = ckv_ref[...].astype(BF16)
    v_ref[...] = _dot(cb, wuv_ref[...]).astype(BF16)
    krb = krb_ref[...]
    for h in range(N_HEADS):
        sl = slice(HEAD_PAD * h, HEAD_PAD * (h + 1))
        _finish_head(_dot(cb, wuk_ref[:, sl]) + krb, None, gk_ref[0:1, :], None, k_ref, sl)


def _kvprep(ckv, krb, wuk, gk, wuv):
    rows = ckv.shape[0]
    tm = TM1
    hw = N_HEADS * HEAD_PAD
    full = lambda shape: pl.BlockSpec(shape, lambda i: (0,) * len(shape))
    row = lambda cols: pl.BlockSpec((tm, cols), lambda i: (i, 0))
    return pl.pallas_call(
        _kvprep_kernel,
        grid=(rows // tm,),
        in_specs=[row(KV_RANK), row(LANES), full((KV_RANK, hw)), full((SUBLANES, HEAD_PAD)),
                  full((KV_RANK, N_HEADS * D_V))],
        out_specs=[row(hw), row(N_HEADS * D_V)],
        out_shape=[jax.ShapeDtypeStruct((rows, hw), BF16), jax.ShapeDtypeStruct((rows, N_HEADS * D_V), BF16)],
        compiler_params=_cparams(("arbitrary",)),
        name="kvprep",
    )(ckv, krb, wuk, gk, wuv)


def _attn_kernel(*refs, heads, nseg, groups):
    q_ref = refs[0]
    k_refs = refs[1:1 + nseg]
    v_refs = refs[1 + nseg:1 + 2 * nseg]
    o_ref = refs[1 + 2 * nseg]
    tq = q_ref.shape[0]
    lane = lax.broadcasted_iota(jnp.int32, (tq, LANES), 1)
    low = lane < D_V
    for pair in range(heads // 2):
        outs = []
        for j in range(2):
            h = 2 * pair + j
            sl = slice(HEAD_PAD * h, HEAD_PAD * (h + 1))
            q = q_ref[:, sl]
            parts = []
            for group in groups:
                s = [lax.dot_general(q, k_refs[seg][r0:r1, sl], (((1,), (1,)), ((), ())),
                                     preferred_element_type=F32) for seg, r0, r1 in group]
                m = jnp.max(s[0], axis=-1, keepdims=True)
                for si in s[1:]:
                    m = jnp.maximum(m, jnp.max(si, axis=-1, keepdims=True))
                acc = None
                den = None
                for si, (seg, r0, r1) in zip(s, group):
                    p = jnp.exp2(si - m)
                    d = jnp.sum(p, axis=-1, keepdims=True)
                    o = _dot(p.astype(BF16), v_refs[seg][r0:r1, LANES * pair:LANES * (pair + 1)])
                    acc = o if acc is None else acc + o
                    den = d if den is None else den + d
                parts.append((m, acc, den))
            if len(parts) == 1:
                _, acc, den = parts[0]
            else:
                mall = parts[0][0]
                for m, _, _ in parts[1:]:
                    mall = jnp.maximum(mall, m)
                acc = None
                den = None
                for m, a, d in parts:
                    w = jnp.exp2(m - mall)
                    acc = w * a if acc is None else acc + w * a
                    den = w * d if den is None else den + w * d
            outs.append(acc / den)
        o_ref[:, LANES * pair:LANES * (pair + 1)] = jnp.where(low, outs[0], outs[1]).astype(BF16)


def _attention_ctx(q, k, v):
    hw = N_HEADS * HEAD_PAD
    vw = N_HEADS * D_V
    return pl.pallas_call(
        functools.partial(_attn_kernel, heads=N_HEADS, nseg=1, groups=(((0, 0, SEQ),),)),
        grid=(BATCH,),
        in_specs=[
            pl.BlockSpec((SEQ, hw), lambda b: (b, 0)),
            pl.BlockSpec((SEQ, hw), lambda b: (b, 0)),
            pl.BlockSpec((SEQ, vw), lambda b: (b, 0)),
        ],
        out_specs=pl.BlockSpec((SEQ, vw), lambda b: (b, 0)),
        out_shape=jax.ShapeDtypeStruct((M_CTX, vw), BF16),
        compiler_params=_cparams(("arbitrary",)),
        name="attn_ctx",
    )(q, k, v)


def _attention_lat(q, k, v, kc, vc):
    tq = ATTN_TQ
    nq = DEC_SEQ // tq
    qoff = M_CTX // tq
    koff = M_CTX // DEC_SEQ
    hp = ATTN_LAT_HEADS
    return pl.pallas_call(
        functools.partial(_attn_kernel, heads=hp, nseg=2, groups=ATTN_LAT_GROUPS),
        grid=(DEC_BATCH, N_HEADS // hp, nq),
        in_specs=[
            pl.BlockSpec((tq, hp * HEAD_PAD), lambda b, p, i: (qoff + b * nq + i, p)),
            pl.BlockSpec((DEC_SEQ, hp * HEAD_PAD), lambda b, p, i: (koff + b, p)),
            pl.BlockSpec((PAST_LEN, hp * HEAD_PAD), lambda b, p, i: (b, p)),
            pl.BlockSpec((DEC_SEQ, hp * D_V), lambda b, p, i: (koff + b, p)),
            pl.BlockSpec((PAST_LEN, hp * D_V), lambda b, p, i: (b, p)),
        ],
        out_specs=pl.BlockSpec((tq, hp * D_V), lambda b, p, i: (b * nq + i, p)),
        out_shape=jax.ShapeDtypeStruct((M_LAT, N_HEADS * D_V), BF16),
        compiler_params=_cparams(("arbitrary", "arbitrary", "arbitrary")),
        name="attn_lat",
    )(q, k, kc, v, vc)


def _lru_kernel(*refs, reverse, tc, nchunks):
    if reverse:
        (up_ref, uc_ref, un_ref, hf_ref, cw_ref, cb_ref, wr_ref, wi_ref, br_ref, bi_ref, lam_ref, h0_ref,
         y_ref, st_ref, ext_sc, a_sc, b_sc, p_sc, h_sc, car_sc) = refs
    else:
        (up_ref, uc_ref, un_ref, cw_ref, cb_ref, wr_ref, wi_ref, br_ref, bi_ref, lam_ref, h0_ref,
         y_ref, st_ref, ext_sc, a_sc, b_sc, p_sc, h_sc, car_sc) = refs
    c = pl.program_id(1)
    chunk = (nchunks - 1 - c) if reverse else c
    prev = jnp.where(chunk == 0, 0.0, up_ref[...])
    nxt = jnp.where(chunk == nchunks - 1, 0.0, un_ref[...])
    ext_sc[0:SUBLANES, :] = prev
    ext_sc[SUBLANES:SUBLANES + tc, :] = uc_ref[...]
    ext_sc[SUBLANES + tc:2 * SUBLANES + tc, :] = nxt
    xc = cb_ref[...]
    for k in range(4):
        xc = xc + cw_ref[k:k + 1, :] * ext_sc[SUBLANES - 2 + k:SUBLANES - 2 + k + tc, :]
    xb = xc.astype(BF16)
    r = _sigmoid(_dot(xb, wr_ref[...]) + br_ref[...])
    gi = _sigmoid(_dot(xb, wi_ref[...]) + bi_ref[...])
    lam = lam_ref[...]
    logsig = -(jnp.maximum(-lam, 0.0) + jnp.log1p(jnp.exp(-jnp.abs(lam))))
    la = LRU_C * r * logsig
    a = jnp.exp(la)
    v = -jnp.tanh(la) * (a * a + 1.0)
    bc = jnp.where(v > 0.0, v * lax.rsqrt(v), 0.0) * (gi * xc)

    @pl.when(c == 0)
    def _():
        car_sc[...] = h0_ref[...]

    nseg = SUBLANES
    sl = tc // nseg
    sp = sl + SUBLANES
    nlb = LRU_W // LANES
    for j in range(nlb):
        for s in range(nseg):
            rows = slice(sl * s, sl * (s + 1))
            dst = slice((j * nseg + s) * sp, (j * nseg + s) * sp + sl)
            a_sc[dst, :] = a[rows, LANES * j:LANES * (j + 1)]
            b_sc[dst, :] = bc[rows, LANES * j:LANES * (j + 1)]

    def body(k, carry):
        i = (sl - 1 - k) if reverse else k
        hs, ps = carry
        hn, pn = [], []
        for j in range(nlb):
            idx = pl.ds(j * nseg * sp + i, nseg, stride=sp)
            av = a_sc[idx, :]
            h = av * hs[j] + b_sc[idx, :]
            p = av * ps[j]
            p_sc[idx, :] = p
            h_sc[idx, :] = h
            hn.append(h)
            pn.append(p)
        return tuple(hn), tuple(pn)

    zero = jnp.zeros((nseg, LANES), F32)
    one = jnp.ones((nseg, LANES), F32)
    hend, pend = lax.fori_loop(0, sl, body, ((zero,) * nlb, (one,) * nlb), unroll=4)

    order = range(nseg - 1, -1, -1) if reverse else range(nseg)
    for j in range(nlb):
        lanes = slice(LANES * j, LANES * (j + 1))
        cin = car_sc[0:1, lanes]
        for s in order:
            rows = slice(sl * s, sl * (s + 1))
            src = slice((j * nseg + s) * sp, (j * nseg + s) * sp + sl)
            h = h_sc[src, :] + p_sc[src, :] * cin
            if reverse:
                y_ref[rows, lanes] = (hf_ref[rows, lanes] + h).astype(BF16)
            else:
                y_ref[rows, lanes] = h
            cin = hend[j][s:s + 1, :] + pend[j][s:s + 1, :] * cin
        car_sc[0:1, lanes] = cin
        st_ref[:, lanes] = jnp.broadcast_to(cin, (SUBLANES, LANES))


def _lru_dir(u, hf, cw, cb, wr, wi, br, bi, lam, h0, *, reverse, row_off, nseq, seqlen, tc):
    nchunks = seqlen // tc
    hb = M_TOK // SUBLANES

    def chunk_of(c):
        return (nchunks - 1 - c) if reverse else c

    def cur(b, c):
        return ((row_off + b * seqlen) // tc + chunk_of(c), 0)

    def prv(b, c):
        return (jnp.maximum((row_off + b * seqlen + chunk_of(c) * tc) // SUBLANES - 1, 0), 0)

    def nxt(b, c):
        return (jnp.minimum((row_off + b * seqlen + (chunk_of(c) + 1) * tc) // SUBLANES, hb - 1), 0)

    def out_cur(b, c):
        return ((b * seqlen) // tc + chunk_of(c), 0)

    full = lambda shape: pl.BlockSpec(shape, lambda b, c: (0,) * len(shape))
    in_specs = [pl.BlockSpec((SUBLANES, LRU_W), prv), pl.BlockSpec((tc, LRU_W), cur),
                pl.BlockSpec((SUBLANES, LRU_W), nxt)]
    args = [u, u, u]
    if reverse:
        in_specs.append(pl.BlockSpec((tc, LRU_W), out_cur))
        args.append(hf)
    in_specs += [full((SUBLANES, LRU_W)), full((1, LRU_W)), full((LRU_W, LRU_W)), full((LRU_W, LRU_W)),
                 full((1, LRU_W)), full((1, LRU_W)), full((1, LRU_W)),
                 pl.BlockSpec((None, SUBLANES, LRU_W), lambda b, c: (b, 0, 0))]
    args += [cw, cb, wr, wi, br, bi, lam, h0]
    return pl.pallas_call(
        functools.partial(_lru_kernel, reverse=reverse, tc=tc, nchunks=nchunks),
        grid=(nseq, nchunks),
        in_specs=in_specs,
        out_specs=[pl.BlockSpec((tc, LRU_W), out_cur),
                   pl.BlockSpec((None, SUBLANES, LRU_W), lambda b, c: (b, 0, 0))],
        out_shape=[jax.ShapeDtypeStruct((nseq * seqlen, LRU_W), BF16 if reverse else F32),
                   jax.ShapeDtypeStruct((nseq, SUBLANES, LRU_W), F32)],
        scratch_shapes=[pltpu.VMEM((tc + 2 * SUBLANES, LRU_W), F32)]
        + [pltpu.VMEM(((LRU_W // LANES) * (tc + SUBLANES * SUBLANES), LANES), F32)] * 4
        + [pltpu.VMEM((SUBLANES, LRU_W), F32)],
        compiler_params=_cparams(("arbitrary", "arbitrary")),
        name="lru_bwd" if reverse else "lru_fwd",
    )(*args)


def _lru_mixer(u, p, h0, *, row_off, nseq, seqlen, tc):
    kw = dict(row_off=row_off, nseq=nseq, seqlen=seqlen, tc=tc)
    hf, stf = _lru_dir(u, None, p["cw"], p["cb"], p["wr"][0], p["wi"][0], p["br"][0], p["bi"][0], p["lam"][0],
                       h0[0], reverse=False, **kw)
    y, stb = _lru_dir(u, hf, p["cw"], p["cb"], p["wr"][1], p["wi"][1], p["br"][1], p["bi"][1], p["lam"][1],
                      h0[1], reverse=True, **kw)
    return y, stf[:, 0, :], stb[:, 0, :]


def _shortconv_kernel(up_ref, uc_ref, un_ref, cw_ref, cb_ref, v_ref, x1_ref, x2_ref, ext_sc, *, tc, nchunks):
    c = pl.program_id(1)
    prev = jnp.where(c == 0, 0.0, up_ref[...])
    nxt = jnp.where(c == nchunks - 1, 0.0, un_ref[...])
    ext_sc[0:SUBLANES, :] = prev
    ext_sc[SUBLANES:SUBLANES + tc, :] = uc_ref[...]
    ext_sc[SUBLANES + tc:2 * SUBLANES + tc, :] = nxt
    for part, o_ref in enumerate((v_ref, x1_ref, x2_ref)):
        cs = slice(HY_W * part, HY_W * (part + 1))
        acc = cb_ref[:, cs]
        for k in range(3):
            acc = acc + cw_ref[k:k + 1, cs] * ext_sc[SUBLANES - 1 + k:SUBLANES - 1 + k + tc, cs]
        o_ref[...] = acc


def _shortconv(u, cw, cb, *, row_off, nseq, seqlen, tc):
    nchunks = seqlen // tc
    w = 3 * HY_W
    hb = M_TOK // SUBLANES
    cur = lambda b, c: ((row_off + b * seqlen) // tc + c, 0)
    prv = lambda b, c: (jnp.maximum((row_off + b * seqlen + c * tc) // SUBLANES - 1, 0), 0)
    nxt = lambda b, c: (jnp.minimum((row_off + b * seqlen + (c + 1) * tc) // SUBLANES, hb - 1), 0)
    out = lambda b, c: ((b * seqlen) // tc + c, 0)
    full = lambda shape: pl.BlockSpec(shape, lambda b, c: (0,) * len(shape))
    rows = nseq * seqlen
    return pl.pallas_call(
        functools.partial(_shortconv_kernel, tc=tc, nchunks=nchunks),
        grid=(nseq, nchunks),
        in_specs=[pl.BlockSpec((SUBLANES, w), prv), pl.BlockSpec((tc, w), cur), pl.BlockSpec((SUBLANES, w), nxt),
                  full((SUBLANES, w)), full((1, w))],
        out_specs=[pl.BlockSpec((tc, HY_W), out)] * 3,
        out_shape=[jax.ShapeDtypeStruct((rows, HY_W), F32)] * 3,
        scratch_shapes=[pltpu.VMEM((tc + 2 * SUBLANES, w), F32)],
        compiler_params=_cparams(("arbitrary", "arbitrary")),
        name="hy_shortconv",
    )(u, u, u, cw, cb)


def _hyfilt_kernel(z_ref, t_ref, w1_ref, b1_ref, f1_ref, w2_ref, b2_ref, f2_ref, w3_ref, ad_ref, h_ref, s_ref):
    i = pl.program_id(0)
    z = z_ref[...].astype(BF16)
    h = jnp.sin(f1_ref[...] * (_dot(z, w1_ref[...]) + b1_ref[...]))
    h = jnp.sin(f2_ref[...] * (_dot(h.astype(BF16), w2_ref[...]) + b2_ref[...]))
    h = _dot(h.astype(BF16), w3_ref[...])
    t = t_ref[...]
    ncol = h.shape[1] // LANES
    win = jnp.concatenate([jnp.exp(-t * ad_ref[:, LANES * j:LANES * (j + 1)]) for j in range(ncol)], axis=1)
    h = h * win
    h_ref[...] = h

    @pl.when(i == 0)
    def _():
        s_ref[...] = jnp.zeros_like(s_ref)

    s_ref[0:1, :] = s_ref[0:1, :] + jnp.sum(jnp.abs(h), axis=0, keepdims=True)


def _hyfilt(feats, tcol, w1, b1, f1, w2, b2, f2, w3, absdelta):
    L = feats.shape[0]
    tl = min(L, 512)
    wcols = HY_ORDER * 2 * HY_W
    full = lambda shape: pl.BlockSpec(shape, lambda i: (0,) * len(shape))
    return pl.pallas_call(
        _hyfilt_kernel,
        grid=(L // tl,),
        in_specs=[pl.BlockSpec((tl, LANES), lambda i: (i, 0)), pl.BlockSpec((tl, LANES), lambda i: (i, 0)),
                  full((LANES, HY_HID)), full((1, HY_HID)), full((1, HY_HID)),
                  full((HY_HID, HY_HID)), full((1, HY_HID)), full((1, HY_HID)),
                  full((HY_HID, wcols)), full((1, wcols))],
        out_specs=[pl.BlockSpec((tl, wcols), lambda i: (i, 0)), full((SUBLANES, wcols))],
        out_shape=[jax.ShapeDtypeStruct((L, wcols), F32), jax.ShapeDtypeStruct((SUBLANES, wcols), F32)],
        compiler_params=_cparams(("arbitrary",)),
        name="hy_filter",
    )(feats, tcol, w1, b1, f1, w2, b2, f2, w3, absdelta)


def _combine_spectrum(zr, zi, s_ref, hr_out, hi_out):
    for o in range(HY_ORDER):
        f = slice(2 * HY_W * o, 2 * HY_W * o + HY_W)
        b = slice(2 * HY_W * o + HY_W, 2 * HY_W * (o + 1))
        den = s_ref[0:1, f] + s_ref[0:1, b] + EPS
        hr_out(o, (zr[:, f] + zr[:, b]) / den)
        hi_out(o, (zi[:, f] - zi[:, b]) / den)


def _ctx_spec_kernel(f_ref, h_ref, s_ref, o_ref):
    n = f_ref.shape[0] // 2
    z = _dot(f_ref[...], h_ref[...].astype(BF16))
    zr, zi = z[:n], z[n:]

    def put_r(o, val):
        o_ref[0, :, HY_W * o:HY_W * (o + 1)] = val

    def put_i(o, val):
        o_ref[1, :, HY_W * o:HY_W * (o + 1)] = val

    _combine_spectrum(zr, zi, s_ref, put_r, put_i)


def _ctx_spectrum(fmat, hdec, s):
    n = fmat.shape[0] // 2
    return pl.pallas_call(
        _ctx_spec_kernel,
        out_shape=jax.ShapeDtypeStruct((2, n, HY_ORDER * HY_W), F32),
        compiler_params=pltpu.CompilerParams(vmem_limit_bytes=VMEM_LIMIT),
        name="hy_ctx_spectrum",
    )(fmat, hdec, s)


def _ctx_conv_kernel(z_ref, x_ref, f_ref, fi_ref, h_ref, bias_ref, o_ref, *, nb, seqlen):
    n = f_ref.shape[0] // 2
    hr = h_ref[0]
    hi = h_ref[1]
    for b in range(nb):
        rs = slice(seqlen * b, seqlen * (b + 1))
        zt = z_ref[rs, :]
        zf = _dot(f_ref[...], zt.astype(BF16))
        zr, zi = zf[:n], zf[n:]
        y = jnp.concatenate([zr * hr - zi * hi, zr * hi + zi * hr], axis=0).astype(BF16)
        conv = _dot(fi_ref[...], y)
        o_ref[rs, :] = (x_ref[rs, :] * (conv + zt * bias_ref[...])).astype(o_ref.dtype)


def _ctx_conv(z, xg, fmat, finv, hspec, bias, order, out_dtype):
    nb = 4
    n = fmat.shape[0] // 2
    rows = nb * SEQ
    return pl.pallas_call(
        functools.partial(_ctx_conv_kernel, nb=nb, seqlen=SEQ),
        grid=(BATCH // nb,),
        in_specs=[pl.BlockSpec((rows, HY_W), lambda i: (i, 0)), pl.BlockSpec((rows, HY_W), lambda i: (i, 0)),
                  pl.BlockSpec(fmat.shape, lambda i: (0, 0)), pl.BlockSpec(finv.shape, lambda i: (0, 0)),
                  pl.BlockSpec((2, n, HY_W), lambda i: (0, 0, order)),
                  pl.BlockSpec((1, HY_W), lambda i: (0, 0))],
        out_specs=pl.BlockSpec((rows, HY_W), lambda i: (i, 0)),
        out_shape=jax.ShapeDtypeStruct((M_CTX, HY_W), out_dtype),
        compiler_params=_cparams(("arbitrary",)),
        name="hy_ctx_conv",
    )(z, xg, fmat, finv, hspec, bias)


K1U = FFT_N1 // 2 + 1
SLABS = 72
PITCH = FFT_N2 + SUBLANES
NROW1 = FFT_N1 // 2
LAT_UNROLL_R = 32
LAT_UNROLL_K = 33


def _pitch_copy_in(src_ref, col, dst_sc):
    for n1 in range(NROW1):
        dst_sc[PITCH * n1:PITCH * n1 + FFT_N2, :] = src_ref[FFT_N2 * n1:FFT_N2 * (n1 + 1), col]


def _dft_stage_a(zp_scs, wa_ref, a_scs):
    def body(r, c):
        x = jnp.concatenate([zp_sc[pl.ds(r, NROW1, stride=PITCH), :] for zp_sc in zp_scs], axis=1)
        a = _dot(wa_ref[...], x.astype(BF16))
        for j, a_sc in enumerate(a_scs):
            a_sc[pl.ds(r, SLABS, stride=PITCH), :] = a[:, LANES * j:LANES * (j + 1)]
        return c

    lax.fori_loop(0, FFT_N2, body, 0, unroll=LAT_UNROLL_R)


def _load_k1(a_sc, k1):
    base = pl.multiple_of(k1 * (2 * PITCH), SUBLANES)
    a = jnp.concatenate([a_sc[pl.ds(base, FFT_N2), :], a_sc[pl.ds(base + PITCH, FFT_N2), :]], axis=0)
    return base, a.astype(BF16)


def _lat_spec_kernel(h_ref, s_ref, wa_ref, g_ref, o_ref, hf_sc, hb_sc, af_sc, ab_sc):
    _pitch_copy_in(h_ref, slice(0, LANES), hf_sc)
    _pitch_copy_in(h_ref, slice(LANES, 2 * LANES), hb_sc)
    _dft_stage_a((hf_sc, hb_sc), wa_ref, (af_sc, ab_sc))
    den = s_ref[0:1, 0:LANES] + s_ref[0:1, LANES:2 * LANES] + EPS

    def kbody(k1, c):
        _, af = _load_k1(af_sc, k1)
        _, ab = _load_k1(ab_sc, k1)
        z = _dot(g_ref[k1], jnp.concatenate([af, ab], axis=1))
        zf, zb = z[:, :LANES], z[:, LANES:]
        o_ref[0, k1] = (zf[:FFT_N2] + zb[:FFT_N2]) / den
        o_ref[1, k1] = (zf[FFT_N2:] - zb[FFT_N2:]) / den
        return c

    lax.fori_loop(0, K1U, kbody, 0, unroll=LAT_UNROLL_K)


def _lat_spectrum(hdec, s, wa, gtab):
    nblk = HY_ORDER * HY_W // LANES
    slab = NROW1 * PITCH
    return pl.pallas_call(
        _lat_spec_kernel,
        grid=(nblk,),
        in_specs=[pl.BlockSpec((DEC_SEQ, 2 * LANES), lambda i: (0, i)),
                  pl.BlockSpec((SUBLANES, 2 * LANES), lambda i: (0, i)),
                  pl.BlockSpec(wa.shape, lambda i: (0, 0)),
                  pl.BlockSpec(gtab.shape, lambda i: (0, 0, 0))],
        out_specs=pl.BlockSpec((2, K1U, FFT_N2, LANES), lambda i: (0, 0, 0, i)),
        out_shape=jax.ShapeDtypeStruct((2, K1U, FFT_N2, HY_ORDER * HY_W), F32),
        scratch_shapes=[pltpu.VMEM((slab, LANES), F32), pltpu.VMEM((slab, LANES), F32),
                        pltpu.VMEM((SLABS * PITCH, LANES), F32), pltpu.VMEM((SLABS * PITCH, LANES), F32)],
        compiler_params=_cparams(("arbitrary",)),
        name="hy_lat_spectrum",
    )(hdec, s, wa, gtab)


def _lat_conv_kernel(z_ref, x_ref, wa_ref, wai_ref, g_ref, gi_ref, h_ref, bias_ref, o_ref,
                     zp_sc, xp_sc, op_sc, a_sc):
    full = slice(None)
    _pitch_copy_in(z_ref, full, zp_sc)
    _pitch_copy_in(x_ref, full, xp_sc)
    _dft_stage_a((zp_sc,), wa_ref, (a_sc,))

    def kbody(k1, c):
        base, a = _load_k1(a_sc, k1)
        z = _dot(g_ref[k1], a)
        zr, zi = z[:FFT_N2], z[FFT_N2:]
        hr = h_ref[0, k1]
        hi = h_ref[1, k1]
        y = jnp.concatenate([zr * hr - zi * hi, zr * hi + zi * hr], axis=0).astype(BF16)
        bp = _dot(gi_ref[k1], y)
        a_sc[pl.ds(base, FFT_N2), :] = bp[:FFT_N2]
        a_sc[pl.ds(base + PITCH, FFT_N2), :] = bp[FFT_N2:]
        return c

    lax.fori_loop(0, K1U, kbody, 0, unroll=LAT_UNROLL_K)
    bias = bias_ref[...]

    def rbody(r, c):
        yb = a_sc[pl.ds(r, SLABS, stride=PITCH), :].astype(BF16)
        conv = _dot(wai_ref[...], yb)
        zz = zp_sc[pl.ds(r, NROW1, stride=PITCH), :]
        xx = xp_sc[pl.ds(r, NROW1, stride=PITCH), :]
        op_sc[pl.ds(r, NROW1, stride=PITCH), :] = xx * (conv + zz * bias)
        return c

    lax.fori_loop(0, FFT_N2, rbody, 0, unroll=LAT_UNROLL_R)
    for n1 in range(NROW1):
        o_ref[FFT_N2 * n1:FFT_N2 * (n1 + 1), :] = op_sc[PITCH * n1:PITCH * n1 + FFT_N2, :]


def _lat_conv(z, xg, wa, wainv, gtab, gitab, hspec, bias, order):
    ncb = HY_W // LANES
    slab = NROW1 * PITCH
    blk = pl.BlockSpec((DEC_SEQ, LANES), lambda cb, b: (b, cb))
    const = lambda a: pl.BlockSpec(a.shape, lambda cb, b: (0,) * a.ndim)
    return pl.pallas_call(
        _lat_conv_kernel,
        grid=(ncb, DEC_BATCH),
        in_specs=[blk, blk, const(wa), const(wainv), const(gtab), const(gitab),
                  pl.BlockSpec((2, K1U, FFT_N2, LANES), lambda cb, b: (0, 0, 0, order * ncb + cb)),
                  pl.BlockSpec((1, LANES), lambda cb, b: (0, cb))],
        out_specs=blk,
        out_shape=jax.ShapeDtypeStruct((M_LAT, HY_W), F32),
        scratch_shapes=[pltpu.VMEM((slab, LANES), F32), pltpu.VMEM((slab, LANES), F32),
                        pltpu.VMEM((slab, LANES), F32), pltpu.VMEM((SLABS * PITCH, LANES), F32)],
        compiler_params=_cparams(("arbitrary", "arbitrary")),
        name="hy_lat_conv",
    )(z, xg, wa, wainv, gtab, gitab, hspec, bias)


def _angle(m, n):
    return (m % n).astype(F32) * (2.0 * math.pi / n)


def _ctx_tables():
    n = 2 * SEQ
    nf = SEQ + SUBLANES
    k = jnp.arange(nf, dtype=jnp.int32)[:, None]
    t = jnp.arange(SEQ, dtype=jnp.int32)[None, :]
    th = _angle(k * t, n)
    live = (k <= SEQ).astype(F32)
    wgt = jnp.where((k == 0) | (k == SEQ), 1.0, 2.0) * live / n
    fmat = jnp.concatenate([live * jnp.cos(th), -live * jnp.sin(th)], axis=0)
    finv = jnp.concatenate([(wgt * jnp.cos(th)).T, (-wgt * jnp.sin(th)).T], axis=1)
    return fmat.astype(BF16), finv.astype(BF16)


def _lat_tables():
    n1, n2 = FFT_N1, FFT_N2
    n = n1 * n2
    k1 = jnp.arange(K1U, dtype=jnp.int32)
    th1 = _angle(k1[:, None] * jnp.arange(NROW1, dtype=jnp.int32)[None, :], n1)
    wa = jnp.stack([jnp.cos(th1), -jnp.sin(th1)], axis=1).reshape(2 * K1U, NROW1)
    wa = jnp.pad(wa, ((0, SLABS - 2 * K1U), (0, 0)))
    wgt = jnp.where((k1 == 0) | (k1 == n1 // 2), 1.0, 2.0)[:, None] / n
    wainv = jnp.stack([wgt * jnp.cos(th1), -wgt * jnp.sin(th1)], axis=1).reshape(2 * K1U, NROW1).T
    wainv = jnp.pad(wainv, ((0, 0), (0, SLABS - 2 * K1U)))
    k2 = jnp.arange(n2, dtype=jnp.int32)
    nn2 = jnp.arange(n2, dtype=jnp.int32)
    ta = _angle(k1[:, None] * nn2[None, :], n)[:, None, :]
    tb = _angle(k2[:, None] * nn2[None, :], n2)[None, :, :]
    ca, sa, cb_, sb = jnp.cos(ta), jnp.sin(ta), jnp.cos(tb), jnp.sin(tb)
    gr = ca * cb_ - sa * sb
    gi = -(sa * cb_ + ca * sb)
    g = jnp.concatenate([jnp.concatenate([gr, -gi], axis=2), jnp.concatenate([gi, gr], axis=2)],
                        axis=1).astype(BF16)
    ginv = jnp.swapaxes(g, 1, 2)
    return wa.astype(BF16), wainv.astype(BF16), g, ginv


def _filter_features(L):
    t = jnp.linspace(0.0, 1.0, L, dtype=F32)[:, None]
    bands = (HY_EMB - 1) // 2
    w = (2.0 * math.pi / L) * jnp.arange(L, dtype=F32)[:, None]
    f = jnp.linspace(1e-4, bands - 1, bands, dtype=F32)[None, :]
    z = jnp.concatenate([t, jnp.cos(f * w), -jnp.sin(f * w)], axis=-1)
    z = jnp.pad(z, ((0, 0), (0, LANES - HY_EMB)))
    return z, jnp.broadcast_to(t, (L, LANES))


def _hyena_filter(p, L, blocked):
    feats, tcol = _filter_features(L)
    deltas = jnp.linspace(math.log(1e-2) / HY_FAST_PCT, math.log(1e-2) / HY_SLOW_PCT, HY_W, dtype=F32)
    absdelta = jnp.tile(jnp.abs(deltas), HY_ORDER * 2)[None, :]
    w3 = p["w3"]
    if blocked:
        reorder = lambda a: a.reshape(a.shape[0], HY_ORDER, 2, HY_W // LANES, LANES).transpose(
            0, 1, 3, 2, 4).reshape(a.shape[0], HY_ORDER * 2 * HY_W)
        w3, absdelta = reorder(w3), reorder(absdelta)
    return _hyfilt(feats, tcol, p["w1"], p["b1"], p["f1"], p["w2"], p["b2"], p["f2"], w3, absdelta)


def _hyena_ctx(u_hy, p, tabs):
    fmat, finv = tabs
    v, x1, x2 = _shortconv(u_hy, p["cw"], p["cb"], row_off=0, nseq=BATCH, seqlen=SEQ, tc=SEQ)
    hdec, s = _hyena_filter(p, SEQ, False)
    hspec = _ctx_spectrum(fmat, hdec, s)
    z = _ctx_conv(v, x1, fmat, finv, hspec, p["bias"][0:1], 0, F32)
    return _ctx_conv(z, x2, fmat, finv, hspec, p["bias"][1:2], 1, F32)


def _hyena_lat(u_hy, p, tabs):
    wa, wainv, gtab, gitab = tabs
    v, x1, x2 = _shortconv(u_hy, p["cw"], p["cb"], row_off=M_CTX, nseq=DEC_BATCH, seqlen=DEC_SEQ, tc=512)
    hdec, s = _hyena_filter(p, DEC_SEQ, True)
    hspec = _lat_spectrum(hdec, s, wa, gtab)
    z = _lat_conv(v, x1, wa, wainv, gtab, gitab, hspec, p["bias"][0:1], 0)
    return _lat_conv(z, x2, wa, wainv, gtab, gitab, hspec, p["bias"][1:2], 1)


def _stage3_kernel(*refs, nx, nctx):
    (mod_ref, g1_ref, wg_ref, ylc_ref, yll_ref, ymc_ref, yml_ref, yhc_ref, yhl_ref,
     wl_ref, wm_ref, wh_ref, wo_ref, g2_ref, xo_ref, xm2_ref) = refs[nx:]
    is_ctx = pl.program_id(0) < nctx
    x = _read_tokens(refs[:nx], nctx)
    xm = _rms(x, g1_ref[...]) * (1.0 + mod_ref[1:2, :]) + mod_ref[0:1, :]
    xb = xm.astype(BF16)
    merged = None
    branches = ((ylc_ref, yll_ref, wl_ref), (ymc_ref, yml_ref, wm_ref), (yhc_ref, yhl_ref, wh_ref))
    for bidx, (yc_ref, yl_ref, w_ref) in enumerate(branches):
        gate = _sigmoid(_dot(xb, wg_ref[:, D_MODEL * bidx:D_MODEL * (bidx + 1)]))
        y = jnp.where(is_ctx, yc_ref[...], yl_ref[...]).astype(BF16)
        term = gate * _dot(y, w_ref[...])
        merged = term if merged is None else merged + term
    xo = x + mod_ref[2:3, :] * _dot(merged.astype(BF16), wo_ref[...])
    xo_ref[...] = xo
    xm2 = _rms(xo, g2_ref[...]) * (1.0 + mod_ref[4:5, :]) + mod_ref[3:4, :]
    xm2_ref[...] = xm2.astype(BF16)


def _stage3(x, modl, g1, wg, ylru, ymla, yhy, wl, wm, wh, wo, g2):
    tm = TM3
    nctx = M_CTX // tm
    full = lambda shape: pl.BlockSpec(shape, lambda i: (0,) * len(shape))
    row = lambda cols: pl.BlockSpec((tm, cols), lambda i: (i, 0))
    ctx = lambda cols: pl.BlockSpec((tm, cols), lambda i: (jnp.minimum(i, nctx - 1), 0))
    lat = lambda cols: pl.BlockSpec((tm, cols), lambda i: (jnp.maximum(i - nctx, 0), 0))
    return pl.pallas_call(
        functools.partial(_stage3_kernel, nx=len(x), nctx=nctx),
        grid=(M_TOK // tm,),
        in_specs=_token_specs(x, tm) + [
                  pl.BlockSpec((None, SUBLANES, D_MODEL), lambda i: (_mod_row(i, tm), 0, 0)),
                  full((1, D_MODEL)), full((D_MODEL, 3 * D_MODEL)),
                  ctx(LRU_W), lat(LRU_W), ctx(N_HEADS * D_V), lat(N_HEADS * D_V), ctx(HY_W), lat(HY_W),
                  full((LRU_W, D_MODEL)), full((N_HEADS * D_V, D_MODEL)), full((HY_W, D_MODEL)),
                  full((D_MODEL, D_MODEL)), full((1, D_MODEL))],
        out_specs=[row(D_MODEL), row(D_MODEL)],
        out_shape=[jax.ShapeDtypeStruct((M_TOK, D_MODEL), F32), jax.ShapeDtypeStruct((M_TOK, D_MODEL), BF16)],
        compiler_params=_cparams(("arbitrary",)),
        name="stage3",
    )(*x, modl, g1, wg, *ylru, *ymla, *yhy, wl, wm, wh, wo, g2)


def _ffn_kernel(xm_ref, x_ref, mod_ref, wg_ref, wu_ref, wd_ref, o_ref, acc_sc, *, nchunks):
    j = pl.program_id(1)
    xb = xm_ref[...]
    g = _dot(xb, wg_ref[...])
    u = _dot(xb, wu_ref[...])
    hid = (g * _sigmoid(g) * u).astype(BF16)
    part = _dot(hid, wd_ref[...])

    @pl.when(j == 0)
    def _():
        acc_sc[...] = part

    @pl.when(j > 0)
    def _():
        acc_sc[...] = acc_sc[...] + part

    @pl.when(j == nchunks - 1)
    def _():
        o_ref[...] = x_ref[...] + mod_ref[5:6, :] * acc_sc[...]


def _ffn_dense(xm2, x, modl, wg, wu, wd):
    tm = TM_FFN
    nchunks = 1
    cw = D_FF // nchunks
    return pl.pallas_call(
        functools.partial(_ffn_kernel, nchunks=nchunks),
        grid=(M_TOK // tm, nchunks),
        in_specs=[pl.BlockSpec((tm, D_MODEL), lambda i, j: (i, 0)),
                  pl.BlockSpec((tm, D_MODEL), lambda i, j: (i, 0)),
                  pl.BlockSpec((None, SUBLANES, D_MODEL), lambda i, j: (_mod_row(i, tm), 0, 0)),
                  pl.BlockSpec((D_MODEL, cw), lambda i, j: (0, j)),
                  pl.BlockSpec((D_MODEL, cw), lambda i, j: (0, j)),
                  pl.BlockSpec((cw, D_MODEL), lambda i, j: (j, 0))],
        out_specs=pl.BlockSpec((tm, D_MODEL), lambda i, j: (i, 0)),
        out_shape=jax.ShapeDtypeStruct((M_TOK, D_MODEL), F32),
        scratch_shapes=[pltpu.VMEM((tm, D_MODEL), F32)],
        compiler_params=pltpu.CompilerParams(
            dimension_semantics=("arbitrary", "arbitrary"), vmem_limit_bytes=VMEM_LIMIT,
            allow_input_fusion=[False, False, False, True, True, True]),
        name="ffn_dense",
    )(xm2, x, modl, wg, wu, wd)


MOE_TILE = 256
MOE_ROWS = 2 * M_TOK + N_EXPERTS * MOE_TILE
MOE_TILES = MOE_ROWS // MOE_TILE
MOE_PAD_ROWS = MOE_ROWS - 2 * M_TOK
MOE_SEG = D_MODEL // LANES


def _to_row_tiles(val, ref, base):
    n = val.shape[0]
    for j in range(MOE_SEG):
        ref[pl.ds(base + j, n, stride=MOE_SEG), :] = val[:, LANES * j:LANES * (j + 1)]


def _from_row_tiles(ref, base, n):
    return jnp.concatenate([ref[pl.ds(base + j, n, stride=MOE_SEG), :] for j in range(MOE_SEG)], axis=1)


def _router_kernel(xm_ref, wr_ref, tri_ref, sel_ref, xp_ref, cnt_ref, base_sc):
    i = pl.program_id(0)
    xb = xm_ref[...]
    tm = xb.shape[0]
    lane = lax.broadcasted_iota(jnp.int32, (tm, LANES), 1)

    @pl.when(i == 0)
    def _():
        base_sc[...] = jnp.zeros_like(base_sc)

    logits = jnp.where(lane < N_EXPERTS, _dot(xb, wr_ref[...]), -1e30)
    mx = jnp.max(logits, axis=-1, keepdims=True)
    ex = jnp.exp(logits - mx)
    probs = ex / jnp.sum(ex, axis=-1, keepdims=True)
    p1 = jnp.max(probs, axis=-1, keepdims=True)
    i1 = jnp.min(jnp.where(probs == p1, lane, LANES), axis=-1, keepdims=True)
    rest = jnp.where(lane == i1, -1.0, probs)
    p2 = jnp.max(rest, axis=-1, keepdims=True)
    i2 = jnp.min(jnp.where(rest == p2, lane, LANES), axis=-1, keepdims=True)
    tot = p1 + p2
    oh1 = lane == i1
    oh2 = lane == i2
    oh = jnp.where(oh1 | oh2, 1.0, 0.0)
    before = base_sc[0:1, :] + _dot(tri_ref[...], oh.astype(BF16))
    r1 = jnp.sum(jnp.where(oh1, before, 0.0), axis=-1, keepdims=True)
    r2 = jnp.sum(jnp.where(oh2, before, 0.0), axis=-1, keepdims=True)
    base_sc[0:1, :] = base_sc[0:1, :] + jnp.sum(oh, axis=0, keepdims=True)
    cnt_ref[...] = base_sc[...]
    sel = jnp.where(lane == 0, p1 / tot, 0.0) + jnp.where(lane == 1, p2 / tot, 0.0)
    sel = sel + jnp.where(lane == 2, i1.astype(F32), 0.0) + jnp.where(lane == 3, i2.astype(F32), 0.0)
    sel_ref[...] = sel + jnp.where(lane == 4, r1, 0.0) + jnp.where(lane == 5, r2, 0.0)
    _to_row_tiles(xb.astype(F32), xp_ref, 0)


def _moe_router(xm2, wr):
    tm = TM_FFN
    tri = jnp.tril(jnp.ones((tm, tm), F32), -1).astype(BF16)
    return pl.pallas_call(
        _router_kernel,
        grid=(M_TOK // tm,),
        in_specs=[pl.BlockSpec((tm, D_MODEL), lambda i: (i, 0)),
                  pl.BlockSpec((D_MODEL, LANES), lambda i: (0, 0)),
                  pl.BlockSpec((tm, tm), lambda i: (0, 0))],
        out_specs=[pl.BlockSpec((tm, LANES), lambda i: (i, 0)),
                   pl.BlockSpec((tm * MOE_SEG, LANES), lambda i: (i, 0)),
                   pl.BlockSpec((SUBLANES, LANES), lambda i: (0, 0))],
        out_shape=[jax.ShapeDtypeStruct((M_TOK, LANES), F32),
                   jax.ShapeDtypeStruct((M_TOK * MOE_SEG, LANES), F32),
                   jax.ShapeDtypeStruct((SUBLANES, LANES), F32)],
        scratch_shapes=[pltpu.VMEM((SUBLANES, LANES), F32)],
        compiler_params=_cparams(("arbitrary",)),
        name="moe_router",
    )(xm2, wr, tri)


def _row_copy(src, srow8, dst, drow8, sem):
    aligned = lambda r: r if isinstance(r, int) else pl.multiple_of(r, MOE_SEG)
    return pltpu.make_async_copy(src.at[pl.ds(aligned(srow8), MOE_SEG), :],
                                 dst.at[pl.ds(aligned(drow8), MOE_SEG), :], sem)


def _dispatch_kernel(pos_ref, pad_ref, xp_ref, xs_ref, ring_sc, zero_sc, sem, *, nsteps):
    i = pl.program_id(0)
    slot = i % 2
    nrow = MOE_TILE
    slot_rows = nrow * MOE_SEG

    def wait_slot(s):
        for _ in range(2):
            pltpu.make_async_copy(ring_sc.at[s], xs_ref.at[pl.ds(0, slot_rows), :], sem.at[s]).wait()

    @pl.when(i >= 2)
    def _():
        wait_slot(slot)

    ring_sc[slot] = xp_ref[...]

    def body(t, c):
        _row_copy(ring_sc.at[slot], t * MOE_SEG, xs_ref, pos_ref[0, 2 * t], sem.at[slot]).start(priority=0)
        _row_copy(ring_sc.at[slot], t * MOE_SEG, xs_ref, pos_ref[0, 2 * t + 1], sem.at[slot]).start(priority=1)
        return c

    lax.fori_loop(0, nrow, body, 0, unroll=4)

    @pl.when(i == nsteps - 1)
    def _():
        zero_sc[...] = jnp.zeros_like(zero_sc)

        def zbody(t, c):
            _row_copy(zero_sc, 0, xs_ref, pad_ref[t], sem.at[2]).start()
            return c

        lax.fori_loop(0, MOE_PAD_ROWS, zbody, 0, unroll=4)
        wait_slot(1 - slot)
        wait_slot(slot)

        for _ in range(MOE_PAD_ROWS // nrow):
            pltpu.make_async_copy(ring_sc.at[0], xs_ref.at[pl.ds(0, slot_rows), :], sem.at[2]).wait()


def _moe_dispatch(xp, pos, padrows):
    nsteps = M_TOK // MOE_TILE
    return pl.pallas_call(
        functools.partial(_dispatch_kernel, nsteps=nsteps),
        grid=(nsteps,),
        in_specs=[pl.BlockSpec((None, 1, 2 * MOE_TILE), lambda i: (i, 0, 0), memory_space=pltpu.SMEM),
                  pl.BlockSpec(memory_space=pltpu.SMEM),
                  pl.BlockSpec((MOE_TILE * MOE_SEG, LANES), lambda i: (i, 0))],
        out_specs=pl.BlockSpec(memory_space=pl.ANY),
        out_shape=jax.ShapeDtypeStruct((MOE_ROWS * MOE_SEG, LANES), F32),
        scratch_shapes=[pltpu.VMEM((2, MOE_TILE * MOE_SEG, LANES), F32), pltpu.VMEM((MOE_SEG, LANES), F32),
                        pltpu.SemaphoreType.DMA((3,))],
        compiler_params=_cparams(("arbitrary",)),
        name="moe_dispatch",
    )(pos, padrows, xp)


def _experts_kernel(te_ref, nu_ref, xs_ref, wg_ref, wu_ref, wd_ref, ys_ref):
    i = pl.program_id(0)

    @pl.when(i < nu_ref[0])
    def _():
        xb = _from_row_tiles(xs_ref, 0, MOE_TILE).astype(BF16)
        g = _dot(xb, wg_ref[...])
        u = _dot(xb, wu_ref[...])
        hid = (g * _sigmoid(g) * u).astype(BF16)
        _to_row_tiles(_dot(hid, wd_ref[...]), ys_ref, 0)

    @pl.when(i >= nu_ref[0])
    def _():
        ys_ref[...] = jnp.zeros_like(ys_ref)


def _moe_experts(tile_expert, n_used, xs, wg, wu, wd):
    grid_spec = pltpu.PrefetchScalarGridSpec(
        num_scalar_prefetch=2,
        grid=(MOE_TILES,),
        in_specs=[pl.BlockSpec((MOE_TILE * MOE_SEG, LANES), lambda i, te, nu: (i, 0)),
                  pl.BlockSpec((None, D_MODEL, D_FF_E), lambda i, te, nu: (te[i], 0, 0)),
                  pl.BlockSpec((None, D_MODEL, D_FF_E), lambda i, te, nu: (te[i], 0, 0)),
                  pl.BlockSpec((None, D_FF_E, D_MODEL), lambda i, te, nu: (te[i], 0, 0))],
        out_specs=pl.BlockSpec((MOE_TILE * MOE_SEG, LANES), lambda i, te, nu: (i, 0)),
    )
    return pl.pallas_call(
        _experts_kernel,
        grid_spec=grid_spec,
        out_shape=jax.ShapeDtypeStruct((MOE_ROWS * MOE_SEG, LANES), F32),
        compiler_params=pltpu.CompilerParams(
            dimension_semantics=("arbitrary",), vmem_limit_bytes=VMEM_LIMIT,
            allow_input_fusion=[False, False, False, True, True, True]),
        name="moe_experts",
    )(tile_expert, n_used, xs, wg, wu, wd)


def _combine_kernel(pos_ref, x_ref, mod_ref, sel_ref, ys_ref, oc_ref, ol_ref, buf_sc, sem, *, nsteps, nctx):
    i = pl.program_id(0)
    slot = i % 2
    nrow = MOE_TILE
    part = nrow * MOE_SEG

    def start(s, off):
        def body(t, c):
            for k in range(2):
                _row_copy(ys_ref, pos_ref[0, off + 2 * t + k], buf_sc, (2 * s + k) * part + t * MOE_SEG,
                          sem.at[s]).start(priority=k)
            return c
        lax.fori_loop(0, nrow, body, 0, unroll=4)

    @pl.when(i == 0)
    def _():
        start(0, 0)

    @pl.when(i + 1 < nsteps)
    def _():
        start(1 - slot, 2 * nrow)

    for k in range(2):
        pltpu.make_async_copy(ys_ref.at[pl.ds(0, part), :], buf_sc.at[pl.ds(0, part), :], sem.at[slot]).wait()
    lane = lax.broadcasted_iota(jnp.int32, (nrow, LANES), 1)
    sel = sel_ref[...]
    w1 = jnp.sum(jnp.where(lane == 0, sel, 0.0), axis=-1, keepdims=True)
    w2 = jnp.sum(jnp.where(lane == 1, sel, 0.0), axis=-1, keepdims=True)
    y = (w1 * _from_row_tiles(buf_sc, 2 * slot * part, nrow)
         + w2 * _from_row_tiles(buf_sc, (2 * slot + 1) * part, nrow))
    val = x_ref[...] + mod_ref[5:6, :] * y

    @pl.when(i < nctx)
    def _():
        oc_ref[...] = val

    @pl.when(i >= nctx)
    def _():
        ol_ref[...] = val


def _moe_combine(pos2, x, modl, sel, ys):
    nsteps = M_TOK // MOE_TILE
    tm = MOE_TILE
    nctx = M_CTX // tm
    return pl.pallas_call(
        functools.partial(_combine_kernel, nsteps=nsteps, nctx=nctx),
        grid=(nsteps,),
        in_specs=[pl.BlockSpec((None, 1, 4 * MOE_TILE), lambda i: (i, 0, 0), memory_space=pltpu.SMEM),
                  pl.BlockSpec((tm, D_MODEL), lambda i: (i, 0)),
                  pl.BlockSpec((None, SUBLANES, D_MODEL), lambda i: (_mod_row(i, tm), 0, 0)),
                  pl.BlockSpec((tm, LANES), lambda i: (i, 0)),
                  pl.BlockSpec(memory_space=pl.ANY)],
        out_specs=[pl.BlockSpec((tm, D_MODEL), lambda i: (jnp.minimum(i, nctx - 1), 0)),
                   pl.BlockSpec((tm, D_MODEL), lambda i: (jnp.maximum(i - nctx, 0), 0))],
        out_shape=[jax.ShapeDtypeStruct((M_CTX, D_MODEL), F32), jax.ShapeDtypeStruct((M_LAT, D_MODEL), F32)],
        scratch_shapes=[pltpu.VMEM((2 * 2 * MOE_TILE * MOE_SEG, LANES), F32), pltpu.SemaphoreType.DMA((2,))],
        compiler_params=_cparams(("arbitrary",)),
        name="moe_combine",
    )(pos2, x, modl, sel, ys)


def _ffn_moe(xm2, x, modl, wr, wg, wu, wd):
    sel, xp, cnt = _moe_router(xm2, wr)
    counts = cnt[0, :N_EXPERTS].astype(jnp.int32)
    padded = ((counts + MOE_TILE - 1) // MOE_TILE) * MOE_TILE
    ends = jnp.cumsum(padded)
    offs = ends - padded
    experts = sel[:, 2:4].astype(jnp.int32)
    ranks = sel[:, 4:6].astype(jnp.int32)
    pos = (offs[experts] + ranks) * MOE_SEG
    tile_start = jnp.arange(MOE_TILES, dtype=jnp.int32) * MOE_TILE
    tile_expert = jnp.minimum(jnp.sum(tile_start[:, None] >= ends[None, :], axis=1), N_EXPERTS - 1)
    n_used = (ends[-1] // MOE_TILE).astype(jnp.int32)[None]
    rows = jnp.arange(MOE_ROWS, dtype=jnp.int32)
    row_expert = jnp.repeat(tile_expert, MOE_TILE)
    written = (rows < ends[-1]) & (rows - offs[row_expert] < counts[row_expert])
    padrows = jnp.nonzero(~written, size=MOE_PAD_ROWS)[0].astype(jnp.int32) * MOE_SEG
    pos_tiles = pos.reshape(M_TOK // MOE_TILE, 1, 2 * MOE_TILE)
    xs = _moe_dispatch(xp, pos_tiles, padrows)
    ys = _moe_experts(tile_expert.astype(jnp.int32), n_used, xs, wg, wu, wd)
    nxt = jnp.concatenate([pos_tiles[1:], pos_tiles[-1:]], axis=0)
    pos2 = jnp.concatenate([pos_tiles, nxt], axis=2)
    return tuple(_moe_combine(pos2, x, modl, sel, ys))


def _block_diag(w):
    nb, bs, _ = w.shape
    eye = jnp.eye(nb, dtype=w.dtype)
    return jnp.einsum("njk,nm->njmk", w, eye).reshape(nb * bs, nb * bs)


def _head_pad_cols(w, width):
    r = w.shape[0]
    return jnp.pad(w, ((0, 0), (0, 0), (0, HEAD_PAD - width))).reshape(r, N_HEADS * HEAD_PAD)


def _swap_rope_pairs(a):
    nope, rope = a[..., :D_NOPE], a[..., D_NOPE:]
    sw = rope.reshape(rope.shape[:-1] + (D_ROPE // 2, 2))[..., ::-1].reshape(rope.shape)
    return jnp.concatenate([nope, sw], axis=-1)


def _head_gain(g):
    rows = jnp.stack([g, _swap_rope_pairs(g)], axis=0)
    return jnp.pad(rows, ((0, SUBLANES - 2), (0, HEAD_PAD - D_QK)))


def _rope_tables(tm):
    rows = DEC_SEQ // GRID_W
    row = jnp.repeat(jnp.arange(rows, dtype=F32), GRID_W)
    col = jnp.tile(jnp.arange(GRID_W, dtype=F32), rows)
    half = D_ROPE // 2
    inv_freq = ROPE_BASE ** (-jnp.arange(0, half, 2, dtype=F32) / half)
    ang = jnp.concatenate([row[:, None] * inv_freq, col[:, None] * inv_freq], axis=-1)
    cos, sin = jnp.cos(ang), jnp.sin(ang)
    cos2 = jnp.repeat(cos, 2, axis=1)
    sin2 = jnp.stack([-sin, sin], axis=-1).reshape(DEC_SEQ, D_ROPE)
    cos_t = jnp.pad(cos2, ((0, 0), (D_NOPE, HEAD_PAD - D_QK)), constant_values=1.0)
    sin_t = jnp.pad(sin2, ((0, 0), (D_NOPE, HEAD_PAD - D_QK)))
    cos_t = jnp.concatenate([jnp.ones((tm, HEAD_PAD), F32), cos_t], axis=0)
    sin_t = jnp.concatenate([jnp.zeros((tm, HEAD_PAD), F32), sin_t], axis=0)
    return cos_t, sin_t


def kernel(x_prompt, x_sample, cache_ckv, cache_krope, state_lru, c, c_ctx, norm1, norm2, w_ada, b_ada, w_in, mla_q_norm, mla_kv_norm, mla_w_uq, mla_w_uk, mla_w_uv, mla_q_qknorm, mla_k_qknorm, lru_conv_w, lru_conv_b, lru_w_gate, lru_b_gate, lru_lambda, hy_conv_w, hy_conv_b, hy_w1, hy_b1, hy_freq1, hy_w2, hy_b2, hy_freq2, hy_w3, hy_bias, w_lru_out, w_mla_out, w_hy_out, w_out, ffn_w_gate, ffn_w_up, ffn_w_down, moe_w_router, moe_w_gate, moe_w_up, moe_w_down):
    x = (x_prompt.reshape(M_CTX, D_MODEL), x_sample.reshape(M_LAT, D_MODEL))

    cond = jnp.concatenate([c_ctx[None, :], c, jnp.zeros((SUBLANES - 1 - DEC_BATCH, D_MODEL), F32)], axis=0)
    mod = _adaln(cond, w_ada, b_ada).reshape(DEPTH, SUBLANES, 6, D_MODEL)
    mod = jnp.pad(mod, ((0, 0), (0, 0), (0, SUBLANES - 6), (0, 0)))

    cos_t, sin_t = _rope_tables(TM1)
    ctx_tabs = _ctx_tables()
    lat_tabs = _lat_tables()
    zero_state = jnp.zeros((BATCH, SUBLANES, LRU_W), F32)

    ckv_out, kr_out, st_out = [], [], []
    for l in range(DEPTH):
        wl = w_in[l]
        wkr = jnp.concatenate([jnp.zeros((D_MODEL, D_NOPE), F32), wl[:, 896:928]], axis=1)
        krblk = lambda w: jnp.pad(w, ((0, 0), (0, HEAD_PAD - D_QK)))
        w1 = jnp.concatenate([wl[:, :896], krblk(wkr), krblk(_swap_rope_pairs(wkr)), wl[:, 928:2464]],
                             axis=1).astype(BF16)
        wgates = wl[:, 2464:].astype(BF16)
        wuq = _head_pad_cols(mla_w_uq[l], D_QK).astype(BF16)
        wuqs = _head_pad_cols(_swap_rope_pairs(mla_w_uq[l]), D_QK).astype(BF16)
        wuk = _head_pad_cols(mla_w_uk[l], D_NOPE).astype(BF16)
        wuv = mla_w_uv[l].reshape(KV_RANK, N_HEADS * D_V).astype(BF16)
        gq = _head_gain(mla_q_qknorm[l])
        gk = _head_gain(mla_k_qknorm[l])

        ulru, uhy, ckv, krb, q, k, v = _stage1(
            x, mod[l], norm1[l][None, :], w1, mla_kv_norm[l][None, :], mla_q_norm[l][None, :],
            wuq, wuqs, gq, wuk, gk, wuv, cos_t, sin_t)
        ckv_out.append(ckv[:M_CTX].reshape(BATCH, SEQ, KV_RANK))
        kr_out.append(krb[:M_CTX, D_NOPE:D_QK].reshape(BATCH, SEQ, D_ROPE))

        kc, vc = _kvprep(cache_ckv[:, l].reshape(DEC_BATCH * PAST_LEN, KV_RANK),
                         jnp.pad(cache_krope[:, l].reshape(DEC_BATCH * PAST_LEN, D_ROPE),
                                 ((0, 0), (D_NOPE, HEAD_PAD - D_QK))),
                         wuk, gk, wuv)
        ymla = (_attention_ctx(q, k, v), _attention_lat(q, k, v, kc, vc))

        lp = dict(
            cw=jnp.pad(lru_conv_w[l], ((0, SUBLANES - 4), (0, 0))), cb=lru_conv_b[l][None, :],
            wr=[_block_diag(lru_w_gate[l, d, 0]).astype(BF16) for d in range(2)],
            wi=[_block_diag(lru_w_gate[l, d, 1]).astype(BF16) for d in range(2)],
            br=[lru_b_gate[l, d, 0][None, :] for d in range(2)],
            bi=[lru_b_gate[l, d, 1][None, :] for d in range(2)],
            lam=[lru_lambda[l, d][None, :] for d in range(2)])
        y_c, stf, stb = _lru_mixer(ulru, lp, (zero_state, zero_state), row_off=0, nseq=BATCH, seqlen=SEQ, tc=SEQ)
        st_out.append(jnp.stack([stf, stb], axis=1))
        h0 = [jnp.broadcast_to(state_lru[:, l, d][:, None, :], (DEC_BATCH, SUBLANES, LRU_W)) for d in range(2)]
        y_l, _, _ = _lru_mixer(ulru, lp, h0, row_off=M_CTX, nseq=DEC_BATCH, seqlen=DEC_SEQ, tc=512)
        ylru = (y_c, y_l)

        hp = dict(
            cw=jnp.pad(hy_conv_w[l], ((0, SUBLANES - 3), (0, 0))), cb=hy_conv_b[l][None, :],
            w1=jnp.pad(hy_w1[l], ((0, LANES - HY_EMB), (0, 0))).astype(BF16), b1=hy_b1[l][None, :],
            f1=hy_freq1[l][None, :], w2=hy_w2[l].astype(BF16), b2=hy_b2[l][None, :], f2=hy_freq2[l][None, :],
            w3=hy_w3[l].astype(BF16), bias=hy_bias[l])
        yhy = (_hyena_ctx(uhy, hp, ctx_tabs), _hyena_lat(uhy, hp, lat_tabs))

        xmid, xm2 = _stage3(x, mod[l], norm1[l][None, :], wgates, ylru, ymla, yhy,
                            w_lru_out[l].astype(BF16), w_mla_out[l].astype(BF16), w_hy_out[l].astype(BF16),
                            w_out[l].astype(BF16), norm2[l][None, :])
        j = l // 2
        if l % 2 == 0:
            x = (_ffn_dense(xm2, xmid, mod[l], ffn_w_gate[j].astype(BF16), ffn_w_up[j].astype(BF16),
                            ffn_w_down[j].astype(BF16)),)
        else:
            wr = jnp.pad(moe_w_router[j], ((0, 0), (0, LANES - N_EXPERTS))).astype(BF16)
            x = _ffn_moe(xm2, xmid, mod[l], wr, moe_w_gate[j].astype(BF16), moe_w_up[j].astype(BF16),
                         moe_w_down[j].astype(BF16))

    xc, xl = x if len(x) == 2 else (x[0][:M_CTX], x[0][M_CTX:])
    y_prompt = xc.reshape(BATCH, SEQ, D_MODEL)
    y_sample = xl.reshape(DEC_BATCH, DEC_SEQ, D_MODEL)
    return (y_prompt, y_sample, jnp.stack(ckv_out, axis=1), jnp.stack(kr_out, axis=1), jnp.stack(st_out, axis=1))
```
